```python
import math
import jax, jax.numpy as jnp
from jax import lax
import numpy as np

D_MODEL = 1024
BATCH = 4
SEQ = 4096
DEPTH = 1

CHUNK = 64
CONV_W = 4
EPS = 1e-6

DN_HEADS = 4
DN_DK = 128
DN_DV = 128
DN_QK = DN_HEADS * DN_DK
DN_V = DN_HEADS * DN_DV
DN_CONV_CH = 2 * DN_QK + DN_V

ML_HEADS = 4
ML_DK = 64
ML_DV = 128
ML_QK = ML_HEADS * ML_DK
ML_V = ML_HEADS * ML_DV

D_MIX = DN_V + ML_V

OFF_DN_QKV = 0
OFF_DN_Z = OFF_DN_QKV + DN_CONV_CH
OFF_DN_B = OFF_DN_Z + DN_V
OFF_DN_A = OFF_DN_B + DN_HEADS
OFF_ML_QK = OFF_DN_A + DN_HEADS
OFF_ML_V = OFF_ML_QK + 2 * ML_QK
OFF_ML_O = OFF_ML_V + ML_V
OFF_ML_I = OFF_ML_O + ML_V
OFF_ML_F = OFF_ML_I + ML_HEADS
D_IN = OFF_ML_F + ML_HEADS

N_EXPERTS = 32
TOP_K = 4
D_FF = D_MODEL
SWIGLU_LIMIT = 7.0
SWIGLU_ALPHA = 1.702
MOE_BLOCK = 256

kernel_name = 'hymba_gdn_mlstm_moe_adaln_block'


def rms_norm(x, w):
    xf = x.astype(jnp.float32)
    y = xf * lax.rsqrt(jnp.mean(xf * xf, axis=-1, keepdims=True) + EPS)
    return (y * w.astype(jnp.float32)).astype(x.dtype)


def modulate(h, shift, scale):
    return h * (1.0 + scale) + shift


def l2_normalize(u):
    return u * lax.rsqrt(jnp.sum(u * u, axis=-1, keepdims=True) + EPS)


def causal_conv_silu(u, w):
    S = u.shape[1]
    up = jnp.pad(u, ((0, 0), (CONV_W - 1, 0), (0, 0)))
    y = sum(up[:, j:j + S] * w[j] for j in range(CONV_W))
    return jax.nn.silu(y)


def to_chunks(u, n_heads):
    B, S, W = u.shape
    return u.reshape(B, S // CHUNK, CHUNK, n_heads, W // n_heads).transpose(0, 3, 1, 2, 4)


def gate_chunks(u):
    B, S, H = u.shape
    return u.reshape(B, S // CHUNK, CHUNK, H).transpose(0, 3, 1, 2)


def from_chunks(u):
    B, H, NC, C, d = u.shape
    return u.transpose(0, 2, 3, 1, 4).reshape(B, NC * C, H, d)


def gated_deltanet(proj, conv_w, a_log, dt_bias, norm_w):
    B, S, _ = proj.shape
    qkv = causal_conv_silu(proj[..., OFF_DN_QKV:OFF_DN_Z], conv_w)
    q = l2_normalize(to_chunks(qkv[..., :DN_QK], DN_HEADS)) * (DN_DK ** -0.5)
    k = l2_normalize(to_chunks(qkv[..., DN_QK:2 * DN_QK], DN_HEADS))
    v = to_chunks(qkv[..., 2 * DN_QK:], DN_HEADS)
    beta = gate_chunks(jax.nn.sigmoid(proj[..., OFF_DN_B:OFF_DN_A]))
    g = gate_chunks(-jnp.exp(a_log) * jax.nn.softplus(proj[..., OFF_DN_A:OFF_ML_QK] + dt_bias))
    G = jnp.cumsum(g, axis=-1)
    incl = jnp.tril(jnp.ones((CHUNK, CHUNK), bool))
    strict = jnp.tril(jnp.ones((CHUNK, CHUNK), bool), k=-1)
    decay = jnp.exp(jnp.where(incl, G[..., :, None] - G[..., None, :], -jnp.inf))
    kk = jnp.einsum('bhncd,bhnsd->bhncs', k, k)
    lower = jnp.where(strict, beta[..., :, None] * kk * decay, 0.0)
    unit_lower = jnp.eye(CHUNK, dtype=lower.dtype) + lower
    rhs = jnp.concatenate([v * beta[..., None], k * (beta * jnp.exp(G))[..., None]], axis=-1)
    sol = lax.linalg.triangular_solve(unit_lower, rhs, left_side=True, lower=True,
                                      unit_diagonal=True)
    w_val, k_cum = sol[..., :DN_DV], sol[..., DN_DV:]
    qk = jnp.einsum('bhncd,bhnsd->bhncs', q, k) * decay
    G_last = G[..., -1:]
    q_dec = q * jnp.exp(G)[..., None]
    k_dec = k * jnp.exp(G_last - G)[..., None]
    s_dec = jnp.exp(G_last[..., 0])

    def step(state, xs):
        w_c, kc_c, qk_c, q_c, k_c, sd_c = xs
        v_new = w_c - jnp.einsum('bhcd,bhde->bhce', kc_c, state)
        o = jnp.einsum('bhcd,bhde->bhce', q_c, state) + jnp.einsum('bhcs,bhse->bhce', qk_c, v_new)
        state = sd_c[..., None, None] * state + jnp.einsum('bhsd,bhse->bhde', k_c, v_new)
        return state, o

    xs = tuple(jnp.moveaxis(t, 2, 0) for t in (w_val, k_cum, qk, q_dec, k_dec, s_dec))
    state0 = jnp.zeros((B, DN_HEADS, DN_DK, DN_DV), jnp.float32)
    _, o = lax.scan(step, state0, xs)
    o = from_chunks(jnp.moveaxis(o, 0, 2))
    z = proj[..., OFF_DN_Z:OFF_DN_B].reshape(B, S, DN_HEADS, DN_DV)
    o = rms_norm(o, norm_w) * jax.nn.silu(z)
    return o.reshape(B, S, DN_V)


def mlstm(proj, conv_w, i_bias, f_bias, norm_w):
    B, S, _ = proj.shape
    qk_in = causal_conv_silu(proj[..., OFF_ML_QK:OFF_ML_V], conv_w)
    q = to_chunks(qk_in[..., :ML_QK], ML_HEADS) * (ML_DK ** -0.5)
    k = to_chunks(qk_in[..., ML_QK:], ML_HEADS)
    v = to_chunks(proj[..., OFF_ML_V:OFF_ML_O], ML_HEADS)
    i_pre = gate_chunks(proj[..., OFF_ML_I:OFF_ML_F] + i_bias)
    log_f = gate_chunks(jax.nn.log_sigmoid(proj[..., OFF_ML_F:D_IN] + f_bias))
    b = jnp.cumsum(log_f, axis=-1)
    incl = jnp.tril(jnp.ones((CHUNK, CHUNK), bool))
    d_mat = jnp.where(incl, b[..., :, None] - b[..., None, :] + i_pre[..., None, :], -jnp.inf)
    m_intra = jnp.max(d_mat, axis=-1)
    b_last = b[..., -1]
    g_end = b_last[..., None] - b + i_pre
    g_end_max = jnp.max(g_end, axis=-1)
    qk = jnp.einsum('bhncd,bhnsd->bhncs', q, k)

    def step(carry, xs):
        c_s, n_s, m_s = carry
        q_c, k_c, v_c, qk_c, d_c, mi_c, b_c, bl_c, ge_c, gm_c = xs
        m_t = jnp.maximum(b_c + m_s[..., None], mi_c)
        inter = jnp.exp(b_c + m_s[..., None] - m_t)
        p = jnp.exp(d_c - m_t[..., None]) * qk_c
        num = (inter[..., None] * jnp.einsum('bhcd,bhde->bhce', q_c, c_s)
               + jnp.einsum('bhcs,bhse->bhce', p, v_c))
        den = inter * jnp.einsum('bhcd,bhd->bhc', q_c, n_s) + jnp.sum(p, axis=-1)
        h = num / jnp.maximum(jnp.abs(den), jnp.exp(-m_t))[..., None]
        m_new = jnp.maximum(bl_c + m_s, gm_c)
        keep = jnp.exp(bl_c + m_s - m_new)
        kw = k_c * jnp.exp(ge_c - m_new[..., None])[..., None]
        c_s = keep[..., None, None] * c_s + jnp.einsum('bhsd,bhse->bhde', kw, v_c)
        n_s = keep[..., None] * n_s + jnp.sum(kw, axis=-2)
        return (c_s, n_s, m_new), h

    xs = tuple(jnp.moveaxis(t, 2, 0) for t in
               (q, k, v, qk, d_mat, m_intra, b, b_last, g_end, g_end_max))
    carry0 = (jnp.zeros((B, ML_HEADS, ML_DK, ML_DV), jnp.float32),
              jnp.zeros((B, ML_HEADS, ML_DK), jnp.float32),
              jnp.zeros((B, ML_HEADS), jnp.float32))
    _, h = lax.scan(step, carry0, xs)
    h = from_chunks(jnp.moveaxis(h, 0, 2))
    h = rms_norm(h, norm_w.reshape(ML_HEADS, ML_DV))
    o_gate = jax.nn.sigmoid(proj[..., OFF_ML_O:OFF_ML_I]).reshape(B, S, ML_HEADS, ML_DV)
    return (h * o_gate).reshape(B, S, ML_V)


def moe(h, w_router, b_router, w_gate_up, b_gate_up, w_down, b_down):
    B, S, D = h.shape
    T = B * S
    xf = h.reshape(T, D)
    logits = (xf @ w_router + b_router).astype(jnp.float32)
    top_logit, top_idx = lax.top_k(logits, TOP_K)
    top_w = jax.nn.softmax(top_logit, axis=-1)
    n_assign = T * TOP_K
    n_blocks = n_assign // MOE_BLOCK + N_EXPERTS
    e_flat = top_idx.reshape(-1)
    order = jnp.argsort(e_flat)
    sorted_e = e_flat[order]
    sorted_tok = (order // TOP_K).astype(jnp.int32)
    sorted_w = top_w.reshape(-1)[order]
    counts = jnp.bincount(e_flat, length=N_EXPERTS)
    padded = (counts + MOE_BLOCK - 1) // MOE_BLOCK * MOE_BLOCK
    start = jnp.cumsum(counts) - counts
    pad_end = jnp.cumsum(padded)
    pad_start = pad_end - padded
    dest = pad_start[sorted_e] + jnp.arange(n_assign, dtype=jnp.int32) - start[sorted_e]
    slot_tok = jnp.full((n_blocks * MOE_BLOCK,), T, jnp.int32).at[dest].set(sorted_tok)
    slot_w = jnp.zeros((n_blocks * MOE_BLOCK,), jnp.float32).at[dest].set(sorted_w)
    block_e = jnp.minimum(
        jnp.searchsorted(pad_end, jnp.arange(n_blocks, dtype=pad_end.dtype) * MOE_BLOCK, side='right'),
        N_EXPERTS - 1)
    x_pad = jnp.concatenate([xf, jnp.zeros((1, D), xf.dtype)], axis=0)

    def expert_block(args):
        tok, e = args
        xb = x_pad[tok]
        gu = xb @ w_gate_up[e] + b_gate_up[e]
        gate = jnp.minimum(gu[:, :D_FF], SWIGLU_LIMIT)
        up = jnp.clip(gu[:, D_FF:], -SWIGLU_LIMIT, SWIGLU_LIMIT)
        act = (up + 1.0) * gate * jax.nn.sigmoid(SWIGLU_ALPHA * gate)
        return act @ w_down[e] + b_down[e]

    y = lax.map(expert_block, (slot_tok.reshape(n_blocks, MOE_BLOCK), block_e))
    y = y.reshape(-1, D) * slot_w[:, None].astype(y.dtype)
    out = jnp.zeros((T + 1, D), y.dtype).at[slot_tok].add(y)[:T]
    return out.reshape(B, S, D).astype(h.dtype)


def setup_inputs(seed: int = 0) -> dict:
    key = jax.random.key(seed)
    ks = jax.random.split(key, 26)
    f32 = jnp.float32

    def nrm(k, shape, scale):
        return scale * jax.random.normal(k, shape, f32)

    x = nrm(ks[0], (BATCH, SEQ, D_MODEL), 1.0)
    c = nrm(ks[1], (BATCH, D_MODEL), 1.0)
    w_ada = nrm(ks[2], (DEPTH, D_MODEL, 6 * D_MODEL), D_MODEL ** -0.5)
    b_ada = nrm(ks[3], (DEPTH, 6 * D_MODEL), 0.02)
    norm_mix = 1.0 + nrm(ks[4], (DEPTH, D_MODEL), 0.02)
    w_in = nrm(ks[5], (DEPTH, D_MODEL, D_IN), D_MODEL ** -0.5)
    dn_conv = nrm(ks[6], (DEPTH, CONV_W, DN_CONV_CH), CONV_W ** -0.5)
    dn_a_log = jnp.log(jax.random.uniform(ks[7], (DEPTH, DN_HEADS), f32, 1.0, 16.0))
    dt = jnp.exp(jax.random.uniform(ks[8], (DEPTH, DN_HEADS), f32, math.log(1e-3), math.log(1e-1)))
    dn_dt_bias = dt + jnp.log(-jnp.expm1(-dt))
    dn_norm = 1.0 + nrm(ks[9], (DEPTH, DN_DV), 0.02)
    ml_conv = nrm(ks[10], (DEPTH, CONV_W, 2 * ML_QK), CONV_W ** -0.5)
    ml_i_bias = nrm(ks[11], (DEPTH, ML_HEADS), 0.1)
    ml_f_bias = jnp.linspace(3.0, 6.0, ML_HEADS, dtype=f32) + nrm(ks[12], (DEPTH, ML_HEADS), 0.1)
    ml_norm = 1.0 + nrm(ks[13], (DEPTH, ML_V), 0.02)
    w_out = nrm(ks[14], (DEPTH, D_MIX, D_MODEL), D_MIX ** -0.5)
    norm_ffn = 1.0 + nrm(ks[15], (DEPTH, D_MODEL), 0.02)
    w_router = nrm(ks[16], (DEPTH, D_MODEL, N_EXPERTS), D_MODEL ** -0.5)
    b_router = nrm(ks[17], (DEPTH, N_EXPERTS), 0.01)
    w_gate_up = nrm(ks[18], (DEPTH, N_EXPERTS, D_MODEL, 2 * D_FF), D_MODEL ** -0.5)
    b_gate_up = nrm(ks[19], (DEPTH, N_EXPERTS, 2 * D_FF), 0.02)
    w_down = nrm(ks[20], (DEPTH, N_EXPERTS, D_FF, D_MODEL), D_FF ** -0.5)
    b_down = nrm(ks[21], (DEPTH, N_EXPERTS, D_MODEL), 0.02)
    w_ada_final = nrm(ks[22], (D_MODEL, 2 * D_MODEL), D_MODEL ** -0.5)
    b_ada_final = nrm(ks[23], (2 * D_MODEL,), 0.02)
    norm_final = 1.0 + nrm(ks[24], (D_MODEL,), 0.02)
    return {'x': x, 'c': c, 'w_ada': w_ada, 'b_ada': b_ada, 'norm_mix': norm_mix,
            'w_in': w_in, 'dn_conv': dn_conv, 'dn_a_log': dn_a_log, 'dn_dt_bias': dn_dt_bias,
            'dn_norm': dn_norm, 'ml_conv': ml_conv, 'ml_i_bias': ml_i_bias,
            'ml_f_bias': ml_f_bias, 'ml_norm': ml_norm, 'w_out': w_out, 'norm_ffn': norm_ffn,
            'w_router': w_router, 'b_router': b_router, 'w_gate_up': w_gate_up,
            'b_gate_up': b_gate_up, 'w_down': w_down, 'b_down': b_down,
            'w_ada_final': w_ada_final, 'b_ada_final': b_ada_final, 'norm_final': norm_final}


def reference(x, c, w_ada, b_ada, norm_mix, w_in, dn_conv, dn_a_log, dn_dt_bias, dn_norm,
              ml_conv, ml_i_bias, ml_f_bias, ml_norm, w_out, norm_ffn, w_router, b_router,
              w_gate_up, b_gate_up, w_down, b_down, w_ada_final, b_ada_final, norm_final):
    cond = jax.nn.silu(c)
    for l in range(DEPTH):
        mod = (cond @ w_ada[l] + b_ada[l])[:, None, :]
        sh1, sc1, g1, sh2, sc2, g2 = jnp.split(mod, 6, axis=-1)
        h = modulate(rms_norm(x, norm_mix[l]), sh1, sc1)
        proj = (h @ w_in[l]).astype(jnp.float32)
        y_a = gated_deltanet(proj, dn_conv[l], dn_a_log[l], dn_dt_bias[l], dn_norm[l])
        y_b = mlstm(proj, ml_conv[l], ml_i_bias[l], ml_f_bias[l], ml_norm[l])
        mix = jnp.concatenate([y_a, y_b], axis=-1).astype(x.dtype) @ w_out[l]
        x = x + g1 * mix
        h = modulate(rms_norm(x, norm_ffn[l]), sh2, sc2)
        x = x + g2 * moe(h, w_router[l], b_router[l], w_gate_up[l], b_gate_up[l],
                         w_down[l], b_down[l])
    sh_f, sc_f = jnp.split((cond @ w_ada_final + b_ada_final)[:, None, :], 2, axis=-1)
    return modulate(rms_norm(x, norm_final), sh_f, sc_f)
```

```python
import functools

import jax
import jax.numpy as jnp
from jax import lax
from jax.experimental import pallas as pl
from jax.experimental.pallas import tpu as pltpu

F32 = jnp.float32
BF16 = jnp.bfloat16

CHUNK = 64
CONV_W = 4
EPS = 1e-6

DN_HEADS = 4
DN_DK = 128
DN_DV = 128
ML_HEADS = 4
ML_DK = 64
ML_DV = 128
HEAD_LANES = 128

N_EXPERTS = 32
TOP_K = 4
SWIGLU_LIMIT = 7.0
SWIGLU_ALPHA = 1.702

C_DNQ = 0
C_DNK = 512
C_DNV = 1024
C_MLQ = 1536
C_MLK = 2048
N_CONV = 2560
C_DNZ = 2560
C_MLV = 3072
C_MLO = 3584
C_GATE = 4096
N_PROJ = 4224
N_REST = C_GATE - N_CONV

V7X_VMEM_LIMIT = 56 * 1024 * 1024

NEG_BIG = -1e30


def _sigmoid(x):
    return 1.0 / (1.0 + jnp.exp(-x))


def _softplus(x):
    return jnp.maximum(x, 0.0) + jnp.log(1.0 + jnp.exp(-jnp.abs(x)))


def _split3(v):
    hi = v.astype(BF16)
    r1 = v - hi.astype(F32)
    mid = r1.astype(BF16)
    lo = (r1 - mid.astype(F32)).astype(BF16)
    return hi, mid, lo


def _dot(a, b):
    return jnp.dot(a, b, preferred_element_type=F32)


def _dot_nt(a, b):
    return lax.dot_general(a, b, (((1,), (1,)), ((), ())), preferred_element_type=F32)


def _dot_tn(a, b):
    return lax.dot_general(a, b, (((0,), (0,)), ((), ())), preferred_element_type=F32)


def _dot_exact_right(sel_bf16, v):
    hi, mid, lo = _split3(v)
    return _dot(sel_bf16, hi) + _dot(sel_bf16, mid) + _dot(sel_bf16, lo)


def _dot_exact_left(v, sel_bf16):
    hi, mid, lo = _split3(v)
    return _dot(hi, sel_bf16) + _dot(mid, sel_bf16) + _dot(lo, sel_bf16)


def _mods_kernel(c_ref, w_ref, b_ref, o_ref):
    c = c_ref[...]
    cond = c * _sigmoid(c)
    ch, cm, cl = _split3(cond)
    wh, wm, wl = _split3(w_ref[...])
    acc = _dot(ch, wh) + (_dot(ch, wm) + _dot(cm, wh)) + (_dot(ch, wl) + _dot(cm, wm) + _dot(cl, wh))
    o_ref[...] = acc + b_ref[...]


def _mods(c_pad, w, b):
    m, d = c_pad.shape
    n = w.shape[1]
    tn = 1024
    return pl.pallas_call(
        _mods_kernel,
        grid=(n // tn,),
        in_specs=[pl.BlockSpec((m, d), lambda j: (0, 0)),
                  pl.BlockSpec((d, tn), lambda j: (0, j)),
                  pl.BlockSpec((1, tn), lambda j: (0, j))],
        out_specs=pl.BlockSpec((m, tn), lambda j: (0, j)),
        out_shape=jax.ShapeDtypeStruct((m, n), F32),
        compiler_params=pltpu.CompilerParams(dimension_semantics=("arbitrary",),
                                             vmem_limit_bytes=V7X_VMEM_LIMIT),
        name="mods",
    )(c_pad, w, b)


def _gate_transform(v, bias, alog, cls):
    vb = v + bias
    beta = _sigmoid(v)
    g = -jnp.exp(alog) * _softplus(vb)
    logf = -_softplus(-vb)
    return jnp.where(cls == 0, beta, jnp.where(cls == 1, g, jnp.where(cls == 2, vb, jnp.where(cls == 3, logf, 0.0))))


def _inproj_kernel(tiles_per_seq, x_ref, sh_ref, sc_ref, nw_ref, w_ref, wgt_ref, cw_ref, gpc_ref, gpr_ref,
                   conv_ref, rest_ref, gcol_ref, grow_ref, cbuf):
    tm = x_ref.shape[0]
    i = pl.program_id(0)
    x = x_ref[...]
    ms = jnp.mean(x * x, axis=-1, keepdims=True)
    h = x * lax.rsqrt(ms + EPS) * nw_ref[...]
    h = h * (1.0 + sc_ref[...]) + sh_ref[...]
    hb = h.astype(BF16)

    pc = _dot(hb, w_ref[:, 0:N_CONV])

    @pl.when(i % tiles_per_seq == 0)
    def _():
        cbuf[0:8, :] = jnp.zeros((8, N_CONV), F32)

    cbuf[8:tm + 8, :] = pc
    acc = cw_ref[CONV_W - 1:CONV_W, :] * pc
    for j in range(CONV_W - 1):
        acc = acc + cw_ref[j:j + 1, :] * cbuf[8 - (CONV_W - 1) + j:8 - (CONV_W - 1) + j + tm, :]
    cbuf[0:8, :] = cbuf[tm:tm + 8, :]
    y = acc * _sigmoid(acc)

    for hh in range(DN_HEADS):
        lo = C_DNQ + hh * HEAD_LANES
        qh = y[:, lo:lo + HEAD_LANES]
        conv_ref[:, lo:lo + HEAD_LANES] = (qh * lax.rsqrt(jnp.sum(qh * qh, axis=-1, keepdims=True) + EPS)) * (DN_DK ** -0.5)
        lo = C_DNK + hh * HEAD_LANES
        kh = y[:, lo:lo + HEAD_LANES]
        conv_ref[:, lo:lo + HEAD_LANES] = kh * lax.rsqrt(jnp.sum(kh * kh, axis=-1, keepdims=True) + EPS)
    conv_ref[:, C_DNV:C_MLQ] = y[:, C_DNV:C_MLQ]
    conv_ref[:, C_MLQ:C_MLK] = y[:, C_MLQ:C_MLK] * (ML_DK ** -0.5)
    conv_ref[:, C_MLK:N_CONV] = y[:, C_MLK:N_CONV]

    pr = _dot(hb, w_ref[:, N_CONV:C_GATE])
    z = pr[:, 0:512]
    rest_ref[:, 0:512] = z * _sigmoid(z)
    rest_ref[:, 512:1024] = pr[:, 512:1024]
    rest_ref[:, 1024:1536] = _sigmoid(pr[:, 1024:1536])

    r_i = lax.broadcasted_iota(jnp.int32, (tm, tm), 0)
    c_i = lax.broadcasted_iota(jnp.int32, (tm, tm), 1)
    same_chunk = (r_i // CHUNK) == (c_i // CHUNK)
    tril = jnp.where(same_chunk & (c_i <= r_i), 1.0, 0.0).astype(BF16)
    triu = jnp.where(same_chunk & (r_i <= c_i), 1.0, 0.0).astype(BF16)

    gc = _dot(hb, w_ref[:, C_GATE:N_PROJ])
    cls_c = lax.broadcasted_iota(jnp.int32, (tm, HEAD_LANES), 1) // 4
    gt = _gate_transform(gc, gpc_ref[0:1, :], gpc_ref[1:2, :], cls_c)
    cs = _dot_exact_right(tril, gt)
    gcol_ref[...] = jnp.where((cls_c == 1) | (cls_c == 3), cs, gt)

    gr = _dot_nt(wgt_ref[...], hb)
    cls_r = lax.broadcasted_iota(jnp.int32, (16, tm), 0) // 4
    gtr = _gate_transform(gr, gpr_ref[:, 0:1], gpr_ref[:, 1:2], cls_r)
    csr = _dot_exact_left(gtr, triu)
    grow_ref[...] = jnp.where((cls_r == 1) | (cls_r == 3), csr, gtr)


def _inproj(x2, sh, sc, nw, w_new, wgt, cw, gpc, gpr, seq, tm):
    t, d = x2.shape
    tps = seq // tm
    kern = functools.partial(_inproj_kernel, tps)
    return pl.pallas_call(
        kern,
        grid=(t // tm,),
        in_specs=[pl.BlockSpec((tm, d), lambda i: (i, 0)),
                  pl.BlockSpec((None, 1, d), lambda i: (i // tps, 0, 0)),
                  pl.BlockSpec((None, 1, d), lambda i: (i // tps, 0, 0)),
                  pl.BlockSpec((1, d), lambda i: (0, 0)),
                  pl.BlockSpec((d, N_PROJ), lambda i: (0, 0)),
                  pl.BlockSpec((16, d), lambda i: (0, 0)),
                  pl.BlockSpec((CONV_W, N_CONV), lambda i: (0, 0)),
                  pl.BlockSpec((8, HEAD_LANES), lambda i: (0, 0)),
                  pl.BlockSpec((16, HEAD_LANES), lambda i: (0, 0))],
        out_specs=[pl.BlockSpec((tm, N_CONV), lambda i: (i, 0)),
                   pl.BlockSpec((tm, N_REST), lambda i: (i, 0)),
                   pl.BlockSpec((tm, HEAD_LANES), lambda i: (i, 0)),
                   pl.BlockSpec((16, tm), lambda i: (0, i))],
        out_shape=[jax.ShapeDtypeStruct((t, N_CONV), F32),
                   jax.ShapeDtypeStruct((t, N_REST), F32),
                   jax.ShapeDtypeStruct((t, HEAD_LANES), F32),
                   jax.ShapeDtypeStruct((16, t), F32)],
        scratch_shapes=[pltpu.VMEM((tm + 8, N_CONV), F32)],
        compiler_params=pltpu.CompilerParams(dimension_semantics=("arbitrary",),
                                             vmem_limit_bytes=V7X_VMEM_LIMIT),
        name="inproj",
    )(x2, sh, sc, nw, w_new, wgt, cw, gpc, gpr)


def _chunk_masks():
    r = lax.broadcasted_iota(jnp.int32, (CHUNK, CHUNK), 0)
    c = lax.broadcasted_iota(jnp.int32, (CHUNK, CHUNK), 1)
    return r >= c, r > c, r == c


def _unit_lower_inverse(lower, eye):
    a = -lower
    t = jnp.where(eye, 1.0, 0.0) + a
    p = a
    n = 2
    while n < CHUNK:
        p = _dot(p, p)
        t = t + _dot(t, p)
        n *= 2
    return t


def _deltanet_kernel(q_ref, k_ref, v_ref, gc_ref, gr_ref, z_ref, nw_ref, o_ref, s_ref):
    n_chunks = gr_ref.shape[0]

    @pl.when(pl.program_id(1) == 0)
    def _():
        s_ref[...] = jnp.zeros(s_ref.shape, F32)

    incl, strict, eye = _chunk_masks()
    nw = nw_ref[...]

    def chunk(c, carry):
        r0 = pl.multiple_of(c * CHUNK, CHUNK)
        gcc = gc_ref[pl.ds(r0, CHUNK), :]
        grr = gr_ref[c]
        for hh in range(DN_HEADS):
            lanes = slice(hh * HEAD_LANES, (hh + 1) * HEAD_LANES)
            q = q_ref[pl.ds(r0, CHUNK), lanes]
            k = k_ref[pl.ds(r0, CHUNK), lanes]
            v = v_ref[pl.ds(r0, CHUNK), lanes]
            beta = gcc[:, hh:hh + 1]
            g_c = gcc[:, 4 + hh:5 + hh]
            g_r = grr[4 + hh:5 + hh, :]
            g_last = g_c[CHUNK - 1:CHUNK, :]
            decay = jnp.exp(jnp.where(incl, g_c - g_r, NEG_BIG))
            kb = k.astype(BF16)
            kk = _dot_nt(kb, kb)
            lower = jnp.where(strict, beta * kk * decay, 0.0)
            tinv = _unit_lower_inverse(lower, eye)
            eg = jnp.exp(g_c)
            rhs = jnp.concatenate([v * beta, k * (beta * eg)], axis=-1)
            sol = _dot(tinv, rhs)
            w_val = sol[:, 0:DN_DV]
            k_cum = sol[:, DN_DV:DN_DV + DN_DK]
            qk = _dot_nt(q.astype(BF16), kb) * decay
            q_dec = q * eg
            k_dec = k * jnp.exp(g_last - g_c)
            s_dec = jnp.exp(g_last)
            state = s_ref[hh]
            sb = state.astype(BF16)
            v_new = w_val - _dot(k_cum.astype(BF16), sb)
            vb = v_new.astype(BF16)
            o = _dot(q_dec.astype(BF16), sb) + _dot(qk.astype(BF16), vb)
            s_ref[hh] = s_dec * state + _dot_tn(k_dec.astype(BF16), vb)
            on = o * lax.rsqrt(jnp.mean(o * o, axis=-1, keepdims=True) + EPS) * nw
            o_ref[pl.ds(r0, CHUNK), lanes] = on * z_ref[pl.ds(r0, CHUNK), lanes]
        return carry

    lax.fori_loop(0, n_chunks, chunk, 0)


def _deltanet(conv_out, rest, gcol, grow3, nw, batch, seq, rows):
    t = conv_out.shape[0]
    nj = seq // rows
    cpb = rows // CHUNK
    width = DN_HEADS * HEAD_LANES

    def rmap(col):
        return lambda b, j: (b * nj + j, col)

    return pl.pallas_call(
        _deltanet_kernel,
        grid=(batch, nj),
        in_specs=[pl.BlockSpec((rows, width), rmap(C_DNQ // width)),
                  pl.BlockSpec((rows, width), rmap(C_DNK // width)),
                  pl.BlockSpec((rows, width), rmap(C_DNV // width)),
                  pl.BlockSpec((rows, HEAD_LANES), rmap(0)),
                  pl.BlockSpec((cpb, 16, CHUNK), lambda b, j: (b * nj + j, 0, 0)),
                  pl.BlockSpec((rows, width), rmap((C_DNZ - N_CONV) // width)),
                  pl.BlockSpec((1, HEAD_LANES), lambda b, j: (0, 0))],
        out_specs=pl.BlockSpec((rows, width), rmap(0)),
        out_shape=jax.ShapeDtypeStruct((t, width), F32),
        scratch_shapes=[pltpu.VMEM((DN_HEADS, DN_DK, DN_DV), F32)],
        compiler_params=pltpu.CompilerParams(dimension_semantics=("arbitrary", "arbitrary"),
                                             vmem_limit_bytes=V7X_VMEM_LIMIT),
        name="deltanet",
    )(conv_out, conv_out, conv_out, gcol, grow3, rest, nw)


def _mlstm_kernel(q_ref, k_ref, v_ref, gc_ref, gr_ref, og_ref, nw_ref, o_ref, c_ref, n_ref, m_ref):
    n_chunks = gr_ref.shape[0]

    @pl.when(pl.program_id(1) == 0)
    def _():
        c_ref[...] = jnp.zeros(c_ref.shape, F32)
        n_ref[...] = jnp.zeros(n_ref.shape, F32)
        m_ref[...] = jnp.zeros(m_ref.shape, F32)

    incl, _, _ = _chunk_masks()

    def chunk(c, carry):
        r0 = pl.multiple_of(c * CHUNK, CHUNK)
        gcc = gc_ref[pl.ds(r0, CHUNK), :]
        grr = gr_ref[c]
        for hh in range(ML_HEADS):
            lanes = slice(hh * HEAD_LANES, (hh + 1) * HEAD_LANES)
            q = q_ref[pl.ds(r0, CHUNK), lanes]
            k = k_ref[pl.ds(r0, CHUNK), lanes]
            v = v_ref[pl.ds(r0, CHUNK), lanes]
            i_c = gcc[:, 8 + hh:9 + hh]
            b_c = gcc[:, 12 + hh:13 + hh]
            i_r = grr[8 + hh:9 + hh, :]
            b_r = grr[12 + hh:13 + hh, :]
            b_last = b_c[CHUNK - 1:CHUNK, :]
            d_mat = jnp.where(incl, b_c - b_r + i_r, NEG_BIG)
            m_intra = jnp.max(d_mat, axis=-1, keepdims=True)
            g_end = b_last - b_c + i_c
            g_end_max = jnp.max(g_end, axis=0, keepdims=True)
            qb = q.astype(BF16)
            kb = k.astype(BF16)
            vb = v.astype(BF16)
            qk = _dot_nt(qb, kb)
            c_s = c_ref[hh]
            n_s = n_ref[hh][0:1, :]
            m_s = m_ref[hh][0:1, 0:1]
            m_t = jnp.maximum(b_c + m_s, m_intra)
            inter = jnp.exp(b_c + m_s - m_t)
            p = jnp.exp(d_mat - m_t) * qk
            num = inter * _dot(qb, c_s.astype(BF16)) + _dot(p.astype(BF16), vb)
            den = inter * jnp.sum(q * n_s, axis=-1, keepdims=True) + jnp.sum(p, axis=-1, keepdims=True)
            h = num / jnp.maximum(jnp.abs(den), jnp.exp(-m_t))
            m_new = jnp.maximum(b_last + m_s, g_end_max)
            keep = jnp.exp(b_last + m_s - m_new)
            kw = k * jnp.exp(g_end - m_new)
            c_ref[hh] = keep * c_s + _dot_tn(kw.astype(BF16), vb)
            n_new = keep * n_s + jnp.sum(kw, axis=0, keepdims=True)
            n_ref[hh] = jnp.broadcast_to(n_new, (8, HEAD_LANES))
            m_ref[hh] = jnp.broadcast_to(m_new, (8, HEAD_LANES))
            hn = h * lax.rsqrt(jnp.mean(h * h, axis=-1, keepdims=True) + EPS) * nw_ref[:, lanes]
            o_ref[pl.ds(r0, CHUNK), lanes] = hn * og_ref[pl.ds(r0, CHUNK), lanes]
        return carry

    lax.fori_loop(0, n_chunks, chunk, 0)


def _mlstm(conv_out, rest, gcol, grow3, nw, batch, seq, rows):
    t = conv_out.shape[0]
    nj = seq // rows
    cpb = rows // CHUNK
    width = ML_HEADS * HEAD_LANES

    def rmap(col):
        return lambda b, j: (b * nj + j, col)

    return pl.pallas_call(
        _mlstm_kernel,
        grid=(batch, nj),
        in_specs=[pl.BlockSpec((rows, width), rmap(C_MLQ // width)),
                  pl.BlockSpec((rows, width), rmap(C_MLK // width)),
                  pl.BlockSpec((rows, width), rmap((C_MLV - N_CONV) // width)),
                  pl.BlockSpec((rows, HEAD_LANES), rmap(0)),
                  pl.BlockSpec((cpb, 16, CHUNK), lambda b, j: (b * nj + j, 0, 0)),
                  pl.BlockSpec((rows, width), rmap((C_MLO - N_CONV) // width)),
                  pl.BlockSpec((1, width), lambda b, j: (0, 0))],
        out_specs=pl.BlockSpec((rows, width), rmap(0)),
        out_shape=jax.ShapeDtypeStruct((t, width), F32),
        scratch_shapes=[pltpu.VMEM((ML_HEADS, HEAD_LANES, ML_DV), F32),
                        pltpu.VMEM((ML_HEADS, 8, HEAD_LANES), F32),
                        pltpu.VMEM((ML_HEADS, 8, HEAD_LANES), F32)],
        compiler_params=pltpu.CompilerParams(dimension_semantics=("arbitrary", "arbitrary"),
                                             vmem_limit_bytes=V7X_VMEM_LIMIT),
        name="mlstm",
    )(conv_out, conv_out, rest, gcol, grow3, rest, nw)


def _route_kernel(ya_ref, yb_ref, woa_ref, wob_ref, x_ref, g1_ref, sh_ref, sc_ref, nw_ref, wrt_ref, br_ref,
                  x1_ref, h2_ref, idx_ref, wrow_ref, rank_ref, wcol_ref, cnt_ref, cnt_acc):
    tm = x_ref.shape[0]
    i = pl.program_id(0)

    @pl.when(i == 0)
    def _():
        cnt_acc[...] = jnp.zeros(cnt_acc.shape, F32)

    mix = _dot(ya_ref[...].astype(BF16), woa_ref[...]) + _dot(yb_ref[...].astype(BF16), wob_ref[...])
    x1 = x_ref[...] + g1_ref[...] * mix
    x1_ref[...] = x1
    ms = jnp.mean(x1 * x1, axis=-1, keepdims=True)
    h2 = x1 * lax.rsqrt(ms + EPS) * nw_ref[...]
    h2 = h2 * (1.0 + sc_ref[...]) + sh_ref[...]
    h2_ref[...] = h2

    hh, hm, _ = _split3(h2)
    wh, wm, _ = _split3(wrt_ref[...])
    logits = _dot_nt(wh, hh) + (_dot_nt(wh, hm) + _dot_nt(wm, hh)) + br_ref[:, 0:1]

    e_i = lax.broadcasted_iota(jnp.int32, (N_EXPERTS, tm), 0)
    work = logits
    tops = []
    sels = []
    hots = []
    for _ in range(TOP_K):
        m = jnp.max(work, axis=0, keepdims=True)
        sel = jnp.min(jnp.where(work == m, e_i, N_EXPERTS), axis=0, keepdims=True)
        hot = e_i == sel
        work = jnp.where(hot, NEG_BIG, work)
        tops.append(m)
        sels.append(sel)
        hots.append(hot)
    exps = [jnp.exp(tl - tops[0]) for tl in tops]
    denom = exps[0] + exps[1] + exps[2] + exps[3]
    ws = [e / denom for e in exps]

    chosen = jnp.zeros((N_EXPERTS, tm), F32)
    for hot in hots:
        chosen = chosen + jnp.where(hot, 1.0, 0.0)

    r_i = lax.broadcasted_iota(jnp.int32, (tm, tm), 0)
    c_i = lax.broadcasted_iota(jnp.int32, (tm, tm), 1)
    strict_upper = jnp.where(r_i < c_i, 1.0, 0.0).astype(BF16)
    prefix = _dot(chosen.astype(BF16), strict_upper) + cnt_acc[:, 0:1]
    cnt_new = cnt_acc[:, 0:1] + jnp.sum(chosen, axis=1, keepdims=True)
    cnt_acc[...] = jnp.broadcast_to(cnt_new, cnt_acc.shape)
    cnt_ref[...] = jnp.broadcast_to(cnt_new, cnt_ref.shape).astype(jnp.int32)

    for kk in range(TOP_K):
        idx_ref[kk:kk + 1, :] = sels[kk]
        wrow_ref[kk:kk + 1, :] = ws[kk]
        rank = jnp.sum(jnp.where(hots[kk], prefix, 0.0), axis=0, keepdims=True)
        rank_ref[kk:kk + 1, :] = rank.astype(jnp.int32)

    wpad = jnp.concatenate(ws + [jnp.zeros((HEAD_LANES - TOP_K, tm), F32)], axis=0)
    eye = jnp.where(r_i == c_i, 1.0, 0.0).astype(BF16)
    ph, pm, plo = _split3(wpad)
    wcol_ref[...] = _dot_nt(eye, ph) + _dot_nt(eye, pm) + _dot_nt(eye, plo)


def _route(ya, yb, woa, wob, x2, g1, sh2, sc2, nw, wrt, br, seq, tm):
    t, d = x2.shape
    tps = seq // tm
    bmap = lambda i: (i // tps, 0, 0)
    return pl.pallas_call(
        _route_kernel,
        grid=(t // tm,),
        in_specs=[pl.BlockSpec((tm, ya.shape[1]), lambda i: (i, 0)),
                  pl.BlockSpec((tm, yb.shape[1]), lambda i: (i, 0)),
                  pl.BlockSpec(woa.shape, lambda i: (0, 0)),
                  pl.BlockSpec(wob.shape, lambda i: (0, 0)),
                  pl.BlockSpec((tm, d), lambda i: (i, 0)),
                  pl.BlockSpec((None, 1, d), bmap),
                  pl.BlockSpec((None, 1, d), bmap),
                  pl.BlockSpec((None, 1, d), bmap),
                  pl.BlockSpec((1, d), lambda i: (0, 0)),
                  pl.BlockSpec((N_EXPERTS, d), lambda i: (0, 0)),
                  pl.BlockSpec((N_EXPERTS, HEAD_LANES), lambda i: (0, 0))],
        out_specs=[pl.BlockSpec((tm, d), lambda i: (i, 0)),
                   pl.BlockSpec((tm, d), lambda i: (i, 0)),
                   pl.BlockSpec((TOP_K, tm), lambda i: (0, i)),
                   pl.BlockSpec((TOP_K, tm), lambda i: (0, i)),
                   pl.BlockSpec((TOP_K, tm), lambda i: (0, i)),
                   pl.BlockSpec((tm, HEAD_LANES), lambda i: (i, 0)),
                   pl.BlockSpec((N_EXPERTS, HEAD_LANES), lambda i: (0, 0))],
        out_shape=[jax.ShapeDtypeStruct((t, d), F32),
                   jax.ShapeDtypeStruct((t, d), F32),
                   jax.ShapeDtypeStruct((TOP_K, t), jnp.int32),
                   jax.ShapeDtypeStruct((TOP_K, t), F32),
                   jax.ShapeDtypeStruct((TOP_K, t), jnp.int32),
                   jax.ShapeDtypeStruct((t, HEAD_LANES), F32),
                   jax.ShapeDtypeStruct((N_EXPERTS, HEAD_LANES), jnp.int32)],
        scratch_shapes=[pltpu.VMEM((N_EXPERTS, HEAD_LANES), F32)],
        compiler_params=pltpu.CompilerParams(dimension_semantics=("arbitrary",),
                                             vmem_limit_bytes=V7X_VMEM_LIMIT),
        name="route",
    )(ya, yb, woa, wob, x2, g1, sh2, sc2, nw, wrt, br)


def _row_copy(src_hbm, src_row, dst_ref, dst_row, sem):
    return pltpu.make_async_copy(src_hbm.at[pl.ds(src_row, 1)], dst_ref.at[pl.ds(dst_row, 1)], sem)


def _dispatch_kernel(dest_ref, h2_hbm, xs_in_hbm, xs_hbm, sem):
    del xs_in_hbm
    tt = dest_ref.shape[1]
    base = pl.program_id(0) * tt

    def issue(tok, carry):
        for kk in range(TOP_K):
            _row_copy(h2_hbm, base + tok, xs_hbm, dest_ref[kk, tok], sem).start()
        return carry

    lax.fori_loop(0, tt, issue, 0)

    def drain(tok, carry):
        for kk in range(TOP_K):
            _row_copy(h2_hbm, 0, xs_hbm, 0, sem).wait()
        return carry

    lax.fori_loop(0, tt, drain, 0)


def _dispatch(dest, h2, xs_zero, tt):
    t = h2.shape[0]
    return pl.pallas_call(
        _dispatch_kernel,
        grid=(t // tt,),
        in_specs=[pl.BlockSpec((TOP_K, tt), lambda i: (0, i), memory_space=pltpu.SMEM),
                  pl.BlockSpec(memory_space=pl.ANY),
                  pl.BlockSpec(memory_space=pl.ANY)],
        out_specs=pl.BlockSpec(memory_space=pl.ANY),
        out_shape=jax.ShapeDtypeStruct(xs_zero.shape, xs_zero.dtype),
        scratch_shapes=[pltpu.SemaphoreType.DMA(())],
        input_output_aliases={2: 0},
        compiler_params=pltpu.CompilerParams(dimension_semantics=("arbitrary",), has_side_effects=True),
        name="dispatch",
    )(dest, h2, xs_zero)


def _experts_kernel(be_ref, nr_ref, x_ref, wgu_ref, bgu_ref, wd_ref, bd_ref, y_ref, wgu_s, wd_s):
    b = pl.program_id(0)
    d_ff = wd_ref.shape[0]

    @pl.when(b < nr_ref[0])
    def _():
        prev = be_ref[jnp.maximum(b - 1, 0)]

        @pl.when((b == 0) | (be_ref[b] != prev))
        def _():
            wgu_s[...] = wgu_ref[...].astype(BF16)
            wd_s[...] = wd_ref[...].astype(BF16)

        xb = x_ref[...].astype(BF16)
        gu = _dot(xb, wgu_s[...]) + bgu_ref[...]
        gate = jnp.minimum(gu[:, 0:d_ff], SWIGLU_LIMIT)
        up = jnp.clip(gu[:, d_ff:2 * d_ff], -SWIGLU_LIMIT, SWIGLU_LIMIT)
        act = (up + 1.0) * gate * _sigmoid(SWIGLU_ALPHA * gate)
        y_ref[...] = _dot(act.astype(BF16), wd_s[...]) + bd_ref[...]

    @pl.when(b >= nr_ref[0])
    def _():
        y_ref[...] = jnp.zeros(y_ref.shape, F32)


def _experts(block_e, n_real, xs, wgu, bgu, wd, bd, bm):
    ns, d = xs.shape
    n_e, _, two_ff = wgu.shape
    d_ff = two_ff // 2
    nb = ns // bm

    def xmap(b, be, nr):
        return (jnp.minimum(b, nr[0] - 1), 0)

    def wmap(b, be, nr):
        return (be[b], 0, 0)

    grid_spec = pltpu.PrefetchScalarGridSpec(
        num_scalar_prefetch=2,
        grid=(nb,),
        in_specs=[pl.BlockSpec((bm, d), xmap),
                  pl.BlockSpec((None, d, two_ff), wmap),
                  pl.BlockSpec((None, 1, two_ff), wmap),
                  pl.BlockSpec((None, d_ff, d), wmap),
                  pl.BlockSpec((None, 1, d), wmap)],
        out_specs=pl.BlockSpec((bm, d), lambda b, be, nr: (b, 0)),
        scratch_shapes=[pltpu.VMEM((d, two_ff), BF16), pltpu.VMEM((d_ff, d), BF16)],
    )
    return pl.pallas_call(
        _experts_kernel,
        grid_spec=grid_spec,
        out_shape=jax.ShapeDtypeStruct((ns, d), F32),
        compiler_params=pltpu.CompilerParams(dimension_semantics=("arbitrary",),
                                             vmem_limit_bytes=V7X_VMEM_LIMIT),
        name="experts",
    )(block_e, n_real, xs, wgu, bgu, wd, bd)


def _combine_kernel(dest_ref, y_hbm, wcol_ref, x1_ref, g2_ref, nw_ref, sh_ref, sc_ref, o_ref, ybuf, sem):
    tt = dest_ref.shape[1]

    def issue(tok, carry):
        for kk in range(TOP_K):
            _row_copy(y_hbm, dest_ref[kk, tok], ybuf.at[kk], tok, sem).start()
        return carry

    lax.fori_loop(0, tt, issue, 0)

    def drain(tok, carry):
        for kk in range(TOP_K):
            _row_copy(y_hbm, 0, ybuf.at[kk], 0, sem).wait()
        return carry

    lax.fori_loop(0, tt, drain, 0)

    wc = wcol_ref[...]
    acc = wc[:, 0:1] * ybuf[0]
    for kk in range(1, TOP_K):
        acc = acc + wc[:, kk:kk + 1] * ybuf[kk]
    xo = x1_ref[...] + g2_ref[...] * acc
    ms = jnp.mean(xo * xo, axis=-1, keepdims=True)
    hn = xo * lax.rsqrt(ms + EPS) * nw_ref[...]
    o_ref[...] = hn * (1.0 + sc_ref[...]) + sh_ref[...]


def _combine(dest, y, wcol, x1, g2, nw, shf, scf, seq, tt):
    t, d = x1.shape
    tps = seq // tt
    bmap = lambda i: (i // tps, 0, 0)
    return pl.pallas_call(
        _combine_kernel,
        grid=(t // tt,),
        in_specs=[pl.BlockSpec((TOP_K, tt), lambda i: (0, i), memory_space=pltpu.SMEM),
                  pl.BlockSpec(memory_space=pl.ANY),
                  pl.BlockSpec((tt, HEAD_LANES), lambda i: (i, 0)),
                  pl.BlockSpec((tt, d), lambda i: (i, 0)),
                  pl.BlockSpec((None, 1, d), bmap),
                  pl.BlockSpec((1, d), lambda i: (0, 0)),
                  pl.BlockSpec((None, 1, d), bmap),
                  pl.BlockSpec((None, 1, d), bmap)],
        out_specs=pl.BlockSpec((tt, d), lambda i: (i, 0)),
        out_shape=jax.ShapeDtypeStruct((t, d), F32),
        scratch_shapes=[pltpu.VMEM((TOP_K, tt, d), F32), pltpu.SemaphoreType.DMA(())],
        compiler_params=pltpu.CompilerParams(dimension_semantics=("arbitrary",),
                                             vmem_limit_bytes=V7X_VMEM_LIMIT),
        name="combine",
    )(dest, y, wcol, x1, g2, nw, shf, scf)


def _pad_heads(w, heads, dk):
    r = w.shape[0]
    w3 = w.reshape(r, heads, dk)
    return jnp.pad(w3, ((0, 0), (0, 0), (0, HEAD_LANES - dk))).reshape(r, heads * HEAD_LANES)


def _pick_tile(n, pref):
    tile = pref
    while n % tile:
        tile //= 2
    return tile


def kernel(x, c, w_ada, b_ada, norm_mix, w_in, dn_conv, dn_a_log, dn_dt_bias, dn_norm, ml_conv, ml_i_bias,
           ml_f_bias, ml_norm, w_out, norm_ffn, w_router, b_router, w_gate_up, b_gate_up, w_down, b_down,
           w_ada_final, b_ada_final, norm_final):
    batch, seq, d = x.shape
    assert w_ada.shape[0] == 1, "single-layer block"
    assert seq % CHUNK == 0
    t = batch * seq
    x2 = x.reshape(t, d)

    c_pad = jnp.pad(c, ((0, 8 - batch % 8 if batch % 8 else 0), (0, 0)))
    mod = _mods(c_pad, w_ada.reshape(d, 6 * d), b_ada.reshape(1, 6 * d))[:batch]
    modf = _mods(c_pad, w_ada_final, b_ada_final.reshape(1, 2 * d))[:batch]
    sh1, sc1, g1, sh2, sc2, g2 = [mod[:, None, j * d:(j + 1) * d] for j in range(6)]
    shf, scf = modf[:, None, 0:d], modf[:, None, d:2 * d]

    wi = w_in.reshape(d, -1)
    o_z = 1536
    o_b = 2048
    o_mq = 2056
    o_mk = o_mq + ML_HEADS * ML_DK
    o_mv = o_mk + ML_HEADS * ML_DK
    o_mo = o_mv + ML_HEADS * ML_DV
    o_mi = o_mo + ML_HEADS * ML_DV
    gates = jnp.concatenate([wi[:, o_b:o_mq], wi[:, o_mi:o_mi + 2 * ML_HEADS]], axis=1)
    w_new = jnp.concatenate([
        wi[:, 0:o_z],
        _pad_heads(wi[:, o_mq:o_mk], ML_HEADS, ML_DK),
        _pad_heads(wi[:, o_mk:o_mv], ML_HEADS, ML_DK),
        wi[:, o_z:o_b],
        wi[:, o_mv:o_mo],
        wi[:, o_mo:o_mi],
        jnp.pad(gates, ((0, 0), (0, HEAD_LANES - 16))),
    ], axis=1).astype(BF16)
    wgt = gates.T.astype(BF16)
    mlc = ml_conv.reshape(CONV_W, -1)
    cw = jnp.concatenate([dn_conv.reshape(CONV_W, -1),
                          _pad_heads(mlc[:, 0:ML_HEADS * ML_DK], ML_HEADS, ML_DK),
                          _pad_heads(mlc[:, ML_HEADS * ML_DK:], ML_HEADS, ML_DK)], axis=1)
    zeros4 = jnp.zeros((4,), F32)
    bias16 = jnp.concatenate([zeros4, dn_dt_bias.reshape(4), ml_i_bias.reshape(4), ml_f_bias.reshape(4)])
    alog16 = jnp.concatenate([zeros4, dn_a_log.reshape(4), zeros4, zeros4])
    gpc = jnp.zeros((8, HEAD_LANES), F32).at[0, 0:16].set(bias16).at[1, 0:16].set(alog16)
    gpr = jnp.zeros((16, HEAD_LANES), F32).at[:, 0].set(bias16).at[:, 1].set(alog16)

    tm_in = _pick_tile(seq, 256)
    conv_out, rest, gcol, grow = _inproj(x2, sh1, sc1, norm_mix.reshape(1, d), w_new, wgt, cw, gpc, gpr, seq, tm_in)
    grow3 = grow.reshape(16, t // CHUNK, CHUNK).transpose(1, 0, 2)

    rows = _pick_tile(seq, 512)
    ya = _deltanet(conv_out, rest, gcol, grow3, dn_norm.reshape(1, DN_DV), batch, seq, rows)
    yb = _mlstm(conv_out, rest, gcol, grow3, ml_norm.reshape(1, ML_HEADS * ML_DV), batch, seq, rows)

    wo = w_out.reshape(-1, d).astype(BF16)
    n_a = DN_HEADS * DN_DV
    tm_r = _pick_tile(seq, 512)
    brp = jnp.broadcast_to(b_router.reshape(N_EXPERTS, 1), (N_EXPERTS, HEAD_LANES))
    x1, h2, idx_t, _, rank_t, wcol, cnt = _route(
        ya, yb, wo[0:n_a], wo[n_a:], x2, g1, sh2, sc2, norm_ffn.reshape(1, d),
        w_router.reshape(d, N_EXPERTS).T, brp, seq, tm_r)

    bm = 256
    counts = cnt[:, 0]
    padded = (counts + bm - 1) // bm * bm
    pad_end = jnp.cumsum(padded)
    pad_start = pad_end - padded
    dest = (pad_start[idx_t] + rank_t).astype(jnp.int32)
    nb = (t * TOP_K) // bm + N_EXPERTS
    block_e = jnp.minimum(jnp.searchsorted(pad_end, jnp.arange(nb, dtype=pad_end.dtype) * bm, side='right'),
                          N_EXPERTS - 1).astype(jnp.int32)
    n_real = (pad_end[-1:] // bm).astype(jnp.int32)

    tt = _pick_tile(seq, 256)
    xs = _dispatch(dest, h2, jnp.zeros((nb * bm, d), F32), tt)
    n_e = N_EXPERTS
    y = _experts(block_e, n_real, xs, w_gate_up.reshape(n_e, d, -1), b_gate_up.reshape(n_e, 1, -1),
                 w_down.reshape(n_e, -1, d), b_down.reshape(n_e, 1, d), bm)
    out = _combine(dest, y, wcol, x1, g2, norm_final.reshape(1, d), shf, scf, seq, tt)
    return out.reshape(batch, seq, d)
```

```python
import functools

import jax
import jax.numpy as jnp
from jax import lax
from jax.experimental import pallas as pl
from jax.experimental.pallas import tpu as pltpu

F32 = jnp.float32
BF16 = jnp.bfloat16

CHUNK = 64
CONV_W = 4
EPS = 1e-6

DN_HEADS = 4
DN_DK = 128
DN_DV = 128
ML_HEADS = 4
ML_DK = 64
ML_DV = 128
HEAD_LANES = 128

N_EXPERTS = 32
TOP_K = 4
SWIGLU_LIMIT = 7.0
SWIGLU_ALPHA = 1.702

C_DNQ = 0
C_DNK = 512
C_DNV = 1024
C_MLQ = 1536
C_MLK = 2048
N_CONV = 2560
C_DNZ = 2560
C_MLV = 3072
C_MLO = 3584
C_GATE = 4096
N_PROJ = 4224
N_REST = C_GATE - N_CONV

V7X_VMEM_LIMIT = 56 * 1024 * 1024

NEG_BIG = -1e30


def _sigmoid(x):
    return 1.0 / (1.0 + jnp.exp(-x))


def _softplus(x):
    return jnp.maximum(x, 0.0) + jnp.log(1.0 + jnp.exp(-jnp.abs(x)))


def _split3(v):
    hi = v.astype(BF16)
    r1 = v - hi.astype(F32)
    mid = r1.astype(BF16)
    lo = (r1 - mid.astype(F32)).astype(BF16)
    return hi, mid, lo


def _dot(a, b):
    return jnp.dot(a, b, preferred_element_type=F32)


def _dot_nt(a, b):
    return lax.dot_general(a, b, (((1,), (1,)), ((), ())), preferred_element_type=F32)


def _dot_tn(a, b):
    return lax.dot_general(a, b, (((0,), (0,)), ((), ())), preferred_element_type=F32)


def _dot_exact_right(sel_bf16, v):
    hi, mid, lo = _split3(v)
    return _dot(sel_bf16, hi) + _dot(sel_bf16, mid) + _dot(sel_bf16, lo)


def _dot_exact_left(v, sel_bf16):
    hi, mid, lo = _split3(v)
    return _dot(hi, sel_bf16) + _dot(mid, sel_bf16) + _dot(lo, sel_bf16)


def _mods_kernel(c_ref, w_ref, b_ref, o_ref):
    c = c_ref[...]
    cond = c * _sigmoid(c)
    ch, cm, cl = _split3(cond)
    wh, wm, wl = _split3(w_ref[...])
    acc = _dot(ch, wh) + (_dot(ch, wm) + _dot(cm, wh)) + (_dot(ch, wl) + _dot(cm, wm) + _dot(cl, wh))
    o_ref[...] = acc + b_ref[...]


def _mods(c_pad, w, b):
    m, d = c_pad.shape
    n = w.shape[1]
    tn = 1024
    return pl.pallas_call(
        _mods_kernel,
        grid=(n // tn,),
        in_specs=[pl.BlockSpec((m, d), lambda j: (0, 0)),
                  pl.BlockSpec((d, tn), lambda j: (0, j)),
                  pl.BlockSpec((1, tn), lambda j: (0, j))],
        out_specs=pl.BlockSpec((m, tn), lambda j: (0, j)),
        out_shape=jax.ShapeDtypeStruct((m, n), F32),
        compiler_params=pltpu.CompilerParams(dimension_semantics=("arbitrary",),
                                             vmem_limit_bytes=V7X_VMEM_LIMIT),
        name="mods",
    )(c_pad, w, b)


def _gate_transform(v, bias, alog, cls):
    vb = v + bias
    beta = _sigmoid(v)
    g = -jnp.exp(alog) * _softplus(vb)
    logf = -_softplus(-vb)
    return jnp.where(cls == 0, beta, jnp.where(cls == 1, g, jnp.where(cls == 2, vb, jnp.where(cls == 3, logf, 0.0))))


def _inproj_kernel(tiles_per_seq, x_ref, sh_ref, sc_ref, nw_ref, w_ref, wgt_ref, cw_ref, gpc_ref, gpr_ref,
                   conv_ref, rest_ref, gcol_ref, grow_ref, cbuf):
    tm = x_ref.shape[0]
    i = pl.program_id(0)
    x = x_ref[...]
    ms = jnp.mean(x * x, axis=-1, keepdims=True)
    h = x * lax.rsqrt(ms + EPS) * nw_ref[...]
    h = h * (1.0 + sc_ref[...]) + sh_ref[...]
    hb = h.astype(BF16)

    pc = _dot(hb, w_ref[:, 0:N_CONV])

    @pl.when(i % tiles_per_seq == 0)
    def _():
        cbuf[0:8, :] = jnp.zeros((8, N_CONV), F32)

    cbuf[8:tm + 8, :] = pc
    acc = cw_ref[CONV_W - 1:CONV_W, :] * pc
    for j in range(CONV_W - 1):
        acc = acc + cw_ref[j:j + 1, :] * cbuf[8 - (CONV_W - 1) + j:8 - (CONV_W - 1) + j + tm, :]
    cbuf[0:8, :] = cbuf[tm:tm + 8, :]
    y = acc * _sigmoid(acc)

    for hh in range(DN_HEADS):
        lo = C_DNQ + hh * HEAD_LANES
        qh = y[:, lo:lo + HEAD_LANES]
        conv_ref[:, lo:lo + HEAD_LANES] = (qh * lax.rsqrt(jnp.sum(qh * qh, axis=-1, keepdims=True) + EPS)) * (DN_DK ** -0.5)
        lo = C_DNK + hh * HEAD_LANES
        kh = y[:, lo:lo + HEAD_LANES]
        conv_ref[:, lo:lo + HEAD_LANES] = kh * lax.rsqrt(jnp.sum(kh * kh, axis=-1, keepdims=True) + EPS)
    conv_ref[:, C_DNV:C_MLQ] = y[:, C_DNV:C_MLQ]
    conv_ref[:, C_MLQ:C_MLK] = y[:, C_MLQ:C_MLK] * (ML_DK ** -0.5)
    conv_ref[:, C_MLK:N_CONV] = y[:, C_MLK:N_CONV]

    pr = _dot(hb, w_ref[:, N_CONV:C_GATE])
    z = pr[:, 0:512]
    rest_ref[:, 0:512] = z * _sigmoid(z)
    rest_ref[:, 512:1024] = pr[:, 512:1024]
    rest_ref[:, 1024:1536] = _sigmoid(pr[:, 1024:1536])

    r_i = lax.broadcasted_iota(jnp.int32, (tm, tm), 0)
    c_i = lax.broadcasted_iota(jnp.int32, (tm, tm), 1)
    same_chunk = (r_i // CHUNK) == (c_i // CHUNK)
    tril = jnp.where(same_chunk & (c_i <= r_i), 1.0, 0.0).astype(BF16)
    triu = jnp.where(same_chunk & (r_i <= c_i), 1.0, 0.0).astype(BF16)

    gc = _dot(hb, w_ref[:, C_GATE:N_PROJ])
    cls_c = lax.broadcasted_iota(jnp.int32, (tm, HEAD_LANES), 1) // 4
    gt = _gate_transform(gc, gpc_ref[0:1, :], gpc_ref[1:2, :], cls_c)
    cs = _dot_exact_right(tril, gt)
    gcol_ref[...] = jnp.where((cls_c == 1) | (cls_c == 3), cs, gt)

    gr = _dot_nt(wgt_ref[...], hb)
    cls_r = lax.broadcasted_iota(jnp.int32, (16, tm), 0) // 4
    gtr = _gate_transform(gr, gpr_ref[:, 0:1], gpr_ref[:, 1:2], cls_r)
    csr = _dot_exact_left(gtr, triu)
    grow_ref[...] = jnp.where((cls_r == 1) | (cls_r == 3), csr, gtr)


def _inproj(x2, sh, sc, nw, w_new, wgt, cw, gpc, gpr, seq, tm):
    t, d = x2.shape
    tps = seq // tm
    kern = functools.partial(_inproj_kernel, tps)
    return pl.pallas_call(
        kern,
        grid=(t // tm,),
        in_specs=[pl.BlockSpec((tm, d), lambda i: (i, 0)),
                  pl.BlockSpec((None, 1, d), lambda i: (i // tps, 0, 0)),
                  pl.BlockSpec((None, 1, d), lambda i: (i // tps, 0, 0)),
                  pl.BlockSpec((1, d), lambda i: (0, 0)),
                  pl.BlockSpec((d, N_PROJ), lambda i: (0, 0)),
                  pl.BlockSpec((16, d), lambda i: (0, 0)),
                  pl.BlockSpec((CONV_W, N_CONV), lambda i: (0, 0)),
                  pl.BlockSpec((8, HEAD_LANES), lambda i: (0, 0)),
                  pl.BlockSpec((16, HEAD_LANES), lambda i: (0, 0))],
        out_specs=[pl.BlockSpec((tm, N_CONV), lambda i: (i, 0)),
                   pl.BlockSpec((tm, N_REST), lambda i: (i, 0)),
                   pl.BlockSpec((tm, HEAD_LANES), lambda i: (i, 0)),
                   pl.BlockSpec((16, tm), lambda i: (0, i))],
        out_shape=[jax.ShapeDtypeStruct((t, N_CONV), F32),
                   jax.ShapeDtypeStruct((t, N_REST), F32),
                   jax.ShapeDtypeStruct((t, HEAD_LANES), F32),
                   jax.ShapeDtypeStruct((16, t), F32)],
        scratch_shapes=[pltpu.VMEM((tm + 8, N_CONV), F32)],
        compiler_params=pltpu.CompilerParams(dimension_semantics=("arbitrary",),
                                             vmem_limit_bytes=V7X_VMEM_LIMIT),
        name="inproj",
    )(x2, sh, sc, nw, w_new, wgt, cw, gpc, gpr)


def _chunk_masks():
    r = lax.broadcasted_iota(jnp.int32, (CHUNK, CHUNK), 0)
    c = lax.broadcasted_iota(jnp.int32, (CHUNK, CHUNK), 1)
    return r >= c, r > c, r == c


def _bdot(a, b):
    return lax.dot_general(a, b, (((2,), (1,)), ((0,), (0,))), preferred_element_type=F32)


def _bdot_nt(a, b):
    return lax.dot_general(a, b, (((2,), (2,)), ((0,), (0,))), preferred_element_type=F32)


def _unit_lower_inverse(lower, row, col):
    x = jnp.where(row == col, 1.0, 0.0) - jnp.where((row >> 1) == (col >> 1), lower, 0.0)
    shift = 1
    while (1 << shift) < CHUNK:
        couple = ((row >> (shift + 1)) == (col >> (shift + 1))) & ((row >> shift) != (col >> shift))
        cb = jnp.where(couple, lower, 0.0).astype(BF16)
        xb = x.astype(BF16)
        x = x - _bdot(_bdot(xb, cb).astype(BF16), xb)
        shift += 1
    return x


def _deltanet_kernel(q_ref, k_ref, v_ref, gc_ref, gr_ref, z_ref, nw_ref, o_ref, s_ref):
    nc = gr_ref.shape[0]

    @pl.when(pl.program_id(1) == 0)
    def _():
        s_ref[...] = jnp.zeros(s_ref.shape, F32)

    row = lax.broadcasted_iota(jnp.int32, (CHUNK, CHUNK), 0)
    col = lax.broadcasted_iota(jnp.int32, (CHUNK, CHUNK), 1)
    incl = row >= col
    strict = row > col
    nw = nw_ref[...]
    gcc = gc_ref[...]
    grr = gr_ref[...]

    for hh in range(DN_HEADS):
        lanes = slice(hh * HEAD_LANES, (hh + 1) * HEAD_LANES)
        q = q_ref[:, lanes].reshape(nc, CHUNK, HEAD_LANES)
        k = k_ref[:, lanes].reshape(nc, CHUNK, HEAD_LANES)
        v = v_ref[:, lanes].reshape(nc, CHUNK, HEAD_LANES)
        beta = gcc[:, hh:hh + 1].reshape(nc, CHUNK, 1)
        g_c = gcc[:, 4 + hh:5 + hh].reshape(nc, CHUNK, 1)
        g_r = grr[:, 4 + hh:5 + hh, :]
        g_last = g_c[:, CHUNK - 1:CHUNK, :]
        decay = jnp.exp(jnp.where(incl, g_c - g_r, NEG_BIG))
        kb = k.astype(BF16)
        kk = _bdot_nt(kb, kb)
        lower = jnp.where(strict, beta * kk * decay, 0.0)
        tinv = _unit_lower_inverse(lower, row, col)
        eg = jnp.exp(g_c)
        rhs = jnp.concatenate([v * beta, k * (beta * eg)], axis=-1)
        sol = _bdot(tinv.astype(BF16), rhs.astype(BF16))
        w_val = sol[:, :, 0:DN_DV]
        kq = jnp.concatenate([sol[:, :, DN_DV:DN_DV + DN_DK], q * eg], axis=1).astype(BF16)
        qk = (_bdot_nt(q.astype(BF16), kb) * decay).astype(BF16)
        k_dec = (k * jnp.exp(g_last - g_c)).astype(BF16)
        s_dec = jnp.exp(g_last)

        state = s_ref[hh]
        for c in range(nc):
            both = _dot(kq[c], state.astype(BF16))
            v_new = w_val[c] - both[0:CHUNK]
            vb = v_new.astype(BF16)
            o = both[CHUNK:2 * CHUNK] + _dot(qk[c], vb)
            state = s_dec[c] * state + _dot_tn(k_dec[c], vb)
            on = o * lax.rsqrt(jnp.mean(o * o, axis=-1, keepdims=True) + EPS) * nw
            o_ref[c * CHUNK:(c + 1) * CHUNK, lanes] = on * z_ref[c * CHUNK:(c + 1) * CHUNK, lanes]
        s_ref[hh] = state


def _deltanet(conv_out, rest, gcol, grow3, nw, batch, seq, rows):
    t = conv_out.shape[0]
    nj = seq // rows
    cpb = rows // CHUNK
    width = DN_HEADS * HEAD_LANES

    def rmap(col):
        return lambda b, j: (b * nj + j, col)

    return pl.pallas_call(
        _deltanet_kernel,
        grid=(batch, nj),
        in_specs=[pl.BlockSpec((rows, width), rmap(C_DNQ // width)),
                  pl.BlockSpec((rows, width), rmap(C_DNK // width)),
                  pl.BlockSpec((rows, width), rmap(C_DNV // width)),
                  pl.BlockSpec((rows, HEAD_LANES), rmap(0)),
                  pl.BlockSpec((cpb, 16, CHUNK), lambda b, j: (b * nj + j, 0, 0)),
                  pl.BlockSpec((rows, width), rmap((C_DNZ - N_CONV) // width)),
                  pl.BlockSpec((1, HEAD_LANES), lambda b, j: (0, 0))],
        out_specs=pl.BlockSpec((rows, width), rmap(0)),
        out_shape=jax.ShapeDtypeStruct((t, width), F32),
        scratch_shapes=[pltpu.VMEM((DN_HEADS, DN_DK, DN_DV), F32)],
        compiler_params=pltpu.CompilerParams(dimension_semantics=("arbitrary", "arbitrary"),
                                             vmem_limit_bytes=V7X_VMEM_LIMIT),
        name="deltanet",
    )(conv_out, conv_out, conv_out, gcol, grow3, rest, nw)


def _mlstm_kernel(q_ref, k_ref, v_ref, gc_ref, gr_ref, og_ref, nw_ref, o_ref, c_ref, n_ref, m_ref):
    n_chunks = gr_ref.shape[0]

    @pl.when(pl.program_id(1) == 0)
    def _():
        c_ref[...] = jnp.zeros(c_ref.shape, F32)
        n_ref[...] = jnp.zeros(n_ref.shape, F32)
        m_ref[...] = jnp.zeros(m_ref.shape, F32)

    incl, _, _ = _chunk_masks()

    def chunk(c, carry):
        r0 = pl.multiple_of(c * CHUNK, CHUNK)
        gcc = gc_ref[pl.ds(r0, CHUNK), :]
        grr = gr_ref[c]
        for hh in range(ML_HEADS):
            lanes = slice(hh * HEAD_LANES, (hh + 1) * HEAD_LANES)
            q = q_ref[pl.ds(r0, CHUNK), lanes]
            k = k_ref[pl.ds(r0, CHUNK), lanes]
            v = v_ref[pl.ds(r0, CHUNK), lanes]
            i_c = gcc[:, 8 + hh:9 + hh]
            b_c = gcc[:, 12 + hh:13 + hh]
            i_r = grr[8 + hh:9 + hh, :]
            b_r = grr[12 + hh:13 + hh, :]
            b_last = b_c[CHUNK - 1:CHUNK, :]
            d_mat = jnp.where(incl, b_c - b_r + i_r, NEG_BIG)
            m_intra = jnp.max(d_mat, axis=-1, keepdims=True)
            g_end = b_last - b_c + i_c
            g_end_max = jnp.max(g_end, axis=0, keepdims=True)
            qb = q.astype(BF16)
            kb = k.astype(BF16)
            vb = v.astype(BF16)
            qk = _dot_nt(qb, kb)
            c_s = c_ref[hh]
            n_s = n_ref[hh][0:1, :]
            m_s = m_ref[hh][0:1, 0:1]
            m_t = jnp.maximum(b_c + m_s, m_intra)
            inter = jnp.exp(b_c + m_s - m_t)
            p = jnp.exp(d_mat - m_t) * qk
            num = inter * _dot(qb, c_s.astype(BF16)) + _dot(p.astype(BF16), vb)
            den = inter * jnp.sum(q * n_s, axis=-1, keepdims=True) + jnp.sum(p, axis=-1, keepdims=True)
            h = num / jnp.maximum(jnp.abs(den), jnp.exp(-m_t))
            m_new = jnp.maximum(b_last + m_s, g_end_max)
            keep = jnp.exp(b_last + m_s - m_new)
            kw = k * jnp.exp(g_end - m_new)
            c_ref[hh] = keep * c_s + _dot_tn(kw.astype(BF16), vb)
            n_new = keep * n_s + jnp.sum(kw, axis=0, keepdims=True)
            n_ref[hh] = jnp.broadcast_to(n_new, (8, HEAD_LANES))
            m_ref[hh] = jnp.broadcast_to(m_new, (8, HEAD_LANES))
            hn = h * lax.rsqrt(jnp.mean(h * h, axis=-1, keepdims=True) + EPS) * nw_ref[:, lanes]
            o_ref[pl.ds(r0, CHUNK), lanes] = hn * og_ref[pl.ds(r0, CHUNK), lanes]
        return carry

    lax.fori_loop(0, n_chunks, chunk, 0)


def _mlstm(conv_out, rest, gcol, grow3, nw, batch, seq, rows):
    t = conv_out.shape[0]
    nj = seq // rows
    cpb = rows // CHUNK
    width = ML_HEADS * HEAD_LANES

    def rmap(col):
        return lambda b, j: (b * nj + j, col)

    return pl.pallas_call(
        _mlstm_kernel,
        grid=(batch, nj),
        in_specs=[pl.BlockSpec((rows, width), rmap(C_MLQ // width)),
                  pl.BlockSpec((rows, width), rmap(C_MLK // width)),
                  pl.BlockSpec((rows, width), rmap((C_MLV - N_CONV) // width)),
                  pl.BlockSpec((rows, HEAD_LANES), rmap(0)),
                  pl.BlockSpec((cpb, 16, CHUNK), lambda b, j: (b * nj + j, 0, 0)),
                  pl.BlockSpec((rows, width), rmap((C_MLO - N_CONV) // width)),
                  pl.BlockSpec((1, width), lambda b, j: (0, 0))],
        out_specs=pl.BlockSpec((rows, width), rmap(0)),
        out_shape=jax.ShapeDtypeStruct((t, width), F32),
        scratch_shapes=[pltpu.VMEM((ML_HEADS, HEAD_LANES, ML_DV), F32),
                        pltpu.VMEM((ML_HEADS, 8, HEAD_LANES), F32),
                        pltpu.VMEM((ML_HEADS, 8, HEAD_LANES), F32)],
        compiler_params=pltpu.CompilerParams(dimension_semantics=("arbitrary", "arbitrary"),
                                             vmem_limit_bytes=V7X_VMEM_LIMIT),
        name="mlstm",
    )(conv_out, conv_out, rest, gcol, grow3, rest, nw)


def _route_kernel(ya_ref, yb_ref, woa_ref, wob_ref, x_ref, g1_ref, sh_ref, sc_ref, nw_ref, wrt_ref, br_ref,
                  x1_ref, h2_ref, idx_ref, wrow_ref, rank_ref, wcol_ref, cnt_ref, cnt_acc):
    tm = x_ref.shape[0]
    i = pl.program_id(0)

    @pl.when(i == 0)
    def _():
        cnt_acc[...] = jnp.zeros(cnt_acc.shape, F32)

    mix = _dot(ya_ref[...].astype(BF16), woa_ref[...]) + _dot(yb_ref[...].astype(BF16), wob_ref[...])
    x1 = x_ref[...] + g1_ref[...] * mix
    x1_ref[...] = x1
    ms = jnp.mean(x1 * x1, axis=-1, keepdims=True)
    h2 = x1 * lax.rsqrt(ms + EPS) * nw_ref[...]
    h2 = h2 * (1.0 + sc_ref[...]) + sh_ref[...]
    h2_ref[...] = h2

    hh, hm, _ = _split3(h2)
    wh, wm, _ = _split3(wrt_ref[...])
    logits = _dot_nt(wh, hh) + (_dot_nt(wh, hm) + _dot_nt(wm, hh)) + br_ref[:, 0:1]

    e_i = lax.broadcasted_iota(jnp.int32, (N_EXPERTS, tm), 0)
    work = logits
    tops = []
    sels = []
    hots = []
    for _ in range(TOP_K):
        m = jnp.max(work, axis=0, keepdims=True)
        sel = jnp.min(jnp.where(work == m, e_i, N_EXPERTS), axis=0, keepdims=True)
        hot = e_i == sel
        work = jnp.where(hot, NEG_BIG, work)
        tops.append(m)
        sels.append(sel)
        hots.append(hot)
    exps = [jnp.exp(tl - tops[0]) for tl in tops]
    denom = exps[0] + exps[1] + exps[2] + exps[3]
    ws = [e / denom for e in exps]

    chosen = jnp.zeros((N_EXPERTS, tm), F32)
    for hot in hots:
        chosen = chosen + jnp.where(hot, 1.0, 0.0)

    r_i = lax.broadcasted_iota(jnp.int32, (tm, tm), 0)
    c_i = lax.broadcasted_iota(jnp.int32, (tm, tm), 1)
    strict_upper = jnp.where(r_i < c_i, 1.0, 0.0).astype(BF16)
    prefix = _dot(chosen.astype(BF16), strict_upper) + cnt_acc[:, 0:1]
    cnt_new = cnt_acc[:, 0:1] + jnp.sum(chosen, axis=1, keepdims=True)
    cnt_acc[...] = jnp.broadcast_to(cnt_new, cnt_acc.shape)
    cnt_ref[...] = jnp.broadcast_to(cnt_new, cnt_ref.shape).astype(jnp.int32)

    for kk in range(TOP_K):
        idx_ref[kk:kk + 1, :] = sels[kk]
        wrow_ref[kk:kk + 1, :] = ws[kk]
        rank = jnp.sum(jnp.where(hots[kk], prefix, 0.0), axis=0, keepdims=True)
        rank_ref[kk:kk + 1, :] = rank.astype(jnp.int32)

    wpad = jnp.concatenate(ws + [jnp.zeros((HEAD_LANES - TOP_K, tm), F32)], axis=0)
    eye = jnp.where(r_i == c_i, 1.0, 0.0).astype(BF16)
    ph, pm, plo = _split3(wpad)
    wcol_ref[...] = _dot_nt(eye, ph) + _dot_nt(eye, pm) + _dot_nt(eye, plo)


def _route(ya, yb, woa, wob, x2, g1, sh2, sc2, nw, wrt, br, seq, tm):
    t, d = x2.shape
    tps = seq // tm
    bmap = lambda i: (i // tps, 0, 0)
    return pl.pallas_call(
        _route_kernel,
        grid=(t // tm,),
        in_specs=[pl.BlockSpec((tm, ya.shape[1]), lambda i: (i, 0)),
                  pl.BlockSpec((tm, yb.shape[1]), lambda i: (i, 0)),
                  pl.BlockSpec(woa.shape, lambda i: (0, 0)),
                  pl.BlockSpec(wob.shape, lambda i: (0, 0)),
                  pl.BlockSpec((tm, d), lambda i: (i, 0)),
                  pl.BlockSpec((None, 1, d), bmap),
                  pl.BlockSpec((None, 1, d), bmap),
                  pl.BlockSpec((None, 1, d), bmap),
                  pl.BlockSpec((1, d), lambda i: (0, 0)),
                  pl.BlockSpec((N_EXPERTS, d), lambda i: (0, 0)),
                  pl.BlockSpec((N_EXPERTS, HEAD_LANES), lambda i: (0, 0))],
        out_specs=[pl.BlockSpec((tm, d), lambda i: (i, 0)),
                   pl.BlockSpec((tm, d), lambda i: (i, 0)),
                   pl.BlockSpec((TOP_K, tm), lambda i: (0, i)),
                   pl.BlockSpec((TOP_K, tm), lambda i: (0, i)),
                   pl.BlockSpec((TOP_K, tm), lambda i: (0, i)),
                   pl.BlockSpec((tm, HEAD_LANES), lambda i: (i, 0)),
                   pl.BlockSpec((N_EXPERTS, HEAD_LANES), lambda i: (0, 0))],
        out_shape=[jax.ShapeDtypeStruct((t, d), F32),
                   jax.ShapeDtypeStruct((t, d), F32),
                   jax.ShapeDtypeStruct((TOP_K, t), jnp.int32),
                   jax.ShapeDtypeStruct((TOP_K, t), F32),
                   jax.ShapeDtypeStruct((TOP_K, t), jnp.int32),
                   jax.ShapeDtypeStruct((t, HEAD_LANES), F32),
                   jax.ShapeDtypeStruct((N_EXPERTS, HEAD_LANES), jnp.int32)],
        scratch_shapes=[pltpu.VMEM((N_EXPERTS, HEAD_LANES), F32)],
        compiler_params=pltpu.CompilerParams(dimension_semantics=("arbitrary",),
                                             vmem_limit_bytes=V7X_VMEM_LIMIT),
        name="route",
    )(ya, yb, woa, wob, x2, g1, sh2, sc2, nw, wrt, br)


def _slots_kernel(bm, idx_ref, rank_ref, cnt_ref, dest_ref, be_ref, nr_ref):
    n_e = cnt_ref.shape[0]
    cnt = cnt_ref[:, 0:1].astype(F32)
    padded = jnp.ceil(cnt * (1.0 / bm)) * bm
    r_i = lax.broadcasted_iota(jnp.int32, (n_e, n_e), 0)
    c_i = lax.broadcasted_iota(jnp.int32, (n_e, n_e), 1)
    tril = jnp.where(c_i <= r_i, 1.0, 0.0).astype(BF16)
    pad_end = _dot_exact_right(tril, jnp.broadcast_to(padded, (n_e, HEAD_LANES)))[:, 0:1]
    pad_start = pad_end - padded
    idx = idx_ref[...]
    dest = rank_ref[...].astype(F32)
    for e in range(n_e):
        dest = dest + jnp.where(idx == e, pad_start[e:e + 1, 0:1], 0.0)
    dest_ref[...] = dest.astype(jnp.int32)
    b_lo = lax.broadcasted_iota(jnp.int32, (n_e, be_ref.shape[1]), 1).astype(F32) * bm
    be = jnp.sum(jnp.where(pad_end <= b_lo, 1.0, 0.0), axis=0, keepdims=True)
    be_ref[...] = jnp.minimum(be, n_e - 1.0).astype(jnp.int32)
    nr_ref[...] = jnp.broadcast_to(pad_end[n_e - 1:n_e, 0:1] * (1.0 / bm), nr_ref.shape).astype(jnp.int32)


def _slots(idx_t, rank_t, cnt, bm, nb):
    assert bm & (bm - 1) == 0, "block rows must be a power of two"
    nbp = (nb + HEAD_LANES - 1) // HEAD_LANES * HEAD_LANES
    return pl.pallas_call(
        functools.partial(_slots_kernel, bm),
        out_shape=[jax.ShapeDtypeStruct(idx_t.shape, jnp.int32),
                   jax.ShapeDtypeStruct((1, nbp), jnp.int32),
                   jax.ShapeDtypeStruct((1, HEAD_LANES), jnp.int32)],
        compiler_params=pltpu.CompilerParams(vmem_limit_bytes=V7X_VMEM_LIMIT),
        name="slots",
    )(idx_t, rank_t, cnt)


def _row_copy(src_hbm, src_row, dst_ref, dst_row, sem):
    return pltpu.make_async_copy(src_hbm.at[pl.ds(src_row, 1)], dst_ref.at[pl.ds(dst_row, 1)], sem)


def _dispatch_kernel(dest_ref, h2_ref, xs_in_hbm, xs_hbm, sem):
    del xs_in_hbm
    tt = dest_ref.shape[1]

    def issue(tok, carry):
        for kk in range(TOP_K):
            _row_copy(h2_ref, tok, xs_hbm, dest_ref[kk, tok], sem).start()
        return carry

    lax.fori_loop(0, tt, issue, 0)

    def drain(tok, carry):
        for kk in range(TOP_K):
            _row_copy(h2_ref, 0, xs_hbm, 0, sem).wait()
        return carry

    lax.fori_loop(0, tt, drain, 0)


def _dispatch(dest, h2, xs_zero, tt):
    t, d = h2.shape
    return pl.pallas_call(
        _dispatch_kernel,
        grid=(t // tt,),
        in_specs=[pl.BlockSpec((TOP_K, tt), lambda i: (0, i), memory_space=pltpu.SMEM),
                  pl.BlockSpec((tt, d), lambda i: (i, 0)),
                  pl.BlockSpec(memory_space=pl.ANY)],
        out_specs=pl.BlockSpec(memory_space=pl.ANY),
        out_shape=jax.ShapeDtypeStruct(xs_zero.shape, xs_zero.dtype),
        scratch_shapes=[pltpu.SemaphoreType.DMA(())],
        input_output_aliases={2: 0},
        compiler_params=pltpu.CompilerParams(dimension_semantics=("arbitrary",), has_side_effects=True),
        name="dispatch",
    )(dest, h2, xs_zero)


def _experts_kernel(be_ref, nr_ref, x_ref, wgu_ref, bgu_ref, wd_ref, bd_ref, y_ref, wgu_s, wd_s):
    b = pl.program_id(0)
    d_ff = wd_ref.shape[0]

    @pl.when(b < nr_ref[0])
    def _():
        prev = be_ref[jnp.maximum(b - 1, 0)]

        @pl.when((b == 0) | (be_ref[b] != prev))
        def _():
            wgu_s[...] = wgu_ref[...].astype(BF16)
            wd_s[...] = wd_ref[...].astype(BF16)

        xb = x_ref[...].astype(BF16)
        gu = _dot(xb, wgu_s[...]) + bgu_ref[...]
        gate = jnp.minimum(gu[:, 0:d_ff], SWIGLU_LIMIT)
        up = jnp.clip(gu[:, d_ff:2 * d_ff], -SWIGLU_LIMIT, SWIGLU_LIMIT)
        act = (up + 1.0) * gate * _sigmoid(SWIGLU_ALPHA * gate)
        y_ref[...] = _dot(act.astype(BF16), wd_s[...]) + bd_ref[...]

    @pl.when(b >= nr_ref[0])
    def _():
        y_ref[...] = jnp.zeros(y_ref.shape, F32)


def _experts(block_e, n_real, xs, wgu, bgu, wd, bd, bm):
    ns, d = xs.shape
    n_e, _, two_ff = wgu.shape
    d_ff = two_ff // 2
    nb = ns // bm

    def xmap(b, be, nr):
        return (jnp.minimum(b, nr[0] - 1), 0)

    def wmap(b, be, nr):
        return (be[b], 0, 0)

    grid_spec = pltpu.PrefetchScalarGridSpec(
        num_scalar_prefetch=2,
        grid=(nb,),
        in_specs=[pl.BlockSpec((bm, d), xmap),
                  pl.BlockSpec((None, d, two_ff), wmap),
                  pl.BlockSpec((None, 1, two_ff), wmap),
                  pl.BlockSpec((None, d_ff, d), wmap),
                  pl.BlockSpec((None, 1, d), wmap)],
        out_specs=pl.BlockSpec((bm, d), lambda b, be, nr: (b, 0)),
        scratch_shapes=[pltpu.VMEM((d, two_ff), BF16), pltpu.VMEM((d_ff, d), BF16)],
    )
    return pl.pallas_call(
        _experts_kernel,
        grid_spec=grid_spec,
        out_shape=jax.ShapeDtypeStruct((ns, d), F32),
        compiler_params=pltpu.CompilerParams(dimension_semantics=("arbitrary",),
                                             vmem_limit_bytes=V7X_VMEM_LIMIT),
        name="experts",
    )(block_e, n_real, xs, wgu, bgu, wd, bd)


def _combine_kernel(dest_ref, y_hbm, wcol_ref, x1_ref, g2_ref, nw_ref, sh_ref, sc_ref, o_ref, ybuf, sem):
    tt = dest_ref.shape[1]

    def issue(tok, carry):
        for kk in range(TOP_K):
            _row_copy(y_hbm, dest_ref[kk, tok], ybuf.at[kk], tok, sem).start()
        return carry

    lax.fori_loop(0, tt, issue, 0)

    def drain(tok, carry):
        for kk in range(TOP_K):
            _row_copy(y_hbm, 0, ybuf.at[kk], 0, sem).wait()
        return carry

    lax.fori_loop(0, tt, drain, 0)

    wc = wcol_ref[...]
    acc = wc[:, 0:1] * ybuf[0]
    for kk in range(1, TOP_K):
        acc = acc + wc[:, kk:kk + 1] * ybuf[kk]
    xo = x1_ref[...] + g2_ref[...] * acc
    ms = jnp.mean(xo * xo, axis=-1, keepdims=True)
    hn = xo * lax.rsqrt(ms + EPS) * nw_ref[...]
    o_ref[...] = hn * (1.0 + sc_ref[...]) + sh_ref[...]


def _combine(dest, y, wcol, x1, g2, nw, shf, scf, seq, tt):
    t, d = x1.shape
    tps = seq // tt
    bmap = lambda i: (i // tps, 0, 0)
    return pl.pallas_call(
        _combine_kernel,
        grid=(t // tt,),
        in_specs=[pl.BlockSpec((TOP_K, tt), lambda i: (0, i), memory_space=pltpu.SMEM),
                  pl.BlockSpec(memory_space=pl.ANY),
                  pl.BlockSpec((tt, HEAD_LANES), lambda i: (i, 0)),
                  pl.BlockSpec((tt, d), lambda i: (i, 0)),
                  pl.BlockSpec((None, 1, d), bmap),
                  pl.BlockSpec((1, d), lambda i: (0, 0)),
                  pl.BlockSpec((None, 1, d), bmap),
                  pl.BlockSpec((None, 1, d), bmap)],
        out_specs=pl.BlockSpec((tt, d), lambda i: (i, 0)),
        out_shape=jax.ShapeDtypeStruct((t, d), F32),
        scratch_shapes=[pltpu.VMEM((TOP_K, tt, d), F32), pltpu.SemaphoreType.DMA(())],
        compiler_params=pltpu.CompilerParams(dimension_semantics=("arbitrary",),
                                             vmem_limit_bytes=V7X_VMEM_LIMIT),
        name="combine",
    )(dest, y, wcol, x1, g2, nw, shf, scf)


def _pad_heads(w, heads, dk):
    r = w.shape[0]
    w3 = w.reshape(r, heads, dk)
    return jnp.pad(w3, ((0, 0), (0, 0), (0, HEAD_LANES - dk))).reshape(r, heads * HEAD_LANES)


def _pick_tile(n, pref):
    tile = pref
    while n % tile:
        tile //= 2
    return tile


def kernel(x, c, w_ada, b_ada, norm_mix, w_in, dn_conv, dn_a_log, dn_dt_bias, dn_norm, ml_conv, ml_i_bias,
           ml_f_bias, ml_norm, w_out, norm_ffn, w_router, b_router, w_gate_up, b_gate_up, w_down, b_down,
           w_ada_final, b_ada_final, norm_final):
    batch, seq, d = x.shape
    assert w_ada.shape[0] == 1, "single-layer block"
    assert seq % CHUNK == 0
    t = batch * seq
    x2 = x.reshape(t, d)

    c_pad = jnp.pad(c, ((0, 8 - batch % 8 if batch % 8 else 0), (0, 0)))
    mod = _mods(c_pad, w_ada.reshape(d, 6 * d), b_ada.reshape(1, 6 * d))[:batch]
    modf = _mods(c_pad, w_ada_final, b_ada_final.reshape(1, 2 * d))[:batch]
    sh1, sc1, g1, sh2, sc2, g2 = [mod[:, None, j * d:(j + 1) * d] for j in range(6)]
    shf, scf = modf[:, None, 0:d], modf[:, None, d:2 * d]

    wi = w_in.reshape(d, -1)
    o_z = 1536
    o_b = 2048
    o_mq = 2056
    o_mk = o_mq + ML_HEADS * ML_DK
    o_mv = o_mk + ML_HEADS * ML_DK
    o_mo = o_mv + ML_HEADS * ML_DV
    o_mi = o_mo + ML_HEADS * ML_DV
    gates = jnp.concatenate([wi[:, o_b:o_mq], wi[:, o_mi:o_mi + 2 * ML_HEADS]], axis=1)
    w_new = jnp.concatenate([
        wi[:, 0:o_z],
        _pad_heads(wi[:, o_mq:o_mk], ML_HEADS, ML_DK),
        _pad_heads(wi[:, o_mk:o_mv], ML_HEADS, ML_DK),
        wi[:, o_z:o_b],
        wi[:, o_mv:o_mo],
        wi[:, o_mo:o_mi],
        jnp.pad(gates, ((0, 0), (0, HEAD_LANES - 16))),
    ], axis=1).astype(BF16)
    wgt = gates.T.astype(BF16)
    mlc = ml_conv.reshape(CONV_W, -1)
    cw = jnp.concatenate([dn_conv.reshape(CONV_W, -1),
                          _pad_heads(mlc[:, 0:ML_HEADS * ML_DK], ML_HEADS, ML_DK),
                          _pad_heads(mlc[:, ML_HEADS * ML_DK:], ML_HEADS, ML_DK)], axis=1)
    zeros4 = jnp.zeros((4,), F32)
    bias16 = jnp.concatenate([zeros4, dn_dt_bias.reshape(4), ml_i_bias.reshape(4), ml_f_bias.reshape(4)])
    alog16 = jnp.concatenate([zeros4, dn_a_log.reshape(4), zeros4, zeros4])
    gpc = jnp.zeros((8, HEAD_LANES), F32).at[0, 0:16].set(bias16).at[1, 0:16].set(alog16)
    gpr = jnp.zeros((16, HEAD_LANES), F32).at[:, 0].set(bias16).at[:, 1].set(alog16)

    tm_in = _pick_tile(seq, 256)
    conv_out, rest, gcol, grow = _inproj(x2, sh1, sc1, norm_mix.reshape(1, d), w_new, wgt, cw, gpc, gpr, seq, tm_in)
    grow3 = grow.reshape(16, t // CHUNK, CHUNK).transpose(1, 0, 2)

    rows = _pick_tile(seq, 512)
    ya = _deltanet(conv_out, rest, gcol, grow3, dn_norm.reshape(1, DN_DV), batch, seq, rows)
    yb = _mlstm(conv_out, rest, gcol, grow3, ml_norm.reshape(1, ML_HEADS * ML_DV), batch, seq, rows)

    wo = w_out.reshape(-1, d).astype(BF16)
    n_a = DN_HEADS * DN_DV
    tm_r = _pick_tile(seq, 512)
    brp = jnp.broadcast_to(b_router.reshape(N_EXPERTS, 1), (N_EXPERTS, HEAD_LANES))
    x1, h2, idx_t, _, rank_t, wcol, cnt = _route(
        ya, yb, wo[0:n_a], wo[n_a:], x2, g1, sh2, sc2, norm_ffn.reshape(1, d),
        w_router.reshape(d, N_EXPERTS).T, brp, seq, tm_r)

    bm = 256
    nb = (t * TOP_K) // bm + N_EXPERTS
    dest, be_row, nr_row = _slots(idx_t, rank_t, cnt, bm, nb)
    block_e = be_row[0, 0:nb]
    n_real = nr_row[0, 0:1]

    tt = _pick_tile(seq, 256)
    xs = _dispatch(dest, h2, jnp.zeros((nb * bm, d), F32), tt)
    n_e = N_EXPERTS
    y = _experts(block_e, n_real, xs, w_gate_up.reshape(n_e, d, -1), b_gate_up.reshape(n_e, 1, -1),
                 w_down.reshape(n_e, -1, d), b_down.reshape(n_e, 1, d), bm)
    out = _combine(dest, y, wcol, x1, g2, norm_final.reshape(1, d), shf, scf, seq, tt)
    return out.reshape(batch, seq, d)
```

```python
import functools

import jax
import jax.numpy as jnp
from jax import lax
from jax.experimental import pallas as pl
from jax.experimental.pallas import tpu as pltpu

F32 = jnp.float32
BF16 = jnp.bfloat16

CHUNK = 64
CONV_W = 4
EPS = 1e-6

DN_HEADS = 4
DN_DK = 128
DN_DV = 128
ML_HEADS = 4
ML_DK = 64
ML_DV = 128
HEAD_LANES = 128

N_EXPERTS = 32
TOP_K = 4
SWIGLU_LIMIT = 7.0
SWIGLU_ALPHA = 1.702

C_DNQ = 0
C_DNK = 512
C_DNV = 1024
C_MLQ = 1536
C_MLK = 2048
N_CONV = 2560
C_DNZ = 2560
C_MLV = 3072
C_MLO = 3584
C_GATE = 4096
N_PROJ = 4224
N_REST = C_GATE - N_CONV

V7X_VMEM_LIMIT = 56 * 1024 * 1024

NEG_BIG = -1e30


def _sigmoid(x):
    return 1.0 / (1.0 + jnp.exp(-x))


def _softplus(x):
    return jnp.maximum(x, 0.0) + jnp.log(1.0 + jnp.exp(-jnp.abs(x)))


def _split3(v):
    hi = v.astype(BF16)
    r1 = v - hi.astype(F32)
    mid = r1.astype(BF16)
    lo = (r1 - mid.astype(F32)).astype(BF16)
    return hi, mid, lo


def _dot(a, b):
    return jnp.dot(a, b, preferred_element_type=F32)


def _dot_nt(a, b):
    return lax.dot_general(a, b, (((1,), (1,)), ((), ())), preferred_element_type=F32)


def _dot_tn(a, b):
    return lax.dot_general(a, b, (((0,), (0,)), ((), ())), preferred_element_type=F32)


def _dot_exact_right(sel_bf16, v):
    hi, mid, lo = _split3(v)
    return _dot(sel_bf16, hi) + _dot(sel_bf16, mid) + _dot(sel_bf16, lo)


def _dot_exact_left(v, sel_bf16):
    hi, mid, lo = _split3(v)
    return _dot(hi, sel_bf16) + _dot(mid, sel_bf16) + _dot(lo, sel_bf16)


def _mods_kernel(c_ref, w_ref, b_ref, o_ref):
    c = c_ref[...]
    cond = c * _sigmoid(c)
    ch, cm, cl = _split3(cond)
    wh, wm, wl = _split3(w_ref[...])
    acc = _dot(ch, wh) + (_dot(ch, wm) + _dot(cm, wh)) + (_dot(ch, wl) + _dot(cm, wm) + _dot(cl, wh))
    o_ref[...] = acc + b_ref[...]


def _mods(c_pad, w, b):
    m, d = c_pad.shape
    n = w.shape[1]
    tn = 1024
    return pl.pallas_call(
        _mods_kernel,
        grid=(n // tn,),
        in_specs=[pl.BlockSpec((m, d), lambda j: (0, 0)),
                  pl.BlockSpec((d, tn), lambda j: (0, j)),
                  pl.BlockSpec((1, tn), lambda j: (0, j))],
        out_specs=pl.BlockSpec((m, tn), lambda j: (0, j)),
        out_shape=jax.ShapeDtypeStruct((m, n), F32),
        compiler_params=pltpu.CompilerParams(dimension_semantics=("arbitrary",),
                                             vmem_limit_bytes=V7X_VMEM_LIMIT),
        name="mods",
    )(c_pad, w, b)


def _gate_transform(v, bias, alog, cls):
    vb = v + bias
    beta = _sigmoid(v)
    g = -jnp.exp(alog) * _softplus(vb)
    logf = -_softplus(-vb)
    return jnp.where(cls == 0, beta, jnp.where(cls == 1, g, jnp.where(cls == 2, vb, jnp.where(cls == 3, logf, 0.0))))


def _inproj_kernel(tiles_per_seq, x_ref, sh_ref, sc_ref, nw_ref, w_ref, wgt_ref, cw_ref, gpc_ref, gpr_ref,
                   conv_ref, rest_ref, gcol_ref, grow_ref, cbuf):
    tm = x_ref.shape[0]
    i = pl.program_id(0)
    x = x_ref[...]
    ms = jnp.mean(x * x, axis=-1, keepdims=True)
    h = x * lax.rsqrt(ms + EPS) * nw_ref[...]
    h = h * (1.0 + sc_ref[...]) + sh_ref[...]
    hb = h.astype(BF16)

    pc = _dot(hb, w_ref[:, 0:N_CONV])

    @pl.when(i % tiles_per_seq == 0)
    def _():
        cbuf[0:8, :] = jnp.zeros((8, N_CONV), F32)

    cbuf[8:tm + 8, :] = pc
    acc = cw_ref[CONV_W - 1:CONV_W, :] * pc
    for j in range(CONV_W - 1):
        acc = acc + cw_ref[j:j + 1, :] * cbuf[8 - (CONV_W - 1) + j:8 - (CONV_W - 1) + j + tm, :]
    cbuf[0:8, :] = cbuf[tm:tm + 8, :]
    y = acc * _sigmoid(acc)

    for hh in range(DN_HEADS):
        lo = C_DNQ + hh * HEAD_LANES
        qh = y[:, lo:lo + HEAD_LANES]
        conv_ref[:, lo:lo + HEAD_LANES] = (qh * lax.rsqrt(jnp.sum(qh * qh, axis=-1, keepdims=True) + EPS)) * (DN_DK ** -0.5)
        lo = C_DNK + hh * HEAD_LANES
        kh = y[:, lo:lo + HEAD_LANES]
        conv_ref[:, lo:lo + HEAD_LANES] = kh * lax.rsqrt(jnp.sum(kh * kh, axis=-1, keepdims=True) + EPS)
    conv_ref[:, C_DNV:C_MLQ] = y[:, C_DNV:C_MLQ]
    conv_ref[:, C_MLQ:C_MLK] = y[:, C_MLQ:C_MLK] * (ML_DK ** -0.5)
    conv_ref[:, C_MLK:N_CONV] = y[:, C_MLK:N_CONV]

    pr = _dot(hb, w_ref[:, N_CONV:C_GATE])
    z = pr[:, 0:512]
    rest_ref[:, 0:512] = z * _sigmoid(z)
    rest_ref[:, 512:1024] = pr[:, 512:1024]
    rest_ref[:, 1024:1536] = _sigmoid(pr[:, 1024:1536])

    r_i = lax.broadcasted_iota(jnp.int32, (tm, tm), 0)
    c_i = lax.broadcasted_iota(jnp.int32, (tm, tm), 1)
    same_chunk = (r_i // CHUNK) == (c_i // CHUNK)
    tril = jnp.where(same_chunk & (c_i <= r_i), 1.0, 0.0).astype(BF16)
    triu = jnp.where(same_chunk & (r_i <= c_i), 1.0, 0.0).astype(BF16)

    gc = _dot(hb, w_ref[:, C_GATE:N_PROJ])
    cls_c = lax.broadcasted_iota(jnp.int32, (tm, HEAD_LANES), 1) // 4
    gt = _gate_transform(gc, gpc_ref[0:1, :], gpc_ref[1:2, :], cls_c)
    cs = _dot_exact_right(tril, gt)
    gcol_ref[...] = jnp.where((cls_c == 1) | (cls_c == 3), cs, gt)

    gr = _dot_nt(wgt_ref[...], hb)
    cls_r = lax.broadcasted_iota(jnp.int32, (16, tm), 0) // 4
    gtr = _gate_transform(gr, gpr_ref[:, 0:1], gpr_ref[:, 1:2], cls_r)
    csr = _dot_exact_left(gtr, triu)
    grow_ref[...] = jnp.where((cls_r == 1) | (cls_r == 3), csr, gtr)


def _inproj(x2, sh, sc, nw, w_new, wgt, cw, gpc, gpr, seq, tm):
    t, d = x2.shape
    tps = seq // tm
    kern = functools.partial(_inproj_kernel, tps)
    return pl.pallas_call(
        kern,
        grid=(t // tm,),
        in_specs=[pl.BlockSpec((tm, d), lambda i: (i, 0)),
                  pl.BlockSpec((None, 1, d), lambda i: (i // tps, 0, 0)),
                  pl.BlockSpec((None, 1, d), lambda i: (i // tps, 0, 0)),
                  pl.BlockSpec((1, d), lambda i: (0, 0)),
                  pl.BlockSpec((d, N_PROJ), lambda i: (0, 0)),
                  pl.BlockSpec((16, d), lambda i: (0, 0)),
                  pl.BlockSpec((CONV_W, N_CONV), lambda i: (0, 0)),
                  pl.BlockSpec((8, HEAD_LANES), lambda i: (0, 0)),
                  pl.BlockSpec((16, HEAD_LANES), lambda i: (0, 0))],
        out_specs=[pl.BlockSpec((tm, N_CONV), lambda i: (i, 0)),
                   pl.BlockSpec((tm, N_REST), lambda i: (i, 0)),
                   pl.BlockSpec((tm, HEAD_LANES), lambda i: (i, 0)),
                   pl.BlockSpec((16, tm), lambda i: (0, i))],
        out_shape=[jax.ShapeDtypeStruct((t, N_CONV), F32),
                   jax.ShapeDtypeStruct((t, N_REST), F32),
                   jax.ShapeDtypeStruct((t, HEAD_LANES), F32),
                   jax.ShapeDtypeStruct((16, t), F32)],
        scratch_shapes=[pltpu.VMEM((tm + 8, N_CONV), F32)],
        compiler_params=pltpu.CompilerParams(dimension_semantics=("arbitrary",),
                                             vmem_limit_bytes=V7X_VMEM_LIMIT),
        name="inproj",
    )(x2, sh, sc, nw, w_new, wgt, cw, gpc, gpr)


def _chunk_masks():
    r = lax.broadcasted_iota(jnp.int32, (CHUNK, CHUNK), 0)
    c = lax.broadcasted_iota(jnp.int32, (CHUNK, CHUNK), 1)
    return r >= c, r > c, r == c


def _bdot(a, b):
    return lax.dot_general(a, b, (((2,), (1,)), ((0,), (0,))), preferred_element_type=F32)


def _bdot_nt(a, b):
    return lax.dot_general(a, b, (((2,), (2,)), ((0,), (0,))), preferred_element_type=F32)


def _unit_lower_inverse(lower, row, col):
    x = jnp.where(row == col, 1.0, 0.0) - jnp.where((row >> 1) == (col >> 1), lower, 0.0)
    shift = 1
    while (1 << shift) < CHUNK:
        couple = ((row >> (shift + 1)) == (col >> (shift + 1))) & ((row >> shift) != (col >> shift))
        cb = jnp.where(couple, lower, 0.0).astype(BF16)
        xb = x.astype(BF16)
        x = x - _bdot(_bdot(xb, cb).astype(BF16), xb)
        shift += 1
    return x


def _deltanet_kernel(q_ref, k_ref, v_ref, gc_ref, gr_ref, z_ref, nw_ref, o_ref, s_ref):
    nc = gr_ref.shape[0]

    @pl.when(pl.program_id(1) == 0)
    def _():
        s_ref[...] = jnp.zeros(s_ref.shape, F32)

    row = lax.broadcasted_iota(jnp.int32, (CHUNK, CHUNK), 0)
    col = lax.broadcasted_iota(jnp.int32, (CHUNK, CHUNK), 1)
    incl = row >= col
    strict = row > col
    nw = nw_ref[...]
    gcc = gc_ref[...]
    grr = gr_ref[...]

    for hh in range(DN_HEADS):
        lanes = slice(hh * HEAD_LANES, (hh + 1) * HEAD_LANES)
        q = q_ref[:, lanes].reshape(nc, CHUNK, HEAD_LANES)
        k = k_ref[:, lanes].reshape(nc, CHUNK, HEAD_LANES)
        v = v_ref[:, lanes].reshape(nc, CHUNK, HEAD_LANES)
        beta = gcc[:, hh:hh + 1].reshape(nc, CHUNK, 1)
        g_c = gcc[:, 4 + hh:5 + hh].reshape(nc, CHUNK, 1)
        g_r = grr[:, 4 + hh:5 + hh, :]
        g_last = g_c[:, CHUNK - 1:CHUNK, :]
        decay = jnp.exp(jnp.where(incl, g_c - g_r, NEG_BIG))
        kb = k.astype(BF16)
        kk = _bdot_nt(kb, kb)
        lower = jnp.where(strict, beta * kk * decay, 0.0)
        tinv = _unit_lower_inverse(lower, row, col)
        eg = jnp.exp(g_c)
        rhs = jnp.concatenate([v * beta, k * (beta * eg)], axis=-1)
        sol = _bdot(tinv.astype(BF16), rhs.astype(BF16))
        w_val = sol[:, :, 0:DN_DV]
        kq = jnp.concatenate([sol[:, :, DN_DV:DN_DV + DN_DK], q * eg], axis=1).astype(BF16)
        qk = (_bdot_nt(q.astype(BF16), kb) * decay).astype(BF16)
        k_dec = (k * jnp.exp(g_last - g_c)).astype(BF16)
        s_dec = jnp.exp(g_last)

        state = s_ref[hh]
        for c in range(nc):
            both = _dot(kq[c], state.astype(BF16))
            v_new = w_val[c] - both[0:CHUNK]
            vb = v_new.astype(BF16)
            o = both[CHUNK:2 * CHUNK] + _dot(qk[c], vb)
            state = s_dec[c] * state + _dot_tn(k_dec[c], vb)
            on = o * lax.rsqrt(jnp.mean(o * o, axis=-1, keepdims=True) + EPS) * nw
            o_ref[c * CHUNK:(c + 1) * CHUNK, lanes] = on * z_ref[c * CHUNK:(c + 1) * CHUNK, lanes]
        s_ref[hh] = state


def _deltanet(conv_out, rest, gcol, grow3, nw, batch, seq, rows):
    t = conv_out.shape[0]
    nj = seq // rows
    cpb = rows // CHUNK
    width = DN_HEADS * HEAD_LANES

    def rmap(col):
        return lambda b, j: (b * nj + j, col)

    return pl.pallas_call(
        _deltanet_kernel,
        grid=(batch, nj),
        in_specs=[pl.BlockSpec((rows, width), rmap(C_DNQ // width)),
                  pl.BlockSpec((rows, width), rmap(C_DNK // width)),
                  pl.BlockSpec((rows, width), rmap(C_DNV // width)),
                  pl.BlockSpec((rows, HEAD_LANES), rmap(0)),
                  pl.BlockSpec((cpb, 16, CHUNK), lambda b, j: (b * nj + j, 0, 0)),
                  pl.BlockSpec((rows, width), rmap((C_DNZ - N_CONV) // width)),
                  pl.BlockSpec((1, HEAD_LANES), lambda b, j: (0, 0))],
        out_specs=pl.BlockSpec((rows, width), rmap(0)),
        out_shape=jax.ShapeDtypeStruct((t, width), F32),
        scratch_shapes=[pltpu.VMEM((DN_HEADS, DN_DK, DN_DV), F32)],
        compiler_params=pltpu.CompilerParams(dimension_semantics=("arbitrary", "arbitrary"),
                                             vmem_limit_bytes=V7X_VMEM_LIMIT),
        name="deltanet",
    )(conv_out, conv_out, conv_out, gcol, grow3, rest, nw)


def _mlstm_kernel(q_ref, k_ref, v_ref, gc_ref, gr_ref, og_ref, nw_ref, o_ref, c_ref, n_ref, m_ref):
    n_chunks = gr_ref.shape[0]

    @pl.when(pl.program_id(1) == 0)
    def _():
        c_ref[...] = jnp.zeros(c_ref.shape, F32)
        n_ref[...] = jnp.zeros(n_ref.shape, F32)
        m_ref[...] = jnp.zeros(m_ref.shape, F32)

    incl, _, _ = _chunk_masks()
    nc = n_chunks
    gcc = gc_ref[...]
    grr = gr_ref[...]

    for hh in range(ML_HEADS):
        lanes = slice(hh * HEAD_LANES, (hh + 1) * HEAD_LANES)
        q = q_ref[:, lanes].reshape(nc, CHUNK, HEAD_LANES)
        k = k_ref[:, lanes].reshape(nc, CHUNK, HEAD_LANES)
        v = v_ref[:, lanes].reshape(nc, CHUNK, HEAD_LANES)
        i_c = gcc[:, 8 + hh:9 + hh].reshape(nc, CHUNK, 1)
        b_c = gcc[:, 12 + hh:13 + hh].reshape(nc, CHUNK, 1)
        i_r = grr[:, 8 + hh:9 + hh, :]
        b_r = grr[:, 12 + hh:13 + hh, :]
        b_last = b_c[:, CHUNK - 1:CHUNK, :]
        d_mat = jnp.where(incl, b_c - b_r + i_r, NEG_BIG)
        m_intra = jnp.max(d_mat, axis=-1, keepdims=True)
        g_end = b_last - b_c + i_c
        g_end_max = jnp.max(g_end, axis=1, keepdims=True)

        m_run = m_ref[hh][0:1, 0:1].reshape(1, 1, 1)
        m_before = []
        for c in range(nc):
            m_before.append(m_run)
            m_run = jnp.maximum(b_last[c:c + 1] + m_run, g_end_max[c:c + 1])
        m_s = jnp.concatenate(m_before, axis=0)
        m_new = jnp.maximum(b_last + m_s, g_end_max)
        keep = jnp.exp(b_last + m_s - m_new)

        qb = q.astype(BF16)
        kb = k.astype(BF16)
        vb = v.astype(BF16)
        m_t = jnp.maximum(b_c + m_s, m_intra)
        inter = jnp.exp(b_c + m_s - m_t)
        p = jnp.exp(d_mat - m_t) * _bdot_nt(qb, kb)
        intra = _bdot(p.astype(BF16), vb)
        p_sum = jnp.sum(p, axis=-1, keepdims=True)
        kw = k * jnp.exp(g_end - m_new)
        kwb = kw.astype(BF16)
        kw_sum = jnp.sum(kw, axis=1, keepdims=True)

        c_s = c_ref[hh]
        n_s = n_ref[hh][0:1, :]
        q_c = []
        q_n = []
        for c in range(nc):
            q_c.append(_dot(qb[c], c_s.astype(BF16)))
            q_n.append(jnp.sum(q[c] * n_s, axis=-1, keepdims=True))
            c_s = keep[c] * c_s + _dot_tn(kwb[c], vb[c])
            n_s = keep[c] * n_s + kw_sum[c]
        c_ref[hh] = c_s
        n_ref[hh] = jnp.broadcast_to(n_s, (8, HEAD_LANES))
        m_ref[hh] = jnp.broadcast_to(m_run.reshape(1, 1), (8, HEAD_LANES))

        inter_state = jnp.stack(q_c, axis=0)
        num = inter * inter_state + intra
        den = inter * jnp.stack(q_n, axis=0) + p_sum
        h = num / jnp.maximum(jnp.abs(den), jnp.exp(-m_t))
        hn = h * lax.rsqrt(jnp.mean(h * h, axis=-1, keepdims=True) + EPS) * nw_ref[:, lanes]
        o_ref[:, lanes] = hn.reshape(nc * CHUNK, HEAD_LANES) * og_ref[:, lanes]


def _mlstm(conv_out, rest, gcol, grow3, nw, batch, seq, rows):
    t = conv_out.shape[0]
    nj = seq // rows
    cpb = rows // CHUNK
    width = ML_HEADS * HEAD_LANES

    def rmap(col):
        return lambda b, j: (b * nj + j, col)

    return pl.pallas_call(
        _mlstm_kernel,
        grid=(batch, nj),
        in_specs=[pl.BlockSpec((rows, width), rmap(C_MLQ // width)),
                  pl.BlockSpec((rows, width), rmap(C_MLK // width)),
                  pl.BlockSpec((rows, width), rmap((C_MLV - N_CONV) // width)),
                  pl.BlockSpec((rows, HEAD_LANES), rmap(0)),
                  pl.BlockSpec((cpb, 16, CHUNK), lambda b, j: (b * nj + j, 0, 0)),
                  pl.BlockSpec((rows, width), rmap((C_MLO - N_CONV) // width)),
                  pl.BlockSpec((1, width), lambda b, j: (0, 0))],
        out_specs=pl.BlockSpec((rows, width), rmap(0)),
        out_shape=jax.ShapeDtypeStruct((t, width), F32),
        scratch_shapes=[pltpu.VMEM((ML_HEADS, HEAD_LANES, ML_DV), F32),
                        pltpu.VMEM((ML_HEADS, 8, HEAD_LANES), F32),
                        pltpu.VMEM((ML_HEADS, 8, HEAD_LANES), F32)],
        compiler_params=pltpu.CompilerParams(dimension_semantics=("arbitrary", "arbitrary"),
                                             vmem_limit_bytes=V7X_VMEM_LIMIT),
        name="mlstm",
    )(conv_out, conv_out, rest, gcol, grow3, rest, nw)


def _route_kernel(ya_ref, yb_ref, woa_ref, wob_ref, x_ref, g1_ref, sh_ref, sc_ref, nw_ref, wrt_ref, br_ref,
                  x1_ref, h2_ref, pos_ref, wrow_ref, len_ref):
    tm = x_ref.shape[0]

    mix = _dot(ya_ref[...].astype(BF16), woa_ref[...]) + _dot(yb_ref[...].astype(BF16), wob_ref[...])
    x1 = x_ref[...] + g1_ref[...] * mix
    x1_ref[...] = x1
    ms = jnp.mean(x1 * x1, axis=-1, keepdims=True)
    h2 = x1 * lax.rsqrt(ms + EPS) * nw_ref[...]
    h2 = h2 * (1.0 + sc_ref[...]) + sh_ref[...]
    h2_ref[...] = h2.astype(BF16)

    hh, hm, _ = _split3(h2)
    wh, wm, _ = _split3(wrt_ref[...])
    logits = _dot_nt(wh, hh) + (_dot_nt(wh, hm) + _dot_nt(wm, hh)) + br_ref[:, 0:1]

    e_i = lax.broadcasted_iota(jnp.int32, (N_EXPERTS, tm), 0)
    work = logits
    tops = []
    sels = []
    hots = []
    for _ in range(TOP_K):
        m = jnp.max(work, axis=0, keepdims=True)
        sel = jnp.min(jnp.where(work == m, e_i, N_EXPERTS), axis=0, keepdims=True)
        hot = e_i == sel
        work = jnp.where(hot, NEG_BIG, work)
        tops.append(m)
        sels.append(sel)
        hots.append(hot)
    exps = [jnp.exp(tl - tops[0]) for tl in tops]
    denom = exps[0] + exps[1] + exps[2] + exps[3]
    ws = [e / denom for e in exps]

    chosen = jnp.zeros((N_EXPERTS, tm), F32)
    for hot in hots:
        chosen = chosen + jnp.where(hot, 1.0, 0.0)

    r_i = lax.broadcasted_iota(jnp.int32, (tm, tm), 0)
    c_i = lax.broadcasted_iota(jnp.int32, (tm, tm), 1)
    strict_upper = jnp.where(r_i < c_i, 1.0, 0.0).astype(BF16)
    prefix = _dot(chosen.astype(BF16), strict_upper)
    n_e = jnp.sum(chosen, axis=1, keepdims=True)
    len8 = jnp.ceil(n_e * 0.125) * 8.0
    er = lax.broadcasted_iota(jnp.int32, (N_EXPERTS, N_EXPERTS), 0)
    ec = lax.broadcasted_iota(jnp.int32, (N_EXPERTS, N_EXPERTS), 1)
    strict_lower = jnp.where(ec < er, 1.0, 0.0).astype(BF16)
    len8b = jnp.broadcast_to(len8, (N_EXPERTS, HEAD_LANES))
    off8 = _dot_exact_right(strict_lower, len8b)[:, 0:1]
    len_ref[...] = len8b.astype(jnp.int32)

    for kk in range(TOP_K):
        wrow_ref[kk:kk + 1, :] = ws[kk]
        pos = jnp.sum(jnp.where(hots[kk], prefix + off8, 0.0), axis=0, keepdims=True)
        pos_ref[kk:kk + 1, :] = pos.astype(jnp.int32)


def _route(ya, yb, woa, wob, x2, g1, sh2, sc2, nw, wrt, br, seq, tm):
    t, d = x2.shape
    tps = seq // tm
    bmap = lambda i: (i // tps, 0, 0)
    return pl.pallas_call(
        _route_kernel,
        grid=(t // tm,),
        in_specs=[pl.BlockSpec((tm, ya.shape[1]), lambda i: (i, 0)),
                  pl.BlockSpec((tm, yb.shape[1]), lambda i: (i, 0)),
                  pl.BlockSpec(woa.shape, lambda i: (0, 0)),
                  pl.BlockSpec(wob.shape, lambda i: (0, 0)),
                  pl.BlockSpec((tm, d), lambda i: (i, 0)),
                  pl.BlockSpec((None, 1, d), bmap),
                  pl.BlockSpec((None, 1, d), bmap),
                  pl.BlockSpec((None, 1, d), bmap),
                  pl.BlockSpec((1, d), lambda i: (0, 0)),
                  pl.BlockSpec((N_EXPERTS, d), lambda i: (0, 0)),
                  pl.BlockSpec((N_EXPERTS, HEAD_LANES), lambda i: (0, 0))],
        out_specs=[pl.BlockSpec((tm, d), lambda i: (i, 0)),
                   pl.BlockSpec((tm, d), lambda i: (i, 0)),
                   pl.BlockSpec((TOP_K, tm), lambda i: (0, i)),
                   pl.BlockSpec((TOP_K, tm), lambda i: (0, i)),
                   pl.BlockSpec((N_EXPERTS, HEAD_LANES), lambda i: (i, 0))],
        out_shape=[jax.ShapeDtypeStruct((t, d), F32),
                   jax.ShapeDtypeStruct((t, d), BF16),
                   jax.ShapeDtypeStruct((TOP_K, t), jnp.int32),
                   jax.ShapeDtypeStruct((TOP_K, t), F32),
                   jax.ShapeDtypeStruct((t // tm * N_EXPERTS, HEAD_LANES), jnp.int32)],
        compiler_params=pltpu.CompilerParams(dimension_semantics=("arbitrary",),
                                             vmem_limit_bytes=V7X_VMEM_LIMIT),
        name="route",
    )(ya, yb, woa, wob, x2, g1, sh2, sc2, nw, wrt, br)


def _slots_kernel(bm, len_te_ref, len_et_ref, ssrc_ref, sdst_ref, tail_ref, be_ref, nr_ref):
    nt, lanes = len_te_ref.shape
    n_e = len_et_ref.shape[0]
    len_te = len_te_ref[...].astype(F32)
    len_et = len_et_ref[...].astype(F32)

    total_c = jnp.sum(len_et, axis=1, keepdims=True)
    padded_c = jnp.ceil(total_c * (1.0 / bm)) * bm
    r_e = lax.broadcasted_iota(jnp.int32, (n_e, n_e), 0)
    c_e = lax.broadcasted_iota(jnp.int32, (n_e, n_e), 1)
    tril_e = jnp.where(c_e <= r_e, 1.0, 0.0).astype(BF16)
    pad_end_c = _dot_exact_right(tril_e, jnp.broadcast_to(padded_c, (n_e, HEAD_LANES)))[:, 0:1]
    b_lo = lax.broadcasted_iota(jnp.int32, (n_e, be_ref.shape[1]), 1).astype(F32) * bm
    be = jnp.sum(jnp.where(pad_end_c <= b_lo, 1.0, 0.0), axis=0, keepdims=True)
    be_ref[...] = jnp.minimum(be, n_e - 1.0).astype(jnp.int32)
    nr_ref[...] = jnp.broadcast_to(pad_end_c[n_e - 1:n_e, 0:1] * (1.0 / bm), nr_ref.shape).astype(jnp.int32)

    r_l = lax.broadcasted_iota(jnp.int32, (lanes, lanes), 0)
    c_l = lax.broadcasted_iota(jnp.int32, (lanes, lanes), 1)
    upper_incl = jnp.where(r_l <= c_l, 1.0, 0.0).astype(BF16)
    upper_strict = jnp.where(r_l < c_l, 1.0, 0.0).astype(BF16)
    total_r = jnp.sum(len_te, axis=0, keepdims=True)
    padded_r = jnp.ceil(total_r * (1.0 / bm)) * bm
    pad_end_r = _dot_exact_left(jnp.broadcast_to(padded_r, (8, lanes)), upper_incl)[0:1, :]
    pad_start_r = pad_end_r - padded_r
    r_t = lax.broadcasted_iota(jnp.int32, (nt, nt), 0)
    c_t = lax.broadcasted_iota(jnp.int32, (nt, nt), 1)
    lower_strict_t = jnp.where(c_t < r_t, 1.0, 0.0).astype(BF16)
    before = _dot_exact_right(lower_strict_t, len_te)
    ssrc_ref[...] = _dot_exact_left(len_te, upper_strict).astype(jnp.int32)
    sdst_ref[...] = (pad_start_r + before).astype(jnp.int32)
    tail_ref[0:1, :] = (pad_start_r + total_r).astype(jnp.int32)
    tail_ref[1:2, :] = (padded_r - total_r).astype(jnp.int32)
    tail_ref[2:3, :] = pad_end_r.astype(jnp.int32)
    tail_ref[3:8, :] = jnp.zeros((5, lanes), jnp.int32)


def _slots(len_te, len_et, bm, nb):
    assert bm & (bm - 1) == 0, "block rows must be a power of two"
    nt, lanes = len_te.shape
    nbp = (nb + HEAD_LANES - 1) // HEAD_LANES * HEAD_LANES
    return pl.pallas_call(
        functools.partial(_slots_kernel, bm),
        out_shape=[jax.ShapeDtypeStruct((nt, lanes), jnp.int32),
                   jax.ShapeDtypeStruct((nt, lanes), jnp.int32),
                   jax.ShapeDtypeStruct((8, lanes), jnp.int32),
                   jax.ShapeDtypeStruct((1, nbp), jnp.int32),
                   jax.ShapeDtypeStruct((1, HEAD_LANES), jnp.int32)],
        compiler_params=pltpu.CompilerParams(vmem_limit_bytes=V7X_VMEM_LIMIT),
        name="slots",
    )(len_te, len_et)


def _segment_pieces(max_rows):
    sizes = []
    s = 8
    while s <= max_rows:
        sizes.append(s)
        s *= 2
    return sizes[::-1]


def _segment_dma(src_ref, src0, dst_ref, dst0, nrows, sizes, sem, start):
    off = 0
    for sz in sizes:
        bit = nrows & sz

        @pl.when(bit != 0)
        def _(off=off, sz=sz):
            s0 = pl.multiple_of(src0 + off, 8)
            d0 = pl.multiple_of(dst0 + off, 8)
            cp = pltpu.make_async_copy(src_ref.at[pl.ds(s0, sz)], dst_ref.at[pl.ds(d0, sz)], sem)
            if start:
                cp.start()
            else:
                cp.wait()

        off = off + bit


def _sorted_onehot(pos_ref, nrows, tt):
    r_i = lax.broadcasted_iota(jnp.int32, (nrows, tt), 0)
    hit = r_i == pos_ref[0:1, :]
    for kk in range(1, TOP_K):
        hit = hit | (r_i == pos_ref[kk:kk + 1, :])
    return hit


def _dispatch_kernel(len_ref, ssrc_ref, sdst_ref, tail_ref, pos_ref, h2_ref, xs_hbm, buf, sem):
    i = pl.program_id(0)
    nt = pl.num_programs(0)
    tt = h2_ref.shape[0]
    nrows = buf.shape[0]
    sizes = _segment_pieces(tt)

    perm = jnp.where(_sorted_onehot(pos_ref, nrows, tt), 1.0, 0.0).astype(BF16)
    buf[...] = _dot(perm, h2_ref[...])

    def seg(start):
        def body(e, carry):
            _segment_dma(buf, ssrc_ref[i, e], xs_hbm, sdst_ref[i, e], len_ref[i, e], sizes, sem, start)
            return carry
        lax.fori_loop(0, N_EXPERTS, body, 0)

    seg(True)
    seg(False)

    @pl.when(i == nt - 1)
    def _():
        zrows = 256
        buf[0:zrows, :] = jnp.zeros((zrows, buf.shape[1]), F32)

        def tail(start):
            def body(e, carry):
                _segment_dma(buf, 0, xs_hbm, tail_ref[0, e], tail_ref[1, e], _segment_pieces(zrows // 2), sem, start)
                return carry
            lax.fori_loop(0, N_EXPERTS, body, 0)

        tail(True)
        tail(False)

        used = tail_ref[2, tail_ref.shape[1] - 1]
        n_unused = (xs_hbm.shape[0] - used) // zrows

        def unused_copy(j):
            d0 = pl.multiple_of(used + j * zrows, 8)
            return pltpu.make_async_copy(buf.at[pl.ds(0, zrows)], xs_hbm.at[pl.ds(d0, zrows)], sem)

        def unused_start(j, carry):
            unused_copy(j).start()
            return carry

        def unused_wait(j, carry):
            unused_copy(j).wait()
            return carry

        lax.fori_loop(0, n_unused, unused_start, 0)
        lax.fori_loop(0, n_unused, unused_wait, 0)


def _dispatch(seg_len, seg_src, seg_dst, tail, pos, h2, n_slots, tt):
    t, d = h2.shape
    nrows = TOP_K * tt + 8 * N_EXPERTS
    smem = pl.BlockSpec(memory_space=pltpu.SMEM)
    return pl.pallas_call(
        _dispatch_kernel,
        grid=(t // tt,),
        in_specs=[smem, smem, smem, smem,
                  pl.BlockSpec((TOP_K, tt), lambda i: (0, i)),
                  pl.BlockSpec((tt, d), lambda i: (i, 0))],
        out_specs=pl.BlockSpec(memory_space=pl.ANY),
        out_shape=jax.ShapeDtypeStruct((n_slots, d), F32),
        scratch_shapes=[pltpu.VMEM((nrows, d), F32), pltpu.SemaphoreType.DMA(())],
        compiler_params=pltpu.CompilerParams(dimension_semantics=("arbitrary",), has_side_effects=True,
                                             vmem_limit_bytes=V7X_VMEM_LIMIT),
        name="dispatch",
    )(seg_len, seg_src, seg_dst, tail, pos, h2)


def _experts_kernel(be_ref, nr_ref, x_ref, wgu_ref, bgu_ref, wd_ref, bd_ref, y_ref, wgu_s, wd_s):
    b = pl.program_id(0)
    d_ff = wd_ref.shape[0]

    @pl.when(b < nr_ref[0])
    def _():
        prev = be_ref[jnp.maximum(b - 1, 0)]

        @pl.when((b == 0) | (be_ref[b] != prev))
        def _():
            wgu_s[...] = wgu_ref[...].astype(BF16)
            wd_s[...] = wd_ref[...].astype(BF16)

        xb = x_ref[...].astype(BF16)
        gu = _dot(xb, wgu_s[...]) + bgu_ref[...]
        gate = jnp.minimum(gu[:, 0:d_ff], SWIGLU_LIMIT)
        up = jnp.clip(gu[:, d_ff:2 * d_ff], -SWIGLU_LIMIT, SWIGLU_LIMIT)
        act = (up + 1.0) * gate * _sigmoid(SWIGLU_ALPHA * gate)
        y_ref[...] = _dot(act.astype(BF16), wd_s[...]) + bd_ref[...]

    @pl.when(b >= nr_ref[0])
    def _():
        y_ref[...] = jnp.zeros(y_ref.shape, F32)


def _experts(block_e, n_real, xs, wgu, bgu, wd, bd, bm):
    ns, d = xs.shape
    n_e, _, two_ff = wgu.shape
    d_ff = two_ff // 2
    nb = ns // bm

    def xmap(b, be, nr):
        return (jnp.minimum(b, nr[0] - 1), 0)

    def wmap(b, be, nr):
        return (be[b], 0, 0)

    grid_spec = pltpu.PrefetchScalarGridSpec(
        num_scalar_prefetch=2,
        grid=(nb,),
        in_specs=[pl.BlockSpec((bm, d), xmap),
                  pl.BlockSpec((None, d, two_ff), wmap),
                  pl.BlockSpec((None, 1, two_ff), wmap),
                  pl.BlockSpec((None, d_ff, d), wmap),
                  pl.BlockSpec((None, 1, d), wmap)],
        out_specs=pl.BlockSpec((bm, d), lambda b, be, nr: (b, 0)),
        scratch_shapes=[pltpu.VMEM((d, two_ff), BF16), pltpu.VMEM((d_ff, d), BF16)],
    )
    return pl.pallas_call(
        _experts_kernel,
        grid_spec=grid_spec,
        out_shape=jax.ShapeDtypeStruct((ns, d), F32),
        compiler_params=pltpu.CompilerParams(dimension_semantics=("arbitrary",),
                                             vmem_limit_bytes=V7X_VMEM_LIMIT),
        name="experts",
    )(block_e, n_real, xs, wgu, bgu, wd, bd)


def _combine_kernel(len_ref, ssrc_ref, sdst_ref, pos_ref, w_ref, y_hbm, x1_ref, g2_ref, nw_ref, sh_ref, sc_ref,
                    o_ref, ybuf, sem):
    i = pl.program_id(0)
    tt = x1_ref.shape[0]
    nrows = ybuf.shape[0]
    sizes = _segment_pieces(tt)

    @pl.when(i == 0)
    def _():
        ybuf[...] = jnp.zeros(ybuf.shape, F32)

    def seg(start):
        def body(e, carry):
            _segment_dma(y_hbm, sdst_ref[i, e], ybuf, ssrc_ref[i, e], len_ref[i, e], sizes, sem, start)
            return carry
        lax.fori_loop(0, N_EXPERTS, body, 0)

    seg(True)
    seg(False)

    r_i = lax.broadcasted_iota(jnp.int32, (nrows, tt), 0)
    wmat = jnp.where(r_i == pos_ref[0:1, :], w_ref[0:1, :], 0.0)
    for kk in range(1, TOP_K):
        wmat = wmat + jnp.where(r_i == pos_ref[kk:kk + 1, :], w_ref[kk:kk + 1, :], 0.0)
    acc = _dot_tn(wmat.astype(BF16), ybuf[...].astype(BF16))
    xo = x1_ref[...] + g2_ref[...] * acc
    ms = jnp.mean(xo * xo, axis=-1, keepdims=True)
    hn = xo * lax.rsqrt(ms + EPS) * nw_ref[...]
    o_ref[...] = hn * (1.0 + sc_ref[...]) + sh_ref[...]


def _combine(seg_len, seg_src, seg_dst, pos, wrow, y, x1, g2, nw, shf, scf, seq, tt):
    t, d = x1.shape
    tps = seq // tt
    bmap = lambda i: (i // tps, 0, 0)
    nrows = TOP_K * tt + 8 * N_EXPERTS
    smem = pl.BlockSpec(memory_space=pltpu.SMEM)
    return pl.pallas_call(
        _combine_kernel,
        grid=(t // tt,),
        in_specs=[smem, smem, smem,
                  pl.BlockSpec((TOP_K, tt), lambda i: (0, i)),
                  pl.BlockSpec((TOP_K, tt), lambda i: (0, i)),
                  pl.BlockSpec(memory_space=pl.ANY),
                  pl.BlockSpec((tt, d), lambda i: (i, 0)),
                  pl.BlockSpec((None, 1, d), bmap),
                  pl.BlockSpec((1, d), lambda i: (0, 0)),
                  pl.BlockSpec((None, 1, d), bmap),
                  pl.BlockSpec((None, 1, d), bmap)],
        out_specs=pl.BlockSpec((tt, d), lambda i: (i, 0)),
        out_shape=jax.ShapeDtypeStruct((t, d), F32),
        scratch_shapes=[pltpu.VMEM((nrows, d), F32), pltpu.SemaphoreType.DMA(())],
        compiler_params=pltpu.CompilerParams(dimension_semantics=("arbitrary",),
                                             vmem_limit_bytes=V7X_VMEM_LIMIT),
        name="combine",
    )(seg_len, seg_src, seg_dst, pos, wrow, y, x1, g2, nw, shf, scf)


def _pad_heads(w, heads, dk):
    r = w.shape[0]
    w3 = w.reshape(r, heads, dk)
    return jnp.pad(w3, ((0, 0), (0, 0), (0, HEAD_LANES - dk))).reshape(r, heads * HEAD_LANES)


def _pick_tile(n, pref):
    tile = pref
    while n % tile:
        tile //= 2
    return tile


def kernel(x, c, w_ada, b_ada, norm_mix, w_in, dn_conv, dn_a_log, dn_dt_bias, dn_norm, ml_conv, ml_i_bias,
           ml_f_bias, ml_norm, w_out, norm_ffn, w_router, b_router, w_gate_up, b_gate_up, w_down, b_down,
           w_ada_final, b_ada_final, norm_final):
    batch, seq, d = x.shape
    assert w_ada.shape[0] == 1, "single-layer block"
    assert seq % CHUNK == 0
    t = batch * seq
    x2 = x.reshape(t, d)

    c_pad = jnp.pad(c, ((0, 8 - batch % 8 if batch % 8 else 0), (0, 0)))
    mod = _mods(c_pad, w_ada.reshape(d, 6 * d), b_ada.reshape(1, 6 * d))[:batch]
    modf = _mods(c_pad, w_ada_final, b_ada_final.reshape(1, 2 * d))[:batch]
    sh1, sc1, g1, sh2, sc2, g2 = [mod[:, None, j * d:(j + 1) * d] for j in range(6)]
    shf, scf = modf[:, None, 0:d], modf[:, None, d:2 * d]

    wi = w_in.reshape(d, -1)
    o_z = 1536
    o_b = 2048
    o_mq = 2056
    o_mk = o_mq + ML_HEADS * ML_DK
    o_mv = o_mk + ML_HEADS * ML_DK
    o_mo = o_mv + ML_HEADS * ML_DV
    o_mi = o_mo + ML_HEADS * ML_DV
    gates = jnp.concatenate([wi[:, o_b:o_mq], wi[:, o_mi:o_mi + 2 * ML_HEADS]], axis=1)
    w_new = jnp.concatenate([
        wi[:, 0:o_z],
        _pad_heads(wi[:, o_mq:o_mk], ML_HEADS, ML_DK),
        _pad_heads(wi[:, o_mk:o_mv], ML_HEADS, ML_DK),
        wi[:, o_z:o_b],
        wi[:, o_mv:o_mo],
        wi[:, o_mo:o_mi],
        jnp.pad(gates, ((0, 0), (0, HEAD_LANES - 16))),
    ], axis=1).astype(BF16)
    wgt = gates.T.astype(BF16)
    mlc = ml_conv.reshape(CONV_W, -1)
    cw = jnp.concatenate([dn_conv.reshape(CONV_W, -1),
                          _pad_heads(mlc[:, 0:ML_HEADS * ML_DK], ML_HEADS, ML_DK),
                          _pad_heads(mlc[:, ML_HEADS * ML_DK:], ML_HEADS, ML_DK)], axis=1)
    zeros4 = jnp.zeros((4,), F32)
    bias16 = jnp.concatenate([zeros4, dn_dt_bias.reshape(4), ml_i_bias.reshape(4), ml_f_bias.reshape(4)])
    alog16 = jnp.concatenate([zeros4, dn_a_log.reshape(4), zeros4, zeros4])
    gpc = jnp.zeros((8, HEAD_LANES), F32).at[0, 0:16].set(bias16).at[1, 0:16].set(alog16)
    gpr = jnp.zeros((16, HEAD_LANES), F32).at[:, 0].set(bias16).at[:, 1].set(alog16)

    tm_in = _pick_tile(seq, 256)
    conv_out, rest, gcol, grow = _inproj(x2, sh1, sc1, norm_mix.reshape(1, d), w_new, wgt, cw, gpc, gpr, seq, tm_in)
    grow3 = grow.reshape(16, t // CHUNK, CHUNK).transpose(1, 0, 2)

    rows = _pick_tile(seq, 512)
    ya = _deltanet(conv_out, rest, gcol, grow3, dn_norm.reshape(1, DN_DV), batch, seq, rows)
    yb = _mlstm(conv_out, rest, gcol, grow3, ml_norm.reshape(1, ML_HEADS * ML_DV), batch, seq, rows)

    wo = w_out.reshape(-1, d).astype(BF16)
    n_a = DN_HEADS * DN_DV
    tm_r = _pick_tile(seq, 512)
    brp = jnp.broadcast_to(b_router.reshape(N_EXPERTS, 1), (N_EXPERTS, HEAD_LANES))
    x1, h2, pos, wrow, len_col = _route(
        ya, yb, wo[0:n_a], wo[n_a:], x2, g1, sh2, sc2, norm_ffn.reshape(1, d),
        w_router.reshape(d, N_EXPERTS).T, brp, seq, tm_r)

    n_e = N_EXPERTS
    nt = t // tm_r
    bm = 256
    len_te = len_col.reshape(nt, n_e, HEAD_LANES)[:, :, 0]
    seg_len = jnp.pad(len_te, ((0, 0), (0, HEAD_LANES - n_e)))
    n_slots_max = t * TOP_K + n_e * (7 * nt + bm)
    nb = (n_slots_max + bm - 1) // bm
    seg_src, seg_dst, tail, be_row, nr_row = _slots(seg_len, len_te.T, bm, nb)
    block_e = be_row[0, 0:nb]
    n_real = nr_row[0, 0:1]

    xs = _dispatch(seg_len, seg_src, seg_dst, tail, pos, h2, nb * bm, tm_r)
    y = _experts(block_e, n_real, xs, w_gate_up.reshape(n_e, d, -1), b_gate_up.reshape(n_e, 1, -1),
                 w_down.reshape(n_e, -1, d), b_down.reshape(n_e, 1, d), bm)
    out = _combine(seg_len, seg_src, seg_dst, pos, wrow, y, x1, g2, norm_final.reshape(1, d), shf, scf, seq, tm_r)
    return out.reshape(batch, seq, d)
```

```python
import functools

import jax
import jax.numpy as jnp
from jax import lax
from jax.experimental import pallas as pl
from jax.experimental.pallas import tpu as pltpu

F32 = jnp.float32
BF16 = jnp.bfloat16

CHUNK = 64
CONV_W = 4
EPS = 1e-6

DN_HEADS = 4
DN_DK = 128
DN_DV = 128
ML_HEADS = 4
ML_DK = 64
ML_DV = 128
HEAD_LANES = 128

N_EXPERTS = 32
TOP_K = 4
SWIGLU_LIMIT = 7.0
SWIGLU_ALPHA = 1.702

C_DNQ = 0
C_DNK = 512
C_DNV = 1024
C_MLQ = 1536
C_MLK = 2048
N_CONV = 2560
C_DNZ = 2560
C_MLV = 3072
C_MLO = 3584
C_GATE = 4096
N_PROJ = 4224
N_REST = C_GATE - N_CONV

V7X_VMEM_LIMIT = 56 * 1024 * 1024

NEG_BIG = -1e30


def _sigmoid(x):
    return 1.0 / (1.0 + jnp.exp(-x))


def _softplus(x):
    return jnp.maximum(x, 0.0) + jnp.log(1.0 + jnp.exp(-jnp.abs(x)))


def _split3(v):
    hi = v.astype(BF16)
    r1 = v - hi.astype(F32)
    mid = r1.astype(BF16)
    lo = (r1 - mid.astype(F32)).astype(BF16)
    return hi, mid, lo


def _dot(a, b):
    return jnp.dot(a, b, preferred_element_type=F32)


def _dot_nt(a, b):
    return lax.dot_general(a, b, (((1,), (1,)), ((), ())), preferred_element_type=F32)


def _dot_tn(a, b):
    return lax.dot_general(a, b, (((0,), (0,)), ((), ())), preferred_element_type=F32)


def _dot_exact_right(sel_bf16, v):
    hi, mid, lo = _split3(v)
    return _dot(sel_bf16, hi) + _dot(sel_bf16, mid) + _dot(sel_bf16, lo)


def _dot_exact_left(v, sel_bf16):
    hi, mid, lo = _split3(v)
    return _dot(hi, sel_bf16) + _dot(mid, sel_bf16) + _dot(lo, sel_bf16)


def _mods_kernel(c_ref, w_ref, b_ref, o_ref):
    c = c_ref[...]
    cond = c * _sigmoid(c)
    ch, cm, cl = _split3(cond)
    wh, wm, wl = _split3(w_ref[...])
    acc = _dot(ch, wh) + (_dot(ch, wm) + _dot(cm, wh)) + (_dot(ch, wl) + _dot(cm, wm) + _dot(cl, wh))
    o_ref[...] = acc + b_ref[...]


def _mods(c_pad, w, b):
    m, d = c_pad.shape
    n = w.shape[1]
    tn = 1024
    return pl.pallas_call(
        _mods_kernel,
        grid=(n // tn,),
        in_specs=[pl.BlockSpec((m, d), lambda j: (0, 0)),
                  pl.BlockSpec((d, tn), lambda j: (0, j)),
                  pl.BlockSpec((1, tn), lambda j: (0, j))],
        out_specs=pl.BlockSpec((m, tn), lambda j: (0, j)),
        out_shape=jax.ShapeDtypeStruct((m, n), F32),
        compiler_params=pltpu.CompilerParams(dimension_semantics=("arbitrary",),
                                             vmem_limit_bytes=V7X_VMEM_LIMIT),
        name="mods",
    )(c_pad, w, b)


def _gate_transform(v, bias, alog, cls):
    vb = v + bias
    beta = _sigmoid(v)
    g = -jnp.exp(alog) * _softplus(vb)
    logf = -_softplus(-vb)
    return jnp.where(cls == 0, beta, jnp.where(cls == 1, g, jnp.where(cls == 2, vb, jnp.where(cls == 3, logf, 0.0))))


def _inproj_kernel(tiles_per_seq, x_ref, sh_ref, sc_ref, nw_ref, w_ref, wgt_ref, cw_ref, gpc_ref, gpr_ref,
                   conv_ref, rest_ref, gcol_ref, grow_ref, cbuf):
    tm = x_ref.shape[0]
    i = pl.program_id(0)
    x = x_ref[...]
    ms = jnp.mean(x * x, axis=-1, keepdims=True)
    h = x * lax.rsqrt(ms + EPS) * nw_ref[...]
    h = h * (1.0 + sc_ref[...]) + sh_ref[...]
    hb = h.astype(BF16)

    @pl.when(i % tiles_per_seq == 0)
    def _():
        cbuf[0:8, :] = jnp.zeros((8, N_CONV), F32)

    group = 4 * HEAD_LANES
    for lo in range(0, N_CONV, group):
        cols = slice(lo, lo + group)
        pc = _dot(hb, w_ref[:, cols])
        cbuf[8:tm + 8, cols] = pc
        acc = cw_ref[CONV_W - 1:CONV_W, cols] * pc
        for j in range(CONV_W - 1):
            acc = acc + cw_ref[j:j + 1, cols] * cbuf[8 - (CONV_W - 1) + j:8 - (CONV_W - 1) + j + tm, cols]
        cbuf[0:8, cols] = cbuf[tm:tm + 8, cols]
        y = acc * _sigmoid(acc)
        if lo in (C_DNQ, C_DNK):
            scale = DN_DK ** -0.5 if lo == C_DNQ else 1.0
            for hh in range(DN_HEADS):
                uh = y[:, hh * HEAD_LANES:(hh + 1) * HEAD_LANES]
                un = uh * lax.rsqrt(jnp.sum(uh * uh, axis=-1, keepdims=True) + EPS)
                conv_ref[:, lo + hh * HEAD_LANES:lo + (hh + 1) * HEAD_LANES] = un * scale if lo == C_DNQ else un
        elif lo == C_MLQ:
            conv_ref[:, cols] = y * (ML_DK ** -0.5)
        else:
            conv_ref[:, cols] = y

    z = _dot(hb, w_ref[:, C_DNZ:C_MLV])
    rest_ref[:, 0:512] = z * _sigmoid(z)
    rest_ref[:, 512:1024] = _dot(hb, w_ref[:, C_MLV:C_MLO])
    rest_ref[:, 1024:1536] = _sigmoid(_dot(hb, w_ref[:, C_MLO:C_GATE]))

    r_i = lax.broadcasted_iota(jnp.int32, (tm, tm), 0)
    c_i = lax.broadcasted_iota(jnp.int32, (tm, tm), 1)
    same_chunk = (r_i // CHUNK) == (c_i // CHUNK)
    tril = jnp.where(same_chunk & (c_i <= r_i), 1.0, 0.0).astype(BF16)
    triu = jnp.where(same_chunk & (r_i <= c_i), 1.0, 0.0).astype(BF16)

    gc = _dot(hb, w_ref[:, C_GATE:N_PROJ])
    cls_c = lax.broadcasted_iota(jnp.int32, (tm, HEAD_LANES), 1) // 4
    gt = _gate_transform(gc, gpc_ref[0:1, :], gpc_ref[1:2, :], cls_c)
    cs = _dot_exact_right(tril, gt)
    gcol_ref[...] = jnp.where((cls_c == 1) | (cls_c == 3), cs, gt)

    gr = _dot_nt(wgt_ref[...], hb)
    cls_r = lax.broadcasted_iota(jnp.int32, (16, tm), 0) // 4
    gtr = _gate_transform(gr, gpr_ref[:, 0:1], gpr_ref[:, 1:2], cls_r)
    csr = _dot_exact_left(gtr, triu)
    grow_ref[...] = jnp.where((cls_r == 1) | (cls_r == 3), csr, gtr)


def _inproj(x2, sh, sc, nw, w_new, wgt, cw, gpc, gpr, seq, tm):
    t, d = x2.shape
    tps = seq // tm
    kern = functools.partial(_inproj_kernel, tps)
    return pl.pallas_call(
        kern,
        grid=(t // tm,),
        in_specs=[pl.BlockSpec((tm, d), lambda i: (i, 0)),
                  pl.BlockSpec((None, 1, d), lambda i: (i // tps, 0, 0)),
                  pl.BlockSpec((None, 1, d), lambda i: (i // tps, 0, 0)),
                  pl.BlockSpec((1, d), lambda i: (0, 0)),
                  pl.BlockSpec((d, N_PROJ), lambda i: (0, 0)),
                  pl.BlockSpec((16, d), lambda i: (0, 0)),
                  pl.BlockSpec((CONV_W, N_CONV), lambda i: (0, 0)),
                  pl.BlockSpec((8, HEAD_LANES), lambda i: (0, 0)),
                  pl.BlockSpec((16, HEAD_LANES), lambda i: (0, 0))],
        out_specs=[pl.BlockSpec((tm, N_CONV), lambda i: (i, 0)),
                   pl.BlockSpec((tm, N_REST), lambda i: (i, 0)),
                   pl.BlockSpec((tm, HEAD_LANES), lambda i: (i, 0)),
                   pl.BlockSpec((16, tm), lambda i: (0, i))],
        out_shape=[jax.ShapeDtypeStruct((t, N_CONV), F32),
                   jax.ShapeDtypeStruct((t, N_REST), F32),
                   jax.ShapeDtypeStruct((t, HEAD_LANES), F32),
                   jax.ShapeDtypeStruct((16, t), F32)],
        scratch_shapes=[pltpu.VMEM((tm + 8, N_CONV), F32)],
        compiler_params=pltpu.CompilerParams(dimension_semantics=("arbitrary",),
                                             vmem_limit_bytes=V7X_VMEM_LIMIT),
        name="inproj",
    )(x2, sh, sc, nw, w_new, wgt, cw, gpc, gpr)


def _chunk_masks():
    r = lax.broadcasted_iota(jnp.int32, (CHUNK, CHUNK), 0)
    c = lax.broadcasted_iota(jnp.int32, (CHUNK, CHUNK), 1)
    return r >= c, r > c, r == c


def _bdot(a, b):
    return lax.dot_general(a, b, (((2,), (1,)), ((0,), (0,))), preferred_element_type=F32)


def _bdot_nt(a, b):
    return lax.dot_general(a, b, (((2,), (2,)), ((0,), (0,))), preferred_element_type=F32)


def _unit_lower_inverse(lower, row, col):
    x = jnp.where(row == col, 1.0, 0.0) - jnp.where((row >> 1) == (col >> 1), lower, 0.0)
    shift = 1
    while (1 << shift) < CHUNK:
        couple = ((row >> (shift + 1)) == (col >> (shift + 1))) & ((row >> shift) != (col >> shift))
        cb = jnp.where(couple, lower, 0.0).astype(BF16)
        xb = x.astype(BF16)
        x = x - _bdot(_bdot(xb, cb).astype(BF16), xb)
        shift += 1
    return x


def _deltanet_kernel(q_ref, k_ref, v_ref, gc_ref, gr_ref, z_ref, nw_ref, o_ref, s_ref):
    nc = gr_ref.shape[0]

    @pl.when(pl.program_id(1) == 0)
    def _():
        s_ref[...] = jnp.zeros(s_ref.shape, F32)

    row = lax.broadcasted_iota(jnp.int32, (CHUNK, CHUNK), 0)
    col = lax.broadcasted_iota(jnp.int32, (CHUNK, CHUNK), 1)
    incl = row >= col
    strict = row > col
    nw = nw_ref[...]
    gcc = gc_ref[...]
    grr = gr_ref[...]

    nh = DN_HEADS
    nb = nh * nc

    def heads(ref):
        return jnp.stack([ref[:, hh * HEAD_LANES:(hh + 1) * HEAD_LANES] for hh in range(nh)],
                         axis=0).reshape(nb, CHUNK, HEAD_LANES)

    def col_gate(lane0):
        return jnp.stack([gcc[:, lane0 + hh:lane0 + hh + 1] for hh in range(nh)], axis=0).reshape(nb, CHUNK, 1)

    q = heads(q_ref)
    k = heads(k_ref)
    v = heads(v_ref)
    beta = col_gate(0)
    g_c = col_gate(4)
    g_r = jnp.stack([grr[:, 4 + hh:5 + hh, :] for hh in range(nh)], axis=0).reshape(nb, 1, CHUNK)
    g_last = g_c[:, CHUNK - 1:CHUNK, :]
    decay = jnp.exp(jnp.where(incl, g_c - g_r, NEG_BIG))
    kb = k.astype(BF16)
    kk = _bdot_nt(kb, kb)
    lower = jnp.where(strict, beta * kk * decay, 0.0)
    tinv = _unit_lower_inverse(lower, row, col)
    eg = jnp.exp(g_c)
    rhs = jnp.concatenate([v * beta, k * (beta * eg)], axis=-1)
    sol = _bdot(tinv.astype(BF16), rhs.astype(BF16))
    w_val = sol[:, :, 0:DN_DV].reshape(nh, nc, CHUNK, DN_DV)
    kq = jnp.concatenate([sol[:, :, DN_DV:DN_DV + DN_DK], q * eg], axis=1).astype(BF16)
    kq = kq.reshape(nh, nc, 2 * CHUNK, DN_DK)
    qk = (_bdot_nt(q.astype(BF16), kb) * decay).astype(BF16).reshape(nh, nc, CHUNK, CHUNK)
    k_dec_t = jnp.swapaxes(k * jnp.exp(g_last - g_c), 1, 2).astype(BF16).reshape(nh, nc, DN_DK, CHUNK)
    s_dec = jnp.exp(g_last).reshape(nh, nc, 1, 1)

    state = s_ref[...]
    outs = []
    for c in range(nc):
        both = _bdot(kq[:, c], state.astype(BF16))
        v_new = w_val[:, c] - both[:, 0:CHUNK]
        vb = v_new.astype(BF16)
        outs.append(both[:, CHUNK:2 * CHUNK] + _bdot(qk[:, c], vb))
        state = s_dec[:, c] * state + _bdot(k_dec_t[:, c], vb)
    s_ref[...] = state

    o = jnp.stack(outs, axis=1)
    on = o * lax.rsqrt(jnp.mean(o * o, axis=-1, keepdims=True) + EPS) * nw
    on = on.reshape(nh, nc * CHUNK, DN_DV)
    for hh in range(nh):
        lanes = slice(hh * HEAD_LANES, (hh + 1) * HEAD_LANES)
        o_ref[:, lanes] = on[hh] * z_ref[:, lanes]


def _deltanet(conv_out, rest, gcol, grow3, nw, batch, seq, rows):
    t = conv_out.shape[0]
    nj = seq // rows
    cpb = rows // CHUNK
    width = DN_HEADS * HEAD_LANES

    def rmap(col):
        return lambda b, j: (b * nj + j, col)

    return pl.pallas_call(
        _deltanet_kernel,
        grid=(batch, nj),
        in_specs=[pl.BlockSpec((rows, width), rmap(C_DNQ // width)),
                  pl.BlockSpec((rows, width), rmap(C_DNK // width)),
                  pl.BlockSpec((rows, width), rmap(C_DNV // width)),
                  pl.BlockSpec((rows, HEAD_LANES), rmap(0)),
                  pl.BlockSpec((cpb, 16, CHUNK), lambda b, j: (b * nj + j, 0, 0)),
                  pl.BlockSpec((rows, width), rmap((C_DNZ - N_CONV) // width)),
                  pl.BlockSpec((1, HEAD_LANES), lambda b, j: (0, 0))],
        out_specs=pl.BlockSpec((rows, width), rmap(0)),
        out_shape=jax.ShapeDtypeStruct((t, width), F32),
        scratch_shapes=[pltpu.VMEM((DN_HEADS, DN_DK, DN_DV), F32)],
        compiler_params=pltpu.CompilerParams(dimension_semantics=("arbitrary", "arbitrary"),
                                             vmem_limit_bytes=V7X_VMEM_LIMIT),
        name="deltanet",
    )(conv_out, conv_out, conv_out, gcol, grow3, rest, nw)


def _mlstm_kernel(q_ref, k_ref, v_ref, gc_ref, gr_ref, og_ref, nw_ref, o_ref, c_ref, n_ref, m_ref):
    n_chunks = gr_ref.shape[0]

    @pl.when(pl.program_id(1) == 0)
    def _():
        c_ref[...] = jnp.zeros(c_ref.shape, F32)
        n_ref[...] = jnp.zeros(n_ref.shape, F32)
        m_ref[...] = jnp.zeros(m_ref.shape, F32)

    incl, _, _ = _chunk_masks()
    nc = n_chunks
    gcc = gc_ref[...]
    grr = gr_ref[...]

    nh = ML_HEADS

    def heads(ref):
        return jnp.stack([ref[:, hh * HEAD_LANES:(hh + 1) * HEAD_LANES] for hh in range(nh)],
                         axis=0).reshape(nh, nc, CHUNK, HEAD_LANES)

    def col_gate(lane0):
        return jnp.stack([gcc[:, lane0 + hh:lane0 + hh + 1] for hh in range(nh)], axis=0).reshape(nh, nc, CHUNK, 1)

    def row_gate(row0):
        return jnp.stack([grr[:, row0 + hh:row0 + hh + 1, :] for hh in range(nh)], axis=0)

    q = heads(q_ref)
    k = heads(k_ref)
    v = heads(v_ref)
    i_c = col_gate(8)
    b_c = col_gate(12)
    i_r = row_gate(8)
    b_r = row_gate(12)
    b_last = b_c[:, :, CHUNK - 1:CHUNK, :]
    d_mat = jnp.where(incl, b_c - b_r + i_r, NEG_BIG)
    m_intra = jnp.max(d_mat, axis=-1, keepdims=True)
    g_end = b_last - b_c + i_c
    g_end_max = jnp.max(g_end, axis=2, keepdims=True)

    m_run = m_ref[:, 0:1, 0:1].reshape(nh, 1, 1, 1)
    m_before = []
    for c in range(nc):
        m_before.append(m_run)
        m_run = jnp.maximum(b_last[:, c:c + 1] + m_run, g_end_max[:, c:c + 1])
    m_s = jnp.concatenate(m_before, axis=1)
    m_new = jnp.maximum(b_last + m_s, g_end_max)
    keep = jnp.exp(b_last + m_s - m_new)

    nb = nh * nc
    qb = q.astype(BF16)
    kb = k.astype(BF16)
    vb = v.astype(BF16)
    m_t = jnp.maximum(b_c + m_s, m_intra)
    inter = jnp.exp(b_c + m_s - m_t)
    qk = _bdot_nt(qb.reshape(nb, CHUNK, HEAD_LANES), kb.reshape(nb, CHUNK, HEAD_LANES))
    p = jnp.exp(d_mat - m_t) * qk.reshape(nh, nc, CHUNK, CHUNK)
    intra = _bdot(p.astype(BF16).reshape(nb, CHUNK, CHUNK), vb.reshape(nb, CHUNK, ML_DV)).reshape(nh, nc, CHUNK, ML_DV)
    p_sum = jnp.sum(p, axis=-1, keepdims=True)
    kw = k * jnp.exp(g_end - m_new)
    kw_t = jnp.swapaxes(kw.reshape(nb, CHUNK, HEAD_LANES), 1, 2).astype(BF16)
    d_state = _bdot(kw_t, vb.reshape(nb, CHUNK, ML_DV)).reshape(nh, nc, HEAD_LANES, ML_DV)
    kw_sum = jnp.sum(kw, axis=2, keepdims=True)

    c_s = c_ref[...]
    n_s = n_ref[:, 0:1, :]
    q_c = []
    q_n = []
    for c in range(nc):
        q_c.append(_bdot(qb[:, c], c_s.astype(BF16)))
        q_n.append(jnp.sum(q[:, c] * n_s, axis=-1, keepdims=True))
        c_s = keep[:, c] * c_s + d_state[:, c]
        n_s = keep[:, c] * n_s + kw_sum[:, c]
    c_ref[...] = c_s
    n_ref[...] = jnp.broadcast_to(n_s, (nh, 8, HEAD_LANES))
    m_ref[...] = jnp.broadcast_to(m_run.reshape(nh, 1, 1), (nh, 8, HEAD_LANES))

    num = inter * jnp.stack(q_c, axis=1) + intra
    den = inter * jnp.stack(q_n, axis=1) + p_sum
    h = num / jnp.maximum(jnp.abs(den), jnp.exp(-m_t))
    hr = h * lax.rsqrt(jnp.mean(h * h, axis=-1, keepdims=True) + EPS)
    hr = hr.reshape(nh, nc * CHUNK, ML_DV)
    for hh in range(nh):
        lanes = slice(hh * HEAD_LANES, (hh + 1) * HEAD_LANES)
        o_ref[:, lanes] = hr[hh] * nw_ref[:, lanes] * og_ref[:, lanes]


def _mlstm(conv_out, rest, gcol, grow3, nw, batch, seq, rows):
    t = conv_out.shape[0]
    nj = seq // rows
    cpb = rows // CHUNK
    width = ML_HEADS * HEAD_LANES

    def rmap(col):
        return lambda b, j: (b * nj + j, col)

    return pl.pallas_call(
        _mlstm_kernel,
        grid=(batch, nj),
        in_specs=[pl.BlockSpec((rows, width), rmap(C_MLQ // width)),
                  pl.BlockSpec((rows, width), rmap(C_MLK // width)),
                  pl.BlockSpec((rows, width), rmap((C_MLV - N_CONV) // width)),
                  pl.BlockSpec((rows, HEAD_LANES), rmap(0)),
                  pl.BlockSpec((cpb, 16, CHUNK), lambda b, j: (b * nj + j, 0, 0)),
                  pl.BlockSpec((rows, width), rmap((C_MLO - N_CONV) // width)),
                  pl.BlockSpec((1, width), lambda b, j: (0, 0))],
        out_specs=pl.BlockSpec((rows, width), rmap(0)),
        out_shape=jax.ShapeDtypeStruct((t, width), F32),
        scratch_shapes=[pltpu.VMEM((ML_HEADS, HEAD_LANES, ML_DV), F32),
                        pltpu.VMEM((ML_HEADS, 8, HEAD_LANES), F32),
                        pltpu.VMEM((ML_HEADS, 8, HEAD_LANES), F32)],
        compiler_params=pltpu.CompilerParams(dimension_semantics=("arbitrary", "arbitrary"),
                                             vmem_limit_bytes=V7X_VMEM_LIMIT),
        name="mlstm",
    )(conv_out, conv_out, rest, gcol, grow3, rest, nw)


def _route_kernel(ya_ref, yb_ref, woa_ref, wob_ref, x_ref, g1_ref, sh_ref, sc_ref, nw_ref, wrt_ref, br_ref,
                  x1_ref, h2_ref, pos_ref, wrow_ref, len_ref):
    tm = x_ref.shape[0]

    mix = _dot(ya_ref[...].astype(BF16), woa_ref[...]) + _dot(yb_ref[...].astype(BF16), wob_ref[...])
    x1 = x_ref[...] + g1_ref[...] * mix
    x1_ref[...] = x1
    ms = jnp.mean(x1 * x1, axis=-1, keepdims=True)
    h2 = x1 * lax.rsqrt(ms + EPS) * nw_ref[...]
    h2 = h2 * (1.0 + sc_ref[...]) + sh_ref[...]
    h2_ref[...] = h2.astype(BF16)

    hh, hm, _ = _split3(h2)
    wh, wm, _ = _split3(wrt_ref[...])
    logits = _dot_nt(wh, hh) + (_dot_nt(wh, hm) + _dot_nt(wm, hh)) + br_ref[:, 0:1]

    e_i = lax.broadcasted_iota(jnp.int32, (N_EXPERTS, tm), 0)
    work = logits
    tops = []
    sels = []
    hots = []
    for _ in range(TOP_K):
        m = jnp.max(work, axis=0, keepdims=True)
        sel = jnp.min(jnp.where(work == m, e_i, N_EXPERTS), axis=0, keepdims=True)
        hot = e_i == sel
        work = jnp.where(hot, NEG_BIG, work)
        tops.append(m)
        sels.append(sel)
        hots.append(hot)
    exps = [jnp.exp(tl - tops[0]) for tl in tops]
    denom = exps[0] + exps[1] + exps[2] + exps[3]
    ws = [e / denom for e in exps]

    chosen = jnp.zeros((N_EXPERTS, tm), F32)
    for hot in hots:
        chosen = chosen + jnp.where(hot, 1.0, 0.0)

    r_i = lax.broadcasted_iota(jnp.int32, (tm, tm), 0)
    c_i = lax.broadcasted_iota(jnp.int32, (tm, tm), 1)
    strict_upper = jnp.where(r_i < c_i, 1.0, 0.0).astype(BF16)
    prefix = _dot(chosen.astype(BF16), strict_upper)
    n_e = jnp.sum(chosen, axis=1, keepdims=True)
    len8 = jnp.ceil(n_e * 0.125) * 8.0
    er = lax.broadcasted_iota(jnp.int32, (N_EXPERTS, N_EXPERTS), 0)
    ec = lax.broadcasted_iota(jnp.int32, (N_EXPERTS, N_EXPERTS), 1)
    strict_lower = jnp.where(ec < er, 1.0, 0.0).astype(BF16)
    len8b = jnp.broadcast_to(len8, (N_EXPERTS, HEAD_LANES))
    off8 = _dot_exact_right(strict_lower, len8b)[:, 0:1]
    len_ref[...] = len8b.astype(jnp.int32)

    for kk in range(TOP_K):
        wrow_ref[kk:kk + 1, :] = ws[kk]
        pos = jnp.sum(jnp.where(hots[kk], prefix + off8, 0.0), axis=0, keepdims=True)
        pos_ref[kk:kk + 1, :] = pos.astype(jnp.int32)


def _route(ya, yb, woa, wob, x2, g1, sh2, sc2, nw, wrt, br, seq, tm):
    t, d = x2.shape
    tps = seq // tm
    bmap = lambda i: (i // tps, 0, 0)
    return pl.pallas_call(
        _route_kernel,
        grid=(t // tm,),
        in_specs=[pl.BlockSpec((tm, ya.shape[1]), lambda i: (i, 0)),
                  pl.BlockSpec((tm, yb.shape[1]), lambda i: (i, 0)),
                  pl.BlockSpec(woa.shape, lambda i: (0, 0)),
                  pl.BlockSpec(wob.shape, lambda i: (0, 0)),
                  pl.BlockSpec((tm, d), lambda i: (i, 0)),
                  pl.BlockSpec((None, 1, d), bmap),
                  pl.BlockSpec((None, 1, d), bmap),
                  pl.BlockSpec((None, 1, d), bmap),
                  pl.BlockSpec((1, d), lambda i: (0, 0)),
                  pl.BlockSpec((N_EXPERTS, d), lambda i: (0, 0)),
                  pl.BlockSpec((N_EXPERTS, HEAD_LANES), lambda i: (0, 0))],
        out_specs=[pl.BlockSpec((tm, d), lambda i: (i, 0)),
                   pl.BlockSpec((tm, d), lambda i: (i, 0)),
                   pl.BlockSpec((TOP_K, tm), lambda i: (0, i)),
                   pl.BlockSpec((TOP_K, tm), lambda i: (0, i)),
                   pl.BlockSpec((N_EXPERTS, HEAD_LANES), lambda i: (i, 0))],
        out_shape=[jax.ShapeDtypeStruct((t, d), F32),
                   jax.ShapeDtypeStruct((t, d), BF16),
                   jax.ShapeDtypeStruct((TOP_K, t), jnp.int32),
                   jax.ShapeDtypeStruct((TOP_K, t), F32),
                   jax.ShapeDtypeStruct((t // tm * N_EXPERTS, HEAD_LANES), jnp.int32)],
        compiler_params=pltpu.CompilerParams(dimension_semantics=("arbitrary",),
                                             vmem_limit_bytes=V7X_VMEM_LIMIT),
        name="route",
    )(ya, yb, woa, wob, x2, g1, sh2, sc2, nw, wrt, br)


TAIL_START, TAIL_LEN, PAD_END, BLOCK_START, BLOCK_COUNT = range(5)


def _slots_kernel(bm, len_te_ref, ssrc_ref, sdst_ref, tab_ref):
    nt, lanes = len_te_ref.shape
    len_te = len_te_ref[...].astype(F32)

    r_l = lax.broadcasted_iota(jnp.int32, (lanes, lanes), 0)
    c_l = lax.broadcasted_iota(jnp.int32, (lanes, lanes), 1)
    upper_incl = jnp.where(r_l <= c_l, 1.0, 0.0).astype(BF16)
    upper_strict = jnp.where(r_l < c_l, 1.0, 0.0).astype(BF16)
    total_r = jnp.sum(len_te, axis=0, keepdims=True)
    padded_r = jnp.ceil(total_r * (1.0 / bm)) * bm
    pad_end_r = _dot_exact_left(jnp.broadcast_to(padded_r, (8, lanes)), upper_incl)[0:1, :]
    pad_start_r = pad_end_r - padded_r
    r_t = lax.broadcasted_iota(jnp.int32, (nt, nt), 0)
    c_t = lax.broadcasted_iota(jnp.int32, (nt, nt), 1)
    lower_strict_t = jnp.where(c_t < r_t, 1.0, 0.0).astype(BF16)
    before = _dot_exact_right(lower_strict_t, len_te)
    ssrc_ref[...] = _dot_exact_left(len_te, upper_strict).astype(jnp.int32)
    sdst_ref[...] = (pad_start_r + before).astype(jnp.int32)
    tab_ref[TAIL_START:TAIL_START + 1, :] = (pad_start_r + total_r).astype(jnp.int32)
    tab_ref[TAIL_LEN:TAIL_LEN + 1, :] = (padded_r - total_r).astype(jnp.int32)
    tab_ref[PAD_END:PAD_END + 1, :] = pad_end_r.astype(jnp.int32)
    tab_ref[BLOCK_START:BLOCK_START + 1, :] = (pad_start_r * (1.0 / bm)).astype(jnp.int32)
    tab_ref[BLOCK_COUNT:BLOCK_COUNT + 1, :] = (padded_r * (1.0 / bm)).astype(jnp.int32)
    tab_ref[5:8, :] = jnp.zeros((3, lanes), jnp.int32)


def _slots(len_te, bm):
    assert bm & (bm - 1) == 0, "block rows must be a power of two"
    nt, lanes = len_te.shape
    return pl.pallas_call(
        functools.partial(_slots_kernel, bm),
        out_shape=[jax.ShapeDtypeStruct((nt, lanes), jnp.int32),
                   jax.ShapeDtypeStruct((nt, lanes), jnp.int32),
                   jax.ShapeDtypeStruct((8, lanes), jnp.int32)],
        compiler_params=pltpu.CompilerParams(vmem_limit_bytes=V7X_VMEM_LIMIT),
        name="slots",
    )(len_te)


def _segment_pieces(max_rows):
    sizes = []
    s = 8
    while s <= max_rows:
        sizes.append(s)
        s *= 2
    return sizes[::-1]


def _segment_dma(src_ref, src0, dst_ref, dst0, nrows, sizes, sem, start):
    off = 0
    for sz in sizes:
        bit = nrows & sz

        @pl.when(bit != 0)
        def _(off=off, sz=sz):
            s0 = pl.multiple_of(src0 + off, 8)
            d0 = pl.multiple_of(dst0 + off, 8)
            cp = pltpu.make_async_copy(src_ref.at[pl.ds(s0, sz)], dst_ref.at[pl.ds(d0, sz)], sem)
            if start:
                cp.start()
            else:
                cp.wait()

        off = off + bit


def _sorted_onehot(pos_ref, nrows, tt):
    r_i = lax.broadcasted_iota(jnp.int32, (nrows, tt), 0)
    hit = r_i == pos_ref[0:1, :]
    for kk in range(1, TOP_K):
        hit = hit | (r_i == pos_ref[kk:kk + 1, :])
    return hit


def _dispatch_kernel(len_ref, ssrc_ref, sdst_ref, tail_ref, pos_ref, h2_ref, xs_hbm, buf, sem):
    i = pl.program_id(0)
    nt = pl.num_programs(0)
    tt = h2_ref.shape[0]
    nrows = buf.shape[0]
    sizes = _segment_pieces(tt)

    perm = jnp.where(_sorted_onehot(pos_ref, nrows, tt), 1.0, 0.0).astype(BF16)
    buf[...] = _dot(perm, h2_ref[...])

    def seg(start):
        def body(e, carry):
            _segment_dma(buf, ssrc_ref[i, e], xs_hbm, sdst_ref[i, e], len_ref[i, e], sizes, sem, start)
            return carry
        lax.fori_loop(0, N_EXPERTS, body, 0)

    seg(True)
    seg(False)

    @pl.when(i == nt - 1)
    def _():
        zrows = 256
        buf[0:zrows, :] = jnp.zeros((zrows, buf.shape[1]), F32)

        def tail(start):
            def body(e, carry):
                _segment_dma(buf, 0, xs_hbm, tail_ref[TAIL_START, e], tail_ref[TAIL_LEN, e],
                             _segment_pieces(zrows // 2), sem, start)
                return carry
            lax.fori_loop(0, N_EXPERTS, body, 0)

        tail(True)
        tail(False)

        used = tail_ref[PAD_END, tail_ref.shape[1] - 1]
        n_unused = (xs_hbm.shape[0] - used) // zrows

        def unused_copy(j):
            d0 = pl.multiple_of(used + j * zrows, 8)
            return pltpu.make_async_copy(buf.at[pl.ds(0, zrows)], xs_hbm.at[pl.ds(d0, zrows)], sem)

        def unused_start(j, carry):
            unused_copy(j).start()
            return carry

        def unused_wait(j, carry):
            unused_copy(j).wait()
            return carry

        lax.fori_loop(0, n_unused, unused_start, 0)
        lax.fori_loop(0, n_unused, unused_wait, 0)


def _dispatch(seg_len, seg_src, seg_dst, tail, pos, h2, n_slots, tt):
    t, d = h2.shape
    nrows = TOP_K * tt + 8 * N_EXPERTS
    smem = pl.BlockSpec(memory_space=pltpu.SMEM)
    return pl.pallas_call(
        _dispatch_kernel,
        grid=(t // tt,),
        in_specs=[smem, smem, smem, smem,
                  pl.BlockSpec((TOP_K, tt), lambda i: (0, i)),
                  pl.BlockSpec((tt, d), lambda i: (i, 0))],
        out_specs=pl.BlockSpec(memory_space=pl.ANY),
        out_shape=jax.ShapeDtypeStruct((n_slots, d), F32),
        scratch_shapes=[pltpu.VMEM((nrows, d), F32), pltpu.SemaphoreType.DMA(())],
        compiler_params=pltpu.CompilerParams(dimension_semantics=("arbitrary",), has_side_effects=True,
                                             vmem_limit_bytes=V7X_VMEM_LIMIT),
        name="dispatch",
    )(seg_len, seg_src, seg_dst, tail, pos, h2)


def _experts_kernel(tab_ref, xs_hbm, wgu_ref, bgu_ref, wd_ref, bd_ref, y_hbm, wgu_s, wd_s, xbuf, ybuf, sem_in,
                    sem_out):
    e = pl.program_id(0)
    bm = xbuf.shape[1]
    d_ff = wd_ref.shape[0]
    first = tab_ref[BLOCK_START, e]
    count = tab_ref[BLOCK_COUNT, e]

    def rows(j):
        return pl.ds(pl.multiple_of((first + j) * bm, bm), bm)

    def x_copy(j, slot):
        return pltpu.make_async_copy(xs_hbm.at[rows(j)], xbuf.at[slot], sem_in.at[slot])

    def y_copy(j, slot):
        return pltpu.make_async_copy(ybuf.at[slot], y_hbm.at[rows(j)], sem_out.at[slot])

    @pl.when(count > 0)
    def _():
        x_copy(0, 0).start()

    wgu_s[...] = wgu_ref[...].astype(BF16)
    wd_s[...] = wd_ref[...].astype(BF16)

    def block(j, carry):
        slot = j % 2
        x_copy(j, slot).wait()

        @pl.when(j + 1 < count)
        def _():
            x_copy(j + 1, 1 - slot).start()

        @pl.when(j >= 2)
        def _():
            y_copy(j - 2, slot).wait()

        xb = xbuf[slot].astype(BF16)
        fc = 512
        acc = None
        for f in range(d_ff // fc):
            gcols = slice(f * fc, (f + 1) * fc)
            ucols = slice(d_ff + f * fc, d_ff + (f + 1) * fc)
            gate = jnp.minimum(_dot(xb, wgu_s[:, gcols]) + bgu_ref[:, gcols], SWIGLU_LIMIT)
            up = jnp.clip(_dot(xb, wgu_s[:, ucols]) + bgu_ref[:, ucols], -SWIGLU_LIMIT, SWIGLU_LIMIT)
            act = (up + 1.0) * gate * _sigmoid(SWIGLU_ALPHA * gate)
            part = _dot(act.astype(BF16), wd_s[gcols, :])
            acc = part if acc is None else acc + part
        ybuf[slot] = acc + bd_ref[...]
        y_copy(j, slot).start()
        return carry

    lax.fori_loop(0, count, block, 0)

    @pl.when(count >= 2)
    def _():
        y_copy(count - 2, count % 2).wait()

    @pl.when(count >= 1)
    def _():
        y_copy(count - 1, (count - 1) % 2).wait()

    @pl.when(e == pl.num_programs(0) - 1)
    def _():
        used = first + count
        n_unused = y_hbm.shape[0] // bm - used
        ybuf[0] = jnp.zeros(ybuf.shape[1:], F32)

        def z_copy(j):
            return pltpu.make_async_copy(ybuf.at[0], y_hbm.at[rows(count + j)], sem_out.at[0])

        def z_start(j, carry):
            z_copy(j).start()
            return carry

        def z_wait(j, carry):
            z_copy(j).wait()
            return carry

        lax.fori_loop(0, n_unused, z_start, 0)
        lax.fori_loop(0, n_unused, z_wait, 0)


def _experts(table, xs, wgu, bgu, wd, bd, bm):
    ns, d = xs.shape
    n_e, _, two_ff = wgu.shape
    d_ff = two_ff // 2
    wmap = lambda e: (e, 0, 0)
    return pl.pallas_call(
        _experts_kernel,
        grid=(n_e,),
        in_specs=[pl.BlockSpec(memory_space=pltpu.SMEM),
                  pl.BlockSpec(memory_space=pl.ANY),
                  pl.BlockSpec((None, d, two_ff), wmap),
                  pl.BlockSpec((None, 1, two_ff), wmap),
                  pl.BlockSpec((None, d_ff, d), wmap),
                  pl.BlockSpec((None, 1, d), wmap)],
        out_specs=pl.BlockSpec(memory_space=pl.ANY),
        out_shape=jax.ShapeDtypeStruct((ns, d), F32),
        scratch_shapes=[pltpu.VMEM((d, two_ff), BF16), pltpu.VMEM((d_ff, d), BF16),
                        pltpu.VMEM((2, bm, d), F32), pltpu.VMEM((2, bm, d), F32),
                        pltpu.SemaphoreType.DMA((2,)), pltpu.SemaphoreType.DMA((2,))],
        compiler_params=pltpu.CompilerParams(dimension_semantics=("arbitrary",), has_side_effects=True,
                                             vmem_limit_bytes=V7X_VMEM_LIMIT),
        name="experts",
    )(table, xs, wgu, bgu, wd, bd)


def _combine_kernel(len_ref, ssrc_ref, sdst_ref, pos_ref, w_ref, y_hbm, x1_ref, g2_ref, nw_ref, sh_ref, sc_ref,
                    o_ref, ybuf, sem):
    i = pl.program_id(0)
    tt = x1_ref.shape[0]
    nrows = ybuf.shape[0]
    sizes = _segment_pieces(tt)

    @pl.when(i == 0)
    def _():
        ybuf[...] = jnp.zeros(ybuf.shape, F32)

    def seg(start):
        def body(e, carry):
            _segment_dma(y_hbm, sdst_ref[i, e], ybuf, ssrc_ref[i, e], len_ref[i, e], sizes, sem, start)
            return carry
        lax.fori_loop(0, N_EXPERTS, body, 0)

    seg(True)
    seg(False)

    r_i = lax.broadcasted_iota(jnp.int32, (nrows, tt), 0)
    wmat = jnp.where(r_i == pos_ref[0:1, :], w_ref[0:1, :], 0.0)
    for kk in range(1, TOP_K):
        wmat = wmat + jnp.where(r_i == pos_ref[kk:kk + 1, :], w_ref[kk:kk + 1, :], 0.0)
    acc = _dot_tn(wmat.astype(BF16), ybuf[...].astype(BF16))
    xo = x1_ref[...] + g2_ref[...] * acc
    ms = jnp.mean(xo * xo, axis=-1, keepdims=True)
    hn = xo * lax.rsqrt(ms + EPS) * nw_ref[...]
    o_ref[...] = hn * (1.0 + sc_ref[...]) + sh_ref[...]


def _combine(seg_len, seg_src, seg_dst, pos, wrow, y, x1, g2, nw, shf, scf, seq, tt):
    t, d = x1.shape
    tps = seq // tt
    bmap = lambda i: (i // tps, 0, 0)
    nrows = TOP_K * tt + 8 * N_EXPERTS
    smem = pl.BlockSpec(memory_space=pltpu.SMEM)
    return pl.pallas_call(
        _combine_kernel,
        grid=(t // tt,),
        in_specs=[smem, smem, smem,
                  pl.BlockSpec((TOP_K, tt), lambda i: (0, i)),
                  pl.BlockSpec((TOP_K, tt), lambda i: (0, i)),
                  pl.BlockSpec(memory_space=pl.ANY),
                  pl.BlockSpec((tt, d), lambda i: (i, 0)),
                  pl.BlockSpec((None, 1, d), bmap),
                  pl.BlockSpec((1, d), lambda i: (0, 0)),
                  pl.BlockSpec((None, 1, d), bmap),
                  pl.BlockSpec((None, 1, d), bmap)],
        out_specs=pl.BlockSpec((tt, d), lambda i: (i, 0)),
        out_shape=jax.ShapeDtypeStruct((t, d), F32),
        scratch_shapes=[pltpu.VMEM((nrows, d), F32), pltpu.SemaphoreType.DMA(())],
        compiler_params=pltpu.CompilerParams(dimension_semantics=("arbitrary",),
                                             vmem_limit_bytes=V7X_VMEM_LIMIT),
        name="combine",
    )(seg_len, seg_src, seg_dst, pos, wrow, y, x1, g2, nw, shf, scf)


def _pad_heads(w, heads, dk):
    r = w.shape[0]
    w3 = w.reshape(r, heads, dk)
    return jnp.pad(w3, ((0, 0), (0, 0), (0, HEAD_LANES - dk))).reshape(r, heads * HEAD_LANES)


def _pick_tile(n, pref):
    tile = pref
    while n % tile:
        tile //= 2
    return tile


def kernel(x, c, w_ada, b_ada, norm_mix, w_in, dn_conv, dn_a_log, dn_dt_bias, dn_norm, ml_conv, ml_i_bias,
           ml_f_bias, ml_norm, w_out, norm_ffn, w_router, b_router, w_gate_up, b_gate_up, w_down, b_down,
           w_ada_final, b_ada_final, norm_final):
    batch, seq, d = x.shape
    assert w_ada.shape[0] == 1, "single-layer block"
    assert seq % CHUNK == 0
    t = batch * seq
    x2 = x.reshape(t, d)

    c_pad = jnp.pad(c, ((0, 8 - batch % 8 if batch % 8 else 0), (0, 0)))
    mod = _mods(c_pad, w_ada.reshape(d, 6 * d), b_ada.reshape(1, 6 * d))[:batch]
    modf = _mods(c_pad, w_ada_final, b_ada_final.reshape(1, 2 * d))[:batch]
    sh1, sc1, g1, sh2, sc2, g2 = [mod[:, None, j * d:(j + 1) * d] for j in range(6)]
    shf, scf = modf[:, None, 0:d], modf[:, None, d:2 * d]

    wi = w_in.reshape(d, -1)
    o_z = 1536
    o_b = 2048
    o_mq = 2056
    o_mk = o_mq + ML_HEADS * ML_DK
    o_mv = o_mk + ML_HEADS * ML_DK
    o_mo = o_mv + ML_HEADS * ML_DV
    o_mi = o_mo + ML_HEADS * ML_DV
    gates = jnp.concatenate([wi[:, o_b:o_mq], wi[:, o_mi:o_mi + 2 * ML_HEADS]], axis=1)
    w_new = jnp.concatenate([
        wi[:, 0:o_z],
        _pad_heads(wi[:, o_mq:o_mk], ML_HEADS, ML_DK),
        _pad_heads(wi[:, o_mk:o_mv], ML_HEADS, ML_DK),
        wi[:, o_z:o_b],
        wi[:, o_mv:o_mo],
        wi[:, o_mo:o_mi],
        jnp.pad(gates, ((0, 0), (0, HEAD_LANES - 16))),
    ], axis=1).astype(BF16)
    wgt = gates.T.astype(BF16)
    mlc = ml_conv.reshape(CONV_W, -1)
    cw = jnp.concatenate([dn_conv.reshape(CONV_W, -1),
                          _pad_heads(mlc[:, 0:ML_HEADS * ML_DK], ML_HEADS, ML_DK),
                          _pad_heads(mlc[:, ML_HEADS * ML_DK:], ML_HEADS, ML_DK)], axis=1)
    zeros4 = jnp.zeros((4,), F32)
    bias16 = jnp.concatenate([zeros4, dn_dt_bias.reshape(4), ml_i_bias.reshape(4), ml_f_bias.reshape(4)])
    alog16 = jnp.concatenate([zeros4, dn_a_log.reshape(4), zeros4, zeros4])
    gpc = jnp.zeros((8, HEAD_LANES), F32).at[0, 0:16].set(bias16).at[1, 0:16].set(alog16)
    gpr = jnp.zeros((16, HEAD_LANES), F32).at[:, 0].set(bias16).at[:, 1].set(alog16)

    tm_in = _pick_tile(seq, 256)
    conv_out, rest, gcol, grow = _inproj(x2, sh1, sc1, norm_mix.reshape(1, d), w_new, wgt, cw, gpc, gpr, seq, tm_in)
    grow3 = grow.reshape(16, t // CHUNK, CHUNK).transpose(1, 0, 2)

    rows = _pick_tile(seq, 512)
    ya = _deltanet(conv_out, rest, gcol, grow3, dn_norm.reshape(1, DN_DV), batch, seq, rows)
    yb = _mlstm(conv_out, rest, gcol, grow3, ml_norm.reshape(1, ML_HEADS * ML_DV), batch, seq, rows)

    wo = w_out.reshape(-1, d).astype(BF16)
    n_a = DN_HEADS * DN_DV
    tm_r = _pick_tile(seq, 512)
    brp = jnp.broadcast_to(b_router.reshape(N_EXPERTS, 1), (N_EXPERTS, HEAD_LANES))
    x1, h2, pos, wrow, len_col = _route(
        ya, yb, wo[0:n_a], wo[n_a:], x2, g1, sh2, sc2, norm_ffn.reshape(1, d),
        w_router.reshape(d, N_EXPERTS).T, brp, seq, tm_r)

    n_e = N_EXPERTS
    nt = t // tm_r
    bm = 256
    len_te = len_col.reshape(nt, n_e, HEAD_LANES)[:, :, 0]
    seg_len = jnp.pad(len_te, ((0, 0), (0, HEAD_LANES - n_e)))
    n_slots_max = t * TOP_K + n_e * (7 * nt + bm)
    nb = (n_slots_max + bm - 1) // bm
    seg_src, seg_dst, table = _slots(seg_len, bm)

    xs = _dispatch(seg_len, seg_src, seg_dst, table, pos, h2, nb * bm, tm_r)
    y = _experts(table, xs, w_gate_up.reshape(n_e, d, -1), b_gate_up.reshape(n_e, 1, -1),
                 w_down.reshape(n_e, -1, d), b_down.reshape(n_e, 1, d), bm)
    out = _combine(seg_len, seg_src, seg_dst, pos, wrow, y, x1, g2, norm_final.reshape(1, d), shf, scf, seq, tm_r)
    return out.reshape(batch, seq, d)
```

```python
import functools

import jax
import jax.numpy as jnp
from jax import lax
from jax.experimental import pallas as pl
from jax.experimental.pallas import tpu as pltpu

F32 = jnp.float32
BF16 = jnp.bfloat16

CHUNK = 64
CONV_W = 4
EPS = 1e-6

DN_HEADS = 4
DN_DK = 128
DN_DV = 128
ML_HEADS = 4
ML_DK = 64
ML_DV = 128
HEAD_LANES = 128

N_EXPERTS = 32
TOP_K = 4
SWIGLU_LIMIT = 7.0
SWIGLU_ALPHA = 1.702

C_DNQ = 0
C_DNK = 512
C_DNV = 1024
C_MLQ = 1536
C_MLK = 2048
N_CONV = 2560
C_DNZ = 2560
C_MLV = 3072
C_MLO = 3584
C_GATE = 4096
N_PROJ = 4224
N_REST = C_GATE - N_CONV

V7X_VMEM_LIMIT = 56 * 1024 * 1024

NEG_BIG = -1e30


def _sigmoid(x):
    return 1.0 / (1.0 + jnp.exp(-x))


def _softplus(x):
    return jnp.maximum(x, 0.0) + jnp.log(1.0 + jnp.exp(-jnp.abs(x)))


def _split3(v):
    hi = v.astype(BF16)
    r1 = v - hi.astype(F32)
    mid = r1.astype(BF16)
    lo = (r1 - mid.astype(F32)).astype(BF16)
    return hi, mid, lo


def _dot(a, b):
    return jnp.dot(a, b, preferred_element_type=F32)


def _dot_nt(a, b):
    return lax.dot_general(a, b, (((1,), (1,)), ((), ())), preferred_element_type=F32)


def _dot_tn(a, b):
    return lax.dot_general(a, b, (((0,), (0,)), ((), ())), preferred_element_type=F32)


def _dot_exact_right(sel_bf16, v):
    hi, mid, lo = _split3(v)
    return _dot(sel_bf16, hi) + _dot(sel_bf16, mid) + _dot(sel_bf16, lo)


def _dot_exact_left(v, sel_bf16):
    hi, mid, lo = _split3(v)
    return _dot(hi, sel_bf16) + _dot(mid, sel_bf16) + _dot(lo, sel_bf16)


def _mods_kernel(c_ref, w_ref, b_ref, o_ref):
    c = c_ref[...]
    cond = c * _sigmoid(c)
    ch, cm, cl = _split3(cond)
    wh, wm, wl = _split3(w_ref[...])
    acc = _dot(ch, wh) + (_dot(ch, wm) + _dot(cm, wh)) + (_dot(ch, wl) + _dot(cm, wm) + _dot(cl, wh))
    o_ref[...] = acc + b_ref[...]


def _mods(c_pad, w, b):
    m, d = c_pad.shape
    n = w.shape[1]
    tn = 1024
    return pl.pallas_call(
        _mods_kernel,
        grid=(n // tn,),
        in_specs=[pl.BlockSpec((m, d), lambda j: (0, 0)),
                  pl.BlockSpec((d, tn), lambda j: (0, j)),
                  pl.BlockSpec((1, tn), lambda j: (0, j))],
        out_specs=pl.BlockSpec((m, tn), lambda j: (0, j)),
        out_shape=jax.ShapeDtypeStruct((m, n), F32),
        compiler_params=pltpu.CompilerParams(dimension_semantics=("arbitrary",),
                                             vmem_limit_bytes=V7X_VMEM_LIMIT),
        name="mods",
    )(c_pad, w, b)


def _gate_transform(v, bias, alog, cls):
    vb = v + bias
    beta = _sigmoid(v)
    g = -jnp.exp(alog) * _softplus(vb)
    logf = -_softplus(-vb)
    return jnp.where(cls == 0, beta, jnp.where(cls == 1, g, jnp.where(cls == 2, vb, jnp.where(cls == 3, logf, 0.0))))


def _inproj_kernel(tiles_per_seq, x_ref, sh_ref, sc_ref, nw_ref, w_ref, wgt_ref, cw_ref, gpc_ref, gpr_ref,
                   conv_ref, rest_ref, gcol_ref, grow_ref, cbuf):
    tm = x_ref.shape[0]
    i = pl.program_id(0)
    x = x_ref[...]
    ms = jnp.mean(x * x, axis=-1, keepdims=True)
    h = x * lax.rsqrt(ms + EPS) * nw_ref[...]
    h = h * (1.0 + sc_ref[...]) + sh_ref[...]
    hb = h.astype(BF16)

    @pl.when(i % tiles_per_seq == 0)
    def _():
        cbuf[0:8, :] = jnp.zeros((8, N_CONV), F32)

    group = 4 * HEAD_LANES
    for lo in range(0, N_CONV, group):
        cols = slice(lo, lo + group)
        pc = _dot(hb, w_ref[:, cols])
        cbuf[8:tm + 8, cols] = pc
        acc = cw_ref[CONV_W - 1:CONV_W, cols] * pc
        for j in range(CONV_W - 1):
            acc = acc + cw_ref[j:j + 1, cols] * cbuf[8 - (CONV_W - 1) + j:8 - (CONV_W - 1) + j + tm, cols]
        cbuf[0:8, cols] = cbuf[tm:tm + 8, cols]
        y = acc * _sigmoid(acc)
        if lo in (C_DNQ, C_DNK):
            scale = DN_DK ** -0.5 if lo == C_DNQ else 1.0
            for hh in range(DN_HEADS):
                uh = y[:, hh * HEAD_LANES:(hh + 1) * HEAD_LANES]
                un = uh * lax.rsqrt(jnp.sum(uh * uh, axis=-1, keepdims=True) + EPS)
                conv_ref[:, lo + hh * HEAD_LANES:lo + (hh + 1) * HEAD_LANES] = un * scale if lo == C_DNQ else un
        elif lo == C_MLQ:
            conv_ref[:, cols] = y * (ML_DK ** -0.5)
        else:
            conv_ref[:, cols] = y

    z = _dot(hb, w_ref[:, C_DNZ:C_MLV])
    rest_ref[:, 0:512] = z * _sigmoid(z)
    rest_ref[:, 512:1024] = _dot(hb, w_ref[:, C_MLV:C_MLO])
    rest_ref[:, 1024:1536] = _sigmoid(_dot(hb, w_ref[:, C_MLO:C_GATE]))

    r_i = lax.broadcasted_iota(jnp.int32, (tm, tm), 0)
    c_i = lax.broadcasted_iota(jnp.int32, (tm, tm), 1)
    same_chunk = (r_i // CHUNK) == (c_i // CHUNK)
    tril = jnp.where(same_chunk & (c_i <= r_i), 1.0, 0.0).astype(BF16)
    triu = jnp.where(same_chunk & (r_i <= c_i), 1.0, 0.0).astype(BF16)

    gc = _dot(hb, w_ref[:, C_GATE:N_PROJ])
    cls_c = lax.broadcasted_iota(jnp.int32, (tm, HEAD_LANES), 1) // 4
    gt = _gate_transform(gc, gpc_ref[0:1, :], gpc_ref[1:2, :], cls_c)
    cs = _dot_exact_right(tril, gt)
    gcol_ref[...] = jnp.where((cls_c == 1) | (cls_c == 3), cs, gt)

    gr = _dot_nt(wgt_ref[...], hb)
    cls_r = lax.broadcasted_iota(jnp.int32, (16, tm), 0) // 4
    gtr = _gate_transform(gr, gpr_ref[:, 0:1], gpr_ref[:, 1:2], cls_r)
    csr = _dot_exact_left(gtr, triu)
    grow_ref[...] = jnp.where((cls_r == 1) | (cls_r == 3), csr, gtr)


def _inproj(x2, sh, sc, nw, w_new, wgt, cw, gpc, gpr, seq, tm):
    t, d = x2.shape
    tps = seq // tm
    kern = functools.partial(_inproj_kernel, tps)
    return pl.pallas_call(
        kern,
        grid=(t // tm,),
        in_specs=[pl.BlockSpec((tm, d), lambda i: (i, 0)),
                  pl.BlockSpec((None, 1, d), lambda i: (i // tps, 0, 0)),
                  pl.BlockSpec((None, 1, d), lambda i: (i // tps, 0, 0)),
                  pl.BlockSpec((1, d), lambda i: (0, 0)),
                  pl.BlockSpec((d, N_PROJ), lambda i: (0, 0)),
                  pl.BlockSpec((16, d), lambda i: (0, 0)),
                  pl.BlockSpec((CONV_W, N_CONV), lambda i: (0, 0)),
                  pl.BlockSpec((8, HEAD_LANES), lambda i: (0, 0)),
                  pl.BlockSpec((16, HEAD_LANES), lambda i: (0, 0))],
        out_specs=[pl.BlockSpec((tm, N_CONV), lambda i: (i, 0)),
                   pl.BlockSpec((tm, N_REST), lambda i: (i, 0)),
                   pl.BlockSpec((tm, HEAD_LANES), lambda i: (i, 0)),
                   pl.BlockSpec((16, tm), lambda i: (0, i))],
        out_shape=[jax.ShapeDtypeStruct((t, N_CONV), F32),
                   jax.ShapeDtypeStruct((t, N_REST), F32),
                   jax.ShapeDtypeStruct((t, HEAD_LANES), F32),
                   jax.ShapeDtypeStruct((16, t), F32)],
        scratch_shapes=[pltpu.VMEM((tm + 8, N_CONV), F32)],
        compiler_params=pltpu.CompilerParams(dimension_semantics=("arbitrary",),
                                             vmem_limit_bytes=V7X_VMEM_LIMIT),
        name="inproj",
    )(x2, sh, sc, nw, w_new, wgt, cw, gpc, gpr)


def _chunk_masks():
    r = lax.broadcasted_iota(jnp.int32, (CHUNK, CHUNK), 0)
    c = lax.broadcasted_iota(jnp.int32, (CHUNK, CHUNK), 1)
    return r >= c, r > c, r == c


def _bdot(a, b):
    return lax.dot_general(a, b, (((2,), (1,)), ((0,), (0,))), preferred_element_type=F32)


def _bdot_nt(a, b):
    return lax.dot_general(a, b, (((2,), (2,)), ((0,), (0,))), preferred_element_type=F32)


def _unit_lower_inverse(lower, row, col):
    x = jnp.where(row == col, 1.0, 0.0) - jnp.where((row >> 1) == (col >> 1), lower, 0.0)
    shift = 1
    while (1 << shift) < CHUNK:
        couple = ((row >> (shift + 1)) == (col >> (shift + 1))) & ((row >> shift) != (col >> shift))
        cb = jnp.where(couple, lower, 0.0).astype(BF16)
        xb = x.astype(BF16)
        x = x - _bdot(_bdot(xb, cb).astype(BF16), xb)
        shift += 1
    return x


def _deltanet_kernel(q_ref, k_ref, v_ref, gc_ref, gr_ref, z_ref, nw_ref, o_ref, s_ref):
    nc = gr_ref.shape[0]

    @pl.when(pl.program_id(1) == 0)
    def _():
        s_ref[...] = jnp.zeros(s_ref.shape, F32)

    row = lax.broadcasted_iota(jnp.int32, (CHUNK, CHUNK), 0)
    col = lax.broadcasted_iota(jnp.int32, (CHUNK, CHUNK), 1)
    incl = row >= col
    strict = row > col
    nw = nw_ref[...]
    gcc = gc_ref[...]
    grr = gr_ref[...]

    nh = DN_HEADS
    nb = nh * nc

    def heads(ref):
        return jnp.stack([ref[:, hh * HEAD_LANES:(hh + 1) * HEAD_LANES] for hh in range(nh)],
                         axis=0).reshape(nb, CHUNK, HEAD_LANES)

    def col_gate(lane0):
        return jnp.stack([gcc[:, lane0 + hh:lane0 + hh + 1] for hh in range(nh)], axis=0).reshape(nb, CHUNK, 1)

    q = heads(q_ref)
    k = heads(k_ref)
    v = heads(v_ref)
    beta = col_gate(0)
    g_c = col_gate(4)
    g_r = jnp.stack([grr[:, 4 + hh:5 + hh, :] for hh in range(nh)], axis=0).reshape(nb, 1, CHUNK)
    g_last = g_c[:, CHUNK - 1:CHUNK, :]
    decay = jnp.exp(jnp.where(incl, g_c - g_r, NEG_BIG))
    kb = k.astype(BF16)
    kk = _bdot_nt(kb, kb)
    lower = jnp.where(strict, beta * kk * decay, 0.0)
    tinv = _unit_lower_inverse(lower, row, col)
    eg = jnp.exp(g_c)
    rhs = jnp.concatenate([v * beta, k * (beta * eg)], axis=-1)
    sol = _bdot(tinv.astype(BF16), rhs.astype(BF16))
    w_val = sol[:, :, 0:DN_DV].reshape(nh, nc, CHUNK, DN_DV)
    kq = jnp.concatenate([sol[:, :, DN_DV:DN_DV + DN_DK], q * eg], axis=1).astype(BF16)
    kq = kq.reshape(nh, nc, 2 * CHUNK, DN_DK)
    qk = (_bdot_nt(q.astype(BF16), kb) * decay).astype(BF16).reshape(nh, nc, CHUNK, CHUNK)
    k_dec_t = jnp.swapaxes(k * jnp.exp(g_last - g_c), 1, 2).astype(BF16).reshape(nh, nc, DN_DK, CHUNK)
    s_dec = jnp.exp(g_last).reshape(nh, nc, 1, 1)

    state = s_ref[...]
    outs = []
    for c in range(nc):
        both = _bdot(kq[:, c], state.astype(BF16))
        v_new = w_val[:, c] - both[:, 0:CHUNK]
        vb = v_new.astype(BF16)
        outs.append(both[:, CHUNK:2 * CHUNK] + _bdot(qk[:, c], vb))
        state = s_dec[:, c] * state + _bdot(k_dec_t[:, c], vb)
    s_ref[...] = state

    o = jnp.stack(outs, axis=1)
    on = o * lax.rsqrt(jnp.mean(o * o, axis=-1, keepdims=True) + EPS) * nw
    on = on.reshape(nh, nc * CHUNK, DN_DV)
    for hh in range(nh):
        lanes = slice(hh * HEAD_LANES, (hh + 1) * HEAD_LANES)
        o_ref[:, lanes] = on[hh] * z_ref[:, lanes]


def _deltanet(conv_out, rest, gcol, grow3, nw, batch, seq, rows):
    t = conv_out.shape[0]
    nj = seq // rows
    cpb = rows // CHUNK
    width = DN_HEADS * HEAD_LANES

    def rmap(col):
        return lambda b, j: (b * nj + j, col)

    return pl.pallas_call(
        _deltanet_kernel,
        grid=(batch, nj),
        in_specs=[pl.BlockSpec((rows, width), rmap(C_DNQ // width)),
                  pl.BlockSpec((rows, width), rmap(C_DNK // width)),
                  pl.BlockSpec((rows, width), rmap(C_DNV // width)),
                  pl.BlockSpec((rows, HEAD_LANES), rmap(0)),
                  pl.BlockSpec((cpb, 16, CHUNK), lambda b, j: (b * nj + j, 0, 0)),
                  pl.BlockSpec((rows, width), rmap((C_DNZ - N_CONV) // width)),
                  pl.BlockSpec((1, HEAD_LANES), lambda b, j: (0, 0))],
        out_specs=pl.BlockSpec((rows, width), rmap(0)),
        out_shape=jax.ShapeDtypeStruct((t, width), F32),
        scratch_shapes=[pltpu.VMEM((DN_HEADS, DN_DK, DN_DV), F32)],
        compiler_params=pltpu.CompilerParams(dimension_semantics=("arbitrary", "arbitrary"),
                                             vmem_limit_bytes=V7X_VMEM_LIMIT),
        name="deltanet",
    )(conv_out, conv_out, conv_out, gcol, grow3, rest, nw)


def _mlstm_kernel(q_ref, k_ref, v_ref, gc_ref, gr_ref, og_ref, nw_ref, o_ref, c_ref, n_ref, m_ref):
    n_chunks = gr_ref.shape[0]

    @pl.when(pl.program_id(1) == 0)
    def _():
        c_ref[...] = jnp.zeros(c_ref.shape, F32)
        n_ref[...] = jnp.zeros(n_ref.shape, F32)
        m_ref[...] = jnp.zeros(m_ref.shape, F32)

    incl, _, _ = _chunk_masks()
    nc = n_chunks
    gcc = gc_ref[...]
    grr = gr_ref[...]

    nh = ML_HEADS

    def heads(ref):
        return jnp.stack([ref[:, hh * HEAD_LANES:(hh + 1) * HEAD_LANES] for hh in range(nh)],
                         axis=0).reshape(nh, nc, CHUNK, HEAD_LANES)

    def col_gate(lane0):
        return jnp.stack([gcc[:, lane0 + hh:lane0 + hh + 1] for hh in range(nh)], axis=0).reshape(nh, nc, CHUNK, 1)

    def row_gate(row0):
        return jnp.stack([grr[:, row0 + hh:row0 + hh + 1, :] for hh in range(nh)], axis=0)

    q = heads(q_ref)
    k = heads(k_ref)
    v = heads(v_ref)
    i_c = col_gate(8)
    b_c = col_gate(12)
    i_r = row_gate(8)
    b_r = row_gate(12)
    b_last = b_c[:, :, CHUNK - 1:CHUNK, :]
    d_mat = jnp.where(incl, b_c - b_r + i_r, NEG_BIG)
    m_intra = jnp.max(d_mat, axis=-1, keepdims=True)
    g_end = b_last - b_c + i_c
    g_end_max = jnp.max(g_end, axis=2, keepdims=True)

    m_run = m_ref[:, 0:1, 0:1].reshape(nh, 1, 1, 1)
    m_before = []
    for c in range(nc):
        m_before.append(m_run)
        m_run = jnp.maximum(b_last[:, c:c + 1] + m_run, g_end_max[:, c:c + 1])
    m_s = jnp.concatenate(m_before, axis=1)
    m_new = jnp.maximum(b_last + m_s, g_end_max)
    keep = jnp.exp(b_last + m_s - m_new)

    nb = nh * nc
    qb = q.astype(BF16)
    kb = k.astype(BF16)
    vb = v.astype(BF16)
    m_t = jnp.maximum(b_c + m_s, m_intra)
    inter = jnp.exp(b_c + m_s - m_t)
    qk = _bdot_nt(qb.reshape(nb, CHUNK, HEAD_LANES), kb.reshape(nb, CHUNK, HEAD_LANES))
    p = jnp.exp(d_mat - m_t) * qk.reshape(nh, nc, CHUNK, CHUNK)
    intra = _bdot(p.astype(BF16).reshape(nb, CHUNK, CHUNK), vb.reshape(nb, CHUNK, ML_DV)).reshape(nh, nc, CHUNK, ML_DV)
    p_sum = jnp.sum(p, axis=-1, keepdims=True)
    kw = k * jnp.exp(g_end - m_new)
    kw_t = jnp.swapaxes(kw.reshape(nb, CHUNK, HEAD_LANES), 1, 2).astype(BF16)
    d_state = _bdot(kw_t, vb.reshape(nb, CHUNK, ML_DV)).reshape(nh, nc, HEAD_LANES, ML_DV)
    kw_sum = jnp.sum(kw, axis=2, keepdims=True)

    c_s = c_ref[...]
    n_s = n_ref[:, 0:1, :]
    q_c = []
    q_n = []
    for c in range(nc):
        q_c.append(_bdot(qb[:, c], c_s.astype(BF16)))
        q_n.append(jnp.sum(q[:, c] * n_s, axis=-1, keepdims=True))
        c_s = keep[:, c] * c_s + d_state[:, c]
        n_s = keep[:, c] * n_s + kw_sum[:, c]
    c_ref[...] = c_s
    n_ref[...] = jnp.broadcast_to(n_s, (nh, 8, HEAD_LANES))
    m_ref[...] = jnp.broadcast_to(m_run.reshape(nh, 1, 1), (nh, 8, HEAD_LANES))

    num = inter * jnp.stack(q_c, axis=1) + intra
    den = inter * jnp.stack(q_n, axis=1) + p_sum
    h = num / jnp.maximum(jnp.abs(den), jnp.exp(-m_t))
    hr = h * lax.rsqrt(jnp.mean(h * h, axis=-1, keepdims=True) + EPS)
    hr = hr.reshape(nh, nc * CHUNK, ML_DV)
    for hh in range(nh):
        lanes = slice(hh * HEAD_LANES, (hh + 1) * HEAD_LANES)
        o_ref[:, lanes] = hr[hh] * nw_ref[:, lanes] * og_ref[:, lanes]


def _mlstm(conv_out, rest, gcol, grow3, nw, batch, seq, rows):
    t = conv_out.shape[0]
    nj = seq // rows
    cpb = rows // CHUNK
    width = ML_HEADS * HEAD_LANES

    def rmap(col):
        return lambda b, j: (b * nj + j, col)

    return pl.pallas_call(
        _mlstm_kernel,
        grid=(batch, nj),
        in_specs=[pl.BlockSpec((rows, width), rmap(C_MLQ // width)),
                  pl.BlockSpec((rows, width), rmap(C_MLK // width)),
                  pl.BlockSpec((rows, width), rmap((C_MLV - N_CONV) // width)),
                  pl.BlockSpec((rows, HEAD_LANES), rmap(0)),
                  pl.BlockSpec((cpb, 16, CHUNK), lambda b, j: (b * nj + j, 0, 0)),
                  pl.BlockSpec((rows, width), rmap((C_MLO - N_CONV) // width)),
                  pl.BlockSpec((1, width), lambda b, j: (0, 0))],
        out_specs=pl.BlockSpec((rows, width), rmap(0)),
        out_shape=jax.ShapeDtypeStruct((t, width), F32),
        scratch_shapes=[pltpu.VMEM((ML_HEADS, HEAD_LANES, ML_DV), F32),
                        pltpu.VMEM((ML_HEADS, 8, HEAD_LANES), F32),
                        pltpu.VMEM((ML_HEADS, 8, HEAD_LANES), F32)],
        compiler_params=pltpu.CompilerParams(dimension_semantics=("arbitrary", "arbitrary"),
                                             vmem_limit_bytes=V7X_VMEM_LIMIT),
        name="mlstm",
    )(conv_out, conv_out, rest, gcol, grow3, rest, nw)


def _route_kernel(ya_ref, yb_ref, woa_ref, wob_ref, x_ref, g1_ref, sh_ref, sc_ref, nw_ref, wrt_ref, br_ref,
                  x1_ref, h2_ref, pos_ref, wrow_ref, len_ref):
    tm = x_ref.shape[0]

    mix = _dot(ya_ref[...].astype(BF16), woa_ref[...]) + _dot(yb_ref[...].astype(BF16), wob_ref[...])
    x1 = x_ref[...] + g1_ref[...] * mix
    x1_ref[...] = x1
    ms = jnp.mean(x1 * x1, axis=-1, keepdims=True)
    h2 = x1 * lax.rsqrt(ms + EPS) * nw_ref[...]
    h2 = h2 * (1.0 + sc_ref[...]) + sh_ref[...]
    h2_ref[...] = h2.astype(BF16)

    hh, hm, _ = _split3(h2)
    wh, wm, _ = _split3(wrt_ref[...])
    logits = _dot_nt(wh, hh) + (_dot_nt(wh, hm) + _dot_nt(wm, hh)) + br_ref[:, 0:1]

    e_i = lax.broadcasted_iota(jnp.int32, (N_EXPERTS, tm), 0)
    work = logits
    tops = []
    sels = []
    hots = []
    for _ in range(TOP_K):
        m = jnp.max(work, axis=0, keepdims=True)
        sel = jnp.min(jnp.where(work == m, e_i, N_EXPERTS), axis=0, keepdims=True)
        hot = e_i == sel
        work = jnp.where(hot, NEG_BIG, work)
        tops.append(m)
        sels.append(sel)
        hots.append(hot)
    exps = [jnp.exp(tl - tops[0]) for tl in tops]
    denom = exps[0] + exps[1] + exps[2] + exps[3]
    ws = [e / denom for e in exps]

    chosen = jnp.zeros((N_EXPERTS, tm), F32)
    for hot in hots:
        chosen = chosen + jnp.where(hot, 1.0, 0.0)

    r_i = lax.broadcasted_iota(jnp.int32, (tm, tm), 0)
    c_i = lax.broadcasted_iota(jnp.int32, (tm, tm), 1)
    strict_upper = jnp.where(r_i < c_i, 1.0, 0.0).astype(BF16)
    prefix = _dot(chosen.astype(BF16), strict_upper)
    n_e = jnp.sum(chosen, axis=1, keepdims=True)
    len8 = jnp.ceil(n_e * 0.125) * 8.0
    er = lax.broadcasted_iota(jnp.int32, (N_EXPERTS, N_EXPERTS), 0)
    ec = lax.broadcasted_iota(jnp.int32, (N_EXPERTS, N_EXPERTS), 1)
    strict_lower = jnp.where(ec < er, 1.0, 0.0).astype(BF16)
    len8b = jnp.broadcast_to(len8, (N_EXPERTS, HEAD_LANES))
    off8 = _dot_exact_right(strict_lower, len8b)[:, 0:1]
    len_ref[...] = len8b.astype(jnp.int32)

    for kk in range(TOP_K):
        wrow_ref[kk:kk + 1, :] = ws[kk]
        pos = jnp.sum(jnp.where(hots[kk], prefix + off8, 0.0), axis=0, keepdims=True)
        pos_ref[kk:kk + 1, :] = pos.astype(jnp.int32)


def _route(ya, yb, woa, wob, x2, g1, sh2, sc2, nw, wrt, br, seq, tm):
    t, d = x2.shape
    tps = seq // tm
    bmap = lambda i: (i // tps, 0, 0)
    return pl.pallas_call(
        _route_kernel,
        grid=(t // tm,),
        in_specs=[pl.BlockSpec((tm, ya.shape[1]), lambda i: (i, 0)),
                  pl.BlockSpec((tm, yb.shape[1]), lambda i: (i, 0)),
                  pl.BlockSpec(woa.shape, lambda i: (0, 0)),
                  pl.BlockSpec(wob.shape, lambda i: (0, 0)),
                  pl.BlockSpec((tm, d), lambda i: (i, 0)),
                  pl.BlockSpec((None, 1, d), bmap),
                  pl.BlockSpec((None, 1, d), bmap),
                  pl.BlockSpec((None, 1, d), bmap),
                  pl.BlockSpec((1, d), lambda i: (0, 0)),
                  pl.BlockSpec((N_EXPERTS, d), lambda i: (0, 0)),
                  pl.BlockSpec((N_EXPERTS, HEAD_LANES), lambda i: (0, 0))],
        out_specs=[pl.BlockSpec((tm, d), lambda i: (i, 0)),
                   pl.BlockSpec((tm, d), lambda i: (i, 0)),
                   pl.BlockSpec((TOP_K, tm), lambda i: (0, i)),
                   pl.BlockSpec((TOP_K, tm), lambda i: (0, i)),
                   pl.BlockSpec((N_EXPERTS, HEAD_LANES), lambda i: (i, 0))],
        out_shape=[jax.ShapeDtypeStruct((t, d), F32),
                   jax.ShapeDtypeStruct((t, d), BF16),
                   jax.ShapeDtypeStruct((TOP_K, t), jnp.int32),
                   jax.ShapeDtypeStruct((TOP_K, t), F32),
                   jax.ShapeDtypeStruct((t // tm * N_EXPERTS, HEAD_LANES), jnp.int32)],
        compiler_params=pltpu.CompilerParams(dimension_semantics=("arbitrary",),
                                             vmem_limit_bytes=V7X_VMEM_LIMIT),
        name="route",
    )(ya, yb, woa, wob, x2, g1, sh2, sc2, nw, wrt, br)


TAIL_START, TAIL_LEN, PAD_END, BLOCK_START, BLOCK_COUNT = range(5)


def _slots_kernel(bm, len_te_ref, ssrc_ref, sdst_ref, tab_ref):
    nt, lanes = len_te_ref.shape
    len_te = len_te_ref[...].astype(F32)

    r_l = lax.broadcasted_iota(jnp.int32, (lanes, lanes), 0)
    c_l = lax.broadcasted_iota(jnp.int32, (lanes, lanes), 1)
    upper_incl = jnp.where(r_l <= c_l, 1.0, 0.0).astype(BF16)
    upper_strict = jnp.where(r_l < c_l, 1.0, 0.0).astype(BF16)
    total_r = jnp.sum(len_te, axis=0, keepdims=True)
    padded_r = jnp.ceil(total_r * (1.0 / bm)) * bm
    pad_end_r = _dot_exact_left(jnp.broadcast_to(padded_r, (8, lanes)), upper_incl)[0:1, :]
    pad_start_r = pad_end_r - padded_r
    r_t = lax.broadcasted_iota(jnp.int32, (nt, nt), 0)
    c_t = lax.broadcasted_iota(jnp.int32, (nt, nt), 1)
    lower_strict_t = jnp.where(c_t < r_t, 1.0, 0.0).astype(BF16)
    before = _dot_exact_right(lower_strict_t, len_te)
    ssrc_ref[...] = _dot_exact_left(len_te, upper_strict).astype(jnp.int32)
    sdst_ref[...] = (pad_start_r + before).astype(jnp.int32)
    tab_ref[TAIL_START:TAIL_START + 1, :] = (pad_start_r + total_r).astype(jnp.int32)
    tab_ref[TAIL_LEN:TAIL_LEN + 1, :] = (padded_r - total_r).astype(jnp.int32)
    tab_ref[PAD_END:PAD_END + 1, :] = pad_end_r.astype(jnp.int32)
    tab_ref[BLOCK_START:BLOCK_START + 1, :] = (pad_start_r * (1.0 / bm)).astype(jnp.int32)
    tab_ref[BLOCK_COUNT:BLOCK_COUNT + 1, :] = (padded_r * (1.0 / bm)).astype(jnp.int32)
    tab_ref[5:8, :] = jnp.zeros((3, lanes), jnp.int32)


def _slots(len_te, bm):
    assert bm & (bm - 1) == 0, "block rows must be a power of two"
    nt, lanes = len_te.shape
    return pl.pallas_call(
        functools.partial(_slots_kernel, bm),
        out_shape=[jax.ShapeDtypeStruct((nt, lanes), jnp.int32),
                   jax.ShapeDtypeStruct((nt, lanes), jnp.int32),
                   jax.ShapeDtypeStruct((8, lanes), jnp.int32)],
        compiler_params=pltpu.CompilerParams(vmem_limit_bytes=V7X_VMEM_LIMIT),
        name="slots",
    )(len_te)


def _segment_pieces(max_rows):
    sizes = []
    s = 8
    while s <= max_rows:
        sizes.append(s)
        s *= 2
    return sizes[::-1]


def _segment_dma(src_ref, src0, dst_ref, dst0, nrows, sizes, sem, start):
    off = 0
    for sz in sizes:
        bit = nrows & sz

        @pl.when(bit != 0)
        def _(off=off, sz=sz):
            s0 = pl.multiple_of(src0 + off, 8)
            d0 = pl.multiple_of(dst0 + off, 8)
            cp = pltpu.make_async_copy(src_ref.at[pl.ds(s0, sz)], dst_ref.at[pl.ds(d0, sz)], sem)
            if start:
                cp.start()
            else:
                cp.wait()

        off = off + bit


def _sorted_onehot(pos_ref, nrows, tt):
    r_i = lax.broadcasted_iota(jnp.int32, (nrows, tt), 0)
    hit = r_i == pos_ref[0:1, :]
    for kk in range(1, TOP_K):
        hit = hit | (r_i == pos_ref[kk:kk + 1, :])
    return hit


def _dispatch_kernel(bm, len_ref, ssrc_ref, sdst_ref, tail_ref, pos_ref, h2_ref, xs_hbm, buf, sem):
    i = pl.program_id(0)
    nt = pl.num_programs(0)
    tt = h2_ref.shape[0]
    nrows = buf.shape[1]
    sizes = _segment_pieces(tt)
    slot = i % 2

    perm = jnp.where(_sorted_onehot(pos_ref, nrows, tt), 1.0, 0.0).astype(BF16)
    buf[slot] = _dot(perm, h2_ref[...])

    def seg(tile, which, start):
        def body(e, carry):
            _segment_dma(buf.at[which], ssrc_ref[tile, e], xs_hbm, sdst_ref[tile, e], len_ref[tile, e], sizes,
                         sem.at[which], start)
            return carry
        lax.fori_loop(0, N_EXPERTS, body, 0)

    @pl.when(i > 0)
    def _():
        seg(i - 1, 1 - slot, False)

    seg(i, slot, True)

    @pl.when(i == nt - 1)
    def _():
        seg(i, slot, False)
        zrows = bm
        zbuf = buf.at[0]
        zsem = sem.at[0]
        buf[0, 0:zrows, :] = jnp.zeros((zrows, buf.shape[2]), F32)

        def tail(start):
            def body(e, carry):
                _segment_dma(zbuf, 0, xs_hbm, tail_ref[TAIL_START, e], tail_ref[TAIL_LEN, e],
                             _segment_pieces(zrows // 2), zsem, start)
                return carry
            lax.fori_loop(0, N_EXPERTS, body, 0)

        tail(True)
        tail(False)

        used = tail_ref[PAD_END, tail_ref.shape[1] - 1]
        n_unused = (xs_hbm.shape[0] - used) // zrows

        def unused_copy(j):
            d0 = pl.multiple_of(used + j * zrows, 8)
            return pltpu.make_async_copy(zbuf.at[pl.ds(0, zrows)], xs_hbm.at[pl.ds(d0, zrows)], zsem)

        def unused_start(j, carry):
            unused_copy(j).start()
            return carry

        def unused_wait(j, carry):
            unused_copy(j).wait()
            return carry

        lax.fori_loop(0, n_unused, unused_start, 0)
        lax.fori_loop(0, n_unused, unused_wait, 0)


def _dispatch(seg_len, seg_src, seg_dst, tail, pos, h2, n_slots, tt, bm):
    t, d = h2.shape
    nrows = TOP_K * tt + 8 * N_EXPERTS
    assert bm <= nrows
    smem = pl.BlockSpec(memory_space=pltpu.SMEM)
    return pl.pallas_call(
        functools.partial(_dispatch_kernel, bm),
        grid=(t // tt,),
        in_specs=[smem, smem, smem, smem,
                  pl.BlockSpec((TOP_K, tt), lambda i: (0, i)),
                  pl.BlockSpec((tt, d), lambda i: (i, 0))],
        out_specs=pl.BlockSpec(memory_space=pl.ANY),
        out_shape=jax.ShapeDtypeStruct((n_slots, d), F32),
        scratch_shapes=[pltpu.VMEM((2, nrows, d), F32), pltpu.SemaphoreType.DMA((2,))],
        compiler_params=pltpu.CompilerParams(dimension_semantics=("arbitrary",), has_side_effects=True,
                                             vmem_limit_bytes=V7X_VMEM_LIMIT),
        name="dispatch",
    )(seg_len, seg_src, seg_dst, tail, pos, h2)


def _experts_kernel(tab_ref, xs_hbm, wgu_ref, bgu_ref, wd_ref, bd_ref, y_hbm, wgu_s, wd_s, xbuf, ybuf, sem_in,
                    sem_out):
    e = pl.program_id(0)
    bm = xbuf.shape[1]
    d_ff = wd_ref.shape[0]
    first = tab_ref[BLOCK_START, e]
    count = tab_ref[BLOCK_COUNT, e]

    def rows(j):
        return pl.ds(pl.multiple_of((first + j) * bm, bm), bm)

    def x_copy(j, slot):
        return pltpu.make_async_copy(xs_hbm.at[rows(j)], xbuf.at[slot], sem_in.at[slot])

    def y_copy(j, slot):
        return pltpu.make_async_copy(ybuf.at[slot], y_hbm.at[rows(j)], sem_out.at[slot])

    @pl.when(count > 0)
    def _():
        x_copy(0, 0).start()

    wgu_s[...] = wgu_ref[...].astype(BF16)
    wd_s[...] = wd_ref[...].astype(BF16)

    def block(j, carry):
        slot = j % 2
        x_copy(j, slot).wait()

        @pl.when(j + 1 < count)
        def _():
            x_copy(j + 1, 1 - slot).start()

        @pl.when(j >= 2)
        def _():
            y_copy(j - 2, slot).wait()

        xb = xbuf[slot].astype(BF16)
        fc = 512
        acc = None
        for f in range(d_ff // fc):
            gcols = slice(f * fc, (f + 1) * fc)
            ucols = slice(d_ff + f * fc, d_ff + (f + 1) * fc)
            gate = jnp.minimum(_dot(xb, wgu_s[:, gcols]) + bgu_ref[:, gcols], SWIGLU_LIMIT)
            up = jnp.clip(_dot(xb, wgu_s[:, ucols]) + bgu_ref[:, ucols], -SWIGLU_LIMIT, SWIGLU_LIMIT)
            act = (up + 1.0) * gate * _sigmoid(SWIGLU_ALPHA * gate)
            part = _dot(act.astype(BF16), wd_s[gcols, :])
            acc = part if acc is None else acc + part
        ybuf[slot] = acc + bd_ref[...]
        y_copy(j, slot).start()
        return carry

    lax.fori_loop(0, count, block, 0)

    @pl.when(count >= 2)
    def _():
        y_copy(count - 2, count % 2).wait()

    @pl.when(count >= 1)
    def _():
        y_copy(count - 1, (count - 1) % 2).wait()

    @pl.when(e == pl.num_programs(0) - 1)
    def _():
        used = first + count
        n_unused = y_hbm.shape[0] // bm - used
        ybuf[0] = jnp.zeros(ybuf.shape[1:], F32)

        def z_copy(j):
            return pltpu.make_async_copy(ybuf.at[0], y_hbm.at[rows(count + j)], sem_out.at[0])

        def z_start(j, carry):
            z_copy(j).start()
            return carry

        def z_wait(j, carry):
            z_copy(j).wait()
            return carry

        lax.fori_loop(0, n_unused, z_start, 0)
        lax.fori_loop(0, n_unused, z_wait, 0)


def _experts(table, xs, wgu, bgu, wd, bd, bm):
    ns, d = xs.shape
    n_e, _, two_ff = wgu.shape
    d_ff = two_ff // 2
    wmap = lambda e: (e, 0, 0)
    return pl.pallas_call(
        _experts_kernel,
        grid=(n_e,),
        in_specs=[pl.BlockSpec(memory_space=pltpu.SMEM),
                  pl.BlockSpec(memory_space=pl.ANY),
                  pl.BlockSpec((None, d, two_ff), wmap),
                  pl.BlockSpec((None, 1, two_ff), wmap),
                  pl.BlockSpec((None, d_ff, d), wmap),
                  pl.BlockSpec((None, 1, d), wmap)],
        out_specs=pl.BlockSpec(memory_space=pl.ANY),
        out_shape=jax.ShapeDtypeStruct((ns, d), F32),
        scratch_shapes=[pltpu.VMEM((d, two_ff), BF16), pltpu.VMEM((d_ff, d), BF16),
                        pltpu.VMEM((2, bm, d), F32), pltpu.VMEM((2, bm, d), F32),
                        pltpu.SemaphoreType.DMA((2,)), pltpu.SemaphoreType.DMA((2,))],
        compiler_params=pltpu.CompilerParams(dimension_semantics=("arbitrary",), has_side_effects=True,
                                             vmem_limit_bytes=V7X_VMEM_LIMIT),
        name="experts",
    )(table, xs, wgu, bgu, wd, bd)


def _combine_kernel(len_ref, ssrc_ref, sdst_ref, pos_ref, w_ref, y_hbm, x1_ref, g2_ref, nw_ref, sh_ref, sc_ref,
                    o_ref, ybuf, sem):
    i = pl.program_id(0)
    nt = pl.num_programs(0)
    tt = x1_ref.shape[0]
    nrows = ybuf.shape[1]
    sizes = _segment_pieces(tt)
    slot = i % 2

    def seg(tile, which, start):
        def body(e, carry):
            _segment_dma(y_hbm, sdst_ref[tile, e], ybuf.at[which], ssrc_ref[tile, e], len_ref[tile, e], sizes,
                         sem.at[which], start)
            return carry
        lax.fori_loop(0, N_EXPERTS, body, 0)

    @pl.when(i == 0)
    def _():
        ybuf[...] = jnp.zeros(ybuf.shape, F32)
        seg(0, 0, True)

    @pl.when(i + 1 < nt)
    def _():
        seg(i + 1, 1 - slot, True)

    seg(i, slot, False)

    r_i = lax.broadcasted_iota(jnp.int32, (nrows, tt), 0)
    wmat = jnp.where(r_i == pos_ref[0:1, :], w_ref[0:1, :], 0.0)
    for kk in range(1, TOP_K):
        wmat = wmat + jnp.where(r_i == pos_ref[kk:kk + 1, :], w_ref[kk:kk + 1, :], 0.0)
    acc = _dot_tn(wmat.astype(BF16), ybuf[slot].astype(BF16))
    xo = x1_ref[...] + g2_ref[...] * acc
    ms = jnp.mean(xo * xo, axis=-1, keepdims=True)
    hn = xo * lax.rsqrt(ms + EPS) * nw_ref[...]
    o_ref[...] = hn * (1.0 + sc_ref[...]) + sh_ref[...]


def _combine(seg_len, seg_src, seg_dst, pos, wrow, y, x1, g2, nw, shf, scf, seq, tt):
    t, d = x1.shape
    tps = seq // tt
    bmap = lambda i: (i // tps, 0, 0)
    nrows = TOP_K * tt + 8 * N_EXPERTS
    smem = pl.BlockSpec(memory_space=pltpu.SMEM)
    return pl.pallas_call(
        _combine_kernel,
        grid=(t // tt,),
        in_specs=[smem, smem, smem,
                  pl.BlockSpec((TOP_K, tt), lambda i: (0, i)),
                  pl.BlockSpec((TOP_K, tt), lambda i: (0, i)),
                  pl.BlockSpec(memory_space=pl.ANY),
                  pl.BlockSpec((tt, d), lambda i: (i, 0)),
                  pl.BlockSpec((None, 1, d), bmap),
                  pl.BlockSpec((1, d), lambda i: (0, 0)),
                  pl.BlockSpec((None, 1, d), bmap),
                  pl.BlockSpec((None, 1, d), bmap)],
        out_specs=pl.BlockSpec((tt, d), lambda i: (i, 0)),
        out_shape=jax.ShapeDtypeStruct((t, d), F32),
        scratch_shapes=[pltpu.VMEM((2, nrows, d), F32), pltpu.SemaphoreType.DMA((2,))],
        compiler_params=pltpu.CompilerParams(dimension_semantics=("arbitrary",),
                                             vmem_limit_bytes=V7X_VMEM_LIMIT),
        name="combine",
    )(seg_len, seg_src, seg_dst, pos, wrow, y, x1, g2, nw, shf, scf)


def _pad_heads(w, heads, dk):
    r = w.shape[0]
    w3 = w.reshape(r, heads, dk)
    return jnp.pad(w3, ((0, 0), (0, 0), (0, HEAD_LANES - dk))).reshape(r, heads * HEAD_LANES)


def _pick_tile(n, pref):
    tile = pref
    while n % tile:
        tile //= 2
    return tile


def kernel(x, c, w_ada, b_ada, norm_mix, w_in, dn_conv, dn_a_log, dn_dt_bias, dn_norm, ml_conv, ml_i_bias,
           ml_f_bias, ml_norm, w_out, norm_ffn, w_router, b_router, w_gate_up, b_gate_up, w_down, b_down,
           w_ada_final, b_ada_final, norm_final):
    batch, seq, d = x.shape
    assert w_ada.shape[0] == 1, "single-layer block"
    assert seq % CHUNK == 0
    t = batch * seq
    x2 = x.reshape(t, d)

    c_pad = jnp.pad(c, ((0, 8 - batch % 8 if batch % 8 else 0), (0, 0)))
    mod = _mods(c_pad, w_ada.reshape(d, 6 * d), b_ada.reshape(1, 6 * d))[:batch]
    modf = _mods(c_pad, w_ada_final, b_ada_final.reshape(1, 2 * d))[:batch]
    sh1, sc1, g1, sh2, sc2, g2 = [mod[:, None, j * d:(j + 1) * d] for j in range(6)]
    shf, scf = modf[:, None, 0:d], modf[:, None, d:2 * d]

    wi = w_in.reshape(d, -1)
    o_z = 1536
    o_b = 2048
    o_mq = 2056
    o_mk = o_mq + ML_HEADS * ML_DK
    o_mv = o_mk + ML_HEADS * ML_DK
    o_mo = o_mv + ML_HEADS * ML_DV
    o_mi = o_mo + ML_HEADS * ML_DV
    gates = jnp.concatenate([wi[:, o_b:o_mq], wi[:, o_mi:o_mi + 2 * ML_HEADS]], axis=1)
    w_new = jnp.concatenate([
        wi[:, 0:o_z],
        _pad_heads(wi[:, o_mq:o_mk], ML_HEADS, ML_DK),
        _pad_heads(wi[:, o_mk:o_mv], ML_HEADS, ML_DK),
        wi[:, o_z:o_b],
        wi[:, o_mv:o_mo],
        wi[:, o_mo:o_mi],
        jnp.pad(gates, ((0, 0), (0, HEAD_LANES - 16))),
    ], axis=1).astype(BF16)
    wgt = gates.T.astype(BF16)
    mlc = ml_conv.reshape(CONV_W, -1)
    cw = jnp.concatenate([dn_conv.reshape(CONV_W, -1),
                          _pad_heads(mlc[:, 0:ML_HEADS * ML_DK], ML_HEADS, ML_DK),
                          _pad_heads(mlc[:, ML_HEADS * ML_DK:], ML_HEADS, ML_DK)], axis=1)
    zeros4 = jnp.zeros((4,), F32)
    bias16 = jnp.concatenate([zeros4, dn_dt_bias.reshape(4), ml_i_bias.reshape(4), ml_f_bias.reshape(4)])
    alog16 = jnp.concatenate([zeros4, dn_a_log.reshape(4), zeros4, zeros4])
    gpc = jnp.zeros((8, HEAD_LANES), F32).at[0, 0:16].set(bias16).at[1, 0:16].set(alog16)
    gpr = jnp.zeros((16, HEAD_LANES), F32).at[:, 0].set(bias16).at[:, 1].set(alog16)

    tm_in = _pick_tile(seq, 256)
    conv_out, rest, gcol, grow = _inproj(x2, sh1, sc1, norm_mix.reshape(1, d), w_new, wgt, cw, gpc, gpr, seq, tm_in)
    grow3 = grow.reshape(16, t // CHUNK, CHUNK).transpose(1, 0, 2)

    rows = _pick_tile(seq, 512)
    ya = _deltanet(conv_out, rest, gcol, grow3, dn_norm.reshape(1, DN_DV), batch, seq, rows)
    yb = _mlstm(conv_out, rest, gcol, grow3, ml_norm.reshape(1, ML_HEADS * ML_DV), batch, seq, rows)

    wo = w_out.reshape(-1, d).astype(BF16)
    n_a = DN_HEADS * DN_DV
    tm_r = _pick_tile(seq, 512)
    brp = jnp.broadcast_to(b_router.reshape(N_EXPERTS, 1), (N_EXPERTS, HEAD_LANES))
    x1, h2, pos, wrow, len_col = _route(
        ya, yb, wo[0:n_a], wo[n_a:], x2, g1, sh2, sc2, norm_ffn.reshape(1, d),
        w_router.reshape(d, N_EXPERTS).T, brp, seq, tm_r)

    n_e = N_EXPERTS
    nt = t // tm_r
    bm = 512
    len_te =len_col.reshape(nt, n_e, HEAD_LANES)[:, :, 0]
    seg_len = jnp.pad(len_te, ((0, 0), (0, HEAD_LANES - n_e)))
    n_slots_max = t * TOP_K + n_e * (7 * nt + bm)
    nb = (n_slots_max + bm - 1) // bm
    seg_src, seg_dst, table = _slots(seg_len, bm)

    xs = _dispatch(seg_len, seg_src, seg_dst, table, pos, h2, nb * bm, tm_r, bm)
    y = _experts(table, xs, w_gate_up.reshape(n_e, d, -1), b_gate_up.reshape(n_e, 1, -1),
                 w_down.reshape(n_e, -1, d), b_down.reshape(n_e, 1, d), bm)
    out = _combine(seg_len, seg_src, seg_dst, pos, wrow, y, x1, g2, norm_final.reshape(1, d), shf, scf, seq, tm_r)
    return out.reshape(batch, seq, d)
```

```python
import functools

import jax
import jax.numpy as jnp
from jax import lax
from jax.experimental import pallas as pl
from jax.experimental.pallas import tpu as pltpu

F32 = jnp.float32
BF16 = jnp.bfloat16

CHUNK = 64
CONV_W = 4
EPS = 1e-6

DN_HEADS = 4
DN_DK = 128
DN_DV = 128
ML_HEADS = 4
ML_DK = 64
ML_DV = 128
HEAD_LANES = 128

N_EXPERTS = 32
TOP_K = 4
SWIGLU_LIMIT = 7.0
SWIGLU_ALPHA = 1.702

C_DNQ = 0
C_DNK = 512
C_DNV = 1024
C_MLQ = 1536
C_MLK = 2048
N_CONV = 2560
C_DNZ = 2560
C_MLV = 3072
C_MLO = 3584
C_GATE = 4096
N_PROJ = 4224
N_REST = C_GATE - N_CONV

V7X_VMEM_LIMIT = 56 * 1024 * 1024

NEG_BIG = -1e30


def _sigmoid(x):
    return 1.0 / (1.0 + jnp.exp(-x))


def _softplus(x):
    return jnp.maximum(x, 0.0) + jnp.log(1.0 + jnp.exp(-jnp.abs(x)))


def _split3(v):
    hi = v.astype(BF16)
    r1 = v - hi.astype(F32)
    mid = r1.astype(BF16)
    lo = (r1 - mid.astype(F32)).astype(BF16)
    return hi, mid, lo


def _dot(a, b):
    return jnp.dot(a, b, preferred_element_type=F32)


def _dot_nt(a, b):
    return lax.dot_general(a, b, (((1,), (1,)), ((), ())), preferred_element_type=F32)


def _dot_tn(a, b):
    return lax.dot_general(a, b, (((0,), (0,)), ((), ())), preferred_element_type=F32)


def _dot_exact_right(sel_bf16, v):
    hi, mid, lo = _split3(v)
    return _dot(sel_bf16, hi) + _dot(sel_bf16, mid) + _dot(sel_bf16, lo)


def _dot_exact_left(v, sel_bf16):
    hi, mid, lo = _split3(v)
    return _dot(hi, sel_bf16) + _dot(mid, sel_bf16) + _dot(lo, sel_bf16)


def _mods_kernel(c_ref, w_ref, b_ref, o_ref):
    c = c_ref[...]
    cond = c * _sigmoid(c)
    ch, cm, cl = _split3(cond)
    wh, wm, wl = _split3(w_ref[...])
    acc = _dot(ch, wh) + (_dot(ch, wm) + _dot(cm, wh)) + (_dot(ch, wl) + _dot(cm, wm) + _dot(cl, wh))
    o_ref[...] = acc + b_ref[...]


def _mods(c_pad, w, b):
    m, d = c_pad.shape
    n = w.shape[1]
    tn = 1024
    return pl.pallas_call(
        _mods_kernel,
        grid=(n // tn,),
        in_specs=[pl.BlockSpec((m, d), lambda j: (0, 0)),
                  pl.BlockSpec((d, tn), lambda j: (0, j)),
                  pl.BlockSpec((1, tn), lambda j: (0, j))],
        out_specs=pl.BlockSpec((m, tn), lambda j: (0, j)),
        out_shape=jax.ShapeDtypeStruct((m, n), F32),
        compiler_params=pltpu.CompilerParams(dimension_semantics=("arbitrary",),
                                             vmem_limit_bytes=V7X_VMEM_LIMIT),
        name="mods",
    )(c_pad, w, b)


def _gate_transform(v, bias, alog, cls):
    vb = v + bias
    beta = _sigmoid(v)
    g = -jnp.exp(alog) * _softplus(vb)
    logf = -_softplus(-vb)
    return jnp.where(cls == 0, beta, jnp.where(cls == 1, g, jnp.where(cls == 2, vb, jnp.where(cls == 3, logf, 0.0))))


def _inproj_kernel(tiles_per_seq, x_ref, sh_ref, sc_ref, nw_ref, w_ref, wgt_ref, cw_ref, gpc_ref, gpr_ref,
                   conv_ref, rest_ref, gcol_ref, grow_ref, cbuf):
    tm = x_ref.shape[0]
    i = pl.program_id(0)
    x = x_ref[...]
    ms = jnp.mean(x * x, axis=-1, keepdims=True)
    h = x * lax.rsqrt(ms + EPS) * nw_ref[...]
    h = h * (1.0 + sc_ref[...]) + sh_ref[...]
    hb = h.astype(BF16)

    @pl.when(i % tiles_per_seq == 0)
    def _():
        cbuf[0:8, :] = jnp.zeros((8, N_CONV), F32)

    group = 4 * HEAD_LANES
    for lo in range(0, N_CONV, group):
        cols = slice(lo, lo + group)
        pc = _dot(hb, w_ref[:, cols])
        cbuf[8:tm + 8, cols] = pc
        acc = cw_ref[CONV_W - 1:CONV_W, cols] * pc
        for j in range(CONV_W - 1):
            acc = acc + cw_ref[j:j + 1, cols] * cbuf[8 - (CONV_W - 1) + j:8 - (CONV_W - 1) + j + tm, cols]
        cbuf[0:8, cols] = cbuf[tm:tm + 8, cols]
        y = acc * _sigmoid(acc)
        if lo in (C_DNQ, C_DNK):
            scale = DN_DK ** -0.5 if lo == C_DNQ else 1.0
            for hh in range(DN_HEADS):
                uh = y[:, hh * HEAD_LANES:(hh + 1) * HEAD_LANES]
                un = uh * lax.rsqrt(jnp.sum(uh * uh, axis=-1, keepdims=True) + EPS)
                conv_ref[:, lo + hh * HEAD_LANES:lo + (hh + 1) * HEAD_LANES] = un * scale if lo == C_DNQ else un
        elif lo == C_MLQ:
            conv_ref[:, cols] = y * (ML_DK ** -0.5)
        else:
            conv_ref[:, cols] = y

    z = _dot(hb, w_ref[:, C_DNZ:C_MLV])
    rest_ref[:, 0:512] = z * _sigmoid(z)
    rest_ref[:, 512:1024] = _dot(hb, w_ref[:, C_MLV:C_MLO])
    rest_ref[:, 1024:1536] = _sigmoid(_dot(hb, w_ref[:, C_MLO:C_GATE]))

    r_i = lax.broadcasted_iota(jnp.int32, (tm, tm), 0)
    c_i = lax.broadcasted_iota(jnp.int32, (tm, tm), 1)
    same_chunk = (r_i // CHUNK) == (c_i // CHUNK)
    tril = jnp.where(same_chunk & (c_i <= r_i), 1.0, 0.0).astype(BF16)
    triu = jnp.where(same_chunk & (r_i <= c_i), 1.0, 0.0).astype(BF16)

    gc = _dot(hb, w_ref[:, C_GATE:N_PROJ])
    cls_c = lax.broadcasted_iota(jnp.int32, (tm, HEAD_LANES), 1) // 4
    gt = _gate_transform(gc, gpc_ref[0:1, :], gpc_ref[1:2, :], cls_c)
    cs = _dot_exact_right(tril, gt)
    gcol_ref[...] = jnp.where((cls_c == 1) | (cls_c == 3), cs, gt)

    gr = _dot_nt(wgt_ref[...], hb)
    cls_r = lax.broadcasted_iota(jnp.int32, (16, tm), 0) // 4
    gtr = _gate_transform(gr, gpr_ref[:, 0:1], gpr_ref[:, 1:2], cls_r)
    csr = _dot_exact_left(gtr, triu)
    grow_ref[...] = jnp.where((cls_r == 1) | (cls_r == 3), csr, gtr)


def _inproj(x2, sh, sc, nw, w_new, wgt, cw, gpc, gpr, seq, tm):
    t, d = x2.shape
    tps = seq // tm
    kern = functools.partial(_inproj_kernel, tps)
    return pl.pallas_call(
        kern,
        grid=(t // tm,),
        in_specs=[pl.BlockSpec((tm, d), lambda i: (i, 0)),
                  pl.BlockSpec((None, 1, d), lambda i: (i // tps, 0, 0)),
                  pl.BlockSpec((None, 1, d), lambda i: (i // tps, 0, 0)),
                  pl.BlockSpec((1, d), lambda i: (0, 0)),
                  pl.BlockSpec((d, N_PROJ), lambda i: (0, 0)),
                  pl.BlockSpec((16, d), lambda i: (0, 0)),
                  pl.BlockSpec((CONV_W, N_CONV), lambda i: (0, 0)),
                  pl.BlockSpec((8, HEAD_LANES), lambda i: (0, 0)),
                  pl.BlockSpec((16, HEAD_LANES), lambda i: (0, 0))],
        out_specs=[pl.BlockSpec((tm, N_CONV), lambda i: (i, 0)),
                   pl.BlockSpec((tm, N_REST), lambda i: (i, 0)),
                   pl.BlockSpec((tm, HEAD_LANES), lambda i: (i, 0)),
                   pl.BlockSpec((16, tm), lambda i: (0, i))],
        out_shape=[jax.ShapeDtypeStruct((t, N_CONV), F32),
                   jax.ShapeDtypeStruct((t, N_REST), F32),
                   jax.ShapeDtypeStruct((t, HEAD_LANES), F32),
                   jax.ShapeDtypeStruct((16, t), F32)],
        scratch_shapes=[pltpu.VMEM((tm + 8, N_CONV), F32)],
        compiler_params=pltpu.CompilerParams(dimension_semantics=("arbitrary",),
                                             vmem_limit_bytes=V7X_VMEM_LIMIT),
        name="inproj",
    )(x2, sh, sc, nw, w_new, wgt, cw, gpc, gpr)


def _chunk_masks():
    r = lax.broadcasted_iota(jnp.int32, (CHUNK, CHUNK), 0)
    c = lax.broadcasted_iota(jnp.int32, (CHUNK, CHUNK), 1)
    return r >= c, r > c, r == c


def _bdot(a, b):
    return lax.dot_general(a, b, (((2,), (1,)), ((0,), (0,))), preferred_element_type=F32)


def _bdot_nt(a, b):
    return lax.dot_general(a, b, (((2,), (2,)), ((0,), (0,))), preferred_element_type=F32)


def _unit_lower_inverse(lower, row, col):
    x = jnp.where(row == col, 1.0, 0.0) - jnp.where((row >> 1) == (col >> 1), lower, 0.0)
    shift = 1
    while (1 << shift) < CHUNK:
        couple = ((row >> (shift + 1)) == (col >> (shift + 1))) & ((row >> shift) != (col >> shift))
        cb = jnp.where(couple, lower, 0.0).astype(BF16)
        xb = x.astype(BF16)
        x = x - _bdot(_bdot(xb, cb).astype(BF16), xb)
        shift += 1
        yield
    return x


def _deltanet_steps(q_ref, k_ref, v_ref, gc_ref, gr_ref, z_ref, nw_ref, o_ref, s_ref):
    nc = gr_ref.shape[0]

    @pl.when(pl.program_id(1) == 0)
    def _():
        s_ref[...] = jnp.zeros(s_ref.shape, F32)

    row = lax.broadcasted_iota(jnp.int32, (CHUNK, CHUNK), 0)
    col = lax.broadcasted_iota(jnp.int32, (CHUNK, CHUNK), 1)
    incl = row >= col
    strict = row > col
    nw = nw_ref[...]
    gcc = gc_ref[...]
    grr = gr_ref[...]

    nh = DN_HEADS
    nb = nh * nc

    def heads(ref):
        return jnp.stack([ref[:, hh * HEAD_LANES:(hh + 1) * HEAD_LANES] for hh in range(nh)],
                         axis=0).reshape(nb, CHUNK, HEAD_LANES)

    def col_gate(lane0):
        return jnp.stack([gcc[:, lane0 + hh:lane0 + hh + 1] for hh in range(nh)], axis=0).reshape(nb, CHUNK, 1)

    q = heads(q_ref)
    k = heads(k_ref)
    v = heads(v_ref)
    beta = col_gate(0)
    g_c = col_gate(4)
    g_r = jnp.stack([grr[:, 4 + hh:5 + hh, :] for hh in range(nh)], axis=0).reshape(nb, 1, CHUNK)
    g_last = g_c[:, CHUNK - 1:CHUNK, :]
    decay = jnp.exp(jnp.where(incl, g_c - g_r, NEG_BIG))
    kb = k.astype(BF16)
    kk = _bdot_nt(kb, kb)
    lower = jnp.where(strict, beta * kk * decay, 0.0)
    yield
    tinv = yield from _unit_lower_inverse(lower, row, col)
    eg = jnp.exp(g_c)
    rhs = jnp.concatenate([v * beta, k * (beta * eg)], axis=-1)
    sol = _bdot(tinv.astype(BF16), rhs.astype(BF16))
    yield
    w_val = sol[:, :, 0:DN_DV].reshape(nh, nc, CHUNK, DN_DV)
    kq = jnp.concatenate([sol[:, :, DN_DV:DN_DV + DN_DK], q * eg], axis=1).astype(BF16)
    kq = kq.reshape(nh, nc, 2 * CHUNK, DN_DK)
    qk = (_bdot_nt(q.astype(BF16), kb) * decay).astype(BF16).reshape(nh, nc, CHUNK, CHUNK)
    k_dec_t = jnp.swapaxes(k * jnp.exp(g_last - g_c), 1, 2).astype(BF16).reshape(nh, nc, DN_DK, CHUNK)
    s_dec = jnp.exp(g_last).reshape(nh, nc, 1, 1)
    yield

    state = s_ref[...]
    outs = []
    for c in range(nc):
        both = _bdot(kq[:, c], state.astype(BF16))
        v_new = w_val[:, c] - both[:, 0:CHUNK]
        vb = v_new.astype(BF16)
        outs.append(both[:, CHUNK:2 * CHUNK] + _bdot(qk[:, c], vb))
        state = s_dec[:, c] * state + _bdot(k_dec_t[:, c], vb)
        yield
    s_ref[...] = state

    o = jnp.stack(outs, axis=1)
    on = o * lax.rsqrt(jnp.mean(o * o, axis=-1, keepdims=True) + EPS) * nw
    on = on.reshape(nh, nc * CHUNK, DN_DV)
    for hh in range(nh):
        lanes = slice(hh * HEAD_LANES, (hh + 1) * HEAD_LANES)
        o_ref[:, lanes] = on[hh] * z_ref[:, lanes]


def _mlstm_steps(q_ref, k_ref, v_ref, gc_ref, gr_ref, og_ref, nw_ref, o_ref, c_ref, n_ref, m_ref):
    n_chunks = gr_ref.shape[0]

    @pl.when(pl.program_id(1) == 0)
    def _():
        c_ref[...] = jnp.zeros(c_ref.shape, F32)
        n_ref[...] = jnp.zeros(n_ref.shape, F32)
        m_ref[...] = jnp.zeros(m_ref.shape, F32)

    incl, _, _ = _chunk_masks()
    nc = n_chunks
    gcc = gc_ref[...]
    grr = gr_ref[...]

    nh = ML_HEADS

    def heads(ref):
        return jnp.stack([ref[:, hh * HEAD_LANES:(hh + 1) * HEAD_LANES] for hh in range(nh)],
                         axis=0).reshape(nh, nc, CHUNK, HEAD_LANES)

    def col_gate(lane0):
        return jnp.stack([gcc[:, lane0 + hh:lane0 + hh + 1] for hh in range(nh)], axis=0).reshape(nh, nc, CHUNK, 1)

    def row_gate(row0):
        return jnp.stack([grr[:, row0 + hh:row0 + hh + 1, :] for hh in range(nh)], axis=0)

    q = heads(q_ref)
    k = heads(k_ref)
    v = heads(v_ref)
    i_c = col_gate(8)
    b_c = col_gate(12)
    i_r = row_gate(8)
    b_r = row_gate(12)
    b_last = b_c[:, :, CHUNK - 1:CHUNK, :]
    d_mat = jnp.where(incl, b_c - b_r + i_r, NEG_BIG)
    m_intra = jnp.max(d_mat, axis=-1, keepdims=True)
    g_end = b_last - b_c + i_c
    g_end_max = jnp.max(g_end, axis=2, keepdims=True)
    yield

    m_run = m_ref[:, 0:1, 0:1].reshape(nh, 1, 1, 1)
    m_before = []
    for c in range(nc):
        m_before.append(m_run)
        m_run = jnp.maximum(b_last[:, c:c + 1] + m_run, g_end_max[:, c:c + 1])
    m_s = jnp.concatenate(m_before, axis=1)
    m_new = jnp.maximum(b_last + m_s, g_end_max)
    keep = jnp.exp(b_last + m_s - m_new)
    yield

    nb = nh * nc
    qb = q.astype(BF16)
    kb = k.astype(BF16)
    vb = v.astype(BF16)
    m_t = jnp.maximum(b_c + m_s, m_intra)
    inter = jnp.exp(b_c + m_s - m_t)
    qk = _bdot_nt(qb.reshape(nb, CHUNK, HEAD_LANES), kb.reshape(nb, CHUNK, HEAD_LANES))
    p = jnp.exp(d_mat - m_t) * qk.reshape(nh, nc, CHUNK, CHUNK)
    yield
    intra = _bdot(p.astype(BF16).reshape(nb, CHUNK, CHUNK), vb.reshape(nb, CHUNK, ML_DV)).reshape(nh, nc, CHUNK, ML_DV)
    p_sum = jnp.sum(p, axis=-1, keepdims=True)
    yield
    kw = k * jnp.exp(g_end - m_new)
    kw_t = jnp.swapaxes(kw.reshape(nb, CHUNK, HEAD_LANES), 1, 2).astype(BF16)
    d_state = _bdot(kw_t, vb.reshape(nb, CHUNK, ML_DV)).reshape(nh, nc, HEAD_LANES, ML_DV)
    kw_sum = jnp.sum(kw, axis=2, keepdims=True)
    yield

    c_s = c_ref[...]
    n_s = n_ref[:, 0:1, :]
    q_c = []
    q_n = []
    for c in range(nc):
        q_c.append(_bdot(qb[:, c], c_s.astype(BF16)))
        q_n.append(jnp.sum(q[:, c] * n_s, axis=-1, keepdims=True))
        c_s = keep[:, c] * c_s + d_state[:, c]
        n_s = keep[:, c] * n_s + kw_sum[:, c]
        yield
    c_ref[...] = c_s
    n_ref[...] = jnp.broadcast_to(n_s, (nh, 8, HEAD_LANES))
    m_ref[...] = jnp.broadcast_to(m_run.reshape(nh, 1, 1), (nh, 8, HEAD_LANES))

    num = inter * jnp.stack(q_c, axis=1) + intra
    den = inter * jnp.stack(q_n, axis=1) + p_sum
    h = num / jnp.maximum(jnp.abs(den), jnp.exp(-m_t))
    hr = h * lax.rsqrt(jnp.mean(h * h, axis=-1, keepdims=True) + EPS)
    hr = hr.reshape(nh, nc * CHUNK, ML_DV)
    for hh in range(nh):
        lanes = slice(hh * HEAD_LANES, (hh + 1) * HEAD_LANES)
        o_ref[:, lanes] = hr[hh] * nw_ref[:, lanes] * og_ref[:, lanes]


def _mixers_kernel(dq_ref, dk_ref, dv_ref, gc_ref, gr_ref, z_ref, dnw_ref, mq_ref, mk_ref, mv_ref, og_ref, mnw_ref,
                   o_ref, s_ref, c_ref, n_ref, m_ref):
    n_a = DN_HEADS * HEAD_LANES
    n_b = ML_HEADS * HEAD_LANES
    stages = [_deltanet_steps(dq_ref, dk_ref, dv_ref, gc_ref, gr_ref, z_ref, dnw_ref, o_ref.at[:, 0:n_a], s_ref),
              _mlstm_steps(mq_ref, mk_ref, mv_ref, gc_ref, gr_ref, og_ref, mnw_ref, o_ref.at[:, n_a:n_a + n_b],
                           c_ref, n_ref, m_ref)]
    while stages:
        for stage in list(stages):
            try:
                next(stage)
            except StopIteration:
                stages.remove(stage)


def _mixers(conv_out, rest, gcol, grow3, dn_nw, ml_nw, batch, seq, rows):
    t = conv_out.shape[0]
    nj = seq // rows
    cpb = rows // CHUNK
    width = 4 * HEAD_LANES
    assert DN_HEADS * HEAD_LANES == width and ML_HEADS * HEAD_LANES == width

    def rmap(col):
        return lambda b, j: (b * nj + j, col)

    def conv_block(c0):
        return pl.BlockSpec((rows, width), rmap(c0 // width))

    def rest_block(c0):
        return pl.BlockSpec((rows, width), rmap((c0 - N_CONV) // width))

    gates = [pl.BlockSpec((rows, HEAD_LANES), rmap(0)),
             pl.BlockSpec((cpb, 16, CHUNK), lambda b, j: (b * nj + j, 0, 0))]
    return pl.pallas_call(
        _mixers_kernel,
        grid=(batch, nj),
        in_specs=[conv_block(C_DNQ), conv_block(C_DNK), conv_block(C_DNV)] + gates
                 + [rest_block(C_DNZ), pl.BlockSpec((1, HEAD_LANES), lambda b, j: (0, 0)),
                    conv_block(C_MLQ), conv_block(C_MLK), rest_block(C_MLV), rest_block(C_MLO),
                    pl.BlockSpec((1, width), lambda b, j: (0, 0))],
        out_specs=pl.BlockSpec((rows, 2 * width), rmap(0)),
        out_shape=jax.ShapeDtypeStruct((t, 2 * width), F32),
        scratch_shapes=[pltpu.VMEM((DN_HEADS, DN_DK, DN_DV), F32),
                        pltpu.VMEM((ML_HEADS, HEAD_LANES, ML_DV), F32),
                        pltpu.VMEM((ML_HEADS, 8, HEAD_LANES), F32),
                        pltpu.VMEM((ML_HEADS, 8, HEAD_LANES), F32)],
        compiler_params=pltpu.CompilerParams(dimension_semantics=("arbitrary", "arbitrary"),
                                             vmem_limit_bytes=V7X_VMEM_LIMIT),
        name="mixers",
    )(conv_out, conv_out, conv_out, gcol, grow3, rest, dn_nw, conv_out, conv_out, rest, rest, ml_nw)


def _route_kernel(y_ref, wo_ref, x_ref, g1_ref, sh_ref, sc_ref, nw_ref, wrt_ref, br_ref,
                  x1_ref, h2_ref, pos_ref, wrow_ref, len_ref):
    tm = x_ref.shape[0]

    mix = _dot(y_ref[...].astype(BF16), wo_ref[...])
    x1 = x_ref[...] + g1_ref[...] * mix
    x1_ref[...] = x1
    ms = jnp.mean(x1 * x1, axis=-1, keepdims=True)
    h2 = x1 * lax.rsqrt(ms + EPS) * nw_ref[...]
    h2 = h2 * (1.0 + sc_ref[...]) + sh_ref[...]
    h2_ref[...] = h2.astype(BF16)

    hh, hm, _ = _split3(h2)
    wh, wm, _ = _split3(wrt_ref[...])
    logits = _dot_nt(wh, hh) + (_dot_nt(wh, hm) + _dot_nt(wm, hh)) + br_ref[:, 0:1]

    e_i = lax.broadcasted_iota(jnp.int32, (N_EXPERTS, tm), 0)
    work = logits
    tops = []
    sels = []
    hots = []
    for _ in range(TOP_K):
        m = jnp.max(work, axis=0, keepdims=True)
        sel = jnp.min(jnp.where(work == m, e_i, N_EXPERTS), axis=0, keepdims=True)
        hot = e_i == sel
        work = jnp.where(hot, NEG_BIG, work)
        tops.append(m)
        sels.append(sel)
        hots.append(hot)
    exps = [jnp.exp(tl - tops[0]) for tl in tops]
    denom = exps[0] + exps[1] + exps[2] + exps[3]
    ws = [e / denom for e in exps]

    chosen = jnp.zeros((N_EXPERTS, tm), F32)
    for hot in hots:
        chosen = chosen + jnp.where(hot, 1.0, 0.0)

    r_i = lax.broadcasted_iota(jnp.int32, (tm, tm), 0)
    c_i = lax.broadcasted_iota(jnp.int32, (tm, tm), 1)
    strict_upper = jnp.where(r_i < c_i, 1.0, 0.0).astype(BF16)
    prefix = _dot(chosen.astype(BF16), strict_upper)
    n_e = jnp.sum(chosen, axis=1, keepdims=True)
    len8 = jnp.ceil(n_e * 0.125) * 8.0
    er = lax.broadcasted_iota(jnp.int32, (N_EXPERTS, N_EXPERTS), 0)
    ec = lax.broadcasted_iota(jnp.int32, (N_EXPERTS, N_EXPERTS), 1)
    strict_lower = jnp.where(ec < er, 1.0, 0.0).astype(BF16)
    len8b = jnp.broadcast_to(len8, (N_EXPERTS, HEAD_LANES))
    off8 = _dot_exact_right(strict_lower, len8b)[:, 0:1]
    len_ref[...] = len8b.astype(jnp.int32)

    for kk in range(TOP_K):
        wrow_ref[kk:kk + 1, :] = ws[kk]
        pos = jnp.sum(jnp.where(hots[kk], prefix + off8, 0.0), axis=0, keepdims=True)
        pos_ref[kk:kk + 1, :] = pos.astype(jnp.int32)


def _route(ymix, wo, x2, g1, sh2, sc2, nw, wrt, br, seq, tm):
    t, d = x2.shape
    tps = seq // tm
    bmap = lambda i: (i // tps, 0, 0)
    return pl.pallas_call(
        _route_kernel,
        grid=(t // tm,),
        in_specs=[pl.BlockSpec((tm, ymix.shape[1]), lambda i: (i, 0)),
                  pl.BlockSpec(wo.shape, lambda i: (0, 0)),
                  pl.BlockSpec((tm, d), lambda i: (i, 0)),
                  pl.BlockSpec((None, 1, d), bmap),
                  pl.BlockSpec((None, 1, d), bmap),
                  pl.BlockSpec((None, 1, d), bmap),
                  pl.BlockSpec((1, d), lambda i: (0, 0)),
                  pl.BlockSpec((N_EXPERTS, d), lambda i: (0, 0)),
                  pl.BlockSpec((N_EXPERTS, HEAD_LANES), lambda i: (0, 0))],
        out_specs=[pl.BlockSpec((tm, d), lambda i: (i, 0)),
                   pl.BlockSpec((tm, d), lambda i: (i, 0)),
                   pl.BlockSpec((TOP_K, tm), lambda i: (0, i)),
                   pl.BlockSpec((TOP_K, tm), lambda i: (0, i)),
                   pl.BlockSpec((N_EXPERTS, HEAD_LANES), lambda i: (i, 0))],
        out_shape=[jax.ShapeDtypeStruct((t, d), F32),
                   jax.ShapeDtypeStruct((t, d), BF16),
                   jax.ShapeDtypeStruct((TOP_K, t), jnp.int32),
                   jax.ShapeDtypeStruct((TOP_K, t), F32),
                   jax.ShapeDtypeStruct((t // tm * N_EXPERTS, HEAD_LANES), jnp.int32)],
        compiler_params=pltpu.CompilerParams(dimension_semantics=("arbitrary",),
                                             vmem_limit_bytes=V7X_VMEM_LIMIT),
        name="route",
    )(ymix, wo, x2, g1, sh2, sc2, nw, wrt, br)


TAIL_START, TAIL_LEN, PAD_END, BLOCK_START, BLOCK_COUNT = range(5)


def _slots_kernel(bm, len_te_ref, ssrc_ref, sdst_ref, tab_ref):
    nt, lanes = len_te_ref.shape
    len_te = len_te_ref[...].astype(F32)

    r_l = lax.broadcasted_iota(jnp.int32, (lanes, lanes), 0)
    c_l = lax.broadcasted_iota(jnp.int32, (lanes, lanes), 1)
    upper_incl = jnp.where(r_l <= c_l, 1.0, 0.0).astype(BF16)
    upper_strict = jnp.where(r_l < c_l, 1.0, 0.0).astype(BF16)
    total_r = jnp.sum(len_te, axis=0, keepdims=True)
    padded_r = jnp.ceil(total_r * (1.0 / bm)) * bm
    pad_end_r = _dot_exact_left(jnp.broadcast_to(padded_r, (8, lanes)), upper_incl)[0:1, :]
    pad_start_r = pad_end_r - padded_r
    r_t = lax.broadcasted_iota(jnp.int32, (nt, nt), 0)
    c_t = lax.broadcasted_iota(jnp.int32, (nt, nt), 1)
    lower_strict_t = jnp.where(c_t < r_t, 1.0, 0.0).astype(BF16)
    before = _dot_exact_right(lower_strict_t, len_te)
    ssrc_ref[...] = _dot_exact_left(len_te, upper_strict).astype(jnp.int32)
    sdst_ref[...] = (pad_start_r + before).astype(jnp.int32)
    tab_ref[TAIL_START:TAIL_START + 1, :] = (pad_start_r + total_r).astype(jnp.int32)
    tab_ref[TAIL_LEN:TAIL_LEN + 1, :] = (padded_r - total_r).astype(jnp.int32)
    tab_ref[PAD_END:PAD_END + 1, :] = pad_end_r.astype(jnp.int32)
    tab_ref[BLOCK_START:BLOCK_START + 1, :] = (pad_start_r * (1.0 / bm)).astype(jnp.int32)
    tab_ref[BLOCK_COUNT:BLOCK_COUNT + 1, :] = (padded_r * (1.0 / bm)).astype(jnp.int32)
    tab_ref[5:8, :] = jnp.zeros((3, lanes), jnp.int32)


def _slots(len_te, bm):
    assert bm & (bm - 1) == 0, "block rows must be a power of two"
    nt, lanes = len_te.shape
    return pl.pallas_call(
        functools.partial(_slots_kernel, bm),
        out_shape=[jax.ShapeDtypeStruct((nt, lanes), jnp.int32),
                   jax.ShapeDtypeStruct((nt, lanes), jnp.int32),
                   jax.ShapeDtypeStruct((8, lanes), jnp.int32)],
        compiler_params=pltpu.CompilerParams(vmem_limit_bytes=V7X_VMEM_LIMIT),
        name="slots",
    )(len_te)


def _segment_pieces(max_rows):
    sizes = []
    s = 8
    while s <= max_rows:
        sizes.append(s)
        s *= 2
    return sizes[::-1]


def _segment_dma(src_ref, src0, dst_ref, dst0, nrows, sizes, sem, start):
    off = 0
    for sz in sizes:
        bit = nrows & sz

        @pl.when(bit != 0)
        def _(off=off, sz=sz):
            s0 = pl.multiple_of(src0 + off, 8)
            d0 = pl.multiple_of(dst0 + off, 8)
            cp = pltpu.make_async_copy(src_ref.at[pl.ds(s0, sz)], dst_ref.at[pl.ds(d0, sz)], sem)
            if start:
                cp.start()
            else:
                cp.wait()

        off = off + bit


def _sorted_onehot(pos_ref, nrows, tt):
    r_i = lax.broadcasted_iota(jnp.int32, (nrows, tt), 0)
    hit = r_i == pos_ref[0:1, :]
    for kk in range(1, TOP_K):
        hit = hit | (r_i == pos_ref[kk:kk + 1, :])
    return hit


def _dispatch_kernel(bm, len_ref, ssrc_ref, sdst_ref, tail_ref, pos_ref, h2_ref, xs_hbm, buf, sem):
    i = pl.program_id(0)
    nt = pl.num_programs(0)
    tt = h2_ref.shape[0]
    nrows = buf.shape[1]
    sizes = _segment_pieces(tt)
    slot = i % 2

    perm = jnp.where(_sorted_onehot(pos_ref, nrows, tt), 1.0, 0.0).astype(BF16)
    buf[slot] = _dot(perm, h2_ref[...])

    def seg(tile, which, start):
        def body(e, carry):
            _segment_dma(buf.at[which], ssrc_ref[tile, e], xs_hbm, sdst_ref[tile, e], len_ref[tile, e], sizes,
                         sem.at[which], start)
            return carry
        lax.fori_loop(0, N_EXPERTS, body, 0)

    @pl.when(i > 0)
    def _():
        seg(i - 1, 1 - slot, False)

    seg(i, slot, True)

    @pl.when(i == nt - 1)
    def _():
        seg(i, slot, False)
        zrows = bm
        zbuf = buf.at[0]
        zsem = sem.at[0]
        buf[0, 0:zrows, :] = jnp.zeros((zrows, buf.shape[2]), F32)

        def tail(start):
            def body(e, carry):
                _segment_dma(zbuf, 0, xs_hbm, tail_ref[TAIL_START, e], tail_ref[TAIL_LEN, e],
                             _segment_pieces(zrows // 2), zsem, start)
                return carry
            lax.fori_loop(0, N_EXPERTS, body, 0)

        tail(True)
        tail(False)

        used = tail_ref[PAD_END, tail_ref.shape[1] - 1]
        n_unused = (xs_hbm.shape[0] - used) // zrows

        def unused_copy(j):
            d0 = pl.multiple_of(used + j * zrows, 8)
            return pltpu.make_async_copy(zbuf.at[pl.ds(0, zrows)], xs_hbm.at[pl.ds(d0, zrows)], zsem)

        def unused_start(j, carry):
            unused_copy(j).start()
            return carry

        def unused_wait(j, carry):
            unused_copy(j).wait()
            return carry

        lax.fori_loop(0, n_unused, unused_start, 0)
        lax.fori_loop(0, n_unused, unused_wait, 0)


def _dispatch(seg_len, seg_src, seg_dst, tail, pos, h2, n_slots, tt, bm):
    t, d = h2.shape
    nrows = TOP_K * tt + 8 * N_EXPERTS
    assert bm <= nrows
    smem = pl.BlockSpec(memory_space=pltpu.SMEM)
    return pl.pallas_call(
        functools.partial(_dispatch_kernel, bm),
        grid=(t // tt,),
        in_specs=[smem, smem, smem, smem,
                  pl.BlockSpec((TOP_K, tt), lambda i: (0, i)),
                  pl.BlockSpec((tt, d), lambda i: (i, 0))],
        out_specs=pl.BlockSpec(memory_space=pl.ANY),
        out_shape=jax.ShapeDtypeStruct((n_slots, d), F32),
        scratch_shapes=[pltpu.VMEM((2, nrows, d), F32), pltpu.SemaphoreType.DMA((2,))],
        compiler_params=pltpu.CompilerParams(dimension_semantics=("arbitrary",), has_side_effects=True,
                                             vmem_limit_bytes=V7X_VMEM_LIMIT),
        name="dispatch",
    )(seg_len, seg_src, seg_dst, tail, pos, h2)


def _experts_kernel(tab_ref, xs_hbm, wgu_ref, bgu_ref, wd_ref, bd_ref, y_hbm, wgu_s, wd_s, xbuf, ybuf, sem_in,
                    sem_out):
    e = pl.program_id(0)
    bm = xbuf.shape[1]
    d_ff = wd_ref.shape[0]
    first = tab_ref[BLOCK_START, e]
    count = tab_ref[BLOCK_COUNT, e]

    def rows(j):
        return pl.ds(pl.multiple_of((first + j) * bm, bm), bm)

    def x_copy(j, slot):
        return pltpu.make_async_copy(xs_hbm.at[rows(j)], xbuf.at[slot], sem_in.at[slot])

    def y_copy(j, slot):
        return pltpu.make_async_copy(ybuf.at[slot], y_hbm.at[rows(j)], sem_out.at[slot])

    @pl.when(count > 0)
    def _():
        x_copy(0, 0).start()

    wgu_s[...] = wgu_ref[...].astype(BF16)
    wd_s[...] = wd_ref[...].astype(BF16)

    def block(j, carry):
        slot = j % 2
        x_copy(j, slot).wait()

        @pl.when(j + 1 < count)
        def _():
            x_copy(j + 1, 1 - slot).start()

        @pl.when(j >= 2)
        def _():
            y_copy(j - 2, slot).wait()

        xb = xbuf[slot].astype(BF16)
        fc = 512
        nf = d_ff // fc

        def gate_up(f):
            gcols = slice(f * fc, (f + 1) * fc)
            ucols = slice(d_ff + f * fc, d_ff + (f + 1) * fc)
            return _dot(xb, wgu_s[:, gcols]) + bgu_ref[:, gcols], _dot(xb, wgu_s[:, ucols]) + bgu_ref[:, ucols]

        acc = None
        pre = gate_up(0)
        for f in range(nf):
            gate = jnp.minimum(pre[0], SWIGLU_LIMIT)
            up = jnp.clip(pre[1], -SWIGLU_LIMIT, SWIGLU_LIMIT)
            if f + 1 < nf:
                pre = gate_up(f + 1)
            act = (up + 1.0) * gate * _sigmoid(SWIGLU_ALPHA * gate)
            part = _dot(act.astype(BF16), wd_s[f * fc:(f + 1) * fc, :])
            acc = part if acc is None else acc + part
        ybuf[slot] = acc + bd_ref[...]
        y_copy(j, slot).start()
        return carry

    lax.fori_loop(0, count, block, 0)

    @pl.when(count >= 2)
    def _():
        y_copy(count - 2, count % 2).wait()

    @pl.when(count >= 1)
    def _():
        y_copy(count - 1, (count - 1) % 2).wait()

    @pl.when(e == pl.num_programs(0) - 1)
    def _():
        used = first + count
        n_unused = y_hbm.shape[0] // bm - used
        ybuf[0] = jnp.zeros(ybuf.shape[1:], F32)

        def z_copy(j):
            return pltpu.make_async_copy(ybuf.at[0], y_hbm.at[rows(count + j)], sem_out.at[0])

        def z_start(j, carry):
            z_copy(j).start()
            return carry

        def z_wait(j, carry):
            z_copy(j).wait()
            return carry

        lax.fori_loop(0, n_unused, z_start, 0)
        lax.fori_loop(0, n_unused, z_wait, 0)


def _experts(table, xs, wgu, bgu, wd, bd, bm):
    ns, d = xs.shape
    n_e, _, two_ff = wgu.shape
    d_ff = two_ff // 2
    wmap = lambda e: (e, 0, 0)
    return pl.pallas_call(
        _experts_kernel,
        grid=(n_e,),
        in_specs=[pl.BlockSpec(memory_space=pltpu.SMEM),
                  pl.BlockSpec(memory_space=pl.ANY),
                  pl.BlockSpec((None, d, two_ff), wmap),
                  pl.BlockSpec((None, 1, two_ff), wmap),
                  pl.BlockSpec((None, d_ff, d), wmap),
                  pl.BlockSpec((None, 1, d), wmap)],
        out_specs=pl.BlockSpec(memory_space=pl.ANY),
        out_shape=jax.ShapeDtypeStruct((ns, d), F32),
        scratch_shapes=[pltpu.VMEM((d, two_ff), BF16), pltpu.VMEM((d_ff, d), BF16),
                        pltpu.VMEM((2, bm, d), F32), pltpu.VMEM((2, bm, d), F32),
                        pltpu.SemaphoreType.DMA((2,)), pltpu.SemaphoreType.DMA((2,))],
        compiler_params=pltpu.CompilerParams(dimension_semantics=("arbitrary",), has_side_effects=True,
                                             vmem_limit_bytes=V7X_VMEM_LIMIT),
        name="experts",
    )(table, xs, wgu, bgu, wd, bd)


def _combine_kernel(len_ref, ssrc_ref, sdst_ref, pos_ref, w_ref, y_hbm, x1_ref, g2_ref, nw_ref, sh_ref, sc_ref,
                    o_ref, ybuf, sem):
    i = pl.program_id(0)
    nt = pl.num_programs(0)
    tt = x1_ref.shape[0]
    nrows = ybuf.shape[1]
    sizes = _segment_pieces(tt)
    slot = i % 2

    def seg(tile, which, start):
        def body(e, carry):
            _segment_dma(y_hbm, sdst_ref[tile, e], ybuf.at[which], ssrc_ref[tile, e], len_ref[tile, e], sizes,
                         sem.at[which], start)
            return carry
        lax.fori_loop(0, N_EXPERTS, body, 0)

    @pl.when(i == 0)
    def _():
        ybuf[...] = jnp.zeros(ybuf.shape, F32)
        seg(0, 0, True)

    @pl.when(i + 1 < nt)
    def _():
        seg(i + 1, 1 - slot, True)

    seg(i, slot, False)

    r_i = lax.broadcasted_iota(jnp.int32, (nrows, tt), 0)
    wmat = jnp.where(r_i == pos_ref[0:1, :], w_ref[0:1, :], 0.0)
    for kk in range(1, TOP_K):
        wmat = wmat + jnp.where(r_i == pos_ref[kk:kk + 1, :], w_ref[kk:kk + 1, :], 0.0)
    acc = _dot_tn(wmat.astype(BF16), ybuf[slot].astype(BF16))
    xo = x1_ref[...] + g2_ref[...] * acc
    ms = jnp.mean(xo * xo, axis=-1, keepdims=True)
    hn = xo * lax.rsqrt(ms + EPS) * nw_ref[...]
    o_ref[...] = hn * (1.0 + sc_ref[...]) + sh_ref[...]


def _combine(seg_len, seg_src, seg_dst, pos, wrow, y, x1, g2, nw, shf, scf, seq, tt):
    t, d = x1.shape
    tps = seq // tt
    bmap = lambda i: (i // tps, 0, 0)
    nrows = TOP_K * tt + 8 * N_EXPERTS
    smem = pl.BlockSpec(memory_space=pltpu.SMEM)
    return pl.pallas_call(
        _combine_kernel,
        grid=(t // tt,),
        in_specs=[smem, smem, smem,
                  pl.BlockSpec((TOP_K, tt), lambda i: (0, i)),
                  pl.BlockSpec((TOP_K, tt), lambda i: (0, i)),
                  pl.BlockSpec(memory_space=pl.ANY),
                  pl.BlockSpec((tt, d), lambda i: (i, 0)),
                  pl.BlockSpec((None, 1, d), bmap),
                  pl.BlockSpec((1, d), lambda i: (0, 0)),
                  pl.BlockSpec((None, 1, d), bmap),
                  pl.BlockSpec((None, 1, d), bmap)],
        out_specs=pl.BlockSpec((tt, d), lambda i: (i, 0)),
        out_shape=jax.ShapeDtypeStruct((t, d), F32),
        scratch_shapes=[pltpu.VMEM((2, nrows, d), F32), pltpu.SemaphoreType.DMA((2,))],
        compiler_params=pltpu.CompilerParams(dimension_semantics=("arbitrary",),
                                             vmem_limit_bytes=V7X_VMEM_LIMIT),
        name="combine",
    )(seg_len, seg_src, seg_dst, pos, wrow, y, x1, g2, nw, shf, scf)


def _pad_heads(w, heads, dk):
    r = w.shape[0]
    w3 = w.reshape(r, heads, dk)
    return jnp.pad(w3, ((0, 0), (0, 0), (0, HEAD_LANES - dk))).reshape(r, heads * HEAD_LANES)


def _pick_tile(n, pref):
    tile = pref
    while n % tile:
        tile //= 2
    return tile


def kernel(x, c, w_ada, b_ada, norm_mix, w_in, dn_conv, dn_a_log, dn_dt_bias, dn_norm, ml_conv, ml_i_bias,
           ml_f_bias, ml_norm, w_out, norm_ffn, w_router, b_router, w_gate_up, b_gate_up, w_down, b_down,
           w_ada_final, b_ada_final, norm_final):
    batch, seq, d = x.shape
    assert w_ada.shape[0] == 1, "single-layer block"
    assert seq % CHUNK == 0
    t = batch * seq
    x2 = x.reshape(t, d)

    c_pad = jnp.pad(c, ((0, 8 - batch % 8 if batch % 8 else 0), (0, 0)))
    mod = _mods(c_pad, w_ada.reshape(d, 6 * d), b_ada.reshape(1, 6 * d))[:batch]
    modf = _mods(c_pad, w_ada_final, b_ada_final.reshape(1, 2 * d))[:batch]
    sh1, sc1, g1, sh2, sc2, g2 = [mod[:, None, j * d:(j + 1) * d] for j in range(6)]
    shf, scf = modf[:, None, 0:d], modf[:, None, d:2 * d]

    wi = w_in.reshape(d, -1)
    o_z = 1536
    o_b = 2048
    o_mq = 2056
    o_mk = o_mq + ML_HEADS * ML_DK
    o_mv = o_mk + ML_HEADS * ML_DK
    o_mo = o_mv + ML_HEADS * ML_DV
    o_mi = o_mo + ML_HEADS * ML_DV
    gates = jnp.concatenate([wi[:, o_b:o_mq], wi[:, o_mi:o_mi + 2 * ML_HEADS]], axis=1)
    w_new = jnp.concatenate([
        wi[:, 0:o_z],
        _pad_heads(wi[:, o_mq:o_mk], ML_HEADS, ML_DK),
        _pad_heads(wi[:, o_mk:o_mv], ML_HEADS, ML_DK),
        wi[:, o_z:o_b],
        wi[:, o_mv:o_mo],
        wi[:, o_mo:o_mi],
        jnp.pad(gates, ((0, 0), (0, HEAD_LANES - 16))),
    ], axis=1).astype(BF16)
    wgt = gates.T.astype(BF16)
    mlc = ml_conv.reshape(CONV_W, -1)
    cw = jnp.concatenate([dn_conv.reshape(CONV_W, -1),
                          _pad_heads(mlc[:, 0:ML_HEADS * ML_DK], ML_HEADS, ML_DK),
                          _pad_heads(mlc[:, ML_HEADS * ML_DK:], ML_HEADS, ML_DK)], axis=1)
    zeros4 = jnp.zeros((4,), F32)
    bias16 = jnp.concatenate([zeros4, dn_dt_bias.reshape(4), ml_i_bias.reshape(4), ml_f_bias.reshape(4)])
    alog16 = jnp.concatenate([zeros4, dn_a_log.reshape(4), zeros4, zeros4])
    gpc = jnp.zeros((8, HEAD_LANES), F32).at[0, 0:16].set(bias16).at[1, 0:16].set(alog16)
    gpr = jnp.zeros((16, HEAD_LANES), F32).at[:, 0].set(bias16).at[:, 1].set(alog16)

    tm_in = _pick_tile(seq, 256)
    conv_out, rest, gcol, grow = _inproj(x2, sh1, sc1, norm_mix.reshape(1, d), w_new, wgt, cw, gpc, gpr, seq, tm_in)
    grow3 = grow.reshape(16, t // CHUNK, CHUNK).transpose(1, 0, 2)

    rows = _pick_tile(seq, 512)
    ymix = _mixers(conv_out, rest, gcol, grow3, dn_norm.reshape(1, DN_DV), ml_norm.reshape(1, ML_HEADS * ML_DV),
                   batch, seq, rows)

    wo = w_out.reshape(-1, d).astype(BF16)
    tm_r = _pick_tile(seq, 512)
    brp = jnp.broadcast_to(b_router.reshape(N_EXPERTS, 1), (N_EXPERTS, HEAD_LANES))
    x1, h2, pos, wrow, len_col = _route(
        ymix, wo, x2, g1, sh2, sc2, norm_ffn.reshape(1, d),
        w_router.reshape(d, N_EXPERTS).T, brp, seq, tm_r)

    n_e = N_EXPERTS
    nt = t // tm_r
    bm = 512
    len_te =len_col.reshape(nt, n_e, HEAD_LANES)[:, :, 0]
    seg_len = jnp.pad(len_te, ((0, 0), (0, HEAD_LANES - n_e)))
    n_slots_max = t * TOP_K + n_e * (7 * nt + bm)
    nb = (n_slots_max + bm - 1) // bm
    seg_src, seg_dst, table = _slots(seg_len, bm)

    xs = _dispatch(seg_len, seg_src, seg_dst, table, pos, h2, nb * bm, tm_r, bm)
    y = _experts(table, xs, w_gate_up.reshape(n_e, d, -1), b_gate_up.reshape(n_e, 1, -1),
                 w_down.reshape(n_e, -1, d), b_down.reshape(n_e, 1, d), bm)
    out = _combine(seg_len, seg_src, seg_dst, pos, wrow, y, x1, g2, norm_final.reshape(1, d), shf, scf, seq, tm_r)
    return out.reshape(batch, seq, d)
```

```python
import functools

import jax
import jax.numpy as jnp
from jax import lax
from jax.experimental import pallas as pl
from jax.experimental.pallas import tpu as pltpu

F32 = jnp.float32
BF16 = jnp.bfloat16

CHUNK = 64
CONV_W = 4
EPS = 1e-6

DN_HEADS = 4
DN_DK = 128
DN_DV = 128
ML_HEADS = 4
ML_DK = 64
ML_DV = 128
HEAD_LANES = 128

N_EXPERTS = 32
TOP_K = 4
SWIGLU_LIMIT = 7.0
SWIGLU_ALPHA = 1.702

C_DNQ = 0
C_DNK = 512
C_DNV = 1024
C_MLQ = 1536
C_MLK = 2048
N_CONV = 2560
C_DNZ = 2560
C_MLV = 3072
C_MLO = 3584
C_GATE = 4096
N_PROJ = 4224
N_REST = C_GATE - N_CONV

V7X_VMEM_LIMIT = 56 * 1024 * 1024
MXU_COLS = 256

NEG_BIG = -1e30


def _sigmoid(x):
    return 1.0 / (1.0 + jnp.exp(-x))


def _softplus(x):
    return jnp.maximum(x, 0.0) + jnp.log(1.0 + jnp.exp(-jnp.abs(x)))


def _split3(v):
    hi = v.astype(BF16)
    r1 = v - hi.astype(F32)
    mid = r1.astype(BF16)
    lo = (r1 - mid.astype(F32)).astype(BF16)
    return hi, mid, lo


def _dot(a, b):
    return jnp.dot(a, b, preferred_element_type=F32)


def _dot_nt(a, b):
    return lax.dot_general(a, b, (((1,), (1,)), ((), ())), preferred_element_type=F32)


def _dot_tn(a, b):
    return lax.dot_general(a, b, (((0,), (0,)), ((), ())), preferred_element_type=F32)


def _dot_exact_right(sel_bf16, v):
    hi, mid, lo = _split3(v)
    return _dot(sel_bf16, hi) + _dot(sel_bf16, mid) + _dot(sel_bf16, lo)


def _dot_exact_left(v, sel_bf16):
    hi, mid, lo = _split3(v)
    return _dot(hi, sel_bf16) + _dot(mid, sel_bf16) + _dot(lo, sel_bf16)


def _mods_kernel(c_ref, w_ref, b_ref, o_ref):
    c = c_ref[...]
    cond = c * _sigmoid(c)
    ch, cm, cl = _split3(cond)
    wh, wm, wl = _split3(w_ref[...])
    acc = _dot(ch, wh) + (_dot(ch, wm) + _dot(cm, wh)) + (_dot(ch, wl) + _dot(cm, wm) + _dot(cl, wh))
    o_ref[...] = acc + b_ref[...]


def _mods(c_pad, w, b):
    m, d = c_pad.shape
    n = w.shape[1]
    tn = 1024
    return pl.pallas_call(
        _mods_kernel,
        grid=(n // tn,),
        in_specs=[pl.BlockSpec((m, d), lambda j: (0, 0)),
                  pl.BlockSpec((d, tn), lambda j: (0, j)),
                  pl.BlockSpec((1, tn), lambda j: (0, j))],
        out_specs=pl.BlockSpec((m, tn), lambda j: (0, j)),
        out_shape=jax.ShapeDtypeStruct((m, n), F32),
        compiler_params=pltpu.CompilerParams(dimension_semantics=("arbitrary",),
                                             vmem_limit_bytes=V7X_VMEM_LIMIT),
        name="mods",
    )(c_pad, w, b)


def _gate_transform(v, bias, alog, cls):
    vb = v + bias
    beta = _sigmoid(v)
    g = -jnp.exp(alog) * _softplus(vb)
    logf = -_softplus(-vb)
    return jnp.where(cls == 0, beta, jnp.where(cls == 1, g, jnp.where(cls == 2, vb, jnp.where(cls == 3, logf, 0.0))))


def _inproj_kernel(tiles_per_seq, x_ref, sh_ref, sc_ref, nw_ref, w_ref, wgt_ref, cw_ref, gpc_ref, gpr_ref,
                   conv_ref, rest_ref, gcol_ref, grow_ref, cbuf):
    tm = x_ref.shape[0]
    i = pl.program_id(0)
    x = x_ref[...]
    ms = jnp.mean(x * x, axis=-1, keepdims=True)
    h = x * lax.rsqrt(ms + EPS) * nw_ref[...]
    h = h * (1.0 + sc_ref[...]) + sh_ref[...]
    hb = h.astype(BF16)

    @pl.when(i % tiles_per_seq == 0)
    def _():
        cbuf[0:8, :] = jnp.zeros((8, N_CONV), F32)

    group = 4 * HEAD_LANES
    for lo in range(0, N_CONV, group):
        cols = slice(lo, lo + group)
        pc = _dot(hb, w_ref[:, cols])
        cbuf[8:tm + 8, cols] = pc
        acc = cw_ref[CONV_W - 1:CONV_W, cols] * pc
        for j in range(CONV_W - 1):
            acc = acc + cw_ref[j:j + 1, cols] * cbuf[8 - (CONV_W - 1) + j:8 - (CONV_W - 1) + j + tm, cols]
        cbuf[0:8, cols] = cbuf[tm:tm + 8, cols]
        y = acc * _sigmoid(acc)
        if lo in (C_DNQ, C_DNK):
            scale = DN_DK ** -0.5 if lo == C_DNQ else 1.0
            for hh in range(DN_HEADS):
                uh = y[:, hh * HEAD_LANES:(hh + 1) * HEAD_LANES]
                un = uh * lax.rsqrt(jnp.sum(uh * uh, axis=-1, keepdims=True) + EPS)
                conv_ref[:, lo + hh * HEAD_LANES:lo + (hh + 1) * HEAD_LANES] = un * scale if lo == C_DNQ else un
        elif lo == C_MLQ:
            conv_ref[:, cols] = y * (ML_DK ** -0.5)
        else:
            conv_ref[:, cols] = y

    z = _dot(hb, w_ref[:, C_DNZ:C_MLV])
    rest_ref[:, 0:512] = z * _sigmoid(z)
    rest_ref[:, 512:1024] = _dot(hb, w_ref[:, C_MLV:C_MLO])
    rest_ref[:, 1024:1536] = _sigmoid(_dot(hb, w_ref[:, C_MLO:C_GATE]))

    r_i = lax.broadcasted_iota(jnp.int32, (tm, tm), 0)
    c_i = lax.broadcasted_iota(jnp.int32, (tm, tm), 1)
    same_chunk = (r_i // CHUNK) == (c_i // CHUNK)
    tril = jnp.where(same_chunk & (c_i <= r_i), 1.0, 0.0).astype(BF16)
    triu = jnp.where(same_chunk & (r_i <= c_i), 1.0, 0.0).astype(BF16)

    gc = _dot(hb, w_ref[:, C_GATE:N_PROJ])
    cls_c = lax.broadcasted_iota(jnp.int32, (tm, HEAD_LANES), 1) // 4
    gt = _gate_transform(gc, gpc_ref[0:1, :], gpc_ref[1:2, :], cls_c)
    cs = _dot_exact_right(tril, gt)
    gcol_ref[...] = jnp.where((cls_c == 1) | (cls_c == 3), cs, gt)

    gr = _dot_nt(wgt_ref[...], hb)
    cls_r = lax.broadcasted_iota(jnp.int32, (16, tm), 0) // 4
    gtr = _gate_transform(gr, gpr_ref[:, 0:1], gpr_ref[:, 1:2], cls_r)
    csr = _dot_exact_left(gtr, triu)
    grow_ref[...] = jnp.where((cls_r == 1) | (cls_r == 3), csr, gtr)


def _inproj(x2, sh, sc, nw, w_new, wgt, cw, gpc, gpr, seq, tm):
    t, d = x2.shape
    tps = seq // tm
    kern = functools.partial(_inproj_kernel, tps)
    return pl.pallas_call(
        kern,
        grid=(t // tm,),
        in_specs=[pl.BlockSpec((tm, d), lambda i: (i, 0)),
                  pl.BlockSpec((None, 1, d), lambda i: (i // tps, 0, 0)),
                  pl.BlockSpec((None, 1, d), lambda i: (i // tps, 0, 0)),
                  pl.BlockSpec((1, d), lambda i: (0, 0)),
                  pl.BlockSpec((d, N_PROJ), lambda i: (0, 0)),
                  pl.BlockSpec((16, d), lambda i: (0, 0)),
                  pl.BlockSpec((CONV_W, N_CONV), lambda i: (0, 0)),
                  pl.BlockSpec((8, HEAD_LANES), lambda i: (0, 0)),
                  pl.BlockSpec((16, HEAD_LANES), lambda i: (0, 0))],
        out_specs=[pl.BlockSpec((tm, N_CONV), lambda i: (i, 0)),
                   pl.BlockSpec((tm, N_REST), lambda i: (i, 0)),
                   pl.BlockSpec((tm, HEAD_LANES), lambda i: (i, 0)),
                   pl.BlockSpec((16, tm), lambda i: (0, i))],
        out_shape=[jax.ShapeDtypeStruct((t, N_CONV), F32),
                   jax.ShapeDtypeStruct((t, N_REST), F32),
                   jax.ShapeDtypeStruct((t, HEAD_LANES), F32),
                   jax.ShapeDtypeStruct((16, t), F32)],
        scratch_shapes=[pltpu.VMEM((tm + 8, N_CONV), F32)],
        compiler_params=pltpu.CompilerParams(dimension_semantics=("arbitrary",),
                                             vmem_limit_bytes=V7X_VMEM_LIMIT),
        name="inproj",
    )(x2, sh, sc, nw, w_new, wgt, cw, gpc, gpr)


def _chunk_masks():
    r = lax.broadcasted_iota(jnp.int32, (CHUNK, CHUNK), 0)
    c = lax.broadcasted_iota(jnp.int32, (CHUNK, CHUNK), 1)
    return r >= c, r > c, r == c


def _bdot(a, b):
    return lax.dot_general(a, b, (((2,), (1,)), ((0,), (0,))), preferred_element_type=F32)


def _bdot_nt(a, b):
    return lax.dot_general(a, b, (((2,), (2,)), ((0,), (0,))), preferred_element_type=F32)


def _unit_lower_inverse(lower, row, col):
    x = jnp.where(row == col, 1.0, 0.0) - jnp.where((row >> 1) == (col >> 1), lower, 0.0)
    shift = 1
    while (1 << shift) < CHUNK:
        couple = ((row >> (shift + 1)) == (col >> (shift + 1))) & ((row >> shift) != (col >> shift))
        cb = jnp.where(couple, lower, 0.0).astype(BF16)
        xb = x.astype(BF16)
        x = x - _bdot(_bdot(xb, cb).astype(BF16), xb)
        shift += 1
        yield
    return x


def _deltanet_steps(q_ref, k_ref, v_ref, gc_ref, gr_ref, z_ref, nw_ref, o_ref, s_ref):
    nc = gr_ref.shape[0]

    @pl.when(pl.program_id(1) == 0)
    def _():
        s_ref[...] = jnp.zeros(s_ref.shape, F32)

    row = lax.broadcasted_iota(jnp.int32, (CHUNK, CHUNK), 0)
    col = lax.broadcasted_iota(jnp.int32, (CHUNK, CHUNK), 1)
    incl = row >= col
    strict = row > col
    nw = nw_ref[...]
    gcc = gc_ref[...]
    grr = gr_ref[...]

    nh = DN_HEADS
    nb = nh * nc

    def heads(ref):
        return jnp.stack([ref[:, hh * HEAD_LANES:(hh + 1) * HEAD_LANES] for hh in range(nh)],
                         axis=0).reshape(nb, CHUNK, HEAD_LANES)

    def col_gate(lane0):
        return jnp.stack([gcc[:, lane0 + hh:lane0 + hh + 1] for hh in range(nh)], axis=0).reshape(nb, CHUNK, 1)

    q = heads(q_ref)
    k = heads(k_ref)
    v = heads(v_ref)
    beta = col_gate(0)
    g_c = col_gate(4)
    g_r = jnp.stack([grr[:, 4 + hh:5 + hh, :] for hh in range(nh)], axis=0).reshape(nb, 1, CHUNK)
    g_last = g_c[:, CHUNK - 1:CHUNK, :]
    decay = jnp.exp(jnp.where(incl, g_c - g_r, NEG_BIG))
    kb = k.astype(BF16)
    kk = _bdot_nt(kb, kb)
    lower = jnp.where(strict, beta * kk * decay, 0.0)
    yield
    tinv = yield from _unit_lower_inverse(lower, row, col)
    eg = jnp.exp(g_c)
    rhs = jnp.concatenate([v * beta, k * (beta * eg)], axis=-1)
    sol = _bdot(tinv.astype(BF16), rhs.astype(BF16))
    yield
    w_val = sol[:, :, 0:DN_DV].reshape(nh, nc, CHUNK, DN_DV)
    kq = jnp.concatenate([sol[:, :, DN_DV:DN_DV + DN_DK], q * eg], axis=1).astype(BF16)
    kq = kq.reshape(nh, nc, 2 * CHUNK, DN_DK)
    qk = (_bdot_nt(q.astype(BF16), kb) * decay).astype(BF16).reshape(nh, nc, CHUNK, CHUNK)
    k_dec_t = jnp.swapaxes(k * jnp.exp(g_last - g_c), 1, 2).astype(BF16).reshape(nh, nc, DN_DK, CHUNK)
    s_dec = jnp.exp(g_last).reshape(nh, nc, 1, 1)
    yield

    state = s_ref[...]
    outs = []
    for c in range(nc):
        both = _bdot(kq[:, c], state.astype(BF16))
        v_new = w_val[:, c] - both[:, 0:CHUNK]
        vb = v_new.astype(BF16)
        outs.append(both[:, CHUNK:2 * CHUNK] + _bdot(qk[:, c], vb))
        state = s_dec[:, c] * state + _bdot(k_dec_t[:, c], vb)
        yield
    s_ref[...] = state

    o = jnp.stack(outs, axis=1)
    on = o * lax.rsqrt(jnp.mean(o * o, axis=-1, keepdims=True) + EPS) * nw
    on = on.reshape(nh, nc * CHUNK, DN_DV)
    for hh in range(nh):
        lanes = slice(hh * HEAD_LANES, (hh + 1) * HEAD_LANES)
        o_ref[:, lanes] = on[hh] * z_ref[:, lanes]


def _mlstm_steps(q_ref, k_ref, v_ref, gc_ref, gr_ref, og_ref, nw_ref, o_ref, c_ref, n_ref, m_ref):
    n_chunks = gr_ref.shape[0]

    @pl.when(pl.program_id(1) == 0)
    def _():
        c_ref[...] = jnp.zeros(c_ref.shape, F32)
        n_ref[...] = jnp.zeros(n_ref.shape, F32)
        m_ref[...] = jnp.zeros(m_ref.shape, F32)

    incl, _, _ = _chunk_masks()
    nc = n_chunks
    gcc = gc_ref[...]
    grr = gr_ref[...]

    nh = ML_HEADS

    def heads(ref):
        return jnp.stack([ref[:, hh * HEAD_LANES:(hh + 1) * HEAD_LANES] for hh in range(nh)],
                         axis=0).reshape(nh, nc, CHUNK, HEAD_LANES)

    def col_gate(lane0):
        return jnp.stack([gcc[:, lane0 + hh:lane0 + hh + 1] for hh in range(nh)], axis=0).reshape(nh, nc, CHUNK, 1)

    def row_gate(row0):
        return jnp.stack([grr[:, row0 + hh:row0 + hh + 1, :] for hh in range(nh)], axis=0)

    q = heads(q_ref)
    k = heads(k_ref)
    v = heads(v_ref)
    i_c = col_gate(8)
    b_c = col_gate(12)
    i_r = row_gate(8)
    b_r = row_gate(12)
    b_last = b_c[:, :, CHUNK - 1:CHUNK, :]
    d_mat = jnp.where(incl, b_c - b_r + i_r, NEG_BIG)
    m_intra = jnp.max(d_mat, axis=-1, keepdims=True)
    g_end = b_last - b_c + i_c
    g_end_max = jnp.max(g_end, axis=2, keepdims=True)
    yield

    m_run = m_ref[:, 0:1, 0:1].reshape(nh, 1, 1, 1)
    m_before = []
    for c in range(nc):
        m_before.append(m_run)
        m_run = jnp.maximum(b_last[:, c:c + 1] + m_run, g_end_max[:, c:c + 1])
    m_s = jnp.concatenate(m_before, axis=1)
    m_new = jnp.maximum(b_last + m_s, g_end_max)
    keep = jnp.exp(b_last + m_s - m_new)
    yield

    nb = nh * nc
    qb = q.astype(BF16)
    kb = k.astype(BF16)
    vb = v.astype(BF16)
    m_t = jnp.maximum(b_c + m_s, m_intra)
    inter = jnp.exp(b_c + m_s - m_t)
    qk = _bdot_nt(qb.reshape(nb, CHUNK, HEAD_LANES), kb.reshape(nb, CHUNK, HEAD_LANES))
    p = jnp.exp(d_mat - m_t) * qk.reshape(nh, nc, CHUNK, CHUNK)
    yield
    intra = _bdot(p.astype(BF16).reshape(nb, CHUNK, CHUNK), vb.reshape(nb, CHUNK, ML_DV)).reshape(nh, nc, CHUNK, ML_DV)
    p_sum = jnp.sum(p, axis=-1, keepdims=True)
    yield
    kw = k * jnp.exp(g_end - m_new)
    kw_t = jnp.swapaxes(kw.reshape(nb, CHUNK, HEAD_LANES), 1, 2).astype(BF16)
    d_state = _bdot(kw_t, vb.reshape(nb, CHUNK, ML_DV)).reshape(nh, nc, HEAD_LANES, ML_DV)
    kw_sum = jnp.sum(kw, axis=2, keepdims=True)
    yield

    c_s = c_ref[...]
    n_s = n_ref[:, 0:1, :]
    q_c = []
    q_n = []
    for c in range(nc):
        q_c.append(_bdot(qb[:, c], c_s.astype(BF16)))
        q_n.append(jnp.sum(q[:, c] * n_s, axis=-1, keepdims=True))
        c_s = keep[:, c] * c_s + d_state[:, c]
        n_s = keep[:, c] * n_s + kw_sum[:, c]
        yield
    c_ref[...] = c_s
    n_ref[...] = jnp.broadcast_to(n_s, (nh, 8, HEAD_LANES))
    m_ref[...] = jnp.broadcast_to(m_run.reshape(nh, 1, 1), (nh, 8, HEAD_LANES))

    num = inter * jnp.stack(q_c, axis=1) + intra
    den = inter * jnp.stack(q_n, axis=1) + p_sum
    h = num / jnp.maximum(jnp.abs(den), jnp.exp(-m_t))
    hr = h * lax.rsqrt(jnp.mean(h * h, axis=-1, keepdims=True) + EPS)
    hr = hr.reshape(nh, nc * CHUNK, ML_DV)
    for hh in range(nh):
        lanes = slice(hh * HEAD_LANES, (hh + 1) * HEAD_LANES)
        o_ref[:, lanes] = hr[hh] * nw_ref[:, lanes] * og_ref[:, lanes]


def _mixers_kernel(dq_ref, dk_ref, dv_ref, gc_ref, gr_ref, z_ref, dnw_ref, mq_ref, mk_ref, mv_ref, og_ref, mnw_ref,
                   o_ref, s_ref, c_ref, n_ref, m_ref):
    n_a = DN_HEADS * HEAD_LANES
    n_b = ML_HEADS * HEAD_LANES
    stages = [_deltanet_steps(dq_ref, dk_ref, dv_ref, gc_ref, gr_ref, z_ref, dnw_ref, o_ref.at[:, 0:n_a], s_ref),
              _mlstm_steps(mq_ref, mk_ref, mv_ref, gc_ref, gr_ref, og_ref, mnw_ref, o_ref.at[:, n_a:n_a + n_b],
                           c_ref, n_ref, m_ref)]
    while stages:
        for stage in list(stages):
            try:
                next(stage)
            except StopIteration:
                stages.remove(stage)


def _mixers(conv_out, rest, gcol, grow3, dn_nw, ml_nw, batch, seq, rows):
    t = conv_out.shape[0]
    nj = seq // rows
    cpb = rows // CHUNK
    width = 4 * HEAD_LANES
    assert DN_HEADS * HEAD_LANES == width and ML_HEADS * HEAD_LANES == width

    def rmap(col):
        return lambda b, j: (b * nj + j, col)

    def conv_block(c0):
        return pl.BlockSpec((rows, width), rmap(c0 // width))

    def rest_block(c0):
        return pl.BlockSpec((rows, width), rmap((c0 - N_CONV) // width))

    gates = [pl.BlockSpec((rows, HEAD_LANES), rmap(0)),
             pl.BlockSpec((cpb, 16, CHUNK), lambda b, j: (b * nj + j, 0, 0))]
    return pl.pallas_call(
        _mixers_kernel,
        grid=(batch, nj),
        in_specs=[conv_block(C_DNQ), conv_block(C_DNK), conv_block(C_DNV)] + gates
                 + [rest_block(C_DNZ), pl.BlockSpec((1, HEAD_LANES), lambda b, j: (0, 0)),
                    conv_block(C_MLQ), conv_block(C_MLK), rest_block(C_MLV), rest_block(C_MLO),
                    pl.BlockSpec((1, width), lambda b, j: (0, 0))],
        out_specs=pl.BlockSpec((rows, 2 * width), rmap(0)),
        out_shape=jax.ShapeDtypeStruct((t, 2 * width), F32),
        scratch_shapes=[pltpu.VMEM((DN_HEADS, DN_DK, DN_DV), F32),
                        pltpu.VMEM((ML_HEADS, HEAD_LANES, ML_DV), F32),
                        pltpu.VMEM((ML_HEADS, 8, HEAD_LANES), F32),
                        pltpu.VMEM((ML_HEADS, 8, HEAD_LANES), F32)],
        compiler_params=pltpu.CompilerParams(dimension_semantics=("arbitrary", "arbitrary"),
                                             vmem_limit_bytes=V7X_VMEM_LIMIT),
        name="mixers",
    )(conv_out, conv_out, conv_out, gcol, grow3, rest, dn_nw, conv_out, conv_out, rest, rest, ml_nw)


def _route_kernel(y_ref, wo_ref, x_ref, g1_ref, sh_ref, sc_ref, nw_ref, wrt_ref, br_ref,
                  x1_ref, h2_ref, pos_ref, wrow_ref, len_ref):
    tm = x_ref.shape[0]

    mix = _dot(y_ref[...].astype(BF16), wo_ref[...])
    x1 = x_ref[...] + g1_ref[...] * mix
    x1_ref[...] = x1
    ms = jnp.mean(x1 * x1, axis=-1, keepdims=True)
    h2 = x1 * lax.rsqrt(ms + EPS) * nw_ref[...]
    h2 = h2 * (1.0 + sc_ref[...]) + sh_ref[...]
    h2_ref[...] = h2.astype(BF16)

    hh, hm, _ = _split3(h2)
    wh, wm, _ = _split3(wrt_ref[...])
    logits = _dot_nt(wh, hh) + (_dot_nt(wh, hm) + _dot_nt(wm, hh)) + br_ref[:, 0:1]

    e_i = lax.broadcasted_iota(jnp.int32, (N_EXPERTS, tm), 0)
    work = logits
    tops = []
    sels = []
    hots = []
    for _ in range(TOP_K):
        m = jnp.max(work, axis=0, keepdims=True)
        sel = jnp.min(jnp.where(work == m, e_i, N_EXPERTS), axis=0, keepdims=True)
        hot = e_i == sel
        work = jnp.where(hot, NEG_BIG, work)
        tops.append(m)
        sels.append(sel)
        hots.append(hot)
    exps = [jnp.exp(tl - tops[0]) for tl in tops]
    denom = exps[0] + exps[1] + exps[2] + exps[3]
    ws = [e / denom for e in exps]

    chosen = jnp.zeros((N_EXPERTS, tm), F32)
    for hot in hots:
        chosen = chosen + jnp.where(hot, 1.0, 0.0)

    r_i = lax.broadcasted_iota(jnp.int32, (tm, tm), 0)
    c_i = lax.broadcasted_iota(jnp.int32, (tm, tm), 1)
    strict_upper = jnp.where(r_i < c_i, 1.0, 0.0).astype(BF16)
    prefix = _dot(chosen.astype(BF16), strict_upper)
    n_e = jnp.sum(chosen, axis=1, keepdims=True)
    len8 = jnp.ceil(n_e * 0.125) * 8.0
    er = lax.broadcasted_iota(jnp.int32, (N_EXPERTS, N_EXPERTS), 0)
    ec = lax.broadcasted_iota(jnp.int32, (N_EXPERTS, N_EXPERTS), 1)
    strict_lower = jnp.where(ec < er, 1.0, 0.0).astype(BF16)
    len8b = jnp.broadcast_to(len8, (N_EXPERTS, HEAD_LANES))
    off8 = _dot_exact_right(strict_lower, len8b)[:, 0:1]
    len_ref[...] = len8b.astype(jnp.int32)

    for kk in range(TOP_K):
        wrow_ref[kk:kk + 1, :] = ws[kk]
        pos = jnp.sum(jnp.where(hots[kk], prefix + off8, 0.0), axis=0, keepdims=True)
        pos_ref[kk:kk + 1, :] = pos.astype(jnp.int32)


def _route(ymix, wo, x2, g1, sh2, sc2, nw, wrt, br, seq, tm):
    t, d = x2.shape
    tps = seq // tm
    bmap = lambda i: (i // tps, 0, 0)
    return pl.pallas_call(
        _route_kernel,
        grid=(t // tm,),
        in_specs=[pl.BlockSpec((tm, ymix.shape[1]), lambda i: (i, 0)),
                  pl.BlockSpec(wo.shape, lambda i: (0, 0)),
                  pl.BlockSpec((tm, d), lambda i: (i, 0)),
                  pl.BlockSpec((None, 1, d), bmap),
                  pl.BlockSpec((None, 1, d), bmap),
                  pl.BlockSpec((None, 1, d), bmap),
                  pl.BlockSpec((1, d), lambda i: (0, 0)),
                  pl.BlockSpec((N_EXPERTS, d), lambda i: (0, 0)),
                  pl.BlockSpec((N_EXPERTS, HEAD_LANES), lambda i: (0, 0))],
        out_specs=[pl.BlockSpec((tm, d), lambda i: (i, 0)),
                   pl.BlockSpec((tm, d), lambda i: (i, 0)),
                   pl.BlockSpec((TOP_K, tm), lambda i: (0, i)),
                   pl.BlockSpec((TOP_K, tm), lambda i: (0, i)),
                   pl.BlockSpec((N_EXPERTS, HEAD_LANES), lambda i: (i, 0))],
        out_shape=[jax.ShapeDtypeStruct((t, d), F32),
                   jax.ShapeDtypeStruct((t, d), BF16),
                   jax.ShapeDtypeStruct((TOP_K, t), jnp.int32),
                   jax.ShapeDtypeStruct((TOP_K, t), F32),
                   jax.ShapeDtypeStruct((t // tm * N_EXPERTS, HEAD_LANES), jnp.int32)],
        compiler_params=pltpu.CompilerParams(dimension_semantics=("arbitrary",),
                                             vmem_limit_bytes=V7X_VMEM_LIMIT),
        name="route",
    )(ymix, wo, x2, g1, sh2, sc2, nw, wrt, br)


TAIL_START, TAIL_LEN, PAD_END, BLOCK_START, BLOCK_COUNT = range(5)


def _slots_kernel(bm, len_te_ref, ssrc_ref, sdst_ref, tab_ref):
    nt, lanes = len_te_ref.shape
    len_te = len_te_ref[...].astype(F32)

    r_l = lax.broadcasted_iota(jnp.int32, (lanes, lanes), 0)
    c_l = lax.broadcasted_iota(jnp.int32, (lanes, lanes), 1)
    upper_incl = jnp.where(r_l <= c_l, 1.0, 0.0).astype(BF16)
    upper_strict = jnp.where(r_l < c_l, 1.0, 0.0).astype(BF16)
    total_r = jnp.sum(len_te, axis=0, keepdims=True)
    padded_r = jnp.ceil(total_r * (1.0 / bm)) * bm
    pad_end_r = _dot_exact_left(jnp.broadcast_to(padded_r, (8, lanes)), upper_incl)[0:1, :]
    pad_start_r = pad_end_r - padded_r
    r_t = lax.broadcasted_iota(jnp.int32, (nt, nt), 0)
    c_t = lax.broadcasted_iota(jnp.int32, (nt, nt), 1)
    lower_strict_t = jnp.where(c_t < r_t, 1.0, 0.0).astype(BF16)
    before = _dot_exact_right(lower_strict_t, len_te)
    ssrc_ref[...] = _dot_exact_left(len_te, upper_strict).astype(jnp.int32)
    sdst_ref[...] = (pad_start_r + before).astype(jnp.int32)
    tab_ref[TAIL_START:TAIL_START + 1, :] = (pad_start_r + total_r).astype(jnp.int32)
    tab_ref[TAIL_LEN:TAIL_LEN + 1, :] = (padded_r - total_r).astype(jnp.int32)
    tab_ref[PAD_END:PAD_END + 1, :] = pad_end_r.astype(jnp.int32)
    tab_ref[BLOCK_START:BLOCK_START + 1, :] = (pad_start_r * (1.0 / bm)).astype(jnp.int32)
    tab_ref[BLOCK_COUNT:BLOCK_COUNT + 1, :] = (padded_r * (1.0 / bm)).astype(jnp.int32)
    tab_ref[5:8, :] = jnp.zeros((3, lanes), jnp.int32)


def _slots(len_te, bm):
    assert bm & (bm - 1) == 0, "block rows must be a power of two"
    nt, lanes = len_te.shape
    return pl.pallas_call(
        functools.partial(_slots_kernel, bm),
        out_shape=[jax.ShapeDtypeStruct((nt, lanes), jnp.int32),
                   jax.ShapeDtypeStruct((nt, lanes), jnp.int32),
                   jax.ShapeDtypeStruct((8, lanes), jnp.int32)],
        compiler_params=pltpu.CompilerParams(vmem_limit_bytes=V7X_VMEM_LIMIT),
        name="slots",
    )(len_te)


def _segment_pieces(max_rows):
    sizes = []
    s = 8
    while s <= max_rows:
        sizes.append(s)
        s *= 2
    return sizes[::-1]


def _segment_dma(src_ref, src0, dst_ref, dst0, nrows, sizes, sem, start):
    off = 0
    for sz in sizes:
        bit = nrows & sz

        @pl.when(bit != 0)
        def _(off=off, sz=sz):
            s0 = pl.multiple_of(src0 + off, 8)
            d0 = pl.multiple_of(dst0 + off, 8)
            cp = pltpu.make_async_copy(src_ref.at[pl.ds(s0, sz)], dst_ref.at[pl.ds(d0, sz)], sem)
            if start:
                cp.start()
            else:
                cp.wait()

        off = off + bit


def _sorted_onehot(pos_ref, nrows, tt):
    r_i = lax.broadcasted_iota(jnp.int32, (nrows, tt), 0)
    hit = r_i == pos_ref[0:1, :]
    for kk in range(1, TOP_K):
        hit = hit | (r_i == pos_ref[kk:kk + 1, :])
    return hit


def _dispatch_kernel(bm, len_ref, ssrc_ref, sdst_ref, tail_ref, pos_ref, h2_ref, xs_hbm, buf, sem):
    i = pl.program_id(0)
    nt = pl.num_programs(0)
    tt = h2_ref.shape[0]
    nrows = buf.shape[1]
    sizes = _segment_pieces(tt)
    slot = i % 2

    perm = jnp.where(_sorted_onehot(pos_ref, nrows, tt), 1.0, 0.0).astype(BF16)
    buf[slot] = _dot(perm, h2_ref[...])

    def seg(tile, which, start):
        def body(e, carry):
            _segment_dma(buf.at[which], ssrc_ref[tile, e], xs_hbm, sdst_ref[tile, e], len_ref[tile, e], sizes,
                         sem.at[which], start)
            return carry
        lax.fori_loop(0, N_EXPERTS, body, 0)

    @pl.when(i > 0)
    def _():
        seg(i - 1, 1 - slot, False)

    seg(i, slot, True)

    @pl.when(i == nt - 1)
    def _():
        seg(i, slot, False)
        zrows = bm
        zbuf = buf.at[0]
        zsem = sem.at[0]
        buf[0, 0:zrows, :] = jnp.zeros((zrows, buf.shape[2]), F32)

        def tail(start):
            def body(e, carry):
                _segment_dma(zbuf, 0, xs_hbm, tail_ref[TAIL_START, e], tail_ref[TAIL_LEN, e],
                             _segment_pieces(zrows // 2), zsem, start)
                return carry
            lax.fori_loop(0, N_EXPERTS, body, 0)

        tail(True)
        tail(False)

        used = tail_ref[PAD_END, tail_ref.shape[1] - 1]
        n_unused = (xs_hbm.shape[0] - used) // zrows

        def unused_copy(j):
            d0 = pl.multiple_of(used + j * zrows, 8)
            return pltpu.make_async_copy(zbuf.at[pl.ds(0, zrows)], xs_hbm.at[pl.ds(d0, zrows)], zsem)

        def unused_start(j, carry):
            unused_copy(j).start()
            return carry

        def unused_wait(j, carry):
            unused_copy(j).wait()
            return carry

        lax.fori_loop(0, n_unused, unused_start, 0)
        lax.fori_loop(0, n_unused, unused_wait, 0)


def _dispatch(seg_len, seg_src, seg_dst, tail, pos, h2, n_slots, tt, bm):
    t, d = h2.shape
    nrows = TOP_K * tt + 8 * N_EXPERTS
    assert bm <= nrows
    smem = pl.BlockSpec(memory_space=pltpu.SMEM)
    return pl.pallas_call(
        functools.partial(_dispatch_kernel, bm),
        grid=(t // tt,),
        in_specs=[smem, smem, smem, smem,
                  pl.BlockSpec((TOP_K, tt), lambda i: (0, i)),
                  pl.BlockSpec((tt, d), lambda i: (i, 0))],
        out_specs=pl.BlockSpec(memory_space=pl.ANY),
        out_shape=jax.ShapeDtypeStruct((n_slots, d), F32),
        scratch_shapes=[pltpu.VMEM((2, nrows, d), F32), pltpu.SemaphoreType.DMA((2,))],
        compiler_params=pltpu.CompilerParams(dimension_semantics=("arbitrary",), has_side_effects=True,
                                             vmem_limit_bytes=V7X_VMEM_LIMIT),
        name="dispatch",
    )(seg_len, seg_src, seg_dst, tail, pos, h2)


def _experts_kernel(tab_ref, xs_hbm, wgu_ref, bgu_ref, wd_ref, bd_ref, y_hbm, wgu_s, wd_s, xbuf, ybuf, sem_in,
                    sem_out):
    e = pl.program_id(0)
    bm = xbuf.shape[1] // 2
    d_ff = wd_ref.shape[0]
    first = tab_ref[BLOCK_START, e]
    count = tab_ref[BLOCK_COUNT, e]
    npair = count // 2
    odd = count % 2
    tslot = npair % 2

    def pair_rows(j):
        return pl.ds(pl.multiple_of((first + 2 * j) * bm, bm), 2 * bm)

    def x_copy(j, slot):
        return pltpu.make_async_copy(xs_hbm.at[pair_rows(j)], xbuf.at[slot], sem_in.at[slot])

    def y_copy(j, slot):
        return pltpu.make_async_copy(ybuf.at[slot], y_hbm.at[pair_rows(j)], sem_out.at[slot])

    def last_rows():
        return pl.ds(pl.multiple_of((first + 2 * npair) * bm, bm), bm)

    def x_last(slot):
        return pltpu.make_async_copy(xs_hbm.at[last_rows()], xbuf.at[slot, pl.ds(0, bm)], sem_in.at[slot])

    def y_last(slot):
        return pltpu.make_async_copy(ybuf.at[slot, pl.ds(0, bm)], y_hbm.at[last_rows()], sem_out.at[slot])

    @pl.when(npair > 0)
    def _():
        x_copy(0, 0).start()

    @pl.when((npair == 0) & (odd == 1))
    def _():
        x_last(0).start()

    ct = wgu_s.shape[2]
    for t in range(wgu_s.shape[0]):
        wgu_s[t] = wgu_ref[:, t * ct:(t + 1) * ct].astype(BF16)
    for t in range(wd_s.shape[0]):
        wd_s[t] = wd_ref[:, t * ct:(t + 1) * ct].astype(BF16)

    def mlp(x):
        xb = x.astype(BF16)
        tpc = 2
        fc = tpc * ct
        nf = d_ff // fc
        n_out = wd_s.shape[0]

        def gate_up(f):
            gates, ups = [], []
            for t in range(f * tpc, (f + 1) * tpc):
                gates.append(_dot(xb, wgu_s[t]) + bgu_ref[:, t * ct:(t + 1) * ct])
                u = d_ff // ct + t
                ups.append(_dot(xb, wgu_s[u]) + bgu_ref[:, u * ct:(u + 1) * ct])
            return jnp.concatenate(gates, axis=-1), jnp.concatenate(ups, axis=-1)

        acts = []
        pre = gate_up(0)
        for f in range(nf):
            gate = jnp.minimum(pre[0], SWIGLU_LIMIT)
            up = jnp.clip(pre[1], -SWIGLU_LIMIT, SWIGLU_LIMIT)
            if f + 1 < nf:
                pre = gate_up(f + 1)
            acts.append(((up + 1.0) * gate * _sigmoid(SWIGLU_ALPHA * gate)).astype(BF16))
        act = jnp.concatenate(acts, axis=-1)
        return [_dot(act, wd_s[n]) + bd_ref[:, n * ct:(n + 1) * ct] for n in range(n_out)]

    def pair(j, carry):
        slot = j % 2
        x_copy(j, slot).wait()

        @pl.when(j + 1 < npair)
        def _():
            x_copy(j + 1, 1 - slot).start()

        @pl.when((j + 1 == npair) & (odd == 1))
        def _():
            x_last(1 - slot).start()

        @pl.when(j >= 2)
        def _():
            y_copy(j - 2, slot).wait()

        for n, piece in enumerate(mlp(xbuf[slot])):
            ybuf[slot, :, n * ct:(n + 1) * ct] = piece
        y_copy(j, slot).start()
        return carry

    lax.fori_loop(0, npair, pair, 0)

    @pl.when(odd == 1)
    def _():
        x_last(tslot).wait()

        @pl.when(npair >= 2)
        def _():
            y_copy(npair - 2, tslot).wait()

        for n, piece in enumerate(mlp(xbuf[tslot, 0:bm, :])):
            ybuf[tslot, 0:bm, n * ct:(n + 1) * ct] = piece
        y_last(tslot).start()

    @pl.when((odd == 0) & (npair >= 2))
    def _():
        y_copy(npair - 2, tslot).wait()

    @pl.when(npair >= 1)
    def _():
        y_copy(npair - 1, 1 - tslot).wait()

    @pl.when(odd == 1)
    def _():
        y_last(tslot).wait()

    @pl.when(e == pl.num_programs(0) - 1)
    def _():
        used = first + count
        n_unused = y_hbm.shape[0] // bm - used
        ybuf[0, 0:bm, :] = jnp.zeros((bm, ybuf.shape[2]), F32)

        def z_copy(j):
            d0 = pl.multiple_of((used + j) * bm, bm)
            return pltpu.make_async_copy(ybuf.at[0, pl.ds(0, bm)], y_hbm.at[pl.ds(d0, bm)], sem_out.at[0])

        def z_start(j, carry):
            z_copy(j).start()
            return carry

        def z_wait(j, carry):
            z_copy(j).wait()
            return carry

        lax.fori_loop(0, n_unused, z_start, 0)
        lax.fori_loop(0, n_unused, z_wait, 0)


def _experts(table, xs, wgu, bgu, wd, bd, bm):
    ns, d = xs.shape
    n_e, _, two_ff = wgu.shape
    d_ff = two_ff // 2
    wmap = lambda e: (e, 0, 0)
    return pl.pallas_call(
        _experts_kernel,
        grid=(n_e,),
        in_specs=[pl.BlockSpec(memory_space=pltpu.SMEM),
                  pl.BlockSpec(memory_space=pl.ANY),
                  pl.BlockSpec((None, d, two_ff), wmap),
                  pl.BlockSpec((None, 1, two_ff), wmap),
                  pl.BlockSpec((None, d_ff, d), wmap),
                  pl.BlockSpec((None, 1, d), wmap)],
        out_specs=pl.BlockSpec(memory_space=pl.ANY),
        out_shape=jax.ShapeDtypeStruct((ns, d), F32),
        scratch_shapes=[pltpu.VMEM((two_ff // MXU_COLS, d, MXU_COLS), BF16),
                        pltpu.VMEM((d // MXU_COLS, d_ff, MXU_COLS), BF16),
                        pltpu.VMEM((2, 2 * bm, d), F32), pltpu.VMEM((2, 2 * bm, d), F32),
                        pltpu.SemaphoreType.DMA((2,)), pltpu.SemaphoreType.DMA((2,))],
        compiler_params=pltpu.CompilerParams(dimension_semantics=("arbitrary",), has_side_effects=True,
                                             vmem_limit_bytes=V7X_VMEM_LIMIT),
        name="experts",
    )(table, xs, wgu, bgu, wd, bd)


def _combine_kernel(len_ref, ssrc_ref, sdst_ref, pos_ref, w_ref, y_hbm, x1_ref, g2_ref, nw_ref, sh_ref, sc_ref,
                    o_ref, ybuf, sem):
    i = pl.program_id(0)
    nt = pl.num_programs(0)
    tt = x1_ref.shape[0]
    nrows = ybuf.shape[1]
    sizes = _segment_pieces(tt)
    slot = i % 2

    def seg(tile, which, start):
        def body(e, carry):
            _segment_dma(y_hbm, sdst_ref[tile, e], ybuf.at[which], ssrc_ref[tile, e], len_ref[tile, e], sizes,
                         sem.at[which], start)
            return carry
        lax.fori_loop(0, N_EXPERTS, body, 0)

    @pl.when(i == 0)
    def _():
        ybuf[...] = jnp.zeros(ybuf.shape, F32)
        seg(0, 0, True)

    @pl.when(i + 1 < nt)
    def _():
        seg(i + 1, 1 - slot, True)

    seg(i, slot, False)

    r_i = lax.broadcasted_iota(jnp.int32, (nrows, tt), 0)
    wmat = jnp.where(r_i == pos_ref[0:1, :], w_ref[0:1, :], 0.0)
    for kk in range(1, TOP_K):
        wmat = wmat + jnp.where(r_i == pos_ref[kk:kk + 1, :], w_ref[kk:kk + 1, :], 0.0)
    acc = _dot_tn(wmat.astype(BF16), ybuf[slot].astype(BF16))
    xo = x1_ref[...] + g2_ref[...] * acc
    ms = jnp.mean(xo * xo, axis=-1, keepdims=True)
    hn = xo * lax.rsqrt(ms + EPS) * nw_ref[...]
    o_ref[...] = hn * (1.0 + sc_ref[...]) + sh_ref[...]


def _combine(seg_len, seg_src, seg_dst, pos, wrow, y, x1, g2, nw, shf, scf, seq, tt):
    t, d = x1.shape
    tps = seq // tt
    bmap = lambda i: (i // tps, 0, 0)
    nrows = TOP_K * tt + 8 * N_EXPERTS
    smem = pl.BlockSpec(memory_space=pltpu.SMEM)
    return pl.pallas_call(
        _combine_kernel,
        grid=(t // tt,),
        in_specs=[smem, smem, smem,
                  pl.BlockSpec((TOP_K, tt), lambda i: (0, i)),
                  pl.BlockSpec((TOP_K, tt), lambda i: (0, i)),
                  pl.BlockSpec(memory_space=pl.ANY),
                  pl.BlockSpec((tt, d), lambda i: (i, 0)),
                  pl.BlockSpec((None, 1, d), bmap),
                  pl.BlockSpec((1, d), lambda i: (0, 0)),
                  pl.BlockSpec((None, 1, d), bmap),
                  pl.BlockSpec((None, 1, d), bmap)],
        out_specs=pl.BlockSpec((tt, d), lambda i: (i, 0)),
        out_shape=jax.ShapeDtypeStruct((t, d), F32),
        scratch_shapes=[pltpu.VMEM((2, nrows, d), F32), pltpu.SemaphoreType.DMA((2,))],
        compiler_params=pltpu.CompilerParams(dimension_semantics=("arbitrary",),
                                             vmem_limit_bytes=V7X_VMEM_LIMIT),
        name="combine",
    )(seg_len, seg_src, seg_dst, pos, wrow, y, x1, g2, nw, shf, scf)


def _pad_heads(w, heads, dk):
    r = w.shape[0]
    w3 = w.reshape(r, heads, dk)
    return jnp.pad(w3, ((0, 0), (0, 0), (0, HEAD_LANES - dk))).reshape(r, heads * HEAD_LANES)


def _pick_tile(n, pref):
    tile = pref
    while n % tile:
        tile //= 2
    return tile


def kernel(x, c, w_ada, b_ada, norm_mix, w_in, dn_conv, dn_a_log, dn_dt_bias, dn_norm, ml_conv, ml_i_bias,
           ml_f_bias, ml_norm, w_out, norm_ffn, w_router, b_router, w_gate_up, b_gate_up, w_down, b_down,
           w_ada_final, b_ada_final, norm_final):
    batch, seq, d = x.shape
    assert w_ada.shape[0] == 1, "single-layer block"
    assert seq % CHUNK == 0
    t = batch * seq
    x2 = x.reshape(t, d)

    c_pad = jnp.pad(c, ((0, 8 - batch % 8 if batch % 8 else 0), (0, 0)))
    mod = _mods(c_pad, w_ada.reshape(d, 6 * d), b_ada.reshape(1, 6 * d))[:batch]
    modf = _mods(c_pad, w_ada_final, b_ada_final.reshape(1, 2 * d))[:batch]
    sh1, sc1, g1, sh2, sc2, g2 = [mod[:, None, j * d:(j + 1) * d] for j in range(6)]
    shf, scf = modf[:, None, 0:d], modf[:, None, d:2 * d]

    wi = w_in.reshape(d, -1)
    o_z = 1536
    o_b = 2048
    o_mq = 2056
    o_mk = o_mq + ML_HEADS * ML_DK
    o_mv = o_mk + ML_HEADS * ML_DK
    o_mo = o_mv + ML_HEADS * ML_DV
    o_mi = o_mo + ML_HEADS * ML_DV
    gates = jnp.concatenate([wi[:, o_b:o_mq], wi[:, o_mi:o_mi + 2 * ML_HEADS]], axis=1)
    w_new = jnp.concatenate([
        wi[:, 0:o_z],
        _pad_heads(wi[:, o_mq:o_mk], ML_HEADS, ML_DK),
        _pad_heads(wi[:, o_mk:o_mv], ML_HEADS, ML_DK),
        wi[:, o_z:o_b],
        wi[:, o_mv:o_mo],
        wi[:, o_mo:o_mi],
        jnp.pad(gates, ((0, 0), (0, HEAD_LANES - 16))),
    ], axis=1).astype(BF16)
    wgt = gates.T.astype(BF16)
    mlc = ml_conv.reshape(CONV_W, -1)
    cw = jnp.concatenate([dn_conv.reshape(CONV_W, -1),
                          _pad_heads(mlc[:, 0:ML_HEADS * ML_DK], ML_HEADS, ML_DK),
                          _pad_heads(mlc[:, ML_HEADS * ML_DK:], ML_HEADS, ML_DK)], axis=1)
    zeros4 = jnp.zeros((4,), F32)
    bias16 = jnp.concatenate([zeros4, dn_dt_bias.reshape(4), ml_i_bias.reshape(4), ml_f_bias.reshape(4)])
    alog16 = jnp.concatenate([zeros4, dn_a_log.reshape(4), zeros4, zeros4])
    gpc = jnp.zeros((8, HEAD_LANES), F32).at[0, 0:16].set(bias16).at[1, 0:16].set(alog16)
    gpr = jnp.zeros((16, HEAD_LANES), F32).at[:, 0].set(bias16).at[:, 1].set(alog16)

    tm_in = _pick_tile(seq, 256)
    conv_out, rest, gcol, grow = _inproj(x2, sh1, sc1, norm_mix.reshape(1, d), w_new, wgt, cw, gpc, gpr, seq, tm_in)
    grow3 = grow.reshape(16, t // CHUNK, CHUNK).transpose(1, 0, 2)

    rows = _pick_tile(seq, 512)
    ymix = _mixers(conv_out, rest, gcol, grow3, dn_norm.reshape(1, DN_DV), ml_norm.reshape(1, ML_HEADS * ML_DV),
                   batch, seq, rows)

    wo = w_out.reshape(-1, d).astype(BF16)
    tm_r = _pick_tile(seq, 512)
    brp = jnp.broadcast_to(b_router.reshape(N_EXPERTS, 1), (N_EXPERTS, HEAD_LANES))
    x1, h2, pos, wrow, len_col = _route(
        ymix, wo, x2, g1, sh2, sc2, norm_ffn.reshape(1, d),
        w_router.reshape(d, N_EXPERTS).T, brp, seq, tm_r)

    n_e = N_EXPERTS
    nt = t // tm_r
    bm = 256
    len_te =len_col.reshape(nt, n_e, HEAD_LANES)[:, :, 0]
    seg_len = jnp.pad(len_te, ((0, 0), (0, HEAD_LANES - n_e)))
    n_slots_max = t * TOP_K + n_e * (7 * nt + bm)
    nb = (n_slots_max + bm - 1) // bm
    seg_src, seg_dst, table = _slots(seg_len, bm)

    xs = _dispatch(seg_len, seg_src, seg_dst, table, pos, h2, nb * bm, tm_r, bm)
    y = _experts(table, xs, w_gate_up.reshape(n_e, d, -1), b_gate_up.reshape(n_e, 1, -1),
                 w_down.reshape(n_e, -1, d), b_down.reshape(n_e, 1, d), bm)
    out = _combine(seg_len, seg_src, seg_dst, pos, wrow, y, x1, g2, norm_final.reshape(1, d), shf, scf, seq, tm_r)
    return out.reshape(batch, seq, d)
```

```python
import functools

import jax
import jax.numpy as jnp
from jax import lax
from jax.experimental import pallas as pl
from jax.experimental.pallas import tpu as pltpu

F32 = jnp.float32
BF16 = jnp.bfloat16

CHUNK = 64
CONV_W = 4
EPS = 1e-6

DN_HEADS = 4
DN_DK = 128
DN_DV = 128
ML_HEADS = 4
ML_DK = 64
ML_DV = 128
HEAD_LANES = 128

N_EXPERTS = 32
TOP_K = 4
SWIGLU_LIMIT = 7.0
SWIGLU_ALPHA = 1.702

C_DNQ = 0
C_DNK = 512
C_DNV = 1024
C_MLQ = 1536
C_MLK = 2048
N_CONV = 2560
C_DNZ = 2560
C_MLV = 3072
C_MLO = 3584
C_GATE = 4096
N_PROJ = 4224
N_REST = C_GATE - N_CONV

V7X_VMEM_LIMIT = 56 * 1024 * 1024
MXU_COLS = 256

NEG_BIG = -1e30


def _sigmoid(x):
    return 1.0 / (1.0 + jnp.exp(-x))


def _softplus(x):
    return jnp.maximum(x, 0.0) + jnp.log(1.0 + jnp.exp(-jnp.abs(x)))


def _split3(v):
    hi = v.astype(BF16)
    r1 = v - hi.astype(F32)
    mid = r1.astype(BF16)
    lo = (r1 - mid.astype(F32)).astype(BF16)
    return hi, mid, lo


def _dot(a, b):
    return jnp.dot(a, b, preferred_element_type=F32)


def _dot_nt(a, b):
    return lax.dot_general(a, b, (((1,), (1,)), ((), ())), preferred_element_type=F32)


def _dot_tn(a, b):
    return lax.dot_general(a, b, (((0,), (0,)), ((), ())), preferred_element_type=F32)


def _dot_exact_right(sel_bf16, v):
    hi, mid, lo = _split3(v)
    return _dot(sel_bf16, hi) + _dot(sel_bf16, mid) + _dot(sel_bf16, lo)


def _dot_exact_left(v, sel_bf16):
    hi, mid, lo = _split3(v)
    return _dot(hi, sel_bf16) + _dot(mid, sel_bf16) + _dot(lo, sel_bf16)


def _mods_kernel(c_ref, w_ref, b_ref, o_ref):
    c = c_ref[...]
    cond = c * _sigmoid(c)
    ch, cm, cl = _split3(cond)
    wh, wm, wl = _split3(w_ref[...])
    acc = _dot(ch, wh) + (_dot(ch, wm) + _dot(cm, wh)) + (_dot(ch, wl) + _dot(cm, wm) + _dot(cl, wh))
    o_ref[...] = acc + b_ref[...]


def _mods(c_pad, w, b):
    m, d = c_pad.shape
    n = w.shape[1]
    tn = 1024
    return pl.pallas_call(
        _mods_kernel,
        grid=(n // tn,),
        in_specs=[pl.BlockSpec((m, d), lambda j: (0, 0)),
                  pl.BlockSpec((d, tn), lambda j: (0, j)),
                  pl.BlockSpec((1, tn), lambda j: (0, j))],
        out_specs=pl.BlockSpec((m, tn), lambda j: (0, j)),
        out_shape=jax.ShapeDtypeStruct((m, n), F32),
        compiler_params=pltpu.CompilerParams(dimension_semantics=("arbitrary",),
                                             vmem_limit_bytes=V7X_VMEM_LIMIT),
        name="mods",
    )(c_pad, w, b)


def _gate_transform(v, bias, alog, cls):
    vb = v + bias
    beta = _sigmoid(v)
    g = -jnp.exp(alog) * _softplus(vb)
    logf = -_softplus(-vb)
    return jnp.where(cls == 0, beta, jnp.where(cls == 1, g, jnp.where(cls == 2, vb, jnp.where(cls == 3, logf, 0.0))))


def _inproj_kernel(tiles_per_seq, x_ref, sh_ref, sc_ref, nw_ref, w_ref, wgt_ref, cw_ref, gpc_ref, gpr_ref,
                   conv_ref, rest_ref, gcol_ref, grow_ref, cbuf):
    tm = x_ref.shape[0]
    i = pl.program_id(0)
    x = x_ref[...]
    ms = jnp.mean(x * x, axis=-1, keepdims=True)
    h = x * lax.rsqrt(ms + EPS) * nw_ref[...]
    h = h * (1.0 + sc_ref[...]) + sh_ref[...]
    hb = h.astype(BF16)

    @pl.when(i % tiles_per_seq == 0)
    def _():
        cbuf[0:8, :] = jnp.zeros((8, N_CONV), F32)

    group = 4 * HEAD_LANES
    for lo in range(0, N_CONV, group):
        cols = slice(lo, lo + group)
        pc = _dot(hb, w_ref[:, cols])
        cbuf[8:tm + 8, cols] = pc
        acc = cw_ref[CONV_W - 1:CONV_W, cols] * pc
        for j in range(CONV_W - 1):
            acc = acc + cw_ref[j:j + 1, cols] * cbuf[8 - (CONV_W - 1) + j:8 - (CONV_W - 1) + j + tm, cols]
        cbuf[0:8, cols] = cbuf[tm:tm + 8, cols]
        y = acc * _sigmoid(acc)
        if lo in (C_DNQ, C_DNK):
            scale = DN_DK ** -0.5 if lo == C_DNQ else 1.0
            for hh in range(DN_HEADS):
                uh = y[:, hh * HEAD_LANES:(hh + 1) * HEAD_LANES]
                un = uh * lax.rsqrt(jnp.sum(uh * uh, axis=-1, keepdims=True) + EPS)
                conv_ref[:, lo + hh * HEAD_LANES:lo + (hh + 1) * HEAD_LANES] = un * scale if lo == C_DNQ else un
        elif lo == C_MLQ:
            conv_ref[:, cols] = y * (ML_DK ** -0.5)
        else:
            conv_ref[:, cols] = y

    z = _dot(hb, w_ref[:, C_DNZ:C_MLV])
    rest_ref[:, 0:512] = z * _sigmoid(z)
    rest_ref[:, 512:1024] = _dot(hb, w_ref[:, C_MLV:C_MLO])
    rest_ref[:, 1024:1536] = _sigmoid(_dot(hb, w_ref[:, C_MLO:C_GATE]))

    r_i = lax.broadcasted_iota(jnp.int32, (tm, tm), 0)
    c_i = lax.broadcasted_iota(jnp.int32, (tm, tm), 1)
    same_chunk = (r_i // CHUNK) == (c_i // CHUNK)
    tril = jnp.where(same_chunk & (c_i <= r_i), 1.0, 0.0).astype(BF16)
    triu = jnp.where(same_chunk & (r_i <= c_i), 1.0, 0.0).astype(BF16)

    gc = _dot(hb, w_ref[:, C_GATE:N_PROJ])
    cls_c = lax.broadcasted_iota(jnp.int32, (tm, HEAD_LANES), 1) // 4
    gt = _gate_transform(gc, gpc_ref[0:1, :], gpc_ref[1:2, :], cls_c)
    cs = _dot_exact_right(tril, gt)
    gcol_ref[...] = jnp.where((cls_c == 1) | (cls_c == 3), cs, gt)

    gr = _dot_nt(wgt_ref[...], hb)
    cls_r = lax.broadcasted_iota(jnp.int32, (16, tm), 0) // 4
    gtr = _gate_transform(gr, gpr_ref[:, 0:1], gpr_ref[:, 1:2], cls_r)
    csr = _dot_exact_left(gtr, triu)
    grow_ref[...] = jnp.where((cls_r == 1) | (cls_r == 3), csr, gtr)


def _inproj(x2, sh, sc, nw, w_new, wgt, cw, gpc, gpr, seq, tm):
    t, d = x2.shape
    tps = seq // tm
    kern = functools.partial(_inproj_kernel, tps)
    return pl.pallas_call(
        kern,
        grid=(t // tm,),
        in_specs=[pl.BlockSpec((tm, d), lambda i: (i, 0)),
                  pl.BlockSpec((None, 1, d), lambda i: (i // tps, 0, 0)),
                  pl.BlockSpec((None, 1, d), lambda i: (i // tps, 0, 0)),
                  pl.BlockSpec((1, d), lambda i: (0, 0)),
                  pl.BlockSpec((d, N_PROJ), lambda i: (0, 0)),
                  pl.BlockSpec((16, d), lambda i: (0, 0)),
                  pl.BlockSpec((CONV_W, N_CONV), lambda i: (0, 0)),
                  pl.BlockSpec((8, HEAD_LANES), lambda i: (0, 0)),
                  pl.BlockSpec((16, HEAD_LANES), lambda i: (0, 0))],
        out_specs=[pl.BlockSpec((tm, N_CONV), lambda i: (i, 0)),
                   pl.BlockSpec((tm, N_REST), lambda i: (i, 0)),
                   pl.BlockSpec((tm, HEAD_LANES), lambda i: (i, 0)),
                   pl.BlockSpec((16, tm), lambda i: (0, i))],
        out_shape=[jax.ShapeDtypeStruct((t, N_CONV), F32),
                   jax.ShapeDtypeStruct((t, N_REST), F32),
                   jax.ShapeDtypeStruct((t, HEAD_LANES), F32),
                   jax.ShapeDtypeStruct((16, t), F32)],
        scratch_shapes=[pltpu.VMEM((tm + 8, N_CONV), F32)],
        compiler_params=pltpu.CompilerParams(dimension_semantics=("arbitrary",),
                                             vmem_limit_bytes=V7X_VMEM_LIMIT),
        name="inproj",
    )(x2, sh, sc, nw, w_new, wgt, cw, gpc, gpr)


def _chunk_masks():
    r = lax.broadcasted_iota(jnp.int32, (CHUNK, CHUNK), 0)
    c = lax.broadcasted_iota(jnp.int32, (CHUNK, CHUNK), 1)
    return r >= c, r > c, r == c


def _bdot(a, b):
    return lax.dot_general(a, b, (((2,), (1,)), ((0,), (0,))), preferred_element_type=F32)


def _bdot_nt(a, b):
    return lax.dot_general(a, b, (((2,), (2,)), ((0,), (0,))), preferred_element_type=F32)


def _unit_lower_inverse(lower, row, col):
    x = jnp.where(row == col, 1.0, 0.0) - jnp.where((row >> 1) == (col >> 1), lower, 0.0)
    shift = 1
    while (1 << shift) < CHUNK:
        couple = ((row >> (shift + 1)) == (col >> (shift + 1))) & ((row >> shift) != (col >> shift))
        cb = jnp.where(couple, lower, 0.0).astype(BF16)
        xb = x.astype(BF16)
        x = x - _bdot(_bdot(xb, cb).astype(BF16), xb)
        shift += 1
        yield
    return x


def _seq_heads(n_seq, n_heads):
    return [(s, hh) for s in range(n_seq) for hh in range(n_heads)]


def _deltanet_steps(q_ref, k_ref, v_ref, gc_ref, gr_ref, z_ref, nw_ref, o_ref, s_ref):
    n_seq, nc = gr_ref.shape[0], gr_ref.shape[1]

    @pl.when(pl.program_id(1) == 0)
    def _():
        s_ref[...] = jnp.zeros(s_ref.shape, F32)

    row = lax.broadcasted_iota(jnp.int32, (CHUNK, CHUNK), 0)
    col = lax.broadcasted_iota(jnp.int32, (CHUNK, CHUNK), 1)
    incl = row >= col
    strict = row > col
    nw = nw_ref[...]
    gcc = gc_ref[...]
    grr = gr_ref[...]

    streams = _seq_heads(n_seq, DN_HEADS)
    nh = len(streams)
    nb = nh * nc

    def heads(ref):
        return jnp.stack([ref[s, :, hh * HEAD_LANES:(hh + 1) * HEAD_LANES] for s, hh in streams],
                         axis=0).reshape(nb, CHUNK, HEAD_LANES)

    def col_gate(lane0):
        return jnp.stack([gcc[s, :, lane0 + hh:lane0 + hh + 1] for s, hh in streams], axis=0).reshape(nb, CHUNK, 1)

    q = heads(q_ref)
    k = heads(k_ref)
    v = heads(v_ref)
    beta = col_gate(0)
    g_c = col_gate(4)
    g_r = jnp.stack([grr[s, :, 4 + hh:5 + hh, :] for s, hh in streams], axis=0).reshape(nb, 1, CHUNK)
    g_last = g_c[:, CHUNK - 1:CHUNK, :]
    decay = jnp.exp(jnp.where(incl, g_c - g_r, NEG_BIG))
    kb = k.astype(BF16)
    kk = _bdot_nt(kb, kb)
    lower = jnp.where(strict, beta * kk * decay, 0.0)
    yield
    tinv = yield from _unit_lower_inverse(lower, row, col)
    eg = jnp.exp(g_c)
    rhs = jnp.concatenate([v * beta, k * (beta * eg)], axis=-1)
    sol = _bdot(tinv.astype(BF16), rhs.astype(BF16))
    yield
    w_val = sol[:, :, 0:DN_DV].reshape(nh, nc, CHUNK, DN_DV)
    kq = jnp.concatenate([sol[:, :, DN_DV:DN_DV + DN_DK], q * eg], axis=1).astype(BF16)
    kq = kq.reshape(nh, nc, 2 * CHUNK, DN_DK)
    qk = (_bdot_nt(q.astype(BF16), kb) * decay).astype(BF16).reshape(nh, nc, CHUNK, CHUNK)
    k_dec_t = jnp.swapaxes(k * jnp.exp(g_last - g_c), 1, 2).astype(BF16).reshape(nh, nc, DN_DK, CHUNK)
    s_dec = jnp.exp(g_last).reshape(nh, nc, 1, 1)
    yield

    state = s_ref[...]
    outs = []
    for c in range(nc):
        both = _bdot(kq[:, c], state.astype(BF16))
        v_new = w_val[:, c] - both[:, 0:CHUNK]
        vb = v_new.astype(BF16)
        outs.append(both[:, CHUNK:2 * CHUNK] + _bdot(qk[:, c], vb))
        state = s_dec[:, c] * state + _bdot(k_dec_t[:, c], vb)
        yield
    s_ref[...] = state

    o = jnp.stack(outs, axis=1)
    on = o * lax.rsqrt(jnp.mean(o * o, axis=-1, keepdims=True) + EPS) * nw
    on = on.reshape(nh, nc * CHUNK, DN_DV)
    for idx, (s, hh) in enumerate(streams):
        lanes = slice(hh * HEAD_LANES, (hh + 1) * HEAD_LANES)
        o_ref[s, :, lanes] = on[idx] * z_ref[s, :, lanes]


def _mlstm_steps(q_ref, k_ref, v_ref, gc_ref, gr_ref, og_ref, nw_ref, o_ref, c_ref, n_ref, m_ref):
    n_seq, nc = gr_ref.shape[0], gr_ref.shape[1]

    @pl.when(pl.program_id(1) == 0)
    def _():
        c_ref[...] = jnp.zeros(c_ref.shape, F32)
        n_ref[...] = jnp.zeros(n_ref.shape, F32)
        m_ref[...] = jnp.zeros(m_ref.shape, F32)

    incl, _, _ = _chunk_masks()
    gcc = gc_ref[...]
    grr = gr_ref[...]

    streams = _seq_heads(n_seq, ML_HEADS)
    nh = len(streams)

    def heads(ref):
        return jnp.stack([ref[s, :, hh * HEAD_LANES:(hh + 1) * HEAD_LANES] for s, hh in streams],
                         axis=0).reshape(nh, nc, CHUNK, HEAD_LANES)

    def col_gate(lane0):
        return jnp.stack([gcc[s, :, lane0 + hh:lane0 + hh + 1] for s, hh in streams],
                         axis=0).reshape(nh, nc, CHUNK, 1)

    def row_gate(row0):
        return jnp.stack([grr[s, :, row0 + hh:row0 + hh + 1, :] for s, hh in streams], axis=0)

    q = heads(q_ref)
    k = heads(k_ref)
    v = heads(v_ref)
    i_c = col_gate(8)
    b_c = col_gate(12)
    i_r = row_gate(8)
    b_r = row_gate(12)
    b_last = b_c[:, :, CHUNK - 1:CHUNK, :]
    d_mat = jnp.where(incl, b_c - b_r + i_r, NEG_BIG)
    m_intra = jnp.max(d_mat, axis=-1, keepdims=True)
    g_end = b_last - b_c + i_c
    g_end_max = jnp.max(g_end, axis=2, keepdims=True)
    yield

    m_run = m_ref[:, 0:1, 0:1].reshape(nh, 1, 1, 1)
    m_before = []
    for c in range(nc):
        m_before.append(m_run)
        m_run = jnp.maximum(b_last[:, c:c + 1] + m_run, g_end_max[:, c:c + 1])
    m_s = jnp.concatenate(m_before, axis=1)
    m_new = jnp.maximum(b_last + m_s, g_end_max)
    keep = jnp.exp(b_last + m_s - m_new)
    yield

    nb = nh * nc
    qb = q.astype(BF16)
    kb = k.astype(BF16)
    vb = v.astype(BF16)
    m_t = jnp.maximum(b_c + m_s, m_intra)
    inter = jnp.exp(b_c + m_s - m_t)
    qk = _bdot_nt(qb.reshape(nb, CHUNK, HEAD_LANES), kb.reshape(nb, CHUNK, HEAD_LANES))
    p = jnp.exp(d_mat - m_t) * qk.reshape(nh, nc, CHUNK, CHUNK)
    yield
    intra = _bdot(p.astype(BF16).reshape(nb, CHUNK, CHUNK), vb.reshape(nb, CHUNK, ML_DV)).reshape(nh, nc, CHUNK, ML_DV)
    p_sum = jnp.sum(p, axis=-1, keepdims=True)
    yield
    kw = k * jnp.exp(g_end - m_new)
    kw_t = jnp.swapaxes(kw.reshape(nb, CHUNK, HEAD_LANES), 1, 2).astype(BF16)
    d_state = _bdot(kw_t, vb.reshape(nb, CHUNK, ML_DV)).reshape(nh, nc, HEAD_LANES, ML_DV)
    kw_sum = jnp.sum(kw, axis=2, keepdims=True)
    yield

    c_s = c_ref[...]
    n_s = n_ref[:, 0:1, :]
    q_c = []
    q_n = []
    for c in range(nc):
        q_c.append(_bdot(qb[:, c], c_s.astype(BF16)))
        q_n.append(jnp.sum(q[:, c] * n_s, axis=-1, keepdims=True))
        c_s = keep[:, c] * c_s + d_state[:, c]
        n_s = keep[:, c] * n_s + kw_sum[:, c]
        yield
    c_ref[...] = c_s
    n_ref[...] = jnp.broadcast_to(n_s, (nh, 8, HEAD_LANES))
    m_ref[...] = jnp.broadcast_to(m_run.reshape(nh, 1, 1), (nh, 8, HEAD_LANES))

    num = inter * jnp.stack(q_c, axis=1) + intra
    den = inter * jnp.stack(q_n, axis=1) + p_sum
    h = num / jnp.maximum(jnp.abs(den), jnp.exp(-m_t))
    hr = h * lax.rsqrt(jnp.mean(h * h, axis=-1, keepdims=True) + EPS)
    hr = hr.reshape(nh, nc * CHUNK, ML_DV)
    for idx, (s, hh) in enumerate(streams):
        lanes = slice(hh * HEAD_LANES, (hh + 1) * HEAD_LANES)
        o_ref[s, :, lanes] = hr[idx] * nw_ref[:, lanes] * og_ref[s, :, lanes]


def _mixers_kernel(dq_ref, dk_ref, dv_ref, gc_ref, gr_ref, z_ref, dnw_ref, mq_ref, mk_ref, mv_ref, og_ref, mnw_ref,
                   o_ref, s_ref, c_ref, n_ref, m_ref):
    n_a = DN_HEADS * HEAD_LANES
    n_b = ML_HEADS * HEAD_LANES
    stages = [_deltanet_steps(dq_ref, dk_ref, dv_ref, gc_ref, gr_ref, z_ref, dnw_ref, o_ref.at[:, :, 0:n_a], s_ref),
              _mlstm_steps(mq_ref, mk_ref, mv_ref, gc_ref, gr_ref, og_ref, mnw_ref, o_ref.at[:, :, n_a:n_a + n_b],
                           c_ref, n_ref, m_ref)]
    while stages:
        for stage in list(stages):
            try:
                next(stage)
            except StopIteration:
                stages.remove(stage)


def _mixers(conv_out, rest, gcol, grow3, dn_nw, ml_nw, batch, seq, rows):
    t = conv_out.shape[0]
    nj = seq // rows
    cpb = rows // CHUNK
    width = 4 * HEAD_LANES
    assert DN_HEADS * HEAD_LANES == width and ML_HEADS * HEAD_LANES == width
    n_seq = 2 if batch % 2 == 0 else 1
    conv3 = conv_out.reshape(batch, seq, -1)
    rest3 = rest.reshape(batch, seq, -1)
    gcol3 = gcol.reshape(batch, seq, -1)
    grow4 = grow3.reshape(batch, seq // CHUNK, 16, CHUNK)

    def conv_block(c0):
        return pl.BlockSpec((n_seq, rows, width), lambda b, j: (b, j, c0 // width))

    def rest_block(c0):
        return pl.BlockSpec((n_seq, rows, width), lambda b, j: (b, j, (c0 - N_CONV) // width))

    gates = [pl.BlockSpec((n_seq, rows, HEAD_LANES), lambda b, j: (b, j, 0)),
             pl.BlockSpec((n_seq, cpb, 16, CHUNK), lambda b, j: (b, j, 0, 0))]
    out = pl.pallas_call(
        _mixers_kernel,
        grid=(batch // n_seq, nj),
        in_specs=[conv_block(C_DNQ), conv_block(C_DNK), conv_block(C_DNV)] + gates
                 + [rest_block(C_DNZ), pl.BlockSpec((1, HEAD_LANES), lambda b, j: (0, 0)),
                    conv_block(C_MLQ), conv_block(C_MLK), rest_block(C_MLV), rest_block(C_MLO),
                    pl.BlockSpec((1, width), lambda b, j: (0, 0))],
        out_specs=pl.BlockSpec((n_seq, rows, 2 * width), lambda b, j: (b, j, 0)),
        out_shape=jax.ShapeDtypeStruct((batch, seq, 2 * width), F32),
        scratch_shapes=[pltpu.VMEM((n_seq * DN_HEADS, DN_DK, DN_DV), F32),
                        pltpu.VMEM((n_seq * ML_HEADS, HEAD_LANES, ML_DV), F32),
                        pltpu.VMEM((n_seq * ML_HEADS, 8, HEAD_LANES), F32),
                        pltpu.VMEM((n_seq * ML_HEADS, 8, HEAD_LANES), F32)],
        compiler_params=pltpu.CompilerParams(dimension_semantics=("arbitrary", "arbitrary"),
                                             vmem_limit_bytes=V7X_VMEM_LIMIT),
        name="mixers",
    )(conv3, conv3, conv3, gcol3, grow4, rest3, dn_nw, conv3, conv3, rest3, rest3, ml_nw)
    return out.reshape(t, 2 * width)


def _route_kernel(y_ref, wo_ref, x_ref, g1_ref, sh_ref, sc_ref, nw_ref, wrt_ref, br_ref,
                  x1_ref, h2_ref, pos_ref, wrow_ref, len_ref):
    tm = x_ref.shape[0]

    mix = _dot(y_ref[...].astype(BF16), wo_ref[...])
    x1 = x_ref[...] + g1_ref[...] * mix
    x1_ref[...] = x1
    ms = jnp.mean(x1 * x1, axis=-1, keepdims=True)
    h2 = x1 * lax.rsqrt(ms + EPS) * nw_ref[...]
    h2 = h2 * (1.0 + sc_ref[...]) + sh_ref[...]
    h2_ref[...] = h2.astype(BF16)

    hh, hm, _ = _split3(h2)
    wh, wm, _ = _split3(wrt_ref[...])
    logits = _dot_nt(wh, hh) + (_dot_nt(wh, hm) + _dot_nt(wm, hh)) + br_ref[:, 0:1]

    e_i = lax.broadcasted_iota(jnp.int32, (N_EXPERTS, tm), 0)
    work = logits
    tops = []
    sels = []
    hots = []
    for _ in range(TOP_K):
        m = jnp.max(work, axis=0, keepdims=True)
        sel = jnp.min(jnp.where(work == m, e_i, N_EXPERTS), axis=0, keepdims=True)
        hot = e_i == sel
        work = jnp.where(hot, NEG_BIG, work)
        tops.append(m)
        sels.append(sel)
        hots.append(hot)
    exps = [jnp.exp(tl - tops[0]) for tl in tops]
    denom = exps[0] + exps[1] + exps[2] + exps[3]
    ws = [e / denom for e in exps]

    chosen = jnp.zeros((N_EXPERTS, tm), F32)
    for hot in hots:
        chosen = chosen + jnp.where(hot, 1.0, 0.0)

    r_i = lax.broadcasted_iota(jnp.int32, (tm, tm), 0)
    c_i = lax.broadcasted_iota(jnp.int32, (tm, tm), 1)
    strict_upper = jnp.where(r_i < c_i, 1.0, 0.0).astype(BF16)
    prefix = _dot(chosen.astype(BF16), strict_upper)
    n_e = jnp.sum(chosen, axis=1, keepdims=True)
    len8 = jnp.ceil(n_e * 0.125) * 8.0
    er = lax.broadcasted_iota(jnp.int32, (N_EXPERTS, N_EXPERTS), 0)
    ec = lax.broadcasted_iota(jnp.int32, (N_EXPERTS, N_EXPERTS), 1)
    strict_lower = jnp.where(ec < er, 1.0, 0.0).astype(BF16)
    len8b = jnp.broadcast_to(len8, (N_EXPERTS, HEAD_LANES))
    off8 = _dot_exact_right(strict_lower, len8b)[:, 0:1]
    len_ref[...] = len8b.astype(jnp.int32)

    for kk in range(TOP_K):
        wrow_ref[kk:kk + 1, :] = ws[kk]
        pos = jnp.sum(jnp.where(hots[kk], prefix + off8, 0.0), axis=0, keepdims=True)
        pos_ref[kk:kk + 1, :] = pos.astype(jnp.int32)


def _route(ymix, wo, x2, g1, sh2, sc2, nw, wrt, br, seq, tm):
    t, d = x2.shape
    tps = seq // tm
    bmap = lambda i: (i // tps, 0, 0)
    return pl.pallas_call(
        _route_kernel,
        grid=(t // tm,),
        in_specs=[pl.BlockSpec((tm, ymix.shape[1]), lambda i: (i, 0)),
                  pl.BlockSpec(wo.shape, lambda i: (0, 0)),
                  pl.BlockSpec((tm, d), lambda i: (i, 0)),
                  pl.BlockSpec((None, 1, d), bmap),
                  pl.BlockSpec((None, 1, d), bmap),
                  pl.BlockSpec((None, 1, d), bmap),
                  pl.BlockSpec((1, d), lambda i: (0, 0)),
                  pl.BlockSpec((N_EXPERTS, d), lambda i: (0, 0)),
                  pl.BlockSpec((N_EXPERTS, HEAD_LANES), lambda i: (0, 0))],
        out_specs=[pl.BlockSpec((tm, d), lambda i: (i, 0)),
                   pl.BlockSpec((tm, d), lambda i: (i, 0)),
                   pl.BlockSpec((TOP_K, tm), lambda i: (0, i)),
                   pl.BlockSpec((TOP_K, tm), lambda i: (0, i)),
                   pl.BlockSpec((N_EXPERTS, HEAD_LANES), lambda i: (i, 0))],
        out_shape=[jax.ShapeDtypeStruct((t, d), F32),
                   jax.ShapeDtypeStruct((t, d), BF16),
                   jax.ShapeDtypeStruct((TOP_K, t), jnp.int32),
                   jax.ShapeDtypeStruct((TOP_K, t), F32),
                   jax.ShapeDtypeStruct((t // tm * N_EXPERTS, HEAD_LANES), jnp.int32)],
        compiler_params=pltpu.CompilerParams(dimension_semantics=("arbitrary",),
                                             vmem_limit_bytes=V7X_VMEM_LIMIT),
        name="route",
    )(ymix, wo, x2, g1, sh2, sc2, nw, wrt, br)


TAIL_START, TAIL_LEN, PAD_END, BLOCK_START, BLOCK_COUNT = range(5)


def _slots_kernel(bm, len_te_ref, ssrc_ref, sdst_ref, tab_ref):
    nt, lanes = len_te_ref.shape
    len_te = len_te_ref[...].astype(F32)

    r_l = lax.broadcasted_iota(jnp.int32, (lanes, lanes), 0)
    c_l = lax.broadcasted_iota(jnp.int32, (lanes, lanes), 1)
    upper_incl = jnp.where(r_l <= c_l, 1.0, 0.0).astype(BF16)
    upper_strict = jnp.where(r_l < c_l, 1.0, 0.0).astype(BF16)
    total_r = jnp.sum(len_te, axis=0, keepdims=True)
    padded_r = jnp.ceil(total_r * (1.0 / bm)) * bm
    pad_end_r = _dot_exact_left(jnp.broadcast_to(padded_r, (8, lanes)), upper_incl)[0:1, :]
    pad_start_r = pad_end_r - padded_r
    r_t = lax.broadcasted_iota(jnp.int32, (nt, nt), 0)
    c_t = lax.broadcasted_iota(jnp.int32, (nt, nt), 1)
    lower_strict_t = jnp.where(c_t < r_t, 1.0, 0.0).astype(BF16)
    before = _dot_exact_right(lower_strict_t, len_te)
    ssrc_ref[...] = _dot_exact_left(len_te, upper_strict).astype(jnp.int32)
    sdst_ref[...] = (pad_start_r + before).astype(jnp.int32)
    tab_ref[TAIL_START:TAIL_START + 1, :] = (pad_start_r + total_r).astype(jnp.int32)
    tab_ref[TAIL_LEN:TAIL_LEN + 1, :] = (padded_r - total_r).astype(jnp.int32)
    tab_ref[PAD_END:PAD_END + 1, :] = pad_end_r.astype(jnp.int32)
    tab_ref[BLOCK_START:BLOCK_START + 1, :] = (pad_start_r * (1.0 / bm)).astype(jnp.int32)
    tab_ref[BLOCK_COUNT:BLOCK_COUNT + 1, :] = (padded_r * (1.0 / bm)).astype(jnp.int32)
    tab_ref[5:8, :] = jnp.zeros((3, lanes), jnp.int32)


def _slots(len_te, bm):
    assert bm & (bm - 1) == 0, "block rows must be a power of two"
    nt, lanes = len_te.shape
    return pl.pallas_call(
        functools.partial(_slots_kernel, bm),
        out_shape=[jax.ShapeDtypeStruct((nt, lanes), jnp.int32),
                   jax.ShapeDtypeStruct((nt, lanes), jnp.int32),
                   jax.ShapeDtypeStruct((8, lanes), jnp.int32)],
        compiler_params=pltpu.CompilerParams(vmem_limit_bytes=V7X_VMEM_LIMIT),
        name="slots",
    )(len_te)


def _segment_pieces(max_rows):
    sizes = []
    s = 8
    while s <= max_rows:
        sizes.append(s)
        s *= 2
    return sizes[::-1]


def _segment_dma(src_ref, src0, dst_ref, dst0, nrows, sizes, sem, start):
    off = 0
    for sz in sizes:
        bit = nrows & sz

        @pl.when(bit != 0)
        def _(off=off, sz=sz):
            s0 = pl.multiple_of(src0 + off, 8)
            d0 = pl.multiple_of(dst0 + off, 8)
            cp = pltpu.make_async_copy(src_ref.at[pl.ds(s0, sz)], dst_ref.at[pl.ds(d0, sz)], sem)
            if start:
                cp.start()
            else:
                cp.wait()

        off = off + bit


def _sorted_onehot(pos_ref, row0, nrows, tt):
    r_i = lax.broadcasted_iota(jnp.int32, (nrows, tt), 0) + row0
    hit = r_i == pos_ref[0:1, :]
    for kk in range(1, TOP_K):
        hit = hit | (r_i == pos_ref[kk:kk + 1, :])
    return hit


def _row_groups(nrows, n_groups):
    size = -(-nrows // n_groups // 8) * 8
    return [(r0, min(size, nrows - r0)) for r0 in range(0, nrows, size)]


def _dispatch_kernel(bm, len_ref, ssrc_ref, sdst_ref, tail_ref, pos_ref, h2_ref, xs_hbm, buf, sem):
    i = pl.program_id(0)
    nt = pl.num_programs(0)
    tt = h2_ref.shape[0]
    nrows = buf.shape[1]
    sizes = _segment_pieces(tt)
    slot = i % 2

    h2 = h2_ref[...]
    for r0, rn in _row_groups(nrows, 4):
        perm = jnp.where(_sorted_onehot(pos_ref, r0, rn, tt), 1.0, 0.0).astype(BF16)
        buf[slot, r0:r0 + rn, :] = _dot(perm, h2)

    def seg(tile, which, start):
        def body(e, carry):
            _segment_dma(buf.at[which], ssrc_ref[tile, e], xs_hbm, sdst_ref[tile, e], len_ref[tile, e], sizes,
                         sem.at[which], start)
            return carry
        lax.fori_loop(0, N_EXPERTS, body, 0)

    @pl.when(i > 0)
    def _():
        seg(i - 1, 1 - slot, False)

    seg(i, slot, True)

    @pl.when(i == nt - 1)
    def _():
        seg(i, slot, False)
        zrows = bm
        zbuf = buf.at[0]
        zsem = sem.at[0]
        buf[0, 0:zrows, :] = jnp.zeros((zrows, buf.shape[2]), F32)

        def tail(start):
            def body(e, carry):
                _segment_dma(zbuf, 0, xs_hbm, tail_ref[TAIL_START, e], tail_ref[TAIL_LEN, e],
                             _segment_pieces(zrows // 2), zsem, start)
                return carry
            lax.fori_loop(0, N_EXPERTS, body, 0)

        tail(True)
        tail(False)

        used = tail_ref[PAD_END, tail_ref.shape[1] - 1]
        n_unused = (xs_hbm.shape[0] - used) // zrows

        def unused_copy(j):
            d0 = pl.multiple_of(used + j * zrows, 8)
            return pltpu.make_async_copy(zbuf.at[pl.ds(0, zrows)], xs_hbm.at[pl.ds(d0, zrows)], zsem)

        def unused_start(j, carry):
            unused_copy(j).start()
            return carry

        def unused_wait(j, carry):
            unused_copy(j).wait()
            return carry

        lax.fori_loop(0, n_unused, unused_start, 0)
        lax.fori_loop(0, n_unused, unused_wait, 0)


def _dispatch(seg_len, seg_src, seg_dst, tail, pos, h2, n_slots, tt, bm):
    t, d = h2.shape
    nrows = TOP_K * tt + 8 * N_EXPERTS
    assert bm <= nrows
    smem = pl.BlockSpec(memory_space=pltpu.SMEM)
    return pl.pallas_call(
        functools.partial(_dispatch_kernel, bm),
        grid=(t // tt,),
        in_specs=[smem, smem, smem, smem,
                  pl.BlockSpec((TOP_K, tt), lambda i: (0, i)),
                  pl.BlockSpec((tt, d), lambda i: (i, 0))],
        out_specs=pl.BlockSpec(memory_space=pl.ANY),
        out_shape=jax.ShapeDtypeStruct((n_slots, d), F32),
        scratch_shapes=[pltpu.VMEM((2, nrows, d), F32), pltpu.SemaphoreType.DMA((2,))],
        compiler_params=pltpu.CompilerParams(dimension_semantics=("arbitrary",), has_side_effects=True,
                                             vmem_limit_bytes=V7X_VMEM_LIMIT),
        name="dispatch",
    )(seg_len, seg_src, seg_dst, tail, pos, h2)


def _experts_kernel(tab_ref, xs_hbm, wgu_ref, bgu_ref, wd_ref, bd_ref, y_hbm, wgu_s, wd_s, xbuf, ybuf, sem_in,
                    sem_out):
    e = pl.program_id(0)
    bm = xbuf.shape[1] // 2
    d_ff = wd_ref.shape[0]
    first = tab_ref[BLOCK_START, e]
    count = tab_ref[BLOCK_COUNT, e]
    npair = count // 2
    odd = count % 2
    tslot = npair % 2

    def pair_rows(j):
        return pl.ds(pl.multiple_of((first + 2 * j) * bm, bm), 2 * bm)

    def x_copy(j, slot):
        return pltpu.make_async_copy(xs_hbm.at[pair_rows(j)], xbuf.at[slot], sem_in.at[slot])

    def y_copy(j, slot):
        return pltpu.make_async_copy(ybuf.at[slot], y_hbm.at[pair_rows(j)], sem_out.at[slot])

    def last_rows():
        return pl.ds(pl.multiple_of((first + 2 * npair) * bm, bm), bm)

    def x_last(slot):
        return pltpu.make_async_copy(xs_hbm.at[last_rows()], xbuf.at[slot, pl.ds(0, bm)], sem_in.at[slot])

    def y_last(slot):
        return pltpu.make_async_copy(ybuf.at[slot, pl.ds(0, bm)], y_hbm.at[last_rows()], sem_out.at[slot])

    @pl.when(npair > 0)
    def _():
        x_copy(0, 0).start()

    @pl.when((npair == 0) & (odd == 1))
    def _():
        x_last(0).start()

    ct = wgu_s.shape[2]
    for t in range(wgu_s.shape[0]):
        wgu_s[t] = wgu_ref[:, t * ct:(t + 1) * ct].astype(BF16)
    for t in range(wd_s.shape[0]):
        wd_s[t] = wd_ref[:, t * ct:(t + 1) * ct].astype(BF16)

    def mlp(x):
        xb = x.astype(BF16)
        tpc = 2
        fc = tpc * ct
        nf = d_ff // fc
        n_out = wd_s.shape[0]

        def gate_up(f):
            gates, ups = [], []
            for t in range(f * tpc, (f + 1) * tpc):
                gates.append(_dot(xb, wgu_s[t]) + bgu_ref[:, t * ct:(t + 1) * ct])
                u = d_ff // ct + t
                ups.append(_dot(xb, wgu_s[u]) + bgu_ref[:, u * ct:(u + 1) * ct])
            return jnp.concatenate(gates, axis=-1), jnp.concatenate(ups, axis=-1)

        acts = []
        pre = gate_up(0)
        for f in range(nf):
            gate = jnp.minimum(pre[0], SWIGLU_LIMIT)
            up = jnp.clip(pre[1], -SWIGLU_LIMIT, SWIGLU_LIMIT)
            if f + 1 < nf:
                pre = gate_up(f + 1)
            acts.append(((up + 1.0) * gate * _sigmoid(SWIGLU_ALPHA * gate)).astype(BF16))
        act = jnp.concatenate(acts, axis=-1)
        return [_dot(act, wd_s[n]) + bd_ref[:, n * ct:(n + 1) * ct] for n in range(n_out)]

    def pair(j, carry):
        slot = j % 2
        x_copy(j, slot).wait()

        @pl.when(j + 1 < npair)
        def _():
            x_copy(j + 1, 1 - slot).start()

        @pl.when((j + 1 == npair) & (odd == 1))
        def _():
            x_last(1 - slot).start()

        @pl.when(j >= 2)
        def _():
            y_copy(j - 2, slot).wait()

        for n, piece in enumerate(mlp(xbuf[slot])):
            ybuf[slot, :, n * ct:(n + 1) * ct] = piece
        y_copy(j, slot).start()
        return carry

    lax.fori_loop(0, npair, pair, 0)

    @pl.when(odd == 1)
    def _():
        x_last(tslot).wait()

        @pl.when(npair >= 2)
        def _():
            y_copy(npair - 2, tslot).wait()

        for n, piece in enumerate(mlp(xbuf[tslot, 0:bm, :])):
            ybuf[tslot, 0:bm, n * ct:(n + 1) * ct] = piece
        y_last(tslot).start()

    @pl.when((odd == 0) & (npair >= 2))
    def _():
        y_copy(npair - 2, tslot).wait()

    @pl.when(npair >= 1)
    def _():
        y_copy(npair - 1, 1 - tslot).wait()

    @pl.when(odd == 1)
    def _():
        y_last(tslot).wait()

    @pl.when(e == pl.num_programs(0) - 1)
    def _():
        used = first + count
        n_unused = y_hbm.shape[0] // bm - used
        ybuf[0, 0:bm, :] = jnp.zeros((bm, ybuf.shape[2]), F32)

        def z_copy(j):
            d0 = pl.multiple_of((used + j) * bm, bm)
            return pltpu.make_async_copy(ybuf.at[0, pl.ds(0, bm)], y_hbm.at[pl.ds(d0, bm)], sem_out.at[0])

        def z_start(j, carry):
            z_copy(j).start()
            return carry

        def z_wait(j, carry):
            z_copy(j).wait()
            return carry

        lax.fori_loop(0, n_unused, z_start, 0)
        lax.fori_loop(0, n_unused, z_wait, 0)


def _experts(table, xs, wgu, bgu, wd, bd, bm):
    ns, d = xs.shape
    n_e, _, two_ff = wgu.shape
    d_ff = two_ff // 2
    wmap = lambda e: (e, 0, 0)
    return pl.pallas_call(
        _experts_kernel,
        grid=(n_e,),
        in_specs=[pl.BlockSpec(memory_space=pltpu.SMEM),
                  pl.BlockSpec(memory_space=pl.ANY),
                  pl.BlockSpec((None, d, two_ff), wmap),
                  pl.BlockSpec((None, 1, two_ff), wmap),
                  pl.BlockSpec((None, d_ff, d), wmap),
                  pl.BlockSpec((None, 1, d), wmap)],
        out_specs=pl.BlockSpec(memory_space=pl.ANY),
        out_shape=jax.ShapeDtypeStruct((ns, d), F32),
        scratch_shapes=[pltpu.VMEM((two_ff // MXU_COLS, d, MXU_COLS), BF16),
                        pltpu.VMEM((d // MXU_COLS, d_ff, MXU_COLS), BF16),
                        pltpu.VMEM((2, 2 * bm, d), F32), pltpu.VMEM((2, 2 * bm, d), F32),
                        pltpu.SemaphoreType.DMA((2,)), pltpu.SemaphoreType.DMA((2,))],
        compiler_params=pltpu.CompilerParams(dimension_semantics=("arbitrary",), has_side_effects=True,
                                             vmem_limit_bytes=V7X_VMEM_LIMIT),
        name="experts",
    )(table, xs, wgu, bgu, wd, bd)


def _combine_kernel(len_ref, ssrc_ref, sdst_ref, pos_ref, w_ref, y_hbm, x1_ref, g2_ref, nw_ref, sh_ref, sc_ref,
                    o_ref, ybuf, sem):
    i = pl.program_id(0)
    nt = pl.num_programs(0)
    tt = x1_ref.shape[0]
    nrows = ybuf.shape[1]
    sizes = _segment_pieces(tt)
    slot = i % 2

    def seg(tile, which, start):
        def body(e, carry):
            _segment_dma(y_hbm, sdst_ref[tile, e], ybuf.at[which], ssrc_ref[tile, e], len_ref[tile, e], sizes,
                         sem.at[which], start)
            return carry
        lax.fori_loop(0, N_EXPERTS, body, 0)

    @pl.when(i == 0)
    def _():
        ybuf[...] = jnp.zeros(ybuf.shape, F32)
        seg(0, 0, True)

    @pl.when(i + 1 < nt)
    def _():
        seg(i + 1, 1 - slot, True)

    seg(i, slot, False)

    acc = None
    for r0, rn in _row_groups(nrows, 3):
        r_i = lax.broadcasted_iota(jnp.int32, (rn, tt), 0) + r0
        wmat = 0.0
        for kk in range(TOP_K):
            wmat = jnp.where(r_i == pos_ref[kk:kk + 1, :], w_ref[kk:kk + 1, :], wmat)
        part = _dot_tn(wmat.astype(BF16), ybuf[slot, r0:r0 + rn, :].astype(BF16))
        acc = part if acc is None else acc + part
    xo = x1_ref[...] + g2_ref[...] * acc
    ms = jnp.mean(xo * xo, axis=-1, keepdims=True)
    hn = xo * lax.rsqrt(ms + EPS) * nw_ref[...]
    o_ref[...] = hn * (1.0 + sc_ref[...]) + sh_ref[...]


def _combine(seg_len, seg_src, seg_dst, pos, wrow, y, x1, g2, nw, shf, scf, seq, tt):
    t, d = x1.shape
    tps = seq // tt
    bmap = lambda i: (i // tps, 0, 0)
    nrows = TOP_K * tt + 8 * N_EXPERTS
    smem = pl.BlockSpec(memory_space=pltpu.SMEM)
    return pl.pallas_call(
        _combine_kernel,
        grid=(t // tt,),
        in_specs=[smem, smem, smem,
                  pl.BlockSpec((TOP_K, tt), lambda i: (0, i)),
                  pl.BlockSpec((TOP_K, tt), lambda i: (0, i)),
                  pl.BlockSpec(memory_space=pl.ANY),
                  pl.BlockSpec((tt, d), lambda i: (i, 0)),
                  pl.BlockSpec((None, 1, d), bmap),
                  pl.BlockSpec((1, d), lambda i: (0, 0)),
                  pl.BlockSpec((None, 1, d), bmap),
                  pl.BlockSpec((None, 1, d), bmap)],
        out_specs=pl.BlockSpec((tt, d), lambda i: (i, 0)),
        out_shape=jax.ShapeDtypeStruct((t, d), F32),
        scratch_shapes=[pltpu.VMEM((2, nrows, d), F32), pltpu.SemaphoreType.DMA((2,))],
        compiler_params=pltpu.CompilerParams(dimension_semantics=("arbitrary",),
                                             vmem_limit_bytes=V7X_VMEM_LIMIT),
        name="combine",
    )(seg_len, seg_src, seg_dst, pos, wrow, y, x1, g2, nw, shf, scf)


def _pad_heads(w, heads, dk):
    r = w.shape[0]
    w3 = w.reshape(r, heads, dk)
    return jnp.pad(w3, ((0, 0), (0, 0), (0, HEAD_LANES - dk))).reshape(r, heads * HEAD_LANES)


def _pick_tile(n, pref):
    tile = pref
    while n % tile:
        tile //= 2
    return tile


def kernel(x, c, w_ada, b_ada, norm_mix, w_in, dn_conv, dn_a_log, dn_dt_bias, dn_norm, ml_conv, ml_i_bias,
           ml_f_bias, ml_norm, w_out, norm_ffn, w_router, b_router, w_gate_up, b_gate_up, w_down, b_down,
           w_ada_final, b_ada_final, norm_final):
    batch, seq, d = x.shape
    assert w_ada.shape[0] == 1, "single-layer block"
    assert seq % CHUNK == 0
    t = batch * seq
    x2 = x.reshape(t, d)

    c_pad = jnp.pad(c, ((0, 8 - batch % 8 if batch % 8 else 0), (0, 0)))
    mod = _mods(c_pad, w_ada.reshape(d, 6 * d), b_ada.reshape(1, 6 * d))[:batch]
    modf = _mods(c_pad, w_ada_final, b_ada_final.reshape(1, 2 * d))[:batch]
    sh1, sc1, g1, sh2, sc2, g2 = [mod[:, None, j * d:(j + 1) * d] for j in range(6)]
    shf, scf = modf[:, None, 0:d], modf[:, None, d:2 * d]

    wi = w_in.reshape(d, -1)
    o_z = 1536
    o_b = 2048
    o_mq = 2056
    o_mk = o_mq + ML_HEADS * ML_DK
    o_mv = o_mk + ML_HEADS * ML_DK
    o_mo = o_mv + ML_HEADS * ML_DV
    o_mi = o_mo + ML_HEADS * ML_DV
    gates = jnp.concatenate([wi[:, o_b:o_mq], wi[:, o_mi:o_mi + 2 * ML_HEADS]], axis=1)
    w_new = jnp.concatenate([
        wi[:, 0:o_z],
        _pad_heads(wi[:, o_mq:o_mk], ML_HEADS, ML_DK),
        _pad_heads(wi[:, o_mk:o_mv], ML_HEADS, ML_DK),
        wi[:, o_z:o_b],
        wi[:, o_mv:o_mo],
        wi[:, o_mo:o_mi],
        jnp.pad(gates, ((0, 0), (0, HEAD_LANES - 16))),
    ], axis=1).astype(BF16)
    wgt = gates.T.astype(BF16)
    mlc = ml_conv.reshape(CONV_W, -1)
    cw = jnp.concatenate([dn_conv.reshape(CONV_W, -1),
                          _pad_heads(mlc[:, 0:ML_HEADS * ML_DK], ML_HEADS, ML_DK),
                          _pad_heads(mlc[:, ML_HEADS * ML_DK:], ML_HEADS, ML_DK)], axis=1)
    zeros4 = jnp.zeros((4,), F32)
    bias16 = jnp.concatenate([zeros4, dn_dt_bias.reshape(4), ml_i_bias.reshape(4), ml_f_bias.reshape(4)])
    alog16 = jnp.concatenate([zeros4, dn_a_log.reshape(4), zeros4, zeros4])
    gpc = jnp.zeros((8, HEAD_LANES), F32).at[0, 0:16].set(bias16).at[1, 0:16].set(alog16)
    gpr = jnp.zeros((16, HEAD_LANES), F32).at[:, 0].set(bias16).at[:, 1].set(alog16)

    tm_in = _pick_tile(seq, 256)
    conv_out, rest, gcol, grow = _inproj(x2, sh1, sc1, norm_mix.reshape(1, d), w_new, wgt, cw, gpc, gpr, seq, tm_in)
    grow3 = grow.reshape(16, t // CHUNK, CHUNK).transpose(1, 0, 2)

    rows = _pick_tile(seq, 256)
    ymix = _mixers(conv_out, rest, gcol, grow3, dn_norm.reshape(1, DN_DV), ml_norm.reshape(1, ML_HEADS * ML_DV),
                   batch, seq, rows)

    wo = w_out.reshape(-1, d).astype(BF16)
    tm_r = _pick_tile(seq, 512)
    brp = jnp.broadcast_to(b_router.reshape(N_EXPERTS, 1), (N_EXPERTS, HEAD_LANES))
    x1, h2, pos, wrow, len_col = _route(
        ymix, wo, x2, g1, sh2, sc2, norm_ffn.reshape(1, d),
        w_router.reshape(d, N_EXPERTS).T, brp, seq, tm_r)

    n_e = N_EXPERTS
    nt = t // tm_r
    bm = 256
    len_te =len_col.reshape(nt, n_e, HEAD_LANES)[:, :, 0]
    seg_len = jnp.pad(len_te, ((0, 0), (0, HEAD_LANES - n_e)))
    n_slots_max = t * TOP_K + n_e * (7 * nt + bm)
    nb = (n_slots_max + bm - 1) // bm
    seg_src, seg_dst, table = _slots(seg_len, bm)

    xs = _dispatch(seg_len, seg_src, seg_dst, table, pos, h2, nb * bm, tm_r, bm)
    y = _experts(table, xs, w_gate_up.reshape(n_e, d, -1), b_gate_up.reshape(n_e, 1, -1),
                 w_down.reshape(n_e, -1, d), b_down.reshape(n_e, 1, d), bm)
    out = _combine(seg_len, seg_src, seg_dst, pos, wrow, y, x1, g2, norm_final.reshape(1, d), shf, scf, seq, tm_r)
    return out.reshape(batch, seq, d)
```

```python
import functools

import jax
import jax.numpy as jnp
from jax import lax
from jax.experimental import pallas as pl
from jax.experimental.pallas import tpu as pltpu

F32 = jnp.float32
BF16 = jnp.bfloat16

CHUNK = 64
CONV_W = 4
EPS = 1e-6

DN_HEADS = 4
DN_DK = 128
DN_DV = 128
ML_HEADS = 4
ML_DK = 64
ML_DV = 128
HEAD_LANES = 128

N_EXPERTS = 32
TOP_K = 4
SWIGLU_LIMIT = 7.0
SWIGLU_ALPHA = 1.702

C_DNQ = 0
C_DNK = 512
C_DNV = 1024
C_MLQ = 1536
C_MLK = 2048
N_CONV = 2560
C_DNZ = 2560
C_MLV = 3072
C_MLO = 3584
C_GATE = 4096
N_PROJ = 4224
N_REST = C_GATE - N_CONV

V7X_VMEM_LIMIT = 56 * 1024 * 1024
MXU_COLS = 256

NEG_BIG = -1e30


def _sigmoid(x):
    return 1.0 / (1.0 + jnp.exp(-x))


def _softplus(x):
    return jnp.maximum(x, 0.0) + jnp.log(1.0 + jnp.exp(-jnp.abs(x)))


def _split3(v):
    hi = v.astype(BF16)
    r1 = v - hi.astype(F32)
    mid = r1.astype(BF16)
    lo = (r1 - mid.astype(F32)).astype(BF16)
    return hi, mid, lo


def _dot(a, b):
    return jnp.dot(a, b, preferred_element_type=F32)


def _dot_nt(a, b):
    return lax.dot_general(a, b, (((1,), (1,)), ((), ())), preferred_element_type=F32)


def _dot_tn(a, b):
    return lax.dot_general(a, b, (((0,), (0,)), ((), ())), preferred_element_type=F32)


def _dot_exact_right(sel_bf16, v):
    hi, mid, lo = _split3(v)
    return _dot(sel_bf16, hi) + _dot(sel_bf16, mid) + _dot(sel_bf16, lo)


def _dot_exact_left(v, sel_bf16):
    hi, mid, lo = _split3(v)
    return _dot(hi, sel_bf16) + _dot(mid, sel_bf16) + _dot(lo, sel_bf16)


def _mods_kernel(c_ref, w_ref, b_ref, o_ref):
    c = c_ref[...]
    cond = c * _sigmoid(c)
    ch, cm, cl = _split3(cond)
    wh, wm, wl = _split3(w_ref[...])
    acc = _dot(ch, wh) + (_dot(ch, wm) + _dot(cm, wh)) + (_dot(ch, wl) + _dot(cm, wm) + _dot(cl, wh))
    o_ref[...] = acc + b_ref[...]


def _mods(c_pad, w, b):
    m, d = c_pad.shape
    n = w.shape[1]
    tn = 1024
    return pl.pallas_call(
        _mods_kernel,
        grid=(n // tn,),
        in_specs=[pl.BlockSpec((m, d), lambda j: (0, 0)),
                  pl.BlockSpec((d, tn), lambda j: (0, j)),
                  pl.BlockSpec((1, tn), lambda j: (0, j))],
        out_specs=pl.BlockSpec((m, tn), lambda j: (0, j)),
        out_shape=jax.ShapeDtypeStruct((m, n), F32),
        compiler_params=pltpu.CompilerParams(dimension_semantics=("arbitrary",),
                                             vmem_limit_bytes=V7X_VMEM_LIMIT),
        name="mods",
    )(c_pad, w, b)


def _gate_transform(v, bias, alog, cls):
    vb = v + bias
    beta = _sigmoid(v)
    g = -jnp.exp(alog) * _softplus(vb)
    logf = -_softplus(-vb)
    return jnp.where(cls == 0, beta, jnp.where(cls == 1, g, jnp.where(cls == 2, vb, jnp.where(cls == 3, logf, 0.0))))


def _inproj_kernel(tiles_per_seq, x_ref, sh_ref, sc_ref, nw_ref, w_ref, wgt_ref, cw_ref, gpc_ref, gpr_ref,
                   conv_ref, rest_ref, gcol_ref, grow_ref, cbuf):
    tm = x_ref.shape[0]
    i = pl.program_id(0)
    x = x_ref[...]
    ms = jnp.mean(x * x, axis=-1, keepdims=True)
    h = x * lax.rsqrt(ms + EPS) * nw_ref[...]
    h = h * (1.0 + sc_ref[...]) + sh_ref[...]
    hb = h.astype(BF16)

    @pl.when(i % tiles_per_seq == 0)
    def _():
        cbuf[0:8, :] = jnp.zeros((8, N_CONV), F32)

    group = 4 * HEAD_LANES
    for lo in range(0, N_CONV, group):
        cols = slice(lo, lo + group)
        pc = _dot(hb, w_ref[:, cols])
        cbuf[8:tm + 8, cols] = pc
        acc = cw_ref[CONV_W - 1:CONV_W, cols] * pc
        for j in range(CONV_W - 1):
            acc = acc + cw_ref[j:j + 1, cols] * cbuf[8 - (CONV_W - 1) + j:8 - (CONV_W - 1) + j + tm, cols]
        cbuf[0:8, cols] = cbuf[tm:tm + 8, cols]
        y = acc * _sigmoid(acc)
        if lo in (C_DNQ, C_DNK):
            scale = DN_DK ** -0.5 if lo == C_DNQ else 1.0
            for hh in range(DN_HEADS):
                uh = y[:, hh * HEAD_LANES:(hh + 1) * HEAD_LANES]
                un = uh * lax.rsqrt(jnp.sum(uh * uh, axis=-1, keepdims=True) + EPS)
                conv_ref[:, lo + hh * HEAD_LANES:lo + (hh + 1) * HEAD_LANES] = un * scale if lo == C_DNQ else un
        elif lo == C_MLQ:
            conv_ref[:, cols] = y * (ML_DK ** -0.5)
        else:
            conv_ref[:, cols] = y

    z = _dot(hb, w_ref[:, C_DNZ:C_MLV])
    rest_ref[:, 0:512] = z * _sigmoid(z)
    rest_ref[:, 512:1024] = _dot(hb, w_ref[:, C_MLV:C_MLO])
    rest_ref[:, 1024:1536] = _sigmoid(_dot(hb, w_ref[:, C_MLO:C_GATE]))

    r_i = lax.broadcasted_iota(jnp.int32, (tm, tm), 0)
    c_i = lax.broadcasted_iota(jnp.int32, (tm, tm), 1)
    same_chunk = (r_i // CHUNK) == (c_i // CHUNK)
    tril = jnp.where(same_chunk & (c_i <= r_i), 1.0, 0.0).astype(BF16)
    triu = jnp.where(same_chunk & (r_i <= c_i), 1.0, 0.0).astype(BF16)

    gc = _dot(hb, w_ref[:, C_GATE:N_PROJ])
    cls_c = lax.broadcasted_iota(jnp.int32, (tm, HEAD_LANES), 1) // 4
    gt = _gate_transform(gc, gpc_ref[0:1, :], gpc_ref[1:2, :], cls_c)
    cs = _dot_exact_right(tril, gt)
    gcol_ref[...] = jnp.where((cls_c == 1) | (cls_c == 3), cs, gt)

    gr = _dot_nt(wgt_ref[...], hb)
    cls_r = lax.broadcasted_iota(jnp.int32, (16, tm), 0) // 4
    gtr = _gate_transform(gr, gpr_ref[:, 0:1], gpr_ref[:, 1:2], cls_r)
    csr = _dot_exact_left(gtr, triu)
    grow_ref[...] = jnp.where((cls_r == 1) | (cls_r == 3), csr, gtr)


def _inproj(x2, sh, sc, nw, w_new, wgt, cw, gpc, gpr, seq, tm):
    t, d = x2.shape
    tps = seq // tm
    kern = functools.partial(_inproj_kernel, tps)
    return pl.pallas_call(
        kern,
        grid=(t // tm,),
        in_specs=[pl.BlockSpec((tm, d), lambda i: (i, 0)),
                  pl.BlockSpec((None, 1, d), lambda i: (i // tps, 0, 0)),
                  pl.BlockSpec((None, 1, d), lambda i: (i // tps, 0, 0)),
                  pl.BlockSpec((1, d), lambda i: (0, 0)),
                  pl.BlockSpec((d, N_PROJ), lambda i: (0, 0)),
                  pl.BlockSpec((16, d), lambda i: (0, 0)),
                  pl.BlockSpec((CONV_W, N_CONV), lambda i: (0, 0)),
                  pl.BlockSpec((8, HEAD_LANES), lambda i: (0, 0)),
                  pl.BlockSpec((16, HEAD_LANES), lambda i: (0, 0))],
        out_specs=[pl.BlockSpec((tm, N_CONV), lambda i: (i, 0)),
                   pl.BlockSpec((tm, N_REST), lambda i: (i, 0)),
                   pl.BlockSpec((tm, HEAD_LANES), lambda i: (i, 0)),
                   pl.BlockSpec((16, tm), lambda i: (0, i))],
        out_shape=[jax.ShapeDtypeStruct((t, N_CONV), F32),
                   jax.ShapeDtypeStruct((t, N_REST), F32),
                   jax.ShapeDtypeStruct((t, HEAD_LANES), F32),
                   jax.ShapeDtypeStruct((16, t), F32)],
        scratch_shapes=[pltpu.VMEM((tm + 8, N_CONV), F32)],
        compiler_params=pltpu.CompilerParams(dimension_semantics=("arbitrary",),
                                             vmem_limit_bytes=V7X_VMEM_LIMIT),
        name="inproj",
    )(x2, sh, sc, nw, w_new, wgt, cw, gpc, gpr)


def _chunk_masks():
    r = lax.broadcasted_iota(jnp.int32, (CHUNK, CHUNK), 0)
    c = lax.broadcasted_iota(jnp.int32, (CHUNK, CHUNK), 1)
    return r >= c, r > c, r == c


def _bdot(a, b):
    return lax.dot_general(a, b, (((2,), (1,)), ((0,), (0,))), preferred_element_type=F32)


def _bdot_nt(a, b):
    return lax.dot_general(a, b, (((2,), (2,)), ((0,), (0,))), preferred_element_type=F32)


def _unit_lower_inverse(lower, row, col):
    x = jnp.where(row == col, 1.0, 0.0) - jnp.where((row >> 1) == (col >> 1), lower, 0.0)
    shift = 1
    while (1 << shift) < CHUNK:
        couple = ((row >> (shift + 1)) == (col >> (shift + 1))) & ((row >> shift) != (col >> shift))
        cb = jnp.where(couple, lower, 0.0).astype(BF16)
        xb = x.astype(BF16)
        x = x - _bdot(_bdot(xb, cb).astype(BF16), xb)
        shift += 1
        yield
    return x


def _seq_heads(n_seq, n_heads):
    return [(s, hh) for s in range(n_seq) for hh in range(n_heads)]


def _deltanet_steps(q_ref, k_ref, v_ref, gc_ref, gr_ref, z_ref, nw_ref, o_ref, s_ref):
    n_seq, nc = gr_ref.shape[0], gr_ref.shape[1]

    @pl.when(pl.program_id(1) == 0)
    def _():
        s_ref[...] = jnp.zeros(s_ref.shape, F32)

    row = lax.broadcasted_iota(jnp.int32, (CHUNK, CHUNK), 0)
    col = lax.broadcasted_iota(jnp.int32, (CHUNK, CHUNK), 1)
    incl = row >= col
    strict = row > col
    nw = nw_ref[...]
    gcc = gc_ref[...]
    grr = gr_ref[...]

    streams = _seq_heads(n_seq, DN_HEADS)
    nh = len(streams)
    nb = nh * nc

    def heads(ref):
        return jnp.stack([ref[s, :, hh * HEAD_LANES:(hh + 1) * HEAD_LANES] for s, hh in streams],
                         axis=0).reshape(nb, CHUNK, HEAD_LANES)

    def col_gate(lane0):
        return jnp.stack([gcc[s, :, lane0 + hh:lane0 + hh + 1] for s, hh in streams], axis=0).reshape(nb, CHUNK, 1)

    q = heads(q_ref)
    k = heads(k_ref)
    v = heads(v_ref)
    beta = col_gate(0)
    g_c = col_gate(4)
    g_r = jnp.stack([grr[s, :, 4 + hh:5 + hh, :] for s, hh in streams], axis=0).reshape(nb, 1, CHUNK)
    g_last = g_c[:, CHUNK - 1:CHUNK, :]
    decay = jnp.exp(jnp.where(incl, g_c - g_r, NEG_BIG))
    kb = k.astype(BF16)
    kk = _bdot_nt(kb, kb)
    lower = jnp.where(strict, beta * kk * decay, 0.0)
    yield
    tinv = yield from _unit_lower_inverse(lower, row, col)
    eg = jnp.exp(g_c)
    rhs = jnp.concatenate([v * beta, k * (beta * eg)], axis=-1)
    sol = _bdot(tinv.astype(BF16), rhs.astype(BF16))
    yield
    w_val = sol[:, :, 0:DN_DV].reshape(nh, nc, CHUNK, DN_DV)
    kq = jnp.concatenate([sol[:, :, DN_DV:DN_DV + DN_DK], q * eg], axis=1).astype(BF16)
    kq = kq.reshape(nh, nc, 2 * CHUNK, DN_DK)
    qk = (_bdot_nt(q.astype(BF16), kb) * decay).astype(BF16).reshape(nh, nc, CHUNK, CHUNK)
    k_dec_t = jnp.swapaxes(k * jnp.exp(g_last - g_c), 1, 2).astype(BF16).reshape(nh, nc, DN_DK, CHUNK)
    s_dec = jnp.exp(g_last).reshape(nh, nc, 1, 1)
    yield

    state = s_ref[...]
    outs = []
    for c in range(nc):
        both = _bdot(kq[:, c], state.astype(BF16))
        v_new = w_val[:, c] - both[:, 0:CHUNK]
        vb = v_new.astype(BF16)
        outs.append(both[:, CHUNK:2 * CHUNK] + _bdot(qk[:, c], vb))
        state = s_dec[:, c] * state + _bdot(k_dec_t[:, c], vb)
        yield
    s_ref[...] = state

    o = jnp.stack(outs, axis=1)
    on = o * lax.rsqrt(jnp.mean(o * o, axis=-1, keepdims=True) + EPS) * nw
    on = on.reshape(nh, nc * CHUNK, DN_DV)
    for idx, (s, hh) in enumerate(streams):
        lanes = slice(hh * HEAD_LANES, (hh + 1) * HEAD_LANES)
        o_ref[s, :, lanes] = on[idx] * z_ref[s, :, lanes]


def _mlstm_steps(q_ref, k_ref, v_ref, gc_ref, gr_ref, og_ref, nw_ref, o_ref, c_ref, n_ref, m_ref):
    n_seq, nc = gr_ref.shape[0], gr_ref.shape[1]

    @pl.when(pl.program_id(1) == 0)
    def _():
        c_ref[...] = jnp.zeros(c_ref.shape, F32)
        n_ref[...] = jnp.zeros(n_ref.shape, F32)
        m_ref[...] = jnp.zeros(m_ref.shape, F32)

    incl, _, _ = _chunk_masks()
    gcc = gc_ref[...]
    grr = gr_ref[...]

    streams = _seq_heads(n_seq, ML_HEADS)
    nh = len(streams)

    def heads(ref):
        return jnp.stack([ref[s, :, hh * HEAD_LANES:(hh + 1) * HEAD_LANES] for s, hh in streams],
                         axis=0).reshape(nh, nc, CHUNK, HEAD_LANES)

    def col_gate(lane0):
        return jnp.stack([gcc[s, :, lane0 + hh:lane0 + hh + 1] for s, hh in streams],
                         axis=0).reshape(nh, nc, CHUNK, 1)

    def row_gate(row0):
        return jnp.stack([grr[s, :, row0 + hh:row0 + hh + 1, :] for s, hh in streams], axis=0)

    q = heads(q_ref)
    k = heads(k_ref)
    v = heads(v_ref)
    i_c = col_gate(8)
    b_c = col_gate(12)
    i_r = row_gate(8)
    b_r = row_gate(12)
    b_last = b_c[:, :, CHUNK - 1:CHUNK, :]
    d_mat = jnp.where(incl, b_c - b_r + i_r, NEG_BIG)
    m_intra = jnp.max(d_mat, axis=-1, keepdims=True)
    g_end = b_last - b_c + i_c
    g_end_max = jnp.max(g_end, axis=2, keepdims=True)
    yield

    m_run = m_ref[:, 0:1, 0:1].reshape(nh, 1, 1, 1)
    m_before = []
    for c in range(nc):
        m_before.append(m_run)
        m_run = jnp.maximum(b_last[:, c:c + 1] + m_run, g_end_max[:, c:c + 1])
    m_s = jnp.concatenate(m_before, axis=1)
    m_new = jnp.maximum(b_last + m_s, g_end_max)
    keep = jnp.exp(b_last + m_s - m_new)
    yield

    nb = nh * nc
    qb = q.astype(BF16)
    kb = k.astype(BF16)
    vb = v.astype(BF16)
    m_t = jnp.maximum(b_c + m_s, m_intra)
    inter = jnp.exp(b_c + m_s - m_t)
    qk = _bdot_nt(qb.reshape(nb, CHUNK, HEAD_LANES), kb.reshape(nb, CHUNK, HEAD_LANES))
    p = jnp.exp(d_mat - m_t) * qk.reshape(nh, nc, CHUNK, CHUNK)
    yield
    intra = _bdot(p.astype(BF16).reshape(nb, CHUNK, CHUNK), vb.reshape(nb, CHUNK, ML_DV)).reshape(nh, nc, CHUNK, ML_DV)
    p_sum = jnp.sum(p, axis=-1, keepdims=True)
    yield
    kw = k * jnp.exp(g_end - m_new)
    kw_t = jnp.swapaxes(kw.reshape(nb, CHUNK, HEAD_LANES), 1, 2).astype(BF16)
    d_state = _bdot(kw_t, vb.reshape(nb, CHUNK, ML_DV)).reshape(nh, nc, HEAD_LANES, ML_DV)
    kw_sum = jnp.sum(kw, axis=2, keepdims=True)
    yield

    c_s = c_ref[...]
    n_s = n_ref[:, 0:1, :]
    q_c = []
    q_n = []
    for c in range(nc):
        q_c.append(_bdot(qb[:, c], c_s.astype(BF16)))
        q_n.append(jnp.sum(q[:, c] * n_s, axis=-1, keepdims=True))
        c_s = keep[:, c] * c_s + d_state[:, c]
        n_s = keep[:, c] * n_s + kw_sum[:, c]
        yield
    c_ref[...] = c_s
    n_ref[...] = jnp.broadcast_to(n_s, (nh, 8, HEAD_LANES))
    m_ref[...] = jnp.broadcast_to(m_run.reshape(nh, 1, 1), (nh, 8, HEAD_LANES))

    num = inter * jnp.stack(q_c, axis=1) + intra
    den = inter * jnp.stack(q_n, axis=1) + p_sum
    h = num / jnp.maximum(jnp.abs(den), jnp.exp(-m_t))
    hr = h * lax.rsqrt(jnp.mean(h * h, axis=-1, keepdims=True) + EPS)
    hr = hr.reshape(nh, nc * CHUNK, ML_DV)
    for idx, (s, hh) in enumerate(streams):
        lanes = slice(hh * HEAD_LANES, (hh + 1) * HEAD_LANES)
        o_ref[s, :, lanes] = hr[idx] * nw_ref[:, lanes] * og_ref[s, :, lanes]


def _mixers_kernel(dq_ref, dk_ref, dv_ref, gc_ref, gr_ref, z_ref, dnw_ref, mq_ref, mk_ref, mv_ref, og_ref, mnw_ref,
                   o_ref, s_ref, c_ref, n_ref, m_ref):
    n_a = DN_HEADS * HEAD_LANES
    n_b = ML_HEADS * HEAD_LANES
    stages = [_deltanet_steps(dq_ref, dk_ref, dv_ref, gc_ref, gr_ref, z_ref, dnw_ref, o_ref.at[:, :, 0:n_a], s_ref),
              _mlstm_steps(mq_ref, mk_ref, mv_ref, gc_ref, gr_ref, og_ref, mnw_ref, o_ref.at[:, :, n_a:n_a + n_b],
                           c_ref, n_ref, m_ref)]
    while stages:
        for stage in list(stages):
            try:
                next(stage)
            except StopIteration:
                stages.remove(stage)


def _mixers(conv_out, rest, gcol, grow3, dn_nw, ml_nw, batch, seq, rows):
    t = conv_out.shape[0]
    nj = seq // rows
    cpb = rows // CHUNK
    width = 4 * HEAD_LANES
    assert DN_HEADS * HEAD_LANES == width and ML_HEADS * HEAD_LANES == width
    n_seq = 2 if batch % 2 == 0 else 1
    conv3 = conv_out.reshape(batch, seq, -1)
    rest3 = rest.reshape(batch, seq, -1)
    gcol3 = gcol.reshape(batch, seq, -1)
    grow4 = grow3.reshape(batch, seq // CHUNK, 16, CHUNK)

    def conv_block(c0):
        return pl.BlockSpec((n_seq, rows, width), lambda b, j: (b, j, c0 // width))

    def rest_block(c0):
        return pl.BlockSpec((n_seq, rows, width), lambda b, j: (b, j, (c0 - N_CONV) // width))

    gates = [pl.BlockSpec((n_seq, rows, HEAD_LANES), lambda b, j: (b, j, 0)),
             pl.BlockSpec((n_seq, cpb, 16, CHUNK), lambda b, j: (b, j, 0, 0))]
    out = pl.pallas_call(
        _mixers_kernel,
        grid=(batch // n_seq, nj),
        in_specs=[conv_block(C_DNQ), conv_block(C_DNK), conv_block(C_DNV)] + gates
                 + [rest_block(C_DNZ), pl.BlockSpec((1, HEAD_LANES), lambda b, j: (0, 0)),
                    conv_block(C_MLQ), conv_block(C_MLK), rest_block(C_MLV), rest_block(C_MLO),
                    pl.BlockSpec((1, width), lambda b, j: (0, 0))],
        out_specs=pl.BlockSpec((n_seq, rows, 2 * width), lambda b, j: (b, j, 0)),
        out_shape=jax.ShapeDtypeStruct((batch, seq, 2 * width), F32),
        scratch_shapes=[pltpu.VMEM((n_seq * DN_HEADS, DN_DK, DN_DV), F32),
                        pltpu.VMEM((n_seq * ML_HEADS, HEAD_LANES, ML_DV), F32),
                        pltpu.VMEM((n_seq * ML_HEADS, 8, HEAD_LANES), F32),
                        pltpu.VMEM((n_seq * ML_HEADS, 8, HEAD_LANES), F32)],
        compiler_params=pltpu.CompilerParams(dimension_semantics=("arbitrary", "arbitrary"),
                                             vmem_limit_bytes=V7X_VMEM_LIMIT),
        name="mixers",
    )(conv3, conv3, conv3, gcol3, grow4, rest3, dn_nw, conv3, conv3, rest3, rest3, ml_nw)
    return out.reshape(t, 2 * width)


def _route_kernel(y_ref, wo_ref, x_ref, g1_ref, sh_ref, sc_ref, nw_ref, wrt_ref, br_ref,
                  x1_ref, h2_ref, pos_ref, wrow_ref, len_ref):
    tm = x_ref.shape[0]

    mix = _dot(y_ref[...].astype(BF16), wo_ref[...])
    x1 = x_ref[...] + g1_ref[...] * mix
    x1_ref[...] = x1
    ms = jnp.mean(x1 * x1, axis=-1, keepdims=True)
    h2 = x1 * lax.rsqrt(ms + EPS) * nw_ref[...]
    h2 = h2 * (1.0 + sc_ref[...]) + sh_ref[...]
    h2_ref[...] = h2.astype(BF16)

    hh, hm, _ = _split3(h2)
    wh, wm, _ = _split3(wrt_ref[...])
    logits = _dot_nt(wh, hh) + (_dot_nt(wh, hm) + _dot_nt(wm, hh)) + br_ref[:, 0:1]

    e_i = lax.broadcasted_iota(jnp.int32, (N_EXPERTS, tm), 0)
    work = logits
    tops = []
    sels = []
    hots = []
    for _ in range(TOP_K):
        m = jnp.max(work, axis=0, keepdims=True)
        sel = jnp.min(jnp.where(work == m, e_i, N_EXPERTS), axis=0, keepdims=True)
        hot = e_i == sel
        work = jnp.where(hot, NEG_BIG, work)
        tops.append(m)
        sels.append(sel)
        hots.append(hot)
    exps = [jnp.exp(tl - tops[0]) for tl in tops]
    denom = exps[0] + exps[1] + exps[2] + exps[3]
    ws = [e / denom for e in exps]

    chosen = jnp.zeros((N_EXPERTS, tm), F32)
    for hot in hots:
        chosen = chosen + jnp.where(hot, 1.0, 0.0)

    r_i = lax.broadcasted_iota(jnp.int32, (tm, tm), 0)
    c_i = lax.broadcasted_iota(jnp.int32, (tm, tm), 1)
    strict_upper = jnp.where(r_i < c_i, 1.0, 0.0).astype(BF16)
    prefix = _dot(chosen.astype(BF16), strict_upper)
    n_e = jnp.sum(chosen, axis=1, keepdims=True)
    len8 = jnp.ceil(n_e * 0.125) * 8.0
    er = lax.broadcasted_iota(jnp.int32, (N_EXPERTS, N_EXPERTS), 0)
    ec = lax.broadcasted_iota(jnp.int32, (N_EXPERTS, N_EXPERTS), 1)
    strict_lower = jnp.where(ec < er, 1.0, 0.0).astype(BF16)
    len8b = jnp.broadcast_to(len8, (N_EXPERTS, HEAD_LANES))
    off8 = _dot_exact_right(strict_lower, len8b)[:, 0:1]
    len_ref[...] = len8b.astype(jnp.int32)

    for kk in range(TOP_K):
        wrow_ref[kk:kk + 1, :] = ws[kk]
        pos = jnp.sum(jnp.where(hots[kk], prefix + off8, 0.0), axis=0, keepdims=True)
        pos_ref[kk:kk + 1, :] = pos.astype(jnp.int32)


def _route(ymix, wo, x2, g1, sh2, sc2, nw, wrt, br, seq, tm):
    t, d = x2.shape
    tps = seq // tm
    bmap = lambda i: (i // tps, 0, 0)
    return pl.pallas_call(
        _route_kernel,
        grid=(t // tm,),
        in_specs=[pl.BlockSpec((tm, ymix.shape[1]), lambda i: (i, 0)),
                  pl.BlockSpec(wo.shape, lambda i: (0, 0)),
                  pl.BlockSpec((tm, d), lambda i: (i, 0)),
                  pl.BlockSpec((None, 1, d), bmap),
                  pl.BlockSpec((None, 1, d), bmap),
                  pl.BlockSpec((None, 1, d), bmap),
                  pl.BlockSpec((1, d), lambda i: (0, 0)),
                  pl.BlockSpec((N_EXPERTS, d), lambda i: (0, 0)),
                  pl.BlockSpec((N_EXPERTS, HEAD_LANES), lambda i: (0, 0))],
        out_specs=[pl.BlockSpec((tm, d), lambda i: (i, 0)),
                   pl.BlockSpec((tm, d), lambda i: (i, 0)),
                   pl.BlockSpec((TOP_K, tm), lambda i: (0, i)),
                   pl.BlockSpec((TOP_K, tm), lambda i: (0, i)),
                   pl.BlockSpec((N_EXPERTS, HEAD_LANES), lambda i: (i, 0))],
        out_shape=[jax.ShapeDtypeStruct((t, d), F32),
                   jax.ShapeDtypeStruct((t, d), BF16),
                   jax.ShapeDtypeStruct((TOP_K, t), jnp.int32),
                   jax.ShapeDtypeStruct((TOP_K, t), F32),
                   jax.ShapeDtypeStruct((t // tm * N_EXPERTS, HEAD_LANES), jnp.int32)],
        compiler_params=pltpu.CompilerParams(dimension_semantics=("arbitrary",),
                                             vmem_limit_bytes=V7X_VMEM_LIMIT),
        name="route",
    )(ymix, wo, x2, g1, sh2, sc2, nw, wrt, br)


TAIL_START, TAIL_LEN, PAD_END, BLOCK_START, BLOCK_COUNT = range(5)


def _slots_kernel(bm, len_te_ref, ssrc_ref, sdst_ref, tab_ref):
    nt, lanes = len_te_ref.shape
    len_te = len_te_ref[...].astype(F32)

    r_l = lax.broadcasted_iota(jnp.int32, (lanes, lanes), 0)
    c_l = lax.broadcasted_iota(jnp.int32, (lanes, lanes), 1)
    upper_incl = jnp.where(r_l <= c_l, 1.0, 0.0).astype(BF16)
    upper_strict = jnp.where(r_l < c_l, 1.0, 0.0).astype(BF16)
    total_r = jnp.sum(len_te, axis=0, keepdims=True)
    padded_r = jnp.ceil(total_r * (1.0 / bm)) * bm
    pad_end_r = _dot_exact_left(jnp.broadcast_to(padded_r, (8, lanes)), upper_incl)[0:1, :]
    pad_start_r = pad_end_r - padded_r
    r_t = lax.broadcasted_iota(jnp.int32, (nt, nt), 0)
    c_t = lax.broadcasted_iota(jnp.int32, (nt, nt), 1)
    lower_strict_t = jnp.where(c_t < r_t, 1.0, 0.0).astype(BF16)
    before = _dot_exact_right(lower_strict_t, len_te)
    ssrc_ref[...] = _dot_exact_left(len_te, upper_strict).astype(jnp.int32)
    sdst_ref[...] = (pad_start_r + before).astype(jnp.int32)
    tab_ref[TAIL_START:TAIL_START + 1, :] = (pad_start_r + total_r).astype(jnp.int32)
    tab_ref[TAIL_LEN:TAIL_LEN + 1, :] = (padded_r - total_r).astype(jnp.int32)
    tab_ref[PAD_END:PAD_END + 1, :] = pad_end_r.astype(jnp.int32)
    tab_ref[BLOCK_START:BLOCK_START + 1, :] = (pad_start_r * (1.0 / bm)).astype(jnp.int32)
    tab_ref[BLOCK_COUNT:BLOCK_COUNT + 1, :] = (padded_r * (1.0 / bm)).astype(jnp.int32)
    tab_ref[5:8, :] = jnp.zeros((3, lanes), jnp.int32)


def _slots(len_te, bm):
    assert bm & (bm - 1) == 0, "block rows must be a power of two"
    nt, lanes = len_te.shape
    return pl.pallas_call(
        functools.partial(_slots_kernel, bm),
        out_shape=[jax.ShapeDtypeStruct((nt, lanes), jnp.int32),
                   jax.ShapeDtypeStruct((nt, lanes), jnp.int32),
                   jax.ShapeDtypeStruct((8, lanes), jnp.int32)],
        compiler_params=pltpu.CompilerParams(vmem_limit_bytes=V7X_VMEM_LIMIT),
        name="slots",
    )(len_te)


def _segment_pieces(max_rows):
    sizes = []
    s = 8
    while s <= max_rows:
        sizes.append(s)
        s *= 2
    return sizes[::-1]


def _segment_dma(src_ref, src0, dst_ref, dst0, nrows, sizes, sem, start):
    off = 0
    for sz in sizes:
        bit = nrows & sz

        @pl.when(bit != 0)
        def _(off=off, sz=sz):
            s0 = pl.multiple_of(src0 + off, 8)
            d0 = pl.multiple_of(dst0 + off, 8)
            cp = pltpu.make_async_copy(src_ref.at[pl.ds(s0, sz)], dst_ref.at[pl.ds(d0, sz)], sem)
            if start:
                cp.start()
            else:
                cp.wait()

        off = off + bit


def _sorted_onehot(pos_ref, row0, nrows, tt):
    r_i = lax.broadcasted_iota(jnp.int32, (nrows, tt), 0) + row0
    hit = r_i == pos_ref[0:1, :]
    for kk in range(1, TOP_K):
        hit = hit | (r_i == pos_ref[kk:kk + 1, :])
    return hit


def _row_groups(nrows, n_groups):
    size = -(-nrows // n_groups // 8) * 8
    return [(r0, min(size, nrows - r0)) for r0 in range(0, nrows, size)]


def _dispatch_kernel(bm, len_ref, ssrc_ref, sdst_ref, tail_ref, pos_ref, h2_ref, xs_hbm, buf, sem):
    i = pl.program_id(0)
    nt = pl.num_programs(0)
    tt = h2_ref.shape[0]
    nrows = buf.shape[1]
    sizes = _segment_pieces(tt)
    slot = i % 2

    h2 = h2_ref[...]
    for r0, rn in _row_groups(nrows, 4):
        perm = jnp.where(_sorted_onehot(pos_ref, r0, rn, tt), 1.0, 0.0).astype(BF16)
        buf[slot, r0:r0 + rn, :] = _dot(perm, h2)

    def seg(tile, which, start):
        def body(e, carry):
            _segment_dma(buf.at[which], ssrc_ref[tile, e], xs_hbm, sdst_ref[tile, e], len_ref[tile, e], sizes,
                         sem.at[which], start)
            return carry
        lax.fori_loop(0, N_EXPERTS, body, 0)

    @pl.when(i > 0)
    def _():
        seg(i - 1, 1 - slot, False)

    seg(i, slot, True)

    @pl.when(i == nt - 1)
    def _():
        seg(i, slot, False)
        zrows = bm
        zbuf = buf.at[0]
        zsem = sem.at[0]
        buf[0, 0:zrows, :] = jnp.zeros((zrows, buf.shape[2]), F32)

        def tail(start):
            def body(e, carry):
                _segment_dma(zbuf, 0, xs_hbm, tail_ref[TAIL_START, e], tail_ref[TAIL_LEN, e],
                             _segment_pieces(zrows // 2), zsem, start)
                return carry
            lax.fori_loop(0, N_EXPERTS, body, 0)

        tail(True)
        tail(False)

        used = tail_ref[PAD_END, tail_ref.shape[1] - 1]
        n_unused = (xs_hbm.shape[0] - used) // zrows

        def unused_copy(j):
            d0 = pl.multiple_of(used + j * zrows, 8)
            return pltpu.make_async_copy(zbuf.at[pl.ds(0, zrows)], xs_hbm.at[pl.ds(d0, zrows)], zsem)

        def unused_start(j, carry):
            unused_copy(j).start()
            return carry

        def unused_wait(j, carry):
            unused_copy(j).wait()
            return carry

        lax.fori_loop(0, n_unused, unused_start, 0)
        lax.fori_loop(0, n_unused, unused_wait, 0)


def _dispatch(seg_len, seg_src, seg_dst, tail, pos, h2, n_slots, tt, bm):
    t, d = h2.shape
    nrows = TOP_K * tt + 8 * N_EXPERTS
    assert bm <= nrows
    smem = pl.BlockSpec(memory_space=pltpu.SMEM)
    return pl.pallas_call(
        functools.partial(_dispatch_kernel, bm),
        grid=(t // tt,),
        in_specs=[smem, smem, smem, smem,
                  pl.BlockSpec((TOP_K, tt), lambda i: (0, i)),
                  pl.BlockSpec((tt, d), lambda i: (i, 0))],
        out_specs=pl.BlockSpec(memory_space=pl.ANY),
        out_shape=jax.ShapeDtypeStruct((n_slots, d), F32),
        scratch_shapes=[pltpu.VMEM((2, nrows, d), F32), pltpu.SemaphoreType.DMA((2,))],
        compiler_params=pltpu.CompilerParams(dimension_semantics=("arbitrary",), has_side_effects=True,
                                             vmem_limit_bytes=V7X_VMEM_LIMIT),
        name="dispatch",
    )(seg_len, seg_src, seg_dst, tail, pos, h2)


def _experts_kernel(tab_ref, xs_hbm, wgu_hbm, bgu_ref, wd_hbm, bd_ref, y_hbm, wgu_s, wd_s, wgu_f, wd_f, xbuf, ybuf,
                    sem_in, sem_out, sem_w):
    e = pl.program_id(0)
    n_e = pl.num_programs(0)
    bm = xbuf.shape[1] // 2
    d_ff = wd_f.shape[1]
    ct = wgu_s.shape[2]
    n_gu = wgu_s.shape[0]
    n_parts = n_gu + wd_s.shape[0]
    wslot = e % 2

    def w_part(expert, which, p):
        cols = slice((p if p < n_gu else p - n_gu) * ct, ((p if p < n_gu else p - n_gu) + 1) * ct)
        if p < n_gu:
            return pltpu.make_async_copy(wgu_hbm.at[expert, :, cols], wgu_f.at[which, :, cols], sem_w.at[which])
        return pltpu.make_async_copy(wd_hbm.at[expert, :, cols], wd_f.at[which, :, cols], sem_w.at[which])

    @pl.when(e == 0)
    def _():
        for p in range(n_parts):
            w_part(0, 0, p).start()

    for p in range(n_parts):
        w_part(e, wslot, p).wait()

    parts_per_unit = 4
    have_next = e + 1 < n_e

    def prefetch_unit(u):
        for p in range(n_parts):
            @pl.when(have_next & (u == p // parts_per_unit))
            def _(p=p):
                w_part(e + 1, 1 - wslot, p).start()
    first = tab_ref[BLOCK_START, e]
    count = tab_ref[BLOCK_COUNT, e]
    npair = count // 2
    odd = count % 2
    tslot = npair % 2

    def pair_rows(j):
        return pl.ds(pl.multiple_of((first + 2 * j) * bm, bm), 2 * bm)

    def x_copy(j, slot):
        return pltpu.make_async_copy(xs_hbm.at[pair_rows(j)], xbuf.at[slot], sem_in.at[slot])

    def y_copy(j, slot):
        return pltpu.make_async_copy(ybuf.at[slot], y_hbm.at[pair_rows(j)], sem_out.at[slot])

    def last_rows():
        return pl.ds(pl.multiple_of((first + 2 * npair) * bm, bm), bm)

    def x_last(slot):
        return pltpu.make_async_copy(xs_hbm.at[last_rows()], xbuf.at[slot, pl.ds(0, bm)], sem_in.at[slot])

    def y_last(slot):
        return pltpu.make_async_copy(ybuf.at[slot, pl.ds(0, bm)], y_hbm.at[last_rows()], sem_out.at[slot])

    @pl.when(npair > 0)
    def _():
        x_copy(0, 0).start()

    @pl.when((npair == 0) & (odd == 1))
    def _():
        x_last(0).start()

    for t in range(wgu_s.shape[0]):
        wgu_s[t] = wgu_f[wslot, :, t * ct:(t + 1) * ct].astype(BF16)
    for t in range(wd_s.shape[0]):
        wd_s[t] = wd_f[wslot, :, t * ct:(t + 1) * ct].astype(BF16)

    def mlp(x):
        xb = x.astype(BF16)
        tpc = 2
        fc = tpc * ct
        nf = d_ff // fc
        n_out = wd_s.shape[0]

        def gate_up(f):
            gates, ups = [], []
            for t in range(f * tpc, (f + 1) * tpc):
                gates.append(_dot(xb, wgu_s[t]) + bgu_ref[:, t * ct:(t + 1) * ct])
                u = d_ff // ct + t
                ups.append(_dot(xb, wgu_s[u]) + bgu_ref[:, u * ct:(u + 1) * ct])
            return jnp.concatenate(gates, axis=-1), jnp.concatenate(ups, axis=-1)

        acts = []
        pre = gate_up(0)
        for f in range(nf):
            gate = jnp.minimum(pre[0], SWIGLU_LIMIT)
            up = jnp.clip(pre[1], -SWIGLU_LIMIT, SWIGLU_LIMIT)
            if f + 1 < nf:
                pre = gate_up(f + 1)
            acts.append(((up + 1.0) * gate * _sigmoid(SWIGLU_ALPHA * gate)).astype(BF16))
        act = jnp.concatenate(acts, axis=-1)
        return [_dot(act, wd_s[n]) + bd_ref[:, n * ct:(n + 1) * ct] for n in range(n_out)]

    def pair(j, carry):
        slot = j % 2
        x_copy(j, slot).wait()

        @pl.when(j + 1 < npair)
        def _():
            x_copy(j + 1, 1 - slot).start()

        @pl.when((j + 1 == npair) & (odd == 1))
        def _():
            x_last(1 - slot).start()

        @pl.when(j >= 2)
        def _():
            y_copy(j - 2, slot).wait()

        prefetch_unit(j)
        for n, piece in enumerate(mlp(xbuf[slot])):
            ybuf[slot, :, n * ct:(n + 1) * ct] = piece
        y_copy(j, slot).start()
        return carry

    lax.fori_loop(0, npair, pair, 0)

    for p in range(n_parts):
        @pl.when(have_next & (p // parts_per_unit >= npair))
        def _(p=p):
            w_part(e + 1, 1 - wslot, p).start()

    @pl.when(odd == 1)
    def _():
        x_last(tslot).wait()

        @pl.when(npair >= 2)
        def _():
            y_copy(npair - 2, tslot).wait()

        for n, piece in enumerate(mlp(xbuf[tslot, 0:bm, :])):
            ybuf[tslot, 0:bm, n * ct:(n + 1) * ct] = piece
        y_last(tslot).start()

    @pl.when((odd == 0) & (npair >= 2))
    def _():
        y_copy(npair - 2, tslot).wait()

    @pl.when(npair >= 1)
    def _():
        y_copy(npair - 1, 1 - tslot).wait()

    @pl.when(odd == 1)
    def _():
        y_last(tslot).wait()

    @pl.when(e == pl.num_programs(0) - 1)
    def _():
        used = first + count
        n_unused = y_hbm.shape[0] // bm - used
        ybuf[0, 0:bm, :] = jnp.zeros((bm, ybuf.shape[2]), F32)

        def z_copy(j):
            d0 = pl.multiple_of((used + j) * bm, bm)
            return pltpu.make_async_copy(ybuf.at[0, pl.ds(0, bm)], y_hbm.at[pl.ds(d0, bm)], sem_out.at[0])

        def z_start(j, carry):
            z_copy(j).start()
            return carry

        def z_wait(j, carry):
            z_copy(j).wait()
            return carry

        lax.fori_loop(0, n_unused, z_start, 0)
        lax.fori_loop(0, n_unused, z_wait, 0)


def _experts(table, xs, wgu, bgu, wd, bd, bm):
    ns, d = xs.shape
    n_e, _, two_ff = wgu.shape
    d_ff = two_ff // 2
    wmap = lambda e: (e, 0, 0)
    return pl.pallas_call(
        _experts_kernel,
        grid=(n_e,),
        in_specs=[pl.BlockSpec(memory_space=pltpu.SMEM),
                  pl.BlockSpec(memory_space=pl.ANY),
                  pl.BlockSpec(memory_space=pl.ANY),
                  pl.BlockSpec((None, 1, two_ff), wmap),
                  pl.BlockSpec(memory_space=pl.ANY),
                  pl.BlockSpec((None, 1, d), wmap)],
        out_specs=pl.BlockSpec(memory_space=pl.ANY),
        out_shape=jax.ShapeDtypeStruct((ns, d), F32),
        scratch_shapes=[pltpu.VMEM((two_ff // MXU_COLS, d, MXU_COLS), BF16),
                        pltpu.VMEM((d // MXU_COLS, d_ff, MXU_COLS), BF16),
                        pltpu.VMEM((2, d, two_ff), F32), pltpu.VMEM((2, d_ff, d), F32),
                        pltpu.VMEM((2, 2 * bm, d), F32), pltpu.VMEM((2, 2 * bm, d), F32),
                        pltpu.SemaphoreType.DMA((2,)), pltpu.SemaphoreType.DMA((2,)),
                        pltpu.SemaphoreType.DMA((2,))],
        compiler_params=pltpu.CompilerParams(dimension_semantics=("arbitrary",), has_side_effects=True,
                                             vmem_limit_bytes=V7X_VMEM_LIMIT),
        name="experts",
    )(table, xs, wgu, bgu, wd, bd)


def _combine_kernel(len_ref, ssrc_ref, sdst_ref, pos_ref, w_ref, y_hbm, x1_ref, g2_ref, nw_ref, sh_ref, sc_ref,
                    o_ref, ybuf, sem):
    i = pl.program_id(0)
    nt = pl.num_programs(0)
    tt = x1_ref.shape[0]
    nrows = ybuf.shape[1]
    sizes = _segment_pieces(tt)
    slot = i % 2

    def seg(tile, which, start):
        def body(e, carry):
            _segment_dma(y_hbm, sdst_ref[tile, e], ybuf.at[which], ssrc_ref[tile, e], len_ref[tile, e], sizes,
                         sem.at[which], start)
            return carry
        lax.fori_loop(0, N_EXPERTS, body, 0)

    @pl.when(i == 0)
    def _():
        ybuf[...] = jnp.zeros(ybuf.shape, F32)
        seg(0, 0, True)

    @pl.when(i + 1 < nt)
    def _():
        seg(i + 1, 1 - slot, True)

    seg(i, slot, False)

    acc = None
    for r0, rn in _row_groups(nrows, 3):
        r_i = lax.broadcasted_iota(jnp.int32, (rn, tt), 0) + r0
        wmat = 0.0
        for kk in range(TOP_K):
            wmat = jnp.where(r_i == pos_ref[kk:kk + 1, :], w_ref[kk:kk + 1, :], wmat)
        part = _dot_tn(wmat.astype(BF16), ybuf[slot, r0:r0 + rn, :].astype(BF16))
        acc = part if acc is None else acc + part
    xo = x1_ref[...] + g2_ref[...] * acc
    ms = jnp.mean(xo * xo, axis=-1, keepdims=True)
    hn = xo * lax.rsqrt(ms + EPS) * nw_ref[...]
    o_ref[...] = hn * (1.0 + sc_ref[...]) + sh_ref[...]


def _combine(seg_len, seg_src, seg_dst, pos, wrow, y, x1, g2, nw, shf, scf, seq, tt):
    t, d = x1.shape
    tps = seq // tt
    bmap = lambda i: (i // tps, 0, 0)
    nrows = TOP_K * tt + 8 * N_EXPERTS
    smem = pl.BlockSpec(memory_space=pltpu.SMEM)
    return pl.pallas_call(
        _combine_kernel,
        grid=(t // tt,),
        in_specs=[smem, smem, smem,
                  pl.BlockSpec((TOP_K, tt), lambda i: (0, i)),
                  pl.BlockSpec((TOP_K, tt), lambda i: (0, i)),
                  pl.BlockSpec(memory_space=pl.ANY),
                  pl.BlockSpec((tt, d), lambda i: (i, 0)),
                  pl.BlockSpec((None, 1, d), bmap),
                  pl.BlockSpec((1, d), lambda i: (0, 0)),
                  pl.BlockSpec((None, 1, d), bmap),
                  pl.BlockSpec((None, 1, d), bmap)],
        out_specs=pl.BlockSpec((tt, d), lambda i: (i, 0)),
        out_shape=jax.ShapeDtypeStruct((t, d), F32),
        scratch_shapes=[pltpu.VMEM((2, nrows, d), F32), pltpu.SemaphoreType.DMA((2,))],
        compiler_params=pltpu.CompilerParams(dimension_semantics=("arbitrary",),
                                             vmem_limit_bytes=V7X_VMEM_LIMIT),
        name="combine",
    )(seg_len, seg_src, seg_dst, pos, wrow, y, x1, g2, nw, shf, scf)


def _pad_heads(w, heads, dk):
    r = w.shape[0]
    w3 = w.reshape(r, heads, dk)
    return jnp.pad(w3, ((0, 0), (0, 0), (0, HEAD_LANES - dk))).reshape(r, heads * HEAD_LANES)


def _pick_tile(n, pref):
    tile = pref
    while n % tile:
        tile //= 2
    return tile


def kernel(x, c, w_ada, b_ada, norm_mix, w_in, dn_conv, dn_a_log, dn_dt_bias, dn_norm, ml_conv, ml_i_bias,
           ml_f_bias, ml_norm, w_out, norm_ffn, w_router, b_router, w_gate_up, b_gate_up, w_down, b_down,
           w_ada_final, b_ada_final, norm_final):
    batch, seq, d = x.shape
    assert w_ada.shape[0] == 1, "single-layer block"
    assert seq % CHUNK == 0
    t = batch * seq
    x2 = x.reshape(t, d)

    c_pad = jnp.pad(c, ((0, 8 - batch % 8 if batch % 8 else 0), (0, 0)))
    mod = _mods(c_pad, w_ada.reshape(d, 6 * d), b_ada.reshape(1, 6 * d))[:batch]
    modf = _mods(c_pad, w_ada_final, b_ada_final.reshape(1, 2 * d))[:batch]
    sh1, sc1, g1, sh2, sc2, g2 = [mod[:, None, j * d:(j + 1) * d] for j in range(6)]
    shf, scf = modf[:, None, 0:d], modf[:, None, d:2 * d]

    wi = w_in.reshape(d, -1)
    o_z = 1536
    o_b = 2048
    o_mq = 2056
    o_mk = o_mq + ML_HEADS * ML_DK
    o_mv = o_mk + ML_HEADS * ML_DK
    o_mo = o_mv + ML_HEADS * ML_DV
    o_mi = o_mo + ML_HEADS * ML_DV
    gates = jnp.concatenate([wi[:, o_b:o_mq], wi[:, o_mi:o_mi + 2 * ML_HEADS]], axis=1)
    w_new = jnp.concatenate([
        wi[:, 0:o_z],
        _pad_heads(wi[:, o_mq:o_mk], ML_HEADS, ML_DK),
        _pad_heads(wi[:, o_mk:o_mv], ML_HEADS, ML_DK),
        wi[:, o_z:o_b],
        wi[:, o_mv:o_mo],
        wi[:, o_mo:o_mi],
        jnp.pad(gates, ((0, 0), (0, HEAD_LANES - 16))),
    ], axis=1).astype(BF16)
    wgt = gates.T.astype(BF16)
    mlc = ml_conv.reshape(CONV_W, -1)
    cw = jnp.concatenate([dn_conv.reshape(CONV_W, -1),
                          _pad_heads(mlc[:, 0:ML_HEADS * ML_DK], ML_HEADS, ML_DK),
                          _pad_heads(mlc[:, ML_HEADS * ML_DK:], ML_HEADS, ML_DK)], axis=1)
    zeros4 = jnp.zeros((4,), F32)
    bias16 = jnp.concatenate([zeros4, dn_dt_bias.reshape(4), ml_i_bias.reshape(4), ml_f_bias.reshape(4)])
    alog16 = jnp.concatenate([zeros4, dn_a_log.reshape(4), zeros4, zeros4])
    gpc = jnp.zeros((8, HEAD_LANES), F32).at[0, 0:16].set(bias16).at[1, 0:16].set(alog16)
    gpr = jnp.zeros((16, HEAD_LANES), F32).at[:, 0].set(bias16).at[:, 1].set(alog16)

    tm_in = _pick_tile(seq, 256)
    conv_out, rest, gcol, grow = _inproj(x2, sh1, sc1, norm_mix.reshape(1, d), w_new, wgt, cw, gpc, gpr, seq, tm_in)
    grow3 = grow.reshape(16, t // CHUNK, CHUNK).transpose(1, 0, 2)

    rows = _pick_tile(seq, 256)
    ymix = _mixers(conv_out, rest, gcol, grow3, dn_norm.reshape(1, DN_DV), ml_norm.reshape(1, ML_HEADS * ML_DV),
                   batch, seq, rows)

    wo = w_out.reshape(-1, d).astype(BF16)
    tm_r = _pick_tile(seq, 512)
    brp = jnp.broadcast_to(b_router.reshape(N_EXPERTS, 1), (N_EXPERTS, HEAD_LANES))
    x1, h2, pos, wrow, len_col = _route(
        ymix, wo, x2, g1, sh2, sc2, norm_ffn.reshape(1, d),
        w_router.reshape(d, N_EXPERTS).T, brp, seq, tm_r)

    n_e = N_EXPERTS
    nt = t // tm_r
    bm = 256
    len_te =len_col.reshape(nt, n_e, HEAD_LANES)[:, :, 0]
    seg_len = jnp.pad(len_te, ((0, 0), (0, HEAD_LANES - n_e)))
    n_slots_max = t * TOP_K + n_e * (7 * nt + bm)
    nb = (n_slots_max + bm - 1) // bm
    seg_src, seg_dst, table = _slots(seg_len, bm)

    xs = _dispatch(seg_len, seg_src, seg_dst, table, pos, h2, nb * bm, tm_r, bm)
    y = _experts(table, xs, w_gate_up.reshape(n_e, d, -1), b_gate_up.reshape(n_e, 1, -1),
                 w_down.reshape(n_e, -1, d), b_down.reshape(n_e, 1, d), bm)
    out = _combine(seg_len, seg_src, seg_dst, pos, wrow, y, x1, g2, norm_final.reshape(1, d), shf, scf, seq, tm_r)
    return out.reshape(batch, seq, d)
```

```python
import functools

import jax
import jax.numpy as jnp
from jax import lax
from jax.experimental import pallas as pl
from jax.experimental.pallas import tpu as pltpu

F32 = jnp.float32
BF16 = jnp.bfloat16

CHUNK = 64
CONV_W = 4
EPS = 1e-6

DN_HEADS = 4
DN_DK = 128
DN_DV = 128
ML_HEADS = 4
ML_DK = 64
ML_DV = 128
HEAD_LANES = 128

N_EXPERTS = 32
TOP_K = 4
SWIGLU_LIMIT = 7.0
SWIGLU_ALPHA = 1.702

C_DNQ = 0
C_DNK = 512
C_DNV = 1024
C_MLQ = 1536
C_MLK = 2048
N_CONV = 2560
C_DNZ = 2560
C_MLV = 3072
C_MLO = 3584
C_GATE = 4096
N_PROJ = 4224
N_REST = C_GATE - N_CONV

V7X_VMEM_LIMIT = 56 * 1024 * 1024
MXU_COLS = 256

NEG_BIG = -1e30


def _sigmoid(x):
    return 1.0 / (1.0 + jnp.exp(-x))


def _softplus(x):
    return jnp.maximum(x, 0.0) + jnp.log(1.0 + jnp.exp(-jnp.abs(x)))


def _split3(v):
    hi = v.astype(BF16)
    r1 = v - hi.astype(F32)
    mid = r1.astype(BF16)
    lo = (r1 - mid.astype(F32)).astype(BF16)
    return hi, mid, lo


def _dot(a, b):
    return jnp.dot(a, b, preferred_element_type=F32)


def _dot_nt(a, b):
    return lax.dot_general(a, b, (((1,), (1,)), ((), ())), preferred_element_type=F32)


def _dot_tn(a, b):
    return lax.dot_general(a, b, (((0,), (0,)), ((), ())), preferred_element_type=F32)


def _dot_exact_right(sel_bf16, v):
    hi, mid, lo = _split3(v)
    return _dot(sel_bf16, hi) + _dot(sel_bf16, mid) + _dot(sel_bf16, lo)


def _dot_exact_left(v, sel_bf16):
    hi, mid, lo = _split3(v)
    return _dot(hi, sel_bf16) + _dot(mid, sel_bf16) + _dot(lo, sel_bf16)


def _mods_kernel(c_ref, w_ref, b_ref, o_ref):
    c = c_ref[...]
    cond = c * _sigmoid(c)
    ch, cm, cl = _split3(cond)
    wh, wm, wl = _split3(w_ref[...])
    acc = _dot(ch, wh) + (_dot(ch, wm) + _dot(cm, wh)) + (_dot(ch, wl) + _dot(cm, wm) + _dot(cl, wh))
    o_ref[...] = acc + b_ref[...]


def _mods(c_pad, w, b):
    m, d = c_pad.shape
    n = w.shape[1]
    tn = 1024
    return pl.pallas_call(
        _mods_kernel,
        grid=(n // tn,),
        in_specs=[pl.BlockSpec((m, d), lambda j: (0, 0)),
                  pl.BlockSpec((d, tn), lambda j: (0, j)),
                  pl.BlockSpec((1, tn), lambda j: (0, j))],
        out_specs=pl.BlockSpec((m, tn), lambda j: (0, j)),
        out_shape=jax.ShapeDtypeStruct((m, n), F32),
        compiler_params=pltpu.CompilerParams(dimension_semantics=("arbitrary",),
                                             vmem_limit_bytes=V7X_VMEM_LIMIT),
        name="mods",
    )(c_pad, w, b)


def _gate_transform(v, bias, alog, cls):
    vb = v + bias
    beta = _sigmoid(v)
    g = -jnp.exp(alog) * _softplus(vb)
    logf = -_softplus(-vb)
    return jnp.where(cls == 0, beta, jnp.where(cls == 1, g, jnp.where(cls == 2, vb, jnp.where(cls == 3, logf, 0.0))))


def _inproj_kernel(tiles_per_seq, x_ref, sh_ref, sc_ref, nw_ref, w_ref, wgt_ref, cw_ref, gpc_ref, gpr_ref,
                   conv_ref, rest_ref, gcol_ref, grow_ref, cbuf):
    tm = x_ref.shape[0]
    i = pl.program_id(0)
    x = x_ref[...]
    ms = jnp.mean(x * x, axis=-1, keepdims=True)
    h = x * lax.rsqrt(ms + EPS) * nw_ref[...]
    h = h * (1.0 + sc_ref[...]) + sh_ref[...]
    hb = h.astype(BF16)

    @pl.when(i % tiles_per_seq == 0)
    def _():
        cbuf[0:8, :] = jnp.zeros((8, N_CONV), F32)

    group = 4 * HEAD_LANES
    for lo in range(0, N_CONV, group):
        cols = slice(lo, lo + group)
        pc = _dot(hb, w_ref[:, cols])
        cbuf[8:tm + 8, cols] = pc
        acc = cw_ref[CONV_W - 1:CONV_W, cols] * pc
        for j in range(CONV_W - 1):
            acc = acc + cw_ref[j:j + 1, cols] * cbuf[8 - (CONV_W - 1) + j:8 - (CONV_W - 1) + j + tm, cols]
        cbuf[0:8, cols] = cbuf[tm:tm + 8, cols]
        y = acc * _sigmoid(acc)
        if lo in (C_DNQ, C_DNK):
            scale = DN_DK ** -0.5 if lo == C_DNQ else 1.0
            for hh in range(DN_HEADS):
                uh = y[:, hh * HEAD_LANES:(hh + 1) * HEAD_LANES]
                un = uh * lax.rsqrt(jnp.sum(uh * uh, axis=-1, keepdims=True) + EPS)
                conv_ref[:, lo + hh * HEAD_LANES:lo + (hh + 1) * HEAD_LANES] = un * scale if lo == C_DNQ else un
        elif lo == C_MLQ:
            conv_ref[:, cols] = y * (ML_DK ** -0.5)
        else:
            conv_ref[:, cols] = y

    z = _dot(hb, w_ref[:, C_DNZ:C_MLV])
    rest_ref[:, 0:512] = z * _sigmoid(z)
    rest_ref[:, 512:1024] = _dot(hb, w_ref[:, C_MLV:C_MLO])
    rest_ref[:, 1024:1536] = _sigmoid(_dot(hb, w_ref[:, C_MLO:C_GATE]))

    r_i = lax.broadcasted_iota(jnp.int32, (tm, tm), 0)
    c_i = lax.broadcasted_iota(jnp.int32, (tm, tm), 1)
    same_chunk = (r_i // CHUNK) == (c_i // CHUNK)
    tril = jnp.where(same_chunk & (c_i <= r_i), 1.0, 0.0).astype(BF16)
    triu = jnp.where(same_chunk & (r_i <= c_i), 1.0, 0.0).astype(BF16)

    gc = _dot(hb, w_ref[:, C_GATE:N_PROJ])
    cls_c = lax.broadcasted_iota(jnp.int32, (tm, HEAD_LANES), 1) // 4
    gt = _gate_transform(gc, gpc_ref[0:1, :], gpc_ref[1:2, :], cls_c)
    cs = _dot_exact_right(tril, gt)
    gcol_ref[...] = jnp.where((cls_c == 1) | (cls_c == 3), cs, gt)

    gr = _dot_nt(wgt_ref[...], hb)
    cls_r = lax.broadcasted_iota(jnp.int32, (16, tm), 0) // 4
    gtr = _gate_transform(gr, gpr_ref[:, 0:1], gpr_ref[:, 1:2], cls_r)
    csr = _dot_exact_left(gtr, triu)
    grow_ref[...] = jnp.where((cls_r == 1) | (cls_r == 3), csr, gtr)


def _inproj(x2, sh, sc, nw, w_new, wgt, cw, gpc, gpr, seq, tm):
    t, d = x2.shape
    tps = seq // tm
    kern = functools.partial(_inproj_kernel, tps)
    return pl.pallas_call(
        kern,
        grid=(t // tm,),
        in_specs=[pl.BlockSpec((tm, d), lambda i: (i, 0)),
                  pl.BlockSpec((None, 1, d), lambda i: (i // tps, 0, 0)),
                  pl.BlockSpec((None, 1, d), lambda i: (i // tps, 0, 0)),
                  pl.BlockSpec((1, d), lambda i: (0, 0)),
                  pl.BlockSpec((d, N_PROJ), lambda i: (0, 0)),
                  pl.BlockSpec((16, d), lambda i: (0, 0)),
                  pl.BlockSpec((CONV_W, N_CONV), lambda i: (0, 0)),
                  pl.BlockSpec((8, HEAD_LANES), lambda i: (0, 0)),
                  pl.BlockSpec((16, HEAD_LANES), lambda i: (0, 0))],
        out_specs=[pl.BlockSpec((tm, N_CONV), lambda i: (i, 0)),
                   pl.BlockSpec((tm, N_REST), lambda i: (i, 0)),
                   pl.BlockSpec((tm, HEAD_LANES), lambda i: (i, 0)),
                   pl.BlockSpec((16, tm), lambda i: (0, i))],
        out_shape=[jax.ShapeDtypeStruct((t, N_CONV), F32),
                   jax.ShapeDtypeStruct((t, N_REST), F32),
                   jax.ShapeDtypeStruct((t, HEAD_LANES), F32),
                   jax.ShapeDtypeStruct((16, t), F32)],
        scratch_shapes=[pltpu.VMEM((tm + 8, N_CONV), F32)],
        compiler_params=pltpu.CompilerParams(dimension_semantics=("arbitrary",),
                                             vmem_limit_bytes=V7X_VMEM_LIMIT),
        name="inproj",
    )(x2, sh, sc, nw, w_new, wgt, cw, gpc, gpr)


def _chunk_masks():
    r = lax.broadcasted_iota(jnp.int32, (CHUNK, CHUNK), 0)
    c = lax.broadcasted_iota(jnp.int32, (CHUNK, CHUNK), 1)
    return r >= c, r > c, r == c


def _bdot(a, b):
    return lax.dot_general(a, b, (((2,), (1,)), ((0,), (0,))), preferred_element_type=F32)


def _bdot_nt(a, b):
    return lax.dot_general(a, b, (((2,), (2,)), ((0,), (0,))), preferred_element_type=F32)


def _unit_lower_inverse(lower, row, col):
    x = jnp.where(row == col, 1.0, 0.0) - jnp.where((row >> 1) == (col >> 1), lower, 0.0)
    shift = 1
    while (1 << shift) < CHUNK:
        couple = ((row >> (shift + 1)) == (col >> (shift + 1))) & ((row >> shift) != (col >> shift))
        cb = jnp.where(couple, lower, 0.0).astype(BF16)
        xb = x.astype(BF16)
        x = x - _bdot(_bdot(xb, cb).astype(BF16), xb)
        shift += 1
        yield
    return x


def _seq_heads(n_seq, n_heads):
    return [(s, hh) for s in range(n_seq) for hh in range(n_heads)]


def _deltanet_steps(q_ref, k_ref, v_ref, gc_ref, gr_ref, z_ref, nw_ref, o_ref, s_ref):
    n_seq, nc = gr_ref.shape[0], gr_ref.shape[1]

    @pl.when(pl.program_id(1) == 0)
    def _():
        s_ref[...] = jnp.zeros(s_ref.shape, F32)

    row = lax.broadcasted_iota(jnp.int32, (CHUNK, CHUNK), 0)
    col = lax.broadcasted_iota(jnp.int32, (CHUNK, CHUNK), 1)
    incl = row >= col
    strict = row > col
    nw = nw_ref[...]
    gcc = gc_ref[...]
    grr = gr_ref[...]

    streams = _seq_heads(n_seq, DN_HEADS)
    nh = len(streams)
    nb = nh * nc

    def heads(ref):
        return jnp.stack([ref[s, :, hh * HEAD_LANES:(hh + 1) * HEAD_LANES] for s, hh in streams],
                         axis=0).reshape(nb, CHUNK, HEAD_LANES)

    def col_gate(lane0):
        return jnp.stack([gcc[s, :, lane0 + hh:lane0 + hh + 1] for s, hh in streams], axis=0).reshape(nb, CHUNK, 1)

    q = heads(q_ref)
    k = heads(k_ref)
    v = heads(v_ref)
    beta = col_gate(0)
    g_c = col_gate(4)
    g_r = jnp.stack([grr[s, :, 4 + hh:5 + hh, :] for s, hh in streams], axis=0).reshape(nb, 1, CHUNK)
    g_last = g_c[:, CHUNK - 1:CHUNK, :]
    decay = jnp.exp(jnp.where(incl, g_c - g_r, NEG_BIG))
    kb = k.astype(BF16)
    kk = _bdot_nt(kb, kb)
    lower = jnp.where(strict, beta * kk * decay, 0.0)
    yield
    tinv = yield from _unit_lower_inverse(lower, row, col)
    eg = jnp.exp(g_c)
    rhs = jnp.concatenate([v * beta, k * (beta * eg)], axis=-1)
    sol = _bdot(tinv.astype(BF16), rhs.astype(BF16))
    yield
    w_val = sol[:, :, 0:DN_DV].reshape(nh, nc, CHUNK, DN_DV)
    kq = jnp.concatenate([sol[:, :, DN_DV:DN_DV + DN_DK], q * eg], axis=1).astype(BF16)
    kq = kq.reshape(nh, nc, 2 * CHUNK, DN_DK)
    qk = (_bdot_nt(q.astype(BF16), kb) * decay).astype(BF16).reshape(nh, nc, CHUNK, CHUNK)
    k_dec_t = jnp.swapaxes(k * jnp.exp(g_last - g_c), 1, 2).astype(BF16).reshape(nh, nc, DN_DK, CHUNK)
    s_dec = jnp.exp(g_last).reshape(nh, nc, 1, 1)
    yield

    state = s_ref[...]
    outs = []
    for c in range(nc):
        both = _bdot(kq[:, c], state.astype(BF16))
        v_new = w_val[:, c] - both[:, 0:CHUNK]
        vb = v_new.astype(BF16)
        outs.append(both[:, CHUNK:2 * CHUNK] + _bdot(qk[:, c], vb))
        state = s_dec[:, c] * state + _bdot(k_dec_t[:, c], vb)
        yield
    s_ref[...] = state

    o = jnp.stack(outs, axis=1)
    on = o * lax.rsqrt(jnp.mean(o * o, axis=-1, keepdims=True) + EPS) * nw
    on = on.reshape(nh, nc * CHUNK, DN_DV)
    for idx, (s, hh) in enumerate(streams):
        lanes = slice(hh * HEAD_LANES, (hh + 1) * HEAD_LANES)
        o_ref[s, :, lanes] = on[idx] * z_ref[s, :, lanes]


def _mlstm_steps(q_ref, k_ref, v_ref, gc_ref, gr_ref, og_ref, nw_ref, o_ref, c_ref, n_ref, m_ref):
    n_seq, nc = gr_ref.shape[0], gr_ref.shape[1]

    @pl.when(pl.program_id(1) == 0)
    def _():
        c_ref[...] = jnp.zeros(c_ref.shape, F32)
        n_ref[...] = jnp.zeros(n_ref.shape, F32)
        m_ref[...] = jnp.zeros(m_ref.shape, F32)

    incl, _, _ = _chunk_masks()
    gcc = gc_ref[...]
    grr = gr_ref[...]

    streams = _seq_heads(n_seq, ML_HEADS)
    nh = len(streams)

    def heads(ref):
        return jnp.stack([ref[s, :, hh * HEAD_LANES:(hh + 1) * HEAD_LANES] for s, hh in streams],
                         axis=0).reshape(nh, nc, CHUNK, HEAD_LANES)

    def col_gate(lane0):
        return jnp.stack([gcc[s, :, lane0 + hh:lane0 + hh + 1] for s, hh in streams],
                         axis=0).reshape(nh, nc, CHUNK, 1)

    def row_gate(row0):
        return jnp.stack([grr[s, :, row0 + hh:row0 + hh + 1, :] for s, hh in streams], axis=0)

    q = heads(q_ref)
    k = heads(k_ref)
    v = heads(v_ref)
    i_c = col_gate(8)
    b_c = col_gate(12)
    i_r = row_gate(8)
    b_r = row_gate(12)
    b_last = b_c[:, :, CHUNK - 1:CHUNK, :]
    d_mat = jnp.where(incl, b_c - b_r + i_r, NEG_BIG)
    m_intra = jnp.max(d_mat, axis=-1, keepdims=True)
    g_end = b_last - b_c + i_c
    g_end_max = jnp.max(g_end, axis=2, keepdims=True)
    yield

    m_run = m_ref[:, 0:1, 0:1].reshape(nh, 1, 1, 1)
    m_before = []
    for c in range(nc):
        m_before.append(m_run)
        m_run = jnp.maximum(b_last[:, c:c + 1] + m_run, g_end_max[:, c:c + 1])
    m_s = jnp.concatenate(m_before, axis=1)
    m_new = jnp.maximum(b_last + m_s, g_end_max)
    keep = jnp.exp(b_last + m_s - m_new)
    yield

    nb = nh * nc
    qb = q.astype(BF16)
    kb = k.astype(BF16)
    vb = v.astype(BF16)
    m_t = jnp.maximum(b_c + m_s, m_intra)
    inter = jnp.exp(b_c + m_s - m_t)
    qk = _bdot_nt(qb.reshape(nb, CHUNK, HEAD_LANES), kb.reshape(nb, CHUNK, HEAD_LANES))
    p = jnp.exp(d_mat - m_t) * qk.reshape(nh, nc, CHUNK, CHUNK)
    yield
    intra = _bdot(p.astype(BF16).reshape(nb, CHUNK, CHUNK), vb.reshape(nb, CHUNK, ML_DV)).reshape(nh, nc, CHUNK, ML_DV)
    p_sum = jnp.sum(p, axis=-1, keepdims=True)
    yield
    kw = k * jnp.exp(g_end - m_new)
    kw_t = jnp.swapaxes(kw.reshape(nb, CHUNK, HEAD_LANES), 1, 2).astype(BF16)
    d_state = _bdot(kw_t, vb.reshape(nb, CHUNK, ML_DV)).reshape(nh, nc, HEAD_LANES, ML_DV)
    kw_sum = jnp.sum(kw, axis=2, keepdims=True)
    yield

    c_s = c_ref[...]
    n_s = n_ref[:, 0:1, :]
    q_c = []
    q_n = []
    for c in range(nc):
        q_c.append(_bdot(qb[:, c], c_s.astype(BF16)))
        q_n.append(jnp.sum(q[:, c] * n_s, axis=-1, keepdims=True))
        c_s = keep[:, c] * c_s + d_state[:, c]
        n_s = keep[:, c] * n_s + kw_sum[:, c]
        yield
    c_ref[...] = c_s
    n_ref[...] = jnp.broadcast_to(n_s, (nh, 8, HEAD_LANES))
    m_ref[...] = jnp.broadcast_to(m_run.reshape(nh, 1, 1), (nh, 8, HEAD_LANES))

    num = inter * jnp.stack(q_c, axis=1) + intra
    den = inter * jnp.stack(q_n, axis=1) + p_sum
    h = num / jnp.maximum(jnp.abs(den), jnp.exp(-m_t))
    hr = h * lax.rsqrt(jnp.mean(h * h, axis=-1, keepdims=True) + EPS)
    hr = hr.reshape(nh, nc * CHUNK, ML_DV)
    for idx, (s, hh) in enumerate(streams):
        lanes = slice(hh * HEAD_LANES, (hh + 1) * HEAD_LANES)
        o_ref[s, :, lanes] = hr[idx] * nw_ref[:, lanes] * og_ref[s, :, lanes]


def _mixers_kernel(dq_ref, dk_ref, dv_ref, gc_ref, gr_ref, z_ref, dnw_ref, mq_ref, mk_ref, mv_ref, og_ref, mnw_ref,
                   o_ref, s_ref, c_ref, n_ref, m_ref):
    n_a = DN_HEADS * HEAD_LANES
    n_b = ML_HEADS * HEAD_LANES
    stages = [_deltanet_steps(dq_ref, dk_ref, dv_ref, gc_ref, gr_ref, z_ref, dnw_ref, o_ref.at[:, :, 0:n_a], s_ref),
              _mlstm_steps(mq_ref, mk_ref, mv_ref, gc_ref, gr_ref, og_ref, mnw_ref, o_ref.at[:, :, n_a:n_a + n_b],
                           c_ref, n_ref, m_ref)]
    while stages:
        for stage in list(stages):
            try:
                next(stage)
            except StopIteration:
                stages.remove(stage)


def _mixers(conv_out, rest, gcol, grow3, dn_nw, ml_nw, batch, seq, rows):
    t = conv_out.shape[0]
    nj = seq // rows
    cpb = rows // CHUNK
    width = 4 * HEAD_LANES
    assert DN_HEADS * HEAD_LANES == width and ML_HEADS * HEAD_LANES == width
    n_seq = 2 if batch % 2 == 0 else 1
    conv3 = conv_out.reshape(batch, seq, -1)
    rest3 = rest.reshape(batch, seq, -1)
    gcol3 = gcol.reshape(batch, seq, -1)
    grow4 = grow3.reshape(batch, seq // CHUNK, 16, CHUNK)

    def conv_block(c0):
        return pl.BlockSpec((n_seq, rows, width), lambda b, j: (b, j, c0 // width))

    def rest_block(c0):
        return pl.BlockSpec((n_seq, rows, width), lambda b, j: (b, j, (c0 - N_CONV) // width))

    gates = [pl.BlockSpec((n_seq, rows, HEAD_LANES), lambda b, j: (b, j, 0)),
             pl.BlockSpec((n_seq, cpb, 16, CHUNK), lambda b, j: (b, j, 0, 0))]
    out = pl.pallas_call(
        _mixers_kernel,
        grid=(batch // n_seq, nj),
        in_specs=[conv_block(C_DNQ), conv_block(C_DNK), conv_block(C_DNV)] + gates
                 + [rest_block(C_DNZ), pl.BlockSpec((1, HEAD_LANES), lambda b, j: (0, 0)),
                    conv_block(C_MLQ), conv_block(C_MLK), rest_block(C_MLV), rest_block(C_MLO),
                    pl.BlockSpec((1, width), lambda b, j: (0, 0))],
        out_specs=pl.BlockSpec((n_seq, rows, 2 * width), lambda b, j: (b, j, 0)),
        out_shape=jax.ShapeDtypeStruct((batch, seq, 2 * width), F32),
        scratch_shapes=[pltpu.VMEM((n_seq * DN_HEADS, DN_DK, DN_DV), F32),
                        pltpu.VMEM((n_seq * ML_HEADS, HEAD_LANES, ML_DV), F32),
                        pltpu.VMEM((n_seq * ML_HEADS, 8, HEAD_LANES), F32),
                        pltpu.VMEM((n_seq * ML_HEADS, 8, HEAD_LANES), F32)],
        compiler_params=pltpu.CompilerParams(dimension_semantics=("arbitrary", "arbitrary"),
                                             vmem_limit_bytes=V7X_VMEM_LIMIT),
        name="mixers",
    )(conv3, conv3, conv3, gcol3, grow4, rest3, dn_nw, conv3, conv3, rest3, rest3, ml_nw)
    return out.reshape(t, 2 * width)


def _route_kernel(y_ref, wo_ref, x_ref, g1_ref, sh_ref, sc_ref, nw_ref, wrt_ref, br_ref,
                  x1_ref, h2_ref, pos_ref, wrow_ref, len_ref):
    tm = x_ref.shape[0]

    mix = _dot(y_ref[...].astype(BF16), wo_ref[...])
    x1 = x_ref[...] + g1_ref[...] * mix
    x1_ref[...] = x1
    ms = jnp.mean(x1 * x1, axis=-1, keepdims=True)
    h2 = x1 * lax.rsqrt(ms + EPS) * nw_ref[...]
    h2 = h2 * (1.0 + sc_ref[...]) + sh_ref[...]
    h2_ref[...] = h2.astype(BF16)

    hh, hm, _ = _split3(h2)
    wh, wm, _ = _split3(wrt_ref[...])
    logits = _dot_nt(wh, hh) + (_dot_nt(wh, hm) + _dot_nt(wm, hh)) + br_ref[:, 0:1]

    e_i = lax.broadcasted_iota(jnp.int32, (N_EXPERTS, tm), 0)
    work = logits
    tops = []
    sels = []
    hots = []
    for _ in range(TOP_K):
        m = jnp.max(work, axis=0, keepdims=True)
        sel = jnp.min(jnp.where(work == m, e_i, N_EXPERTS), axis=0, keepdims=True)
        hot = e_i == sel
        work = jnp.where(hot, NEG_BIG, work)
        tops.append(m)
        sels.append(sel)
        hots.append(hot)
    exps = [jnp.exp(tl - tops[0]) for tl in tops]
    denom = exps[0] + exps[1] + exps[2] + exps[3]
    ws = [e / denom for e in exps]

    chosen = jnp.zeros((N_EXPERTS, tm), F32)
    for hot in hots:
        chosen = chosen + jnp.where(hot, 1.0, 0.0)

    r_i = lax.broadcasted_iota(jnp.int32, (tm, tm), 0)
    c_i = lax.broadcasted_iota(jnp.int32, (tm, tm), 1)
    strict_upper = jnp.where(r_i < c_i, 1.0, 0.0).astype(BF16)
    prefix = _dot(chosen.astype(BF16), strict_upper)
    n_e = jnp.sum(chosen, axis=1, keepdims=True)
    len8 = jnp.ceil(n_e * 0.125) * 8.0
    er = lax.broadcasted_iota(jnp.int32, (N_EXPERTS, N_EXPERTS), 0)
    ec = lax.broadcasted_iota(jnp.int32, (N_EXPERTS, N_EXPERTS), 1)
    strict_lower = jnp.where(ec < er, 1.0, 0.0).astype(BF16)
    len8b = jnp.broadcast_to(len8, (N_EXPERTS, HEAD_LANES))
    off8 = _dot_exact_right(strict_lower, len8b)[:, 0:1]
    len_ref[...] = len8b.astype(jnp.int32)

    for kk in range(TOP_K):
        wrow_ref[kk:kk + 1, :] = ws[kk]
        pos = jnp.sum(jnp.where(hots[kk], prefix + off8, 0.0), axis=0, keepdims=True)
        pos_ref[kk:kk + 1, :] = pos.astype(jnp.int32)


def _route(ymix, wo, x2, g1, sh2, sc2, nw, wrt, br, seq, tm):
    t, d = x2.shape
    tps = seq // tm
    bmap = lambda i: (i // tps, 0, 0)
    return pl.pallas_call(
        _route_kernel,
        grid=(t // tm,),
        in_specs=[pl.BlockSpec((tm, ymix.shape[1]), lambda i: (i, 0)),
                  pl.BlockSpec(wo.shape, lambda i: (0, 0)),
                  pl.BlockSpec((tm, d), lambda i: (i, 0)),
                  pl.BlockSpec((None, 1, d), bmap),
                  pl.BlockSpec((None, 1, d), bmap),
                  pl.BlockSpec((None, 1, d), bmap),
                  pl.BlockSpec((1, d), lambda i: (0, 0)),
                  pl.BlockSpec((N_EXPERTS, d), lambda i: (0, 0)),
                  pl.BlockSpec((N_EXPERTS, HEAD_LANES), lambda i: (0, 0))],
        out_specs=[pl.BlockSpec((tm, d), lambda i: (i, 0)),
                   pl.BlockSpec((tm, d), lambda i: (i, 0)),
                   pl.BlockSpec((TOP_K, tm), lambda i: (0, i)),
                   pl.BlockSpec((TOP_K, tm), lambda i: (0, i)),
                   pl.BlockSpec((N_EXPERTS, HEAD_LANES), lambda i: (i, 0))],
        out_shape=[jax.ShapeDtypeStruct((t, d), F32),
                   jax.ShapeDtypeStruct((t, d), BF16),
                   jax.ShapeDtypeStruct((TOP_K, t), jnp.int32),
                   jax.ShapeDtypeStruct((TOP_K, t), F32),
                   jax.ShapeDtypeStruct((t // tm * N_EXPERTS, HEAD_LANES), jnp.int32)],
        compiler_params=pltpu.CompilerParams(dimension_semantics=("arbitrary",),
                                             vmem_limit_bytes=V7X_VMEM_LIMIT),
        name="route",
    )(ymix, wo, x2, g1, sh2, sc2, nw, wrt, br)


TAIL_START, TAIL_LEN, PAD_END, BLOCK_START, BLOCK_COUNT = range(5)


def _slots_kernel(bm, len_te_ref, ssrc_ref, sdst_ref, tab_ref):
    nt, lanes = len_te_ref.shape
    len_te = len_te_ref[...].astype(F32)

    r_l = lax.broadcasted_iota(jnp.int32, (lanes, lanes), 0)
    c_l = lax.broadcasted_iota(jnp.int32, (lanes, lanes), 1)
    upper_incl = jnp.where(r_l <= c_l, 1.0, 0.0).astype(BF16)
    upper_strict = jnp.where(r_l < c_l, 1.0, 0.0).astype(BF16)
    total_r = jnp.sum(len_te, axis=0, keepdims=True)
    padded_r = jnp.ceil(total_r * (1.0 / bm)) * bm
    pad_end_r = _dot_exact_left(jnp.broadcast_to(padded_r, (8, lanes)), upper_incl)[0:1, :]
    pad_start_r = pad_end_r - padded_r
    r_t = lax.broadcasted_iota(jnp.int32, (nt, nt), 0)
    c_t = lax.broadcasted_iota(jnp.int32, (nt, nt), 1)
    lower_strict_t = jnp.where(c_t < r_t, 1.0, 0.0).astype(BF16)
    before = _dot_exact_right(lower_strict_t, len_te)
    ssrc_ref[...] = _dot_exact_left(len_te, upper_strict).astype(jnp.int32)
    sdst_ref[...] = (pad_start_r + before).astype(jnp.int32)
    tab_ref[TAIL_START:TAIL_START + 1, :] = (pad_start_r + total_r).astype(jnp.int32)
    tab_ref[TAIL_LEN:TAIL_LEN + 1, :] = (padded_r - total_r).astype(jnp.int32)
    tab_ref[PAD_END:PAD_END + 1, :] = pad_end_r.astype(jnp.int32)
    tab_ref[BLOCK_START:BLOCK_START + 1, :] = (pad_start_r * (1.0 / bm)).astype(jnp.int32)
    tab_ref[BLOCK_COUNT:BLOCK_COUNT + 1, :] = (padded_r * (1.0 / bm)).astype(jnp.int32)
    tab_ref[5:8, :] = jnp.zeros((3, lanes), jnp.int32)


def _slots(len_te, bm):
    assert bm & (bm - 1) == 0, "block rows must be a power of two"
    nt, lanes = len_te.shape
    return pl.pallas_call(
        functools.partial(_slots_kernel, bm),
        out_shape=[jax.ShapeDtypeStruct((nt, lanes), jnp.int32),
                   jax.ShapeDtypeStruct((nt, lanes), jnp.int32),
                   jax.ShapeDtypeStruct((8, lanes), jnp.int32)],
        compiler_params=pltpu.CompilerParams(vmem_limit_bytes=V7X_VMEM_LIMIT),
        name="slots",
    )(len_te)


def _segment_pieces(max_rows):
    sizes = []
    s = 8
    while s <= max_rows:
        sizes.append(s)
        s *= 2
    return sizes[::-1]


def _segment_dma(src_ref, src0, dst_ref, dst0, nrows, sizes, sem, start):
    off = 0
    for sz in sizes:
        bit = nrows & sz

        @pl.when(bit != 0)
        def _(off=off, sz=sz):
            s0 = pl.multiple_of(src0 + off, 8)
            d0 = pl.multiple_of(dst0 + off, 8)
            cp = pltpu.make_async_copy(src_ref.at[pl.ds(s0, sz)], dst_ref.at[pl.ds(d0, sz)], sem)
            if start:
                cp.start()
            else:
                cp.wait()

        off = off + bit


def _sorted_onehot(pos_ref, row0, nrows, tt):
    r_i = lax.broadcasted_iota(jnp.int32, (nrows, tt), 0) + row0
    hit = r_i == pos_ref[0:1, :]
    for kk in range(1, TOP_K):
        hit = hit | (r_i == pos_ref[kk:kk + 1, :])
    return hit


def _row_groups(nrows, n_groups):
    size = -(-nrows // n_groups // 8) * 8
    return [(r0, min(size, nrows - r0)) for r0 in range(0, nrows, size)]


def _dispatch_kernel(bm, len_ref, ssrc_ref, sdst_ref, tail_ref, pos_ref, h2_ref, xs_hbm, buf, sem):
    i = pl.program_id(0)
    nt = pl.num_programs(0)
    tt = h2_ref.shape[0]
    nrows = buf.shape[1]
    sizes = _segment_pieces(tt)
    slot = i % 2

    h2 = h2_ref[...]
    for r0, rn in _row_groups(nrows, 4):
        perm = jnp.where(_sorted_onehot(pos_ref, r0, rn, tt), 1.0, 0.0).astype(BF16)
        buf[slot, r0:r0 + rn, :] = _dot(perm, h2)

    def seg(tile, which, start):
        def body(e, carry):
            _segment_dma(buf.at[which], ssrc_ref[tile, e], xs_hbm, sdst_ref[tile, e], len_ref[tile, e], sizes,
                         sem.at[which], start)
            return carry
        lax.fori_loop(0, N_EXPERTS, body, 0)

    @pl.when(i > 0)
    def _():
        seg(i - 1, 1 - slot, False)

    seg(i, slot, True)

    @pl.when(i == nt - 1)
    def _():
        seg(i, slot, False)
        zrows = bm
        zbuf = buf.at[0]
        zsem = sem.at[0]
        buf[0, 0:zrows, :] = jnp.zeros((zrows, buf.shape[2]), F32)

        def tail(start):
            def body(e, carry):
                _segment_dma(zbuf, 0, xs_hbm, tail_ref[TAIL_START, e], tail_ref[TAIL_LEN, e],
                             _segment_pieces(zrows // 2), zsem, start)
                return carry
            lax.fori_loop(0, N_EXPERTS, body, 0)

        tail(True)
        tail(False)

        used = tail_ref[PAD_END, tail_ref.shape[1] - 1]
        n_unused = (xs_hbm.shape[0] - used) // zrows

        def unused_copy(j):
            d0 = pl.multiple_of(used + j * zrows, 8)
            return pltpu.make_async_copy(zbuf.at[pl.ds(0, zrows)], xs_hbm.at[pl.ds(d0, zrows)], zsem)

        def unused_start(j, carry):
            unused_copy(j).start()
            return carry

        def unused_wait(j, carry):
            unused_copy(j).wait()
            return carry

        lax.fori_loop(0, n_unused, unused_start, 0)
        lax.fori_loop(0, n_unused, unused_wait, 0)


def _dispatch(seg_len, seg_src, seg_dst, tail, pos, h2, n_slots, tt, bm):
    t, d = h2.shape
    nrows = TOP_K * tt + 8 * N_EXPERTS
    assert bm <= nrows
    smem = pl.BlockSpec(memory_space=pltpu.SMEM)
    return pl.pallas_call(
        functools.partial(_dispatch_kernel, bm),
        grid=(t // tt,),
        in_specs=[smem, smem, smem, smem,
                  pl.BlockSpec((TOP_K, tt), lambda i: (0, i)),
                  pl.BlockSpec((tt, d), lambda i: (i, 0))],
        out_specs=pl.BlockSpec(memory_space=pl.ANY),
        out_shape=jax.ShapeDtypeStruct((n_slots, d), F32),
        scratch_shapes=[pltpu.VMEM((2, nrows, d), F32), pltpu.SemaphoreType.DMA((2,))],
        compiler_params=pltpu.CompilerParams(dimension_semantics=("arbitrary",), has_side_effects=True,
                                             vmem_limit_bytes=V7X_VMEM_LIMIT),
        name="dispatch",
    )(seg_len, seg_src, seg_dst, tail, pos, h2)


def _experts_kernel(tab_ref, xs_hbm, wgu_hbm, bgu_ref, wd_hbm, bd_ref, y_hbm, wgu_s, wd_s, wgu_f, wd_f, xbuf, ybuf,
                    sem_in, sem_out, sem_w):
    e = pl.program_id(0)
    n_e = pl.num_programs(0)
    bm = xbuf.shape[1] // 2
    d_ff = wd_f.shape[1]
    ct = wgu_s.shape[2]
    wslot = e % 2

    def w_copies(expert, which):
        return (pltpu.make_async_copy(wgu_hbm.at[expert], wgu_f.at[which], sem_w.at[which]),
                pltpu.make_async_copy(wd_hbm.at[expert], wd_f.at[which], sem_w.at[which]))

    @pl.when(e == 0)
    def _():
        for cp in w_copies(0, 0):
            cp.start()

    for cp in w_copies(e, wslot):
        cp.wait()

    have_next = e + 1 < n_e

    def prefetch_next_weights(cond):
        @pl.when(have_next & cond)
        def _():
            for cp in w_copies(e + 1, 1 - wslot):
                cp.start()
    first = tab_ref[BLOCK_START, e]
    count = tab_ref[BLOCK_COUNT, e]
    npair = count // 2
    odd = count % 2
    tslot = npair % 2

    def pair_rows(j):
        return pl.ds(pl.multiple_of((first + 2 * j) * bm, bm), 2 * bm)

    def x_copy(j, slot):
        return pltpu.make_async_copy(xs_hbm.at[pair_rows(j)], xbuf.at[slot], sem_in.at[slot])

    def y_copy(j, slot):
        return pltpu.make_async_copy(ybuf.at[slot], y_hbm.at[pair_rows(j)], sem_out.at[slot])

    def last_rows():
        return pl.ds(pl.multiple_of((first + 2 * npair) * bm, bm), bm)

    def x_last(slot):
        return pltpu.make_async_copy(xs_hbm.at[last_rows()], xbuf.at[slot, pl.ds(0, bm)], sem_in.at[slot])

    def y_last(slot):
        return pltpu.make_async_copy(ybuf.at[slot, pl.ds(0, bm)], y_hbm.at[last_rows()], sem_out.at[slot])

    @pl.when(npair > 0)
    def _():
        x_copy(0, 0).start()

    @pl.when((npair == 0) & (odd == 1))
    def _():
        x_last(0).start()

    for t in range(wgu_s.shape[0]):
        wgu_s[t] = wgu_f[wslot, :, t * ct:(t + 1) * ct].astype(BF16)
    for t in range(wd_s.shape[0]):
        wd_s[t] = wd_f[wslot, :, t * ct:(t + 1) * ct].astype(BF16)

    def mlp(x):
        xb = x.astype(BF16)
        tpc = 2
        fc = tpc * ct
        nf = d_ff // fc
        n_out = wd_s.shape[0]

        def gate_up(f):
            gates, ups = [], []
            for t in range(f * tpc, (f + 1) * tpc):
                gates.append(_dot(xb, wgu_s[t]) + bgu_ref[:, t * ct:(t + 1) * ct])
                u = d_ff // ct + t
                ups.append(_dot(xb, wgu_s[u]) + bgu_ref[:, u * ct:(u + 1) * ct])
            return jnp.concatenate(gates, axis=-1), jnp.concatenate(ups, axis=-1)

        acts = []
        pre = gate_up(0)
        for f in range(nf):
            gate = jnp.minimum(pre[0], SWIGLU_LIMIT)
            up = jnp.clip(pre[1], -SWIGLU_LIMIT, SWIGLU_LIMIT)
            if f + 1 < nf:
                pre = gate_up(f + 1)
            acts.append(((up + 1.0) * gate * _sigmoid(SWIGLU_ALPHA * gate)).astype(BF16))
        act = jnp.concatenate(acts, axis=-1)
        return [_dot(act, wd_s[n]) + bd_ref[:, n * ct:(n + 1) * ct] for n in range(n_out)]

    def pair(j, carry):
        slot = j % 2
        x_copy(j, slot).wait()

        @pl.when(j + 1 < npair)
        def _():
            x_copy(j + 1, 1 - slot).start()

        @pl.when((j + 1 == npair) & (odd == 1))
        def _():
            x_last(1 - slot).start()

        @pl.when(j >= 2)
        def _():
            y_copy(j - 2, slot).wait()

        prefetch_next_weights(j == 0)
        for n, piece in enumerate(mlp(xbuf[slot])):
            ybuf[slot, :, n * ct:(n + 1) * ct] = piece
        y_copy(j, slot).start()
        return carry

    lax.fori_loop(0, npair, pair, 0)

    prefetch_next_weights(npair == 0)

    @pl.when(odd == 1)
    def _():
        x_last(tslot).wait()

        @pl.when(npair >= 2)
        def _():
            y_copy(npair - 2, tslot).wait()

        for n, piece in enumerate(mlp(xbuf[tslot, 0:bm, :])):
            ybuf[tslot, 0:bm, n * ct:(n + 1) * ct] = piece
        y_last(tslot).start()

    @pl.when((odd == 0) & (npair >= 2))
    def _():
        y_copy(npair - 2, tslot).wait()

    @pl.when(npair >= 1)
    def _():
        y_copy(npair - 1, 1 - tslot).wait()

    @pl.when(odd == 1)
    def _():
        y_last(tslot).wait()

    @pl.when(e == pl.num_programs(0) - 1)
    def _():
        used = first + count
        n_unused = y_hbm.shape[0] // bm - used
        ybuf[0, 0:bm, :] = jnp.zeros((bm, ybuf.shape[2]), F32)

        def z_copy(j):
            d0 = pl.multiple_of((used + j) * bm, bm)
            return pltpu.make_async_copy(ybuf.at[0, pl.ds(0, bm)], y_hbm.at[pl.ds(d0, bm)], sem_out.at[0])

        def z_start(j, carry):
            z_copy(j).start()
            return carry

        def z_wait(j, carry):
            z_copy(j).wait()
            return carry

        lax.fori_loop(0, n_unused, z_start, 0)
        lax.fori_loop(0, n_unused, z_wait, 0)


def _experts(table, xs, wgu, bgu, wd, bd, bm):
    ns, d = xs.shape
    n_e, _, two_ff = wgu.shape
    d_ff = two_ff // 2
    wmap = lambda e: (e, 0, 0)
    return pl.pallas_call(
        _experts_kernel,
        grid=(n_e,),
        in_specs=[pl.BlockSpec(memory_space=pltpu.SMEM),
                  pl.BlockSpec(memory_space=pl.ANY),
                  pl.BlockSpec(memory_space=pl.ANY),
                  pl.BlockSpec((None, 1, two_ff), wmap),
                  pl.BlockSpec(memory_space=pl.ANY),
                  pl.BlockSpec((None, 1, d), wmap)],
        out_specs=pl.BlockSpec(memory_space=pl.ANY),
        out_shape=jax.ShapeDtypeStruct((ns, d), F32),
        scratch_shapes=[pltpu.VMEM((two_ff // MXU_COLS, d, MXU_COLS), BF16),
                        pltpu.VMEM((d // MXU_COLS, d_ff, MXU_COLS), BF16),
                        pltpu.VMEM((2, d, two_ff), F32), pltpu.VMEM((2, d_ff, d), F32),
                        pltpu.VMEM((2, 2 * bm, d), F32), pltpu.VMEM((2, 2 * bm, d), F32),
                        pltpu.SemaphoreType.DMA((2,)), pltpu.SemaphoreType.DMA((2,)),
                        pltpu.SemaphoreType.DMA((2,))],
        compiler_params=pltpu.CompilerParams(dimension_semantics=("arbitrary",), has_side_effects=True,
                                             vmem_limit_bytes=V7X_VMEM_LIMIT),
        name="experts",
    )(table, xs, wgu, bgu, wd, bd)


def _combine_kernel(len_ref, ssrc_ref, sdst_ref, pos_ref, w_ref, y_hbm, x1_ref, g2_ref, nw_ref, sh_ref, sc_ref,
                    o_ref, ybuf, sem):
    i = pl.program_id(0)
    nt = pl.num_programs(0)
    tt = x1_ref.shape[0]
    nrows = ybuf.shape[1]
    sizes = _segment_pieces(tt)
    slot = i % 2

    def seg(tile, which, start):
        def body(e, carry):
            _segment_dma(y_hbm, sdst_ref[tile, e], ybuf.at[which], ssrc_ref[tile, e], len_ref[tile, e], sizes,
                         sem.at[which], start)
            return carry
        lax.fori_loop(0, N_EXPERTS, body, 0)

    @pl.when(i == 0)
    def _():
        ybuf[...] = jnp.zeros(ybuf.shape, F32)
        seg(0, 0, True)

    @pl.when(i + 1 < nt)
    def _():
        seg(i + 1, 1 - slot, True)

    seg(i, slot, False)

    acc = None
    for r0, rn in _row_groups(nrows, 3):
        r_i = lax.broadcasted_iota(jnp.int32, (rn, tt), 0) + r0
        wmat = 0.0
        for kk in range(TOP_K):
            wmat = jnp.where(r_i == pos_ref[kk:kk + 1, :], w_ref[kk:kk + 1, :], wmat)
        part = _dot_tn(wmat.astype(BF16), ybuf[slot, r0:r0 + rn, :].astype(BF16))
        acc = part if acc is None else acc + part
    xo = x1_ref[...] + g2_ref[...] * acc
    ms = jnp.mean(xo * xo, axis=-1, keepdims=True)
    hn = xo * lax.rsqrt(ms + EPS) * nw_ref[...]
    o_ref[...] = hn * (1.0 + sc_ref[...]) + sh_ref[...]


def _combine(seg_len, seg_src, seg_dst, pos, wrow, y, x1, g2, nw, shf, scf, seq, tt):
    t, d = x1.shape
    tps = seq // tt
    bmap = lambda i: (i // tps, 0, 0)
    nrows = TOP_K * tt + 8 * N_EXPERTS
    smem = pl.BlockSpec(memory_space=pltpu.SMEM)
    return pl.pallas_call(
        _combine_kernel,
        grid=(t // tt,),
        in_specs=[smem, smem, smem,
                  pl.BlockSpec((TOP_K, tt), lambda i: (0, i)),
                  pl.BlockSpec((TOP_K, tt), lambda i: (0, i)),
                  pl.BlockSpec(memory_space=pl.ANY),
                  pl.BlockSpec((tt, d), lambda i: (i, 0)),
                  pl.BlockSpec((None, 1, d), bmap),
                  pl.BlockSpec((1, d), lambda i: (0, 0)),
                  pl.BlockSpec((None, 1, d), bmap),
                  pl.BlockSpec((None, 1, d), bmap)],
        out_specs=pl.BlockSpec((tt, d), lambda i: (i, 0)),
        out_shape=jax.ShapeDtypeStruct((t, d), F32),
        scratch_shapes=[pltpu.VMEM((2, nrows, d), F32), pltpu.SemaphoreType.DMA((2,))],
        compiler_params=pltpu.CompilerParams(dimension_semantics=("arbitrary",),
                                             vmem_limit_bytes=V7X_VMEM_LIMIT),
        name="combine",
    )(seg_len, seg_src, seg_dst, pos, wrow, y, x1, g2, nw, shf, scf)


def _pad_heads(w, heads, dk):
    r = w.shape[0]
    w3 = w.reshape(r, heads, dk)
    return jnp.pad(w3, ((0, 0), (0, 0), (0, HEAD_LANES - dk))).reshape(r, heads * HEAD_LANES)


def _pick_tile(n, pref):
    tile = pref
    while n % tile:
        tile //= 2
    return tile


def kernel(x, c, w_ada, b_ada, norm_mix, w_in, dn_conv, dn_a_log, dn_dt_bias, dn_norm, ml_conv, ml_i_bias,
           ml_f_bias, ml_norm, w_out, norm_ffn, w_router, b_router, w_gate_up, b_gate_up, w_down, b_down,
           w_ada_final, b_ada_final, norm_final):
    batch, seq, d = x.shape
    assert w_ada.shape[0] == 1, "single-layer block"
    assert seq % CHUNK == 0
    t = batch * seq
    x2 = x.reshape(t, d)

    c_pad = jnp.pad(c, ((0, 8 - batch % 8 if batch % 8 else 0), (0, 0)))
    mod = _mods(c_pad, w_ada.reshape(d, 6 * d), b_ada.reshape(1, 6 * d))[:batch]
    modf = _mods(c_pad, w_ada_final, b_ada_final.reshape(1, 2 * d))[:batch]
    sh1, sc1, g1, sh2, sc2, g2 = [mod[:, None, j * d:(j + 1) * d] for j in range(6)]
    shf, scf = modf[:, None, 0:d], modf[:, None, d:2 * d]

    wi = w_in.reshape(d, -1)
    o_z = 1536
    o_b = 2048
    o_mq = 2056
    o_mk = o_mq + ML_HEADS * ML_DK
    o_mv = o_mk + ML_HEADS * ML_DK
    o_mo = o_mv + ML_HEADS * ML_DV
    o_mi = o_mo + ML_HEADS * ML_DV
    gates = jnp.concatenate([wi[:, o_b:o_mq], wi[:, o_mi:o_mi + 2 * ML_HEADS]], axis=1)
    w_new = jnp.concatenate([
        wi[:, 0:o_z],
        _pad_heads(wi[:, o_mq:o_mk], ML_HEADS, ML_DK),
        _pad_heads(wi[:, o_mk:o_mv], ML_HEADS, ML_DK),
        wi[:, o_z:o_b],
        wi[:, o_mv:o_mo],
        wi[:, o_mo:o_mi],
        jnp.pad(gates, ((0, 0), (0, HEAD_LANES - 16))),
    ], axis=1).astype(BF16)
    wgt = gates.T.astype(BF16)
    mlc = ml_conv.reshape(CONV_W, -1)
    cw = jnp.concatenate([dn_conv.reshape(CONV_W, -1),
                          _pad_heads(mlc[:, 0:ML_HEADS * ML_DK], ML_HEADS, ML_DK),
                          _pad_heads(mlc[:, ML_HEADS * ML_DK:], ML_HEADS, ML_DK)], axis=1)
    zeros4 = jnp.zeros((4,), F32)
    bias16 = jnp.concatenate([zeros4, dn_dt_bias.reshape(4), ml_i_bias.reshape(4), ml_f_bias.reshape(4)])
    alog16 = jnp.concatenate([zeros4, dn_a_log.reshape(4), zeros4, zeros4])
    gpc = jnp.zeros((8, HEAD_LANES), F32).at[0, 0:16].set(bias16).at[1, 0:16].set(alog16)
    gpr = jnp.zeros((16, HEAD_LANES), F32).at[:, 0].set(bias16).at[:, 1].set(alog16)

    tm_in = _pick_tile(seq, 256)
    conv_out, rest, gcol, grow = _inproj(x2, sh1, sc1, norm_mix.reshape(1, d), w_new, wgt, cw, gpc, gpr, seq, tm_in)
    grow3 = grow.reshape(16, t // CHUNK, CHUNK).transpose(1, 0, 2)

    rows = _pick_tile(seq, 256)
    ymix = _mixers(conv_out, rest, gcol, grow3, dn_norm.reshape(1, DN_DV), ml_norm.reshape(1, ML_HEADS * ML_DV),
                   batch, seq, rows)

    wo = w_out.reshape(-1, d).astype(BF16)
    tm_r = _pick_tile(seq, 512)
    brp = jnp.broadcast_to(b_router.reshape(N_EXPERTS, 1), (N_EXPERTS, HEAD_LANES))
    x1, h2, pos, wrow, len_col = _route(
        ymix, wo, x2, g1, sh2, sc2, norm_ffn.reshape(1, d),
        w_router.reshape(d, N_EXPERTS).T, brp, seq, tm_r)

    n_e = N_EXPERTS
    nt = t // tm_r
    bm = 256
    len_te =len_col.reshape(nt, n_e, HEAD_LANES)[:, :, 0]
    seg_len = jnp.pad(len_te, ((0, 0), (0, HEAD_LANES - n_e)))
    n_slots_max = t * TOP_K + n_e * (7 * nt + bm)
    nb = (n_slots_max + bm - 1) // bm
    seg_src, seg_dst, table = _slots(seg_len, bm)

    xs = _dispatch(seg_len, seg_src, seg_dst, table, pos, h2, nb * bm, tm_r, bm)
    y = _experts(table, xs, w_gate_up.reshape(n_e, d, -1), b_gate_up.reshape(n_e, 1, -1),
                 w_down.reshape(n_e, -1, d), b_down.reshape(n_e, 1, d), bm)
    out = _combine(seg_len, seg_src, seg_dst, pos, wrow, y, x1, g2, norm_final.reshape(1, d), shf, scf, seq, tm_r)
    return out.reshape(batch, seq, d)
```

```python
import functools

import jax
import jax.numpy as jnp
from jax import lax
from jax.experimental import pallas as pl
from jax.experimental.pallas import tpu as pltpu

F32 = jnp.float32
BF16 = jnp.bfloat16

CHUNK = 64
CONV_W = 4
EPS = 1e-6

DN_HEADS = 4
DN_DK = 128
DN_DV = 128
ML_HEADS = 4
ML_DK = 64
ML_DV = 128
HEAD_LANES = 128

N_EXPERTS = 32
TOP_K = 4
SWIGLU_LIMIT = 7.0
SWIGLU_ALPHA = 1.702

C_DNQ = 0
C_DNK = 512
C_DNV = 1024
C_MLQ = 1536
C_MLK = 2048
N_CONV = 2560
C_DNZ = 2560
C_MLV = 3072
C_MLO = 3584
C_GATE = 4096
N_PROJ = 4224
N_REST = C_GATE - N_CONV

V7X_VMEM_LIMIT = 56 * 1024 * 1024
MXU_COLS = 256
COMMON_SEGMENT_ROWS = 128

NEG_BIG = -1e30


def _sigmoid(x):
    return 1.0 / (1.0 + jnp.exp(-x))


def _softplus(x):
    return jnp.maximum(x, 0.0) + jnp.log(1.0 + jnp.exp(-jnp.abs(x)))


def _split3(v):
    hi = v.astype(BF16)
    r1 = v - hi.astype(F32)
    mid = r1.astype(BF16)
    lo = (r1 - mid.astype(F32)).astype(BF16)
    return hi, mid, lo


def _dot(a, b):
    return jnp.dot(a, b, preferred_element_type=F32)


def _dot_nt(a, b):
    return lax.dot_general(a, b, (((1,), (1,)), ((), ())), preferred_element_type=F32)


def _dot_tn(a, b):
    return lax.dot_general(a, b, (((0,), (0,)), ((), ())), preferred_element_type=F32)


def _dot_exact_right(sel_bf16, v):
    hi, mid, lo = _split3(v)
    return _dot(sel_bf16, hi) + _dot(sel_bf16, mid) + _dot(sel_bf16, lo)


def _dot_exact_left(v, sel_bf16):
    hi, mid, lo = _split3(v)
    return _dot(hi, sel_bf16) + _dot(mid, sel_bf16) + _dot(lo, sel_bf16)


def _mods_kernel(c_ref, w_ref, b_ref, o_ref):
    c = c_ref[...]
    cond = c * _sigmoid(c)
    ch, cm, cl = _split3(cond)
    wh, wm, wl = _split3(w_ref[...])
    acc = _dot(ch, wh) + (_dot(ch, wm) + _dot(cm, wh)) + (_dot(ch, wl) + _dot(cm, wm) + _dot(cl, wh))
    o_ref[...] = acc + b_ref[...]


def _mods(c_pad, w, b):
    m, d = c_pad.shape
    n = w.shape[1]
    tn = 1024
    return pl.pallas_call(
        _mods_kernel,
        grid=(n // tn,),
        in_specs=[pl.BlockSpec((m, d), lambda j: (0, 0)),
                  pl.BlockSpec((d, tn), lambda j: (0, j)),
                  pl.BlockSpec((1, tn), lambda j: (0, j))],
        out_specs=pl.BlockSpec((m, tn), lambda j: (0, j)),
        out_shape=jax.ShapeDtypeStruct((m, n), F32),
        compiler_params=pltpu.CompilerParams(dimension_semantics=("arbitrary",),
                                             vmem_limit_bytes=V7X_VMEM_LIMIT),
        name="mods",
    )(c_pad, w, b)


def _gate_transform(v, bias, alog, cls):
    vb = v + bias
    beta = _sigmoid(v)
    g = -jnp.exp(alog) * _softplus(vb)
    logf = -_softplus(-vb)
    return jnp.where(cls == 0, beta, jnp.where(cls == 1, g, jnp.where(cls == 2, vb, jnp.where(cls == 3, logf, 0.0))))


def _inproj_kernel(tiles_per_seq, x_ref, sh_ref, sc_ref, nw_ref, w_ref, wgt_ref, cw_ref, gpc_ref, gpr_ref,
                   conv_ref, rest_ref, gcol_ref, grow_ref, cbuf):
    tm = x_ref.shape[0]
    i = pl.program_id(0)
    x = x_ref[...]
    ms = jnp.mean(x * x, axis=-1, keepdims=True)
    h = x * lax.rsqrt(ms + EPS) * nw_ref[...]
    h = h * (1.0 + sc_ref[...]) + sh_ref[...]
    hb = h.astype(BF16)

    @pl.when(i % tiles_per_seq == 0)
    def _():
        cbuf[0:8, :] = jnp.zeros((8, N_CONV), F32)

    group = 4 * HEAD_LANES
    for lo in range(0, N_CONV, group):
        cols = slice(lo, lo + group)
        pc = _dot(hb, w_ref[:, cols])
        cbuf[8:tm + 8, cols] = pc
        acc = cw_ref[CONV_W - 1:CONV_W, cols] * pc
        for j in range(CONV_W - 1):
            acc = acc + cw_ref[j:j + 1, cols] * cbuf[8 - (CONV_W - 1) + j:8 - (CONV_W - 1) + j + tm, cols]
        cbuf[0:8, cols] = cbuf[tm:tm + 8, cols]
        y = acc * _sigmoid(acc)
        if lo in (C_DNQ, C_DNK):
            scale = DN_DK ** -0.5 if lo == C_DNQ else 1.0
            for hh in range(DN_HEADS):
                uh = y[:, hh * HEAD_LANES:(hh + 1) * HEAD_LANES]
                un = uh * lax.rsqrt(jnp.sum(uh * uh, axis=-1, keepdims=True) + EPS)
                conv_ref[:, lo + hh * HEAD_LANES:lo + (hh + 1) * HEAD_LANES] = un * scale if lo == C_DNQ else un
        elif lo == C_MLQ:
            conv_ref[:, cols] = y * (ML_DK ** -0.5)
        else:
            conv_ref[:, cols] = y

    z = _dot(hb, w_ref[:, C_DNZ:C_MLV])
    rest_ref[:, 0:512] = z * _sigmoid(z)
    rest_ref[:, 512:1024] = _dot(hb, w_ref[:, C_MLV:C_MLO])
    rest_ref[:, 1024:1536] = _sigmoid(_dot(hb, w_ref[:, C_MLO:C_GATE]))

    r_i = lax.broadcasted_iota(jnp.int32, (tm, tm), 0)
    c_i = lax.broadcasted_iota(jnp.int32, (tm, tm), 1)
    same_chunk = (r_i // CHUNK) == (c_i // CHUNK)
    tril = jnp.where(same_chunk & (c_i <= r_i), 1.0, 0.0).astype(BF16)
    triu = jnp.where(same_chunk & (r_i <= c_i), 1.0, 0.0).astype(BF16)

    gc = _dot(hb, w_ref[:, C_GATE:N_PROJ])
    cls_c = lax.broadcasted_iota(jnp.int32, (tm, HEAD_LANES), 1) // 4
    gt = _gate_transform(gc, gpc_ref[0:1, :], gpc_ref[1:2, :], cls_c)
    cs = _dot_exact_right(tril, gt)
    gcol_ref[...] = jnp.where((cls_c == 1) | (cls_c == 3), cs, gt)

    gr = _dot_nt(wgt_ref[...], hb)
    cls_r = lax.broadcasted_iota(jnp.int32, (16, tm), 0) // 4
    gtr = _gate_transform(gr, gpr_ref[:, 0:1], gpr_ref[:, 1:2], cls_r)
    csr = _dot_exact_left(gtr, triu)
    grow_ref[...] = jnp.where((cls_r == 1) | (cls_r == 3), csr, gtr)


def _inproj(x2, sh, sc, nw, w_new, wgt, cw, gpc, gpr, seq, tm):
    t, d = x2.shape
    tps = seq // tm
    kern = functools.partial(_inproj_kernel, tps)
    return pl.pallas_call(
        kern,
        grid=(t // tm,),
        in_specs=[pl.BlockSpec((tm, d), lambda i: (i, 0)),
                  pl.BlockSpec((None, 1, d), lambda i: (i // tps, 0, 0)),
                  pl.BlockSpec((None, 1, d), lambda i: (i // tps, 0, 0)),
                  pl.BlockSpec((1, d), lambda i: (0, 0)),
                  pl.BlockSpec((d, N_PROJ), lambda i: (0, 0)),
                  pl.BlockSpec((16, d), lambda i: (0, 0)),
                  pl.BlockSpec((CONV_W, N_CONV), lambda i: (0, 0)),
                  pl.BlockSpec((8, HEAD_LANES), lambda i: (0, 0)),
                  pl.BlockSpec((16, HEAD_LANES), lambda i: (0, 0))],
        out_specs=[pl.BlockSpec((tm, N_CONV), lambda i: (i, 0)),
                   pl.BlockSpec((tm, N_REST), lambda i: (i, 0)),
                   pl.BlockSpec((tm, HEAD_LANES), lambda i: (i, 0)),
                   pl.BlockSpec((16, tm), lambda i: (0, i))],
        out_shape=[jax.ShapeDtypeStruct((t, N_CONV), F32),
                   jax.ShapeDtypeStruct((t, N_REST), F32),
                   jax.ShapeDtypeStruct((t, HEAD_LANES), F32),
                   jax.ShapeDtypeStruct((16, t), F32)],
        scratch_shapes=[pltpu.VMEM((tm + 8, N_CONV), F32)],
        compiler_params=pltpu.CompilerParams(dimension_semantics=("arbitrary",),
                                             vmem_limit_bytes=V7X_VMEM_LIMIT),
        name="inproj",
    )(x2, sh, sc, nw, w_new, wgt, cw, gpc, gpr)


def _chunk_masks():
    r = lax.broadcasted_iota(jnp.int32, (CHUNK, CHUNK), 0)
    c = lax.broadcasted_iota(jnp.int32, (CHUNK, CHUNK), 1)
    return r >= c, r > c, r == c


def _bdot(a, b):
    return lax.dot_general(a, b, (((2,), (1,)), ((0,), (0,))), preferred_element_type=F32)


def _bdot_nt(a, b):
    return lax.dot_general(a, b, (((2,), (2,)), ((0,), (0,))), preferred_element_type=F32)


def _unit_lower_inverse(lower, row, col):
    x = jnp.where(row == col, 1.0, 0.0) - jnp.where((row >> 1) == (col >> 1), lower, 0.0)
    shift = 1
    while (1 << shift) < CHUNK:
        couple = ((row >> (shift + 1)) == (col >> (shift + 1))) & ((row >> shift) != (col >> shift))
        cb = jnp.where(couple, lower, 0.0).astype(BF16)
        xb = x.astype(BF16)
        x = x - _bdot(_bdot(xb, cb).astype(BF16), xb)
        shift += 1
        yield
    return x


def _seq_heads(n_seq, n_heads):
    return [(s, hh) for s in range(n_seq) for hh in range(n_heads)]


def _deltanet_steps(q_ref, k_ref, v_ref, gc_ref, gr_ref, z_ref, nw_ref, o_ref, s_ref):
    n_seq, nc = gr_ref.shape[0], gr_ref.shape[1]

    @pl.when(pl.program_id(1) == 0)
    def _():
        s_ref[...] = jnp.zeros(s_ref.shape, F32)

    row = lax.broadcasted_iota(jnp.int32, (CHUNK, CHUNK), 0)
    col = lax.broadcasted_iota(jnp.int32, (CHUNK, CHUNK), 1)
    incl = row >= col
    strict = row > col
    nw = nw_ref[...]
    gcc = gc_ref[...]
    grr = gr_ref[...]

    streams = _seq_heads(n_seq, DN_HEADS)
    nh = len(streams)
    nb = nh * nc

    def heads(ref):
        return jnp.stack([ref[s, :, hh * HEAD_LANES:(hh + 1) * HEAD_LANES] for s, hh in streams],
                         axis=0).reshape(nb, CHUNK, HEAD_LANES)

    def col_gate(lane0):
        return jnp.stack([gcc[s, :, lane0 + hh:lane0 + hh + 1] for s, hh in streams], axis=0).reshape(nb, CHUNK, 1)

    q = heads(q_ref)
    k = heads(k_ref)
    v = heads(v_ref)
    beta = col_gate(0)
    g_c = col_gate(4)
    g_r = jnp.stack([grr[s, :, 4 + hh:5 + hh, :] for s, hh in streams], axis=0).reshape(nb, 1, CHUNK)
    g_last = g_c[:, CHUNK - 1:CHUNK, :]
    decay = jnp.exp(jnp.where(incl, g_c - g_r, NEG_BIG))
    kb = k.astype(BF16)
    kk = _bdot_nt(kb, kb)
    lower = jnp.where(strict, beta * kk * decay, 0.0)
    yield
    tinv = yield from _unit_lower_inverse(lower, row, col)
    eg = jnp.exp(g_c)
    rhs = jnp.concatenate([v * beta, k * (beta * eg)], axis=-1)
    sol = _bdot(tinv.astype(BF16), rhs.astype(BF16))
    yield
    w_val = sol[:, :, 0:DN_DV].reshape(nh, nc, CHUNK, DN_DV)
    kq = jnp.concatenate([sol[:, :, DN_DV:DN_DV + DN_DK], q * eg], axis=1).astype(BF16)
    kq = kq.reshape(nh, nc, 2 * CHUNK, DN_DK)
    qk = (_bdot_nt(q.astype(BF16), kb) * decay).astype(BF16).reshape(nh, nc, CHUNK, CHUNK)
    k_dec_t = jnp.swapaxes(k * jnp.exp(g_last - g_c), 1, 2).astype(BF16).reshape(nh, nc, DN_DK, CHUNK)
    s_dec = jnp.exp(g_last).reshape(nh, nc, 1, 1)
    yield

    state = s_ref[...]
    outs = []
    for c in range(nc):
        both = _bdot(kq[:, c], state.astype(BF16))
        v_new = w_val[:, c] - both[:, 0:CHUNK]
        vb = v_new.astype(BF16)
        outs.append(both[:, CHUNK:2 * CHUNK] + _bdot(qk[:, c], vb))
        state = s_dec[:, c] * state + _bdot(k_dec_t[:, c], vb)
        yield
    s_ref[...] = state

    o = jnp.stack(outs, axis=1)
    on = o * lax.rsqrt(jnp.mean(o * o, axis=-1, keepdims=True) + EPS) * nw
    on = on.reshape(nh, nc * CHUNK, DN_DV)
    for idx, (s, hh) in enumerate(streams):
        lanes = slice(hh * HEAD_LANES, (hh + 1) * HEAD_LANES)
        o_ref[s, :, lanes] = on[idx] * z_ref[s, :, lanes]


def _mlstm_steps(q_ref, k_ref, v_ref, gc_ref, gr_ref, og_ref, nw_ref, o_ref, c_ref, n_ref, m_ref):
    n_seq, nc = gr_ref.shape[0], gr_ref.shape[1]

    @pl.when(pl.program_id(1) == 0)
    def _():
        c_ref[...] = jnp.zeros(c_ref.shape, F32)
        n_ref[...] = jnp.zeros(n_ref.shape, F32)
        m_ref[...] = jnp.zeros(m_ref.shape, F32)

    incl, _, _ = _chunk_masks()
    gcc = gc_ref[...]
    grr = gr_ref[...]

    streams = _seq_heads(n_seq, ML_HEADS)
    nh = len(streams)

    def heads(ref):
        return jnp.stack([ref[s, :, hh * HEAD_LANES:(hh + 1) * HEAD_LANES] for s, hh in streams],
                         axis=0).reshape(nh, nc, CHUNK, HEAD_LANES)

    def col_gate(lane0):
        return jnp.stack([gcc[s, :, lane0 + hh:lane0 + hh + 1] for s, hh in streams],
                         axis=0).reshape(nh, nc, CHUNK, 1)

    def row_gate(row0):
        return jnp.stack([grr[s, :, row0 + hh:row0 + hh + 1, :] for s, hh in streams], axis=0)

    q = heads(q_ref)
    k = heads(k_ref)
    v = heads(v_ref)
    i_c = col_gate(8)
    b_c = col_gate(12)
    i_r = row_gate(8)
    b_r = row_gate(12)
    b_last = b_c[:, :, CHUNK - 1:CHUNK, :]
    d_mat = jnp.where(incl, b_c - b_r + i_r, NEG_BIG)
    m_intra = jnp.max(d_mat, axis=-1, keepdims=True)
    g_end = b_last - b_c + i_c
    g_end_max = jnp.max(g_end, axis=2, keepdims=True)
    yield

    m_run = m_ref[:, 0:1, 0:1].reshape(nh, 1, 1, 1)
    m_before = []
    for c in range(nc):
        m_before.append(m_run)
        m_run = jnp.maximum(b_last[:, c:c + 1] + m_run, g_end_max[:, c:c + 1])
    m_s = jnp.concatenate(m_before, axis=1)
    m_new = jnp.maximum(b_last + m_s, g_end_max)
    keep = jnp.exp(b_last + m_s - m_new)
    yield

    nb = nh * nc
    qb = q.astype(BF16)
    kb = k.astype(BF16)
    vb = v.astype(BF16)
    m_t = jnp.maximum(b_c + m_s, m_intra)
    inter = jnp.exp(b_c + m_s - m_t)
    qk = _bdot_nt(qb.reshape(nb, CHUNK, HEAD_LANES), kb.reshape(nb, CHUNK, HEAD_LANES))
    p = jnp.exp(d_mat - m_t) * qk.reshape(nh, nc, CHUNK, CHUNK)
    yield
    intra = _bdot(p.astype(BF16).reshape(nb, CHUNK, CHUNK), vb.reshape(nb, CHUNK, ML_DV)).reshape(nh, nc, CHUNK, ML_DV)
    p_sum = jnp.sum(p, axis=-1, keepdims=True)
    yield
    kw = k * jnp.exp(g_end - m_new)
    kw_t = jnp.swapaxes(kw.reshape(nb, CHUNK, HEAD_LANES), 1, 2).astype(BF16)
    d_state = _bdot(kw_t, vb.reshape(nb, CHUNK, ML_DV)).reshape(nh, nc, HEAD_LANES, ML_DV)
    kw_sum = jnp.sum(kw, axis=2, keepdims=True)
    yield

    c_s = c_ref[...]
    n_s = n_ref[:, 0:1, :]
    q_c = []
    q_n = []
    for c in range(nc):
        q_c.append(_bdot(qb[:, c], c_s.astype(BF16)))
        q_n.append(jnp.sum(q[:, c] * n_s, axis=-1, keepdims=True))
        c_s = keep[:, c] * c_s + d_state[:, c]
        n_s = keep[:, c] * n_s + kw_sum[:, c]
        yield
    c_ref[...] = c_s
    n_ref[...] = jnp.broadcast_to(n_s, (nh, 8, HEAD_LANES))
    m_ref[...] = jnp.broadcast_to(m_run.reshape(nh, 1, 1), (nh, 8, HEAD_LANES))

    num = inter * jnp.stack(q_c, axis=1) + intra
    den = inter * jnp.stack(q_n, axis=1) + p_sum
    h = num / jnp.maximum(jnp.abs(den), jnp.exp(-m_t))
    hr = h * lax.rsqrt(jnp.mean(h * h, axis=-1, keepdims=True) + EPS)
    hr = hr.reshape(nh, nc * CHUNK, ML_DV)
    for idx, (s, hh) in enumerate(streams):
        lanes = slice(hh * HEAD_LANES, (hh + 1) * HEAD_LANES)
        o_ref[s, :, lanes] = hr[idx] * nw_ref[:, lanes] * og_ref[s, :, lanes]


def _mixers_kernel(dq_ref, dk_ref, dv_ref, gc_ref, gr_ref, z_ref, dnw_ref, mq_ref, mk_ref, mv_ref, og_ref, mnw_ref,
                   o_ref, s_ref, c_ref, n_ref, m_ref):
    n_a = DN_HEADS * HEAD_LANES
    n_b = ML_HEADS * HEAD_LANES
    stages = [_deltanet_steps(dq_ref, dk_ref, dv_ref, gc_ref, gr_ref, z_ref, dnw_ref, o_ref.at[:, :, 0:n_a], s_ref),
              _mlstm_steps(mq_ref, mk_ref, mv_ref, gc_ref, gr_ref, og_ref, mnw_ref, o_ref.at[:, :, n_a:n_a + n_b],
                           c_ref, n_ref, m_ref)]
    while stages:
        for stage in list(stages):
            try:
                next(stage)
            except StopIteration:
                stages.remove(stage)


def _mixers(conv_out, rest, gcol, grow3, dn_nw, ml_nw, batch, seq, rows):
    t = conv_out.shape[0]
    nj = seq // rows
    cpb = rows // CHUNK
    width = 4 * HEAD_LANES
    assert DN_HEADS * HEAD_LANES == width and ML_HEADS * HEAD_LANES == width
    n_seq = 2 if batch % 2 == 0 else 1
    conv3 = conv_out.reshape(batch, seq, -1)
    rest3 = rest.reshape(batch, seq, -1)
    gcol3 = gcol.reshape(batch, seq, -1)
    grow4 = grow3.reshape(batch, seq // CHUNK, 16, CHUNK)

    def conv_block(c0):
        return pl.BlockSpec((n_seq, rows, width), lambda b, j: (b, j, c0 // width))

    def rest_block(c0):
        return pl.BlockSpec((n_seq, rows, width), lambda b, j: (b, j, (c0 - N_CONV) // width))

    gates = [pl.BlockSpec((n_seq, rows, HEAD_LANES), lambda b, j: (b, j, 0)),
             pl.BlockSpec((n_seq, cpb, 16, CHUNK), lambda b, j: (b, j, 0, 0))]
    out = pl.pallas_call(
        _mixers_kernel,
        grid=(batch // n_seq, nj),
        in_specs=[conv_block(C_DNQ), conv_block(C_DNK), conv_block(C_DNV)] + gates
                 + [rest_block(C_DNZ), pl.BlockSpec((1, HEAD_LANES), lambda b, j: (0, 0)),
                    conv_block(C_MLQ), conv_block(C_MLK), rest_block(C_MLV), rest_block(C_MLO),
                    pl.BlockSpec((1, width), lambda b, j: (0, 0))],
        out_specs=pl.BlockSpec((n_seq, rows, 2 * width), lambda b, j: (b, j, 0)),
        out_shape=jax.ShapeDtypeStruct((batch, seq, 2 * width), F32),
        scratch_shapes=[pltpu.VMEM((n_seq * DN_HEADS, DN_DK, DN_DV), F32),
                        pltpu.VMEM((n_seq * ML_HEADS, HEAD_LANES, ML_DV), F32),
                        pltpu.VMEM((n_seq * ML_HEADS, 8, HEAD_LANES), F32),
                        pltpu.VMEM((n_seq * ML_HEADS, 8, HEAD_LANES), F32)],
        compiler_params=pltpu.CompilerParams(dimension_semantics=("arbitrary", "arbitrary"),
                                             vmem_limit_bytes=V7X_VMEM_LIMIT),
        name="mixers",
    )(conv3, conv3, conv3, gcol3, grow4, rest3, dn_nw, conv3, conv3, rest3, rest3, ml_nw)
    return out.reshape(t, 2 * width)


def _route_kernel(y_ref, wo_ref, x_ref, g1_ref, sh_ref, sc_ref, nw_ref, wrt_ref, br_ref,
                  x1_ref, h2_ref, pos_ref, wrow_ref, len_ref):
    tm = x_ref.shape[0]

    mix = _dot(y_ref[...].astype(BF16), wo_ref[...])
    x1 = x_ref[...] + g1_ref[...] * mix
    x1_ref[...] = x1
    ms = jnp.mean(x1 * x1, axis=-1, keepdims=True)
    h2 = x1 * lax.rsqrt(ms + EPS) * nw_ref[...]
    h2 = h2 * (1.0 + sc_ref[...]) + sh_ref[...]
    h2_ref[...] = h2.astype(BF16)

    hh, hm, _ = _split3(h2)
    wh, wm, _ = _split3(wrt_ref[...])
    logits = _dot_nt(wh, hh) + (_dot_nt(wh, hm) + _dot_nt(wm, hh)) + br_ref[:, 0:1]

    e_i = lax.broadcasted_iota(jnp.int32, (N_EXPERTS, tm), 0)
    work = logits
    tops = []
    sels = []
    hots = []
    for _ in range(TOP_K):
        m = jnp.max(work, axis=0, keepdims=True)
        sel = jnp.min(jnp.where(work == m, e_i, N_EXPERTS), axis=0, keepdims=True)
        hot = e_i == sel
        work = jnp.where(hot, NEG_BIG, work)
        tops.append(m)
        sels.append(sel)
        hots.append(hot)
    exps = [jnp.exp(tl - tops[0]) for tl in tops]
    denom = exps[0] + exps[1] + exps[2] + exps[3]
    ws = [e / denom for e in exps]

    chosen = jnp.zeros((N_EXPERTS, tm), F32)
    for hot in hots:
        chosen = chosen + jnp.where(hot, 1.0, 0.0)

    r_i = lax.broadcasted_iota(jnp.int32, (tm, tm), 0)
    c_i = lax.broadcasted_iota(jnp.int32, (tm, tm), 1)
    strict_upper = jnp.where(r_i < c_i, 1.0, 0.0).astype(BF16)
    prefix = _dot(chosen.astype(BF16), strict_upper)
    n_e = jnp.sum(chosen, axis=1, keepdims=True)
    len8 = jnp.ceil(n_e * 0.125) * 8.0
    er = lax.broadcasted_iota(jnp.int32, (N_EXPERTS, N_EXPERTS), 0)
    ec = lax.broadcasted_iota(jnp.int32, (N_EXPERTS, N_EXPERTS), 1)
    strict_lower = jnp.where(ec < er, 1.0, 0.0).astype(BF16)
    len8b = jnp.broadcast_to(len8, (N_EXPERTS, HEAD_LANES))
    off8 = _dot_exact_right(strict_lower, len8b)[:, 0:1]
    len_ref[...] = len8b.astype(jnp.int32)

    for kk in range(TOP_K):
        wrow_ref[kk:kk + 1, :] = ws[kk]
        pos = jnp.sum(jnp.where(hots[kk], prefix + off8, 0.0), axis=0, keepdims=True)
        pos_ref[kk:kk + 1, :] = pos.astype(jnp.int32)


def _route(ymix, wo, x2, g1, sh2, sc2, nw, wrt, br, seq, tm):
    t, d = x2.shape
    tps = seq // tm
    bmap = lambda i: (i // tps, 0, 0)
    return pl.pallas_call(
        _route_kernel,
        grid=(t // tm,),
        in_specs=[pl.BlockSpec((tm, ymix.shape[1]), lambda i: (i, 0)),
                  pl.BlockSpec(wo.shape, lambda i: (0, 0)),
                  pl.BlockSpec((tm, d), lambda i: (i, 0)),
                  pl.BlockSpec((None, 1, d), bmap),
                  pl.BlockSpec((None, 1, d), bmap),
                  pl.BlockSpec((None, 1, d), bmap),
                  pl.BlockSpec((1, d), lambda i: (0, 0)),
                  pl.BlockSpec((N_EXPERTS, d), lambda i: (0, 0)),
                  pl.BlockSpec((N_EXPERTS, HEAD_LANES), lambda i: (0, 0))],
        out_specs=[pl.BlockSpec((tm, d), lambda i: (i, 0)),
                   pl.BlockSpec((tm, d), lambda i: (i, 0)),
                   pl.BlockSpec((TOP_K, tm), lambda i: (0, i)),
                   pl.BlockSpec((TOP_K, tm), lambda i: (0, i)),
                   pl.BlockSpec((N_EXPERTS, HEAD_LANES), lambda i: (i, 0))],
        out_shape=[jax.ShapeDtypeStruct((t, d), F32),
                   jax.ShapeDtypeStruct((t, d), BF16),
                   jax.ShapeDtypeStruct((TOP_K, t), jnp.int32),
                   jax.ShapeDtypeStruct((TOP_K, t), F32),
                   jax.ShapeDtypeStruct((t // tm * N_EXPERTS, HEAD_LANES), jnp.int32)],
        compiler_params=pltpu.CompilerParams(dimension_semantics=("arbitrary",),
                                             vmem_limit_bytes=V7X_VMEM_LIMIT),
        name="route",
    )(ymix, wo, x2, g1, sh2, sc2, nw, wrt, br)


TAIL_START, TAIL_LEN, PAD_END, BLOCK_START, BLOCK_COUNT = range(5)


def _slots_kernel(bm, len_te_ref, ssrc_ref, sdst_ref, tab_ref):
    nt, lanes = len_te_ref.shape
    len_te = len_te_ref[...].astype(F32)

    r_l = lax.broadcasted_iota(jnp.int32, (lanes, lanes), 0)
    c_l = lax.broadcasted_iota(jnp.int32, (lanes, lanes), 1)
    upper_incl = jnp.where(r_l <= c_l, 1.0, 0.0).astype(BF16)
    upper_strict = jnp.where(r_l < c_l, 1.0, 0.0).astype(BF16)
    total_r = jnp.sum(len_te, axis=0, keepdims=True)
    padded_r = jnp.ceil(total_r * (1.0 / bm)) * bm
    pad_end_r = _dot_exact_left(jnp.broadcast_to(padded_r, (8, lanes)), upper_incl)[0:1, :]
    pad_start_r = pad_end_r - padded_r
    r_t = lax.broadcasted_iota(jnp.int32, (nt, nt), 0)
    c_t = lax.broadcasted_iota(jnp.int32, (nt, nt), 1)
    lower_strict_t = jnp.where(c_t < r_t, 1.0, 0.0).astype(BF16)
    before = _dot_exact_right(lower_strict_t, len_te)
    ssrc_ref[...] = _dot_exact_left(len_te, upper_strict).astype(jnp.int32)
    sdst_ref[...] = (pad_start_r + before).astype(jnp.int32)
    tab_ref[TAIL_START:TAIL_START + 1, :] = (pad_start_r + total_r).astype(jnp.int32)
    tab_ref[TAIL_LEN:TAIL_LEN + 1, :] = (padded_r - total_r).astype(jnp.int32)
    tab_ref[PAD_END:PAD_END + 1, :] = pad_end_r.astype(jnp.int32)
    tab_ref[BLOCK_START:BLOCK_START + 1, :] = (pad_start_r * (1.0 / bm)).astype(jnp.int32)
    tab_ref[BLOCK_COUNT:BLOCK_COUNT + 1, :] = (padded_r * (1.0 / bm)).astype(jnp.int32)
    tab_ref[5:8, :] = jnp.zeros((3, lanes), jnp.int32)


def _slots(len_te, bm):
    assert bm & (bm - 1) == 0, "block rows must be a power of two"
    nt, lanes = len_te.shape
    return pl.pallas_call(
        functools.partial(_slots_kernel, bm),
        out_shape=[jax.ShapeDtypeStruct((nt, lanes), jnp.int32),
                   jax.ShapeDtypeStruct((nt, lanes), jnp.int32),
                   jax.ShapeDtypeStruct((8, lanes), jnp.int32)],
        compiler_params=pltpu.CompilerParams(vmem_limit_bytes=V7X_VMEM_LIMIT),
        name="slots",
    )(len_te)


def _segment_pieces(max_rows):
    sizes = []
    s = 8
    while s <= max_rows:
        sizes.append(s)
        s *= 2
    return sizes[::-1]


def _segment_dma(src_ref, src0, dst_ref, dst0, nrows, sizes, sem, start):
    def pieces(group, off):
        for sz in group:
            bit = nrows & sz

            @pl.when(bit != 0)
            def _(off=off, sz=sz):
                s0 = pl.multiple_of(src0 + off, 8)
                d0 = pl.multiple_of(dst0 + off, 8)
                cp = pltpu.make_async_copy(src_ref.at[pl.ds(s0, sz)], dst_ref.at[pl.ds(d0, sz)], sem)
                if start:
                    cp.start()
                else:
                    cp.wait()

            off = off + bit

    large = [sz for sz in sizes if sz >= COMMON_SEGMENT_ROWS]
    small = [sz for sz in sizes if sz < COMMON_SEGMENT_ROWS]
    if large and small:
        @pl.when(nrows >= COMMON_SEGMENT_ROWS)
        def _():
            pieces(large, 0)

        pieces(small, nrows & ~(COMMON_SEGMENT_ROWS - 1))
    else:
        pieces(sizes, 0)


def _sorted_onehot(pos_ref, row0, nrows, tt):
    r_i = lax.broadcasted_iota(jnp.int32, (nrows, tt), 0) + row0
    hit = r_i == pos_ref[0:1, :]
    for kk in range(1, TOP_K):
        hit = hit | (r_i == pos_ref[kk:kk + 1, :])
    return hit


def _row_groups(nrows, n_groups):
    size = -(-nrows // n_groups // 8) * 8
    return [(r0, min(size, nrows - r0)) for r0 in range(0, nrows, size)]


def _dispatch_kernel(bm, len_ref, ssrc_ref, sdst_ref, tail_ref, pos_ref, h2_ref, xs_hbm, buf, sem):
    i = pl.program_id(0)
    nt = pl.num_programs(0)
    tt = h2_ref.shape[0]
    nrows = buf.shape[1]
    sizes = _segment_pieces(tt)
    slot = i % 2

    h2 = h2_ref[...]
    for r0, rn in _row_groups(nrows, 4):
        perm = jnp.where(_sorted_onehot(pos_ref, r0, rn, tt), 1.0, 0.0).astype(BF16)
        buf[slot, r0:r0 + rn, :] = _dot(perm, h2)

    def seg_start(tile, which):
        def body(e, carry):
            _segment_dma(buf.at[which], ssrc_ref[tile, e], xs_hbm, sdst_ref[tile, e], len_ref[tile, e], sizes,
                         sem.at[which], True)
            return carry
        lax.fori_loop(0, N_EXPERTS, body, 0)

    def seg_wait(tile, which):
        total = ssrc_ref[tile, N_EXPERTS - 1] + len_ref[tile, N_EXPERTS - 1]
        _segment_dma(buf.at[which], 0, xs_hbm, 0, total, _segment_pieces(nrows), sem.at[which], False)

    @pl.when(i > 0)
    def _():
        seg_wait(i - 1, 1 - slot)

    seg_start(i, slot)

    @pl.when(i == nt - 1)
    def _():
        seg_wait(i, slot)
        zrows = bm
        zbuf = buf.at[0]
        zsem = sem.at[0]
        buf[0, 0:zrows, :] = jnp.zeros((zrows, buf.shape[2]), F32)

        def tail(start):
            def body(e, carry):
                _segment_dma(zbuf, 0, xs_hbm, tail_ref[TAIL_START, e], tail_ref[TAIL_LEN, e],
                             _segment_pieces(zrows // 2), zsem, start)
                return carry
            lax.fori_loop(0, N_EXPERTS, body, 0)

        tail(True)
        tail(False)

        used = tail_ref[PAD_END, tail_ref.shape[1] - 1]
        n_unused = (xs_hbm.shape[0] - used) // zrows

        def unused_copy(j):
            d0 = pl.multiple_of(used + j * zrows, 8)
            return pltpu.make_async_copy(zbuf.at[pl.ds(0, zrows)], xs_hbm.at[pl.ds(d0, zrows)], zsem)

        def unused_start(j, carry):
            unused_copy(j).start()
            return carry

        def unused_wait(j, carry):
            unused_copy(j).wait()
            return carry

        lax.fori_loop(0, n_unused, unused_start, 0)
        lax.fori_loop(0, n_unused, unused_wait, 0)


def _dispatch(seg_len, seg_src, seg_dst, tail, pos, h2, n_slots, tt, bm):
    t, d = h2.shape
    nrows = TOP_K * tt + 8 * N_EXPERTS
    assert bm <= nrows
    smem = pl.BlockSpec(memory_space=pltpu.SMEM)
    return pl.pallas_call(
        functools.partial(_dispatch_kernel, bm),
        grid=(t // tt,),
        in_specs=[smem, smem, smem, smem,
                  pl.BlockSpec((TOP_K, tt), lambda i: (0, i)),
                  pl.BlockSpec((tt, d), lambda i: (i, 0))],
        out_specs=pl.BlockSpec(memory_space=pl.ANY),
        out_shape=jax.ShapeDtypeStruct((n_slots, d), F32),
        scratch_shapes=[pltpu.VMEM((2, nrows, d), F32), pltpu.SemaphoreType.DMA((2,))],
        compiler_params=pltpu.CompilerParams(dimension_semantics=("arbitrary",), has_side_effects=True,
                                             vmem_limit_bytes=V7X_VMEM_LIMIT),
        name="dispatch",
    )(seg_len, seg_src, seg_dst, tail, pos, h2)


def _experts_kernel(tab_ref, xs_hbm, wgu_hbm, bgu_ref, wd_hbm, bd_ref, y_hbm, wgu_s, wd_s, wgu_f, wd_f, xbuf, ybuf,
                    sem_in, sem_out, sem_w):
    e = pl.program_id(0)
    n_e = pl.num_programs(0)
    bm = xbuf.shape[1] // 2
    d_ff = wd_f.shape[1]
    ct = wgu_s.shape[2]
    wslot = e % 2

    def w_copies(expert, which):
        return (pltpu.make_async_copy(wgu_hbm.at[expert], wgu_f.at[which], sem_w.at[which]),
                pltpu.make_async_copy(wd_hbm.at[expert], wd_f.at[which], sem_w.at[which]))

    @pl.when(e == 0)
    def _():
        for cp in w_copies(0, 0):
            cp.start()

    for cp in w_copies(e, wslot):
        cp.wait()

    have_next = e + 1 < n_e

    def prefetch_next_weights(cond):
        @pl.when(have_next & cond)
        def _():
            for cp in w_copies(e + 1, 1 - wslot):
                cp.start()
    first = tab_ref[BLOCK_START, e]
    count = tab_ref[BLOCK_COUNT, e]
    npair = count // 2
    odd = count % 2
    tslot = npair % 2

    def pair_rows(j):
        return pl.ds(pl.multiple_of((first + 2 * j) * bm, bm), 2 * bm)

    def x_copy(j, slot):
        return pltpu.make_async_copy(xs_hbm.at[pair_rows(j)], xbuf.at[slot], sem_in.at[slot])

    def y_copy(j, slot):
        return pltpu.make_async_copy(ybuf.at[slot], y_hbm.at[pair_rows(j)], sem_out.at[slot])

    def last_rows():
        return pl.ds(pl.multiple_of((first + 2 * npair) * bm, bm), bm)

    def x_last(slot):
        return pltpu.make_async_copy(xs_hbm.at[last_rows()], xbuf.at[slot, pl.ds(0, bm)], sem_in.at[slot])

    def y_last(slot):
        return pltpu.make_async_copy(ybuf.at[slot, pl.ds(0, bm)], y_hbm.at[last_rows()], sem_out.at[slot])

    @pl.when(npair > 0)
    def _():
        x_copy(0, 0).start()

    @pl.when((npair == 0) & (odd == 1))
    def _():
        x_last(0).start()

    for t in range(wgu_s.shape[0]):
        wgu_s[t] = wgu_f[wslot, :, t * ct:(t + 1) * ct].astype(BF16)
    for t in range(wd_s.shape[0]):
        wd_s[t] = wd_f[wslot, :, t * ct:(t + 1) * ct].astype(BF16)

    def mlp(x):
        xb = x.astype(BF16)
        tpc = 2
        fc = tpc * ct
        nf = d_ff // fc
        n_out = wd_s.shape[0]

        def gate_up(f):
            gates, ups = [], []
            for t in range(f * tpc, (f + 1) * tpc):
                gates.append(_dot(xb, wgu_s[t]) + bgu_ref[:, t * ct:(t + 1) * ct])
                u = d_ff // ct + t
                ups.append(_dot(xb, wgu_s[u]) + bgu_ref[:, u * ct:(u + 1) * ct])
            return jnp.concatenate(gates, axis=-1), jnp.concatenate(ups, axis=-1)

        acts = []
        pre = gate_up(0)
        for f in range(nf):
            gate = jnp.minimum(pre[0], SWIGLU_LIMIT)
            up = jnp.clip(pre[1], -SWIGLU_LIMIT, SWIGLU_LIMIT)
            if f + 1 < nf:
                pre = gate_up(f + 1)
            acts.append(((up + 1.0) * gate * _sigmoid(SWIGLU_ALPHA * gate)).astype(BF16))
        act = jnp.concatenate(acts, axis=-1)
        return [_dot(act, wd_s[n]) + bd_ref[:, n * ct:(n + 1) * ct] for n in range(n_out)]

    def pair(j, carry):
        slot = j % 2
        x_copy(j, slot).wait()

        @pl.when(j + 1 < npair)
        def _():
            x_copy(j + 1, 1 - slot).start()

        @pl.when((j + 1 == npair) & (odd == 1))
        def _():
            x_last(1 - slot).start()

        @pl.when(j >= 2)
        def _():
            y_copy(j - 2, slot).wait()

        prefetch_next_weights(j == 0)
        for n, piece in enumerate(mlp(xbuf[slot])):
            ybuf[slot, :, n * ct:(n + 1) * ct] = piece
        y_copy(j, slot).start()
        return carry

    lax.fori_loop(0, npair, pair, 0)

    prefetch_next_weights(npair == 0)

    @pl.when(odd == 1)
    def _():
        x_last(tslot).wait()

        @pl.when(npair >= 2)
        def _():
            y_copy(npair - 2, tslot).wait()

        for n, piece in enumerate(mlp(xbuf[tslot, 0:bm, :])):
            ybuf[tslot, 0:bm, n * ct:(n + 1) * ct] = piece
        y_last(tslot).start()

    @pl.when((odd == 0) & (npair >= 2))
    def _():
        y_copy(npair - 2, tslot).wait()

    @pl.when(npair >= 1)
    def _():
        y_copy(npair - 1, 1 - tslot).wait()

    @pl.when(odd == 1)
    def _():
        y_last(tslot).wait()

    @pl.when(e == pl.num_programs(0) - 1)
    def _():
        used = first + count
        n_unused = y_hbm.shape[0] // bm - used
        ybuf[0, 0:bm, :] = jnp.zeros((bm, ybuf.shape[2]), F32)

        def z_copy(j):
            d0 = pl.multiple_of((used + j) * bm, bm)
            return pltpu.make_async_copy(ybuf.at[0, pl.ds(0, bm)], y_hbm.at[pl.ds(d0, bm)], sem_out.at[0])

        def z_start(j, carry):
            z_copy(j).start()
            return carry

        def z_wait(j, carry):
            z_copy(j).wait()
            return carry

        lax.fori_loop(0, n_unused, z_start, 0)
        lax.fori_loop(0, n_unused, z_wait, 0)


def _experts(table, xs, wgu, bgu, wd, bd, bm):
    ns, d = xs.shape
    n_e, _, two_ff = wgu.shape
    d_ff = two_ff // 2
    wmap = lambda e: (e, 0, 0)
    return pl.pallas_call(
        _experts_kernel,
        grid=(n_e,),
        in_specs=[pl.BlockSpec(memory_space=pltpu.SMEM),
                  pl.BlockSpec(memory_space=pl.ANY),
                  pl.BlockSpec(memory_space=pl.ANY),
                  pl.BlockSpec((None, 1, two_ff), wmap),
                  pl.BlockSpec(memory_space=pl.ANY),
                  pl.BlockSpec((None, 1, d), wmap)],
        out_specs=pl.BlockSpec(memory_space=pl.ANY),
        out_shape=jax.ShapeDtypeStruct((ns, d), F32),
        scratch_shapes=[pltpu.VMEM((two_ff // MXU_COLS, d, MXU_COLS), BF16),
                        pltpu.VMEM((d // MXU_COLS, d_ff, MXU_COLS), BF16),
                        pltpu.VMEM((2, d, two_ff), F32), pltpu.VMEM((2, d_ff, d), F32),
                        pltpu.VMEM((2, 2 * bm, d), F32), pltpu.VMEM((2, 2 * bm, d), F32),
                        pltpu.SemaphoreType.DMA((2,)), pltpu.SemaphoreType.DMA((2,)),
                        pltpu.SemaphoreType.DMA((2,))],
        compiler_params=pltpu.CompilerParams(dimension_semantics=("arbitrary",), has_side_effects=True,
                                             vmem_limit_bytes=V7X_VMEM_LIMIT),
        name="experts",
    )(table, xs, wgu, bgu, wd, bd)


def _combine_kernel(len_ref, ssrc_ref, sdst_ref, pos_ref, w_ref, y_hbm, x1_ref, g2_ref, nw_ref, sh_ref, sc_ref,
                    o_ref, ybuf, sem):
    i = pl.program_id(0)
    nt = pl.num_programs(0)
    tt = x1_ref.shape[0]
    nrows = ybuf.shape[1]
    sizes = _segment_pieces(tt)
    slot = i % 2

    def seg_start(tile, which):
        def body(e, carry):
            _segment_dma(y_hbm, sdst_ref[tile, e], ybuf.at[which], ssrc_ref[tile, e], len_ref[tile, e], sizes,
                         sem.at[which], True)
            return carry
        lax.fori_loop(0, N_EXPERTS, body, 0)

    def seg_wait(tile, which):
        total = ssrc_ref[tile, N_EXPERTS - 1] + len_ref[tile, N_EXPERTS - 1]
        _segment_dma(y_hbm, 0, ybuf.at[which], 0, total, _segment_pieces(nrows), sem.at[which], False)

    @pl.when(i == 0)
    def _():
        ybuf[...] = jnp.zeros(ybuf.shape, F32)
        seg_start(0, 0)

    @pl.when(i + 1 < nt)
    def _():
        seg_start(i + 1, 1 - slot)

    seg_wait(i, slot)

    acc = None
    for r0, rn in _row_groups(nrows, 3):
        r_i = lax.broadcasted_iota(jnp.int32, (rn, tt), 0) + r0
        wmat = 0.0
        for kk in range(TOP_K):
            wmat = jnp.where(r_i == pos_ref[kk:kk + 1, :], w_ref[kk:kk + 1, :], wmat)
        part = _dot_tn(wmat.astype(BF16), ybuf[slot, r0:r0 + rn, :].astype(BF16))
        acc = part if acc is None else acc + part
    xo = x1_ref[...] + g2_ref[...] * acc
    ms = jnp.mean(xo * xo, axis=-1, keepdims=True)
    hn = xo * lax.rsqrt(ms + EPS) * nw_ref[...]
    o_ref[...] = hn * (1.0 + sc_ref[...]) + sh_ref[...]


def _combine(seg_len, seg_src, seg_dst, pos, wrow, y, x1, g2, nw, shf, scf, seq, tt):
    t, d = x1.shape
    tps = seq // tt
    bmap = lambda i: (i // tps, 0, 0)
    nrows = TOP_K * tt + 8 * N_EXPERTS
    smem = pl.BlockSpec(memory_space=pltpu.SMEM)
    return pl.pallas_call(
        _combine_kernel,
        grid=(t // tt,),
        in_specs=[smem, smem, smem,
                  pl.BlockSpec((TOP_K, tt), lambda i: (0, i)),
                  pl.BlockSpec((TOP_K, tt), lambda i: (0, i)),
                  pl.BlockSpec(memory_space=pl.ANY),
                  pl.BlockSpec((tt, d), lambda i: (i, 0)),
                  pl.BlockSpec((None, 1, d), bmap),
                  pl.BlockSpec((1, d), lambda i: (0, 0)),
                  pl.BlockSpec((None, 1, d), bmap),
                  pl.BlockSpec((None, 1, d), bmap)],
        out_specs=pl.BlockSpec((tt, d), lambda i: (i, 0)),
        out_shape=jax.ShapeDtypeStruct((t, d), F32),
        scratch_shapes=[pltpu.VMEM((2, nrows, d), F32), pltpu.SemaphoreType.DMA((2,))],
        compiler_params=pltpu.CompilerParams(dimension_semantics=("arbitrary",),
                                             vmem_limit_bytes=V7X_VMEM_LIMIT),
        name="combine",
    )(seg_len, seg_src, seg_dst, pos, wrow, y, x1, g2, nw, shf, scf)


def _pad_heads(w, heads, dk):
    r = w.shape[0]
    w3 = w.reshape(r, heads, dk)
    return jnp.pad(w3, ((0, 0), (0, 0), (0, HEAD_LANES - dk))).reshape(r, heads * HEAD_LANES)


def _pick_tile(n, pref):
    tile = pref
    while n % tile:
        tile //= 2
    return tile


def kernel(x, c, w_ada, b_ada, norm_mix, w_in, dn_conv, dn_a_log, dn_dt_bias, dn_norm, ml_conv, ml_i_bias,
           ml_f_bias, ml_norm, w_out, norm_ffn, w_router, b_router, w_gate_up, b_gate_up, w_down, b_down,
           w_ada_final, b_ada_final, norm_final):
    batch, seq, d = x.shape
    assert w_ada.shape[0] == 1, "single-layer block"
    assert seq % CHUNK == 0
    t = batch * seq
    x2 = x.reshape(t, d)

    c_pad = jnp.pad(c, ((0, 8 - batch % 8 if batch % 8 else 0), (0, 0)))
    mod = _mods(c_pad, w_ada.reshape(d, 6 * d), b_ada.reshape(1, 6 * d))[:batch]
    modf = _mods(c_pad, w_ada_final, b_ada_final.reshape(1, 2 * d))[:batch]
    sh1, sc1, g1, sh2, sc2, g2 = [mod[:, None, j * d:(j + 1) * d] for j in range(6)]
    shf, scf = modf[:, None, 0:d], modf[:, None, d:2 * d]

    wi = w_in.reshape(d, -1)
    o_z = 1536
    o_b = 2048
    o_mq = 2056
    o_mk = o_mq + ML_HEADS * ML_DK
    o_mv = o_mk + ML_HEADS * ML_DK
    o_mo = o_mv + ML_HEADS * ML_DV
    o_mi = o_mo + ML_HEADS * ML_DV
    gates = jnp.concatenate([wi[:, o_b:o_mq], wi[:, o_mi:o_mi + 2 * ML_HEADS]], axis=1)
    w_new = jnp.concatenate([
        wi[:, 0:o_z],
        _pad_heads(wi[:, o_mq:o_mk], ML_HEADS, ML_DK),
        _pad_heads(wi[:, o_mk:o_mv], ML_HEADS, ML_DK),
        wi[:, o_z:o_b],
        wi[:, o_mv:o_mo],
        wi[:, o_mo:o_mi],
        jnp.pad(gates, ((0, 0), (0, HEAD_LANES - 16))),
    ], axis=1).astype(BF16)
    wgt = gates.T.astype(BF16)
    mlc = ml_conv.reshape(CONV_W, -1)
    cw = jnp.concatenate([dn_conv.reshape(CONV_W, -1),
                          _pad_heads(mlc[:, 0:ML_HEADS * ML_DK], ML_HEADS, ML_DK),
                          _pad_heads(mlc[:, ML_HEADS * ML_DK:], ML_HEADS, ML_DK)], axis=1)
    zeros4 = jnp.zeros((4,), F32)
    bias16 = jnp.concatenate([zeros4, dn_dt_bias.reshape(4), ml_i_bias.reshape(4), ml_f_bias.reshape(4)])
    alog16 = jnp.concatenate([zeros4, dn_a_log.reshape(4), zeros4, zeros4])
    gpc = jnp.zeros((8, HEAD_LANES), F32).at[0, 0:16].set(bias16).at[1, 0:16].set(alog16)
    gpr = jnp.zeros((16, HEAD_LANES), F32).at[:, 0].set(bias16).at[:, 1].set(alog16)

    tm_in = _pick_tile(seq, 256)
    conv_out, rest, gcol, grow = _inproj(x2, sh1, sc1, norm_mix.reshape(1, d), w_new, wgt, cw, gpc, gpr, seq, tm_in)
    grow3 = grow.reshape(16, t // CHUNK, CHUNK).transpose(1, 0, 2)

    rows = _pick_tile(seq, 256)
    ymix = _mixers(conv_out, rest, gcol, grow3, dn_norm.reshape(1, DN_DV), ml_norm.reshape(1, ML_HEADS * ML_DV),
                   batch, seq, rows)

    wo = w_out.reshape(-1, d).astype(BF16)
    tm_r = _pick_tile(seq, 512)
    brp = jnp.broadcast_to(b_router.reshape(N_EXPERTS, 1), (N_EXPERTS, HEAD_LANES))
    x1, h2, pos, wrow, len_col = _route(
        ymix, wo, x2, g1, sh2, sc2, norm_ffn.reshape(1, d),
        w_router.reshape(d, N_EXPERTS).T, brp, seq, tm_r)

    n_e = N_EXPERTS
    nt = t // tm_r
    bm = 256
    len_te =len_col.reshape(nt, n_e, HEAD_LANES)[:, :, 0]
    seg_len = jnp.pad(len_te, ((0, 0), (0, HEAD_LANES - n_e)))
    n_slots_max = t * TOP_K + n_e * (7 * nt + bm)
    nb = (n_slots_max + bm - 1) // bm
    seg_src, seg_dst, table = _slots(seg_len, bm)

    xs = _dispatch(seg_len, seg_src, seg_dst, table, pos, h2, nb * bm, tm_r, bm)
    y = _experts(table, xs, w_gate_up.reshape(n_e, d, -1), b_gate_up.reshape(n_e, 1, -1),
                 w_down.reshape(n_e, -1, d), b_down.reshape(n_e, 1, d), bm)
    out = _combine(seg_len, seg_src, seg_dst, pos, wrow, y, x1, g2, norm_final.reshape(1, d), shf, scf, seq, tm_r)
    return out.reshape(batch, seq, d)
```

```python
import functools

import jax
import jax.numpy as jnp
from jax import lax
from jax.experimental import pallas as pl
from jax.experimental.pallas import tpu as pltpu

F32 = jnp.float32
BF16 = jnp.bfloat16

CHUNK = 64
CONV_W = 4
EPS = 1e-6

DN_HEADS = 4
DN_DK = 128
DN_DV = 128
ML_HEADS = 4
ML_DK = 64
ML_DV = 128
HEAD_LANES = 128

N_EXPERTS = 32
TOP_K = 4
SWIGLU_LIMIT = 7.0
SWIGLU_ALPHA = 1.702

C_DNQ = 0
C_DNK = 512
C_DNV = 1024
C_MLQ = 1536
C_MLK = 2048
N_CONV = 2560
C_DNZ = 2560
C_MLV = 3072
C_MLO = 3584
C_GATE = 4096
N_PROJ = 4224
N_REST = C_GATE - N_CONV

V7X_VMEM_LIMIT = 56 * 1024 * 1024
MXU_COLS = 256
COMMON_SEGMENT_ROWS = 128
INPROJ_GROUP_COLS = 512

NEG_BIG = -1e30


def _sigmoid(x):
    return 1.0 / (1.0 + jnp.exp(-x))


def _softplus(x):
    return jnp.maximum(x, 0.0) + jnp.log(1.0 + jnp.exp(-jnp.abs(x)))


def _split3(v):
    hi = v.astype(BF16)
    r1 = v - hi.astype(F32)
    mid = r1.astype(BF16)
    lo = (r1 - mid.astype(F32)).astype(BF16)
    return hi, mid, lo


def _dot(a, b):
    return jnp.dot(a, b, preferred_element_type=F32)


def _dot_nt(a, b):
    return lax.dot_general(a, b, (((1,), (1,)), ((), ())), preferred_element_type=F32)


def _dot_tn(a, b):
    return lax.dot_general(a, b, (((0,), (0,)), ((), ())), preferred_element_type=F32)


def _dot_exact_right(sel_bf16, v):
    hi, mid, lo = _split3(v)
    return _dot(sel_bf16, hi) + _dot(sel_bf16, mid) + _dot(sel_bf16, lo)


def _dot_exact_left(v, sel_bf16):
    hi, mid, lo = _split3(v)
    return _dot(hi, sel_bf16) + _dot(mid, sel_bf16) + _dot(lo, sel_bf16)


def _mods_kernel(c_ref, w_ref, b_ref, o_ref):
    c = c_ref[...]
    cond = c * _sigmoid(c)
    ch, cm, cl = _split3(cond)
    wh, wm, wl = _split3(w_ref[...])
    acc = _dot(ch, wh) + (_dot(ch, wm) + _dot(cm, wh)) + (_dot(ch, wl) + _dot(cm, wm) + _dot(cl, wh))
    o_ref[...] = acc + b_ref[...]


def _mods(c_pad, w, b):
    m, d = c_pad.shape
    n = w.shape[1]
    tn = 1024
    return pl.pallas_call(
        _mods_kernel,
        grid=(n // tn,),
        in_specs=[pl.BlockSpec((m, d), lambda j: (0, 0)),
                  pl.BlockSpec((d, tn), lambda j: (0, j)),
                  pl.BlockSpec((1, tn), lambda j: (0, j))],
        out_specs=pl.BlockSpec((m, tn), lambda j: (0, j)),
        out_shape=jax.ShapeDtypeStruct((m, n), F32),
        compiler_params=pltpu.CompilerParams(dimension_semantics=("arbitrary",),
                                             vmem_limit_bytes=V7X_VMEM_LIMIT),
        name="mods",
    )(c_pad, w, b)


def _gate_transform(v, bias, alog, cls):
    vb = v + bias
    beta = _sigmoid(v)
    g = -jnp.exp(alog) * _softplus(vb)
    logf = -_softplus(-vb)
    return jnp.where(cls == 0, beta, jnp.where(cls == 1, g, jnp.where(cls == 2, vb, jnp.where(cls == 3, logf, 0.0))))


def _inproj_kernel(tiles_per_seq, x_ref, sh_ref, sc_ref, nw_ref, w_ref, wgt_ref, cw_ref, gpc_ref, gpr_ref,
                   conv_ref, rest_ref, gcol_ref, grow_ref, cbuf):
    tm = x_ref.shape[0]
    i = pl.program_id(0)
    x = x_ref[...]
    ms = jnp.mean(x * x, axis=-1, keepdims=True)
    h = x * lax.rsqrt(ms + EPS) * nw_ref[...]
    h = h * (1.0 + sc_ref[...]) + sh_ref[...]
    hb = h.astype(BF16)

    @pl.when(i % tiles_per_seq == 0)
    def _():
        cbuf[0:8, :] = jnp.zeros((8, N_CONV), F32)

    group = INPROJ_GROUP_COLS
    for lo in range(0, N_CONV, group):
        cols = slice(lo, lo + group)
        pc = _dot(hb, w_ref[:, cols])
        cbuf[8:tm + 8, cols] = pc
        acc = cw_ref[CONV_W - 1:CONV_W, cols] * pc
        for j in range(CONV_W - 1):
            acc = acc + cw_ref[j:j + 1, cols] * cbuf[8 - (CONV_W - 1) + j:8 - (CONV_W - 1) + j + tm, cols]
        cbuf[0:8, cols] = cbuf[tm:tm + 8, cols]
        y = acc * _sigmoid(acc)
        if lo < C_DNV:
            for h0 in range(0, group, HEAD_LANES):
                uh = y[:, h0:h0 + HEAD_LANES]
                un = uh * lax.rsqrt(jnp.sum(uh * uh, axis=-1, keepdims=True) + EPS)
                conv_ref[:, lo + h0:lo + h0 + HEAD_LANES] = un * (DN_DK ** -0.5) if lo < C_DNK else un
        elif C_MLQ <= lo < C_MLK:
            conv_ref[:, cols] = y * (ML_DK ** -0.5)
        else:
            conv_ref[:, cols] = y

    z = _dot(hb, w_ref[:, C_DNZ:C_MLV])
    rest_ref[:, 0:512] = z * _sigmoid(z)
    rest_ref[:, 512:1024] = _dot(hb, w_ref[:, C_MLV:C_MLO])
    rest_ref[:, 1024:1536] = _sigmoid(_dot(hb, w_ref[:, C_MLO:C_GATE]))

    r_i = lax.broadcasted_iota(jnp.int32, (tm, tm), 0)
    c_i = lax.broadcasted_iota(jnp.int32, (tm, tm), 1)
    same_chunk = (r_i // CHUNK) == (c_i // CHUNK)
    tril = jnp.where(same_chunk & (c_i <= r_i), 1.0, 0.0).astype(BF16)
    triu = jnp.where(same_chunk & (r_i <= c_i), 1.0, 0.0).astype(BF16)

    gc = _dot(hb, w_ref[:, C_GATE:N_PROJ])
    cls_c = lax.broadcasted_iota(jnp.int32, (tm, HEAD_LANES), 1) // 4
    gt = _gate_transform(gc, gpc_ref[0:1, :], gpc_ref[1:2, :], cls_c)
    cs = _dot_exact_right(tril, gt)
    gcol_ref[...] = jnp.where((cls_c == 1) | (cls_c == 3), cs, gt)

    gr = _dot_nt(wgt_ref[...], hb)
    cls_r = lax.broadcasted_iota(jnp.int32, (16, tm), 0) // 4
    gtr = _gate_transform(gr, gpr_ref[:, 0:1], gpr_ref[:, 1:2], cls_r)
    csr = _dot_exact_left(gtr, triu)
    grow_ref[...] = jnp.where((cls_r == 1) | (cls_r == 3), csr, gtr)


def _inproj(x2, sh, sc, nw, w_new, wgt, cw, gpc, gpr, seq, tm):
    t, d = x2.shape
    tps = seq // tm
    kern = functools.partial(_inproj_kernel, tps)
    return pl.pallas_call(
        kern,
        grid=(t // tm,),
        in_specs=[pl.BlockSpec((tm, d), lambda i: (i, 0)),
                  pl.BlockSpec((None, 1, d), lambda i: (i // tps, 0, 0)),
                  pl.BlockSpec((None, 1, d), lambda i: (i // tps, 0, 0)),
                  pl.BlockSpec((1, d), lambda i: (0, 0)),
                  pl.BlockSpec((d, N_PROJ), lambda i: (0, 0)),
                  pl.BlockSpec((16, d), lambda i: (0, 0)),
                  pl.BlockSpec((CONV_W, N_CONV), lambda i: (0, 0)),
                  pl.BlockSpec((8, HEAD_LANES), lambda i: (0, 0)),
                  pl.BlockSpec((16, HEAD_LANES), lambda i: (0, 0))],
        out_specs=[pl.BlockSpec((tm, N_CONV), lambda i: (i, 0)),
                   pl.BlockSpec((tm, N_REST), lambda i: (i, 0)),
                   pl.BlockSpec((tm, HEAD_LANES), lambda i: (i, 0)),
                   pl.BlockSpec((16, tm), lambda i: (0, i))],
        out_shape=[jax.ShapeDtypeStruct((t, N_CONV), F32),
                   jax.ShapeDtypeStruct((t, N_REST), F32),
                   jax.ShapeDtypeStruct((t, HEAD_LANES), F32),
                   jax.ShapeDtypeStruct((16, t), F32)],
        scratch_shapes=[pltpu.VMEM((tm + 8, N_CONV), F32)],
        compiler_params=pltpu.CompilerParams(dimension_semantics=("arbitrary",),
                                             vmem_limit_bytes=V7X_VMEM_LIMIT),
        name="inproj",
    )(x2, sh, sc, nw, w_new, wgt, cw, gpc, gpr)


def _chunk_masks():
    r = lax.broadcasted_iota(jnp.int32, (CHUNK, CHUNK), 0)
    c = lax.broadcasted_iota(jnp.int32, (CHUNK, CHUNK), 1)
    return r >= c, r > c, r == c


def _bdot(a, b):
    return lax.dot_general(a, b, (((2,), (1,)), ((0,), (0,))), preferred_element_type=F32)


def _bdot_nt(a, b):
    return lax.dot_general(a, b, (((2,), (2,)), ((0,), (0,))), preferred_element_type=F32)


def _unit_lower_inverse(lower, row, col):
    x = jnp.where(row == col, 1.0, 0.0) - jnp.where((row >> 1) == (col >> 1), lower, 0.0)
    shift = 1
    while (1 << shift) < CHUNK:
        couple = ((row >> (shift + 1)) == (col >> (shift + 1))) & ((row >> shift) != (col >> shift))
        cb = jnp.where(couple, lower, 0.0).astype(BF16)
        xb = x.astype(BF16)
        x = x - _bdot(_bdot(xb, cb).astype(BF16), xb)
        shift += 1
        yield
    return x


def _seq_heads(n_seq, n_heads):
    return [(s, hh) for s in range(n_seq) for hh in range(n_heads)]


def _deltanet_steps(q_ref, k_ref, v_ref, gc_ref, gr_ref, z_ref, nw_ref, o_ref, s_ref):
    n_seq, nc = gr_ref.shape[0], gr_ref.shape[1]

    @pl.when(pl.program_id(1) == 0)
    def _():
        s_ref[...] = jnp.zeros(s_ref.shape, F32)

    row = lax.broadcasted_iota(jnp.int32, (CHUNK, CHUNK), 0)
    col = lax.broadcasted_iota(jnp.int32, (CHUNK, CHUNK), 1)
    incl = row >= col
    strict = row > col
    nw = nw_ref[...]
    gcc = gc_ref[...]
    grr = gr_ref[...]

    streams = _seq_heads(n_seq, DN_HEADS)
    nh = len(streams)
    nb = nh * nc

    def heads(ref):
        return jnp.stack([ref[s, :, hh * HEAD_LANES:(hh + 1) * HEAD_LANES] for s, hh in streams],
                         axis=0).reshape(nb, CHUNK, HEAD_LANES)

    def col_gate(lane0):
        return jnp.stack([gcc[s, :, lane0 + hh:lane0 + hh + 1] for s, hh in streams], axis=0).reshape(nb, CHUNK, 1)

    q = heads(q_ref)
    k = heads(k_ref)
    v = heads(v_ref)
    beta = col_gate(0)
    g_c = col_gate(4)
    g_r = jnp.stack([grr[s, :, 4 + hh:5 + hh, :] for s, hh in streams], axis=0).reshape(nb, 1, CHUNK)
    g_last = g_c[:, CHUNK - 1:CHUNK, :]
    decay = jnp.exp(jnp.where(incl, g_c - g_r, NEG_BIG))
    kb = k.astype(BF16)
    kk = _bdot_nt(kb, kb)
    lower = jnp.where(strict, beta * kk * decay, 0.0)
    yield
    tinv = yield from _unit_lower_inverse(lower, row, col)
    eg = jnp.exp(g_c)
    rhs = jnp.concatenate([v * beta, k * (beta * eg)], axis=-1)
    sol = _bdot(tinv.astype(BF16), rhs.astype(BF16))
    yield
    w_val = sol[:, :, 0:DN_DV].reshape(nh, nc, CHUNK, DN_DV)
    kq = jnp.concatenate([sol[:, :, DN_DV:DN_DV + DN_DK], q * eg], axis=1).astype(BF16)
    kq = kq.reshape(nh, nc, 2 * CHUNK, DN_DK)
    qk = (_bdot_nt(q.astype(BF16), kb) * decay).astype(BF16).reshape(nh, nc, CHUNK, CHUNK)
    k_dec_t = jnp.swapaxes(k * jnp.exp(g_last - g_c), 1, 2).astype(BF16).reshape(nh, nc, DN_DK, CHUNK)
    s_dec = jnp.exp(g_last).reshape(nh, nc, 1, 1)
    yield

    state = s_ref[...]
    outs = []
    for c in range(nc):
        both = _bdot(kq[:, c], state.astype(BF16))
        v_new = w_val[:, c] - both[:, 0:CHUNK]
        vb = v_new.astype(BF16)
        outs.append(both[:, CHUNK:2 * CHUNK] + _bdot(qk[:, c], vb))
        state = s_dec[:, c] * state + _bdot(k_dec_t[:, c], vb)
        yield
    s_ref[...] = state

    o = jnp.stack(outs, axis=1)
    on = o * lax.rsqrt(jnp.mean(o * o, axis=-1, keepdims=True) + EPS) * nw
    on = on.reshape(nh, nc * CHUNK, DN_DV)
    for idx, (s, hh) in enumerate(streams):
        lanes = slice(hh * HEAD_LANES, (hh + 1) * HEAD_LANES)
        o_ref[s, :, lanes] = on[idx] * z_ref[s, :, lanes]


def _mlstm_steps(q_ref, k_ref, v_ref, gc_ref, gr_ref, og_ref, nw_ref, o_ref, c_ref, n_ref, m_ref):
    n_seq, nc = gr_ref.shape[0], gr_ref.shape[1]

    @pl.when(pl.program_id(1) == 0)
    def _():
        c_ref[...] = jnp.zeros(c_ref.shape, F32)
        n_ref[...] = jnp.zeros(n_ref.shape, F32)
        m_ref[...] = jnp.zeros(m_ref.shape, F32)

    incl, _, _ = _chunk_masks()
    gcc = gc_ref[...]
    grr = gr_ref[...]

    streams = _seq_heads(n_seq, ML_HEADS)
    nh = len(streams)

    def heads(ref):
        return jnp.stack([ref[s, :, hh * HEAD_LANES:(hh + 1) * HEAD_LANES] for s, hh in streams],
                         axis=0).reshape(nh, nc, CHUNK, HEAD_LANES)

    def col_gate(lane0):
        return jnp.stack([gcc[s, :, lane0 + hh:lane0 + hh + 1] for s, hh in streams],
                         axis=0).reshape(nh, nc, CHUNK, 1)

    def row_gate(row0):
        return jnp.stack([grr[s, :, row0 + hh:row0 + hh + 1, :] for s, hh in streams], axis=0)

    q = heads(q_ref)
    k = heads(k_ref)
    v = heads(v_ref)
    i_c = col_gate(8)
    b_c = col_gate(12)
    i_r = row_gate(8)
    b_r = row_gate(12)
    b_last = b_c[:, :, CHUNK - 1:CHUNK, :]
    d_mat = jnp.where(incl, b_c - b_r + i_r, NEG_BIG)
    m_intra = jnp.max(d_mat, axis=-1, keepdims=True)
    g_end = b_last - b_c + i_c
    g_end_max = jnp.max(g_end, axis=2, keepdims=True)
    yield

    m_run = m_ref[:, 0:1, 0:1].reshape(nh, 1, 1, 1)
    m_before = []
    for c in range(nc):
        m_before.append(m_run)
        m_run = jnp.maximum(b_last[:, c:c + 1] + m_run, g_end_max[:, c:c + 1])
    m_s = jnp.concatenate(m_before, axis=1)
    m_new = jnp.maximum(b_last + m_s, g_end_max)
    keep = jnp.exp(b_last + m_s - m_new)
    yield

    nb = nh * nc
    qb = q.astype(BF16)
    kb = k.astype(BF16)
    vb = v.astype(BF16)
    m_t = jnp.maximum(b_c + m_s, m_intra)
    inter = jnp.exp(b_c + m_s - m_t)
    qk = _bdot_nt(qb.reshape(nb, CHUNK, HEAD_LANES), kb.reshape(nb, CHUNK, HEAD_LANES))
    p = jnp.exp(d_mat - m_t) * qk.reshape(nh, nc, CHUNK, CHUNK)
    yield
    intra = _bdot(p.astype(BF16).reshape(nb, CHUNK, CHUNK), vb.reshape(nb, CHUNK, ML_DV)).reshape(nh, nc, CHUNK, ML_DV)
    p_sum = jnp.sum(p, axis=-1, keepdims=True)
    yield
    kw = k * jnp.exp(g_end - m_new)
    kw_t = jnp.swapaxes(kw.reshape(nb, CHUNK, HEAD_LANES), 1, 2).astype(BF16)
    d_state = _bdot(kw_t, vb.reshape(nb, CHUNK, ML_DV)).reshape(nh, nc, HEAD_LANES, ML_DV)
    kw_sum = jnp.sum(kw, axis=2, keepdims=True)
    yield

    c_s = c_ref[...]
    n_s = n_ref[:, 0:1, :]
    q_c = []
    q_n = []
    for c in range(nc):
        q_c.append(_bdot(qb[:, c], c_s.astype(BF16)))
        q_n.append(jnp.sum(q[:, c] * n_s, axis=-1, keepdims=True))
        c_s = keep[:, c] * c_s + d_state[:, c]
        n_s = keep[:, c] * n_s + kw_sum[:, c]
        yield
    c_ref[...] = c_s
    n_ref[...] = jnp.broadcast_to(n_s, (nh, 8, HEAD_LANES))
    m_ref[...] = jnp.broadcast_to(m_run.reshape(nh, 1, 1), (nh, 8, HEAD_LANES))

    num = inter * jnp.stack(q_c, axis=1) + intra
    den = inter * jnp.stack(q_n, axis=1) + p_sum
    h = num / jnp.maximum(jnp.abs(den), jnp.exp(-m_t))
    hr = h * lax.rsqrt(jnp.mean(h * h, axis=-1, keepdims=True) + EPS)
    hr = hr.reshape(nh, nc * CHUNK, ML_DV)
    for idx, (s, hh) in enumerate(streams):
        lanes = slice(hh * HEAD_LANES, (hh + 1) * HEAD_LANES)
        o_ref[s, :, lanes] = hr[idx] * nw_ref[:, lanes] * og_ref[s, :, lanes]


def _mixers_kernel(dq_ref, dk_ref, dv_ref, gc_ref, gr_ref, z_ref, dnw_ref, mq_ref, mk_ref, mv_ref, og_ref, mnw_ref,
                   o_ref, s_ref, c_ref, n_ref, m_ref):
    n_a = DN_HEADS * HEAD_LANES
    n_b = ML_HEADS * HEAD_LANES
    stages = [_deltanet_steps(dq_ref, dk_ref, dv_ref, gc_ref, gr_ref, z_ref, dnw_ref, o_ref.at[:, :, 0:n_a], s_ref),
              _mlstm_steps(mq_ref, mk_ref, mv_ref, gc_ref, gr_ref, og_ref, mnw_ref, o_ref.at[:, :, n_a:n_a + n_b],
                           c_ref, n_ref, m_ref)]
    while stages:
        for stage in list(stages):
            try:
                next(stage)
            except StopIteration:
                stages.remove(stage)


def _mixers(conv_out, rest, gcol, grow3, dn_nw, ml_nw, batch, seq, rows):
    t = conv_out.shape[0]
    nj = seq // rows
    cpb = rows // CHUNK
    width = 4 * HEAD_LANES
    assert DN_HEADS * HEAD_LANES == width and ML_HEADS * HEAD_LANES == width
    n_seq = 2 if batch % 2 == 0 else 1
    conv3 = conv_out.reshape(batch, seq, -1)
    rest3 = rest.reshape(batch, seq, -1)
    gcol3 = gcol.reshape(batch, seq, -1)
    grow4 = grow3.reshape(batch, seq // CHUNK, 16, CHUNK)

    def conv_block(c0):
        return pl.BlockSpec((n_seq, rows, width), lambda b, j: (b, j, c0 // width))

    def rest_block(c0):
        return pl.BlockSpec((n_seq, rows, width), lambda b, j: (b, j, (c0 - N_CONV) // width))

    gates = [pl.BlockSpec((n_seq, rows, HEAD_LANES), lambda b, j: (b, j, 0)),
             pl.BlockSpec((n_seq, cpb, 16, CHUNK), lambda b, j: (b, j, 0, 0))]
    out = pl.pallas_call(
        _mixers_kernel,
        grid=(batch // n_seq, nj),
        in_specs=[conv_block(C_DNQ), conv_block(C_DNK), conv_block(C_DNV)] + gates
                 + [rest_block(C_DNZ), pl.BlockSpec((1, HEAD_LANES), lambda b, j: (0, 0)),
                    conv_block(C_MLQ), conv_block(C_MLK), rest_block(C_MLV), rest_block(C_MLO),
                    pl.BlockSpec((1, width), lambda b, j: (0, 0))],
        out_specs=pl.BlockSpec((n_seq, rows, 2 * width), lambda b, j: (b, j, 0)),
        out_shape=jax.ShapeDtypeStruct((batch, seq, 2 * width), F32),
        scratch_shapes=[pltpu.VMEM((n_seq * DN_HEADS, DN_DK, DN_DV), F32),
                        pltpu.VMEM((n_seq * ML_HEADS, HEAD_LANES, ML_DV), F32),
                        pltpu.VMEM((n_seq * ML_HEADS, 8, HEAD_LANES), F32),
                        pltpu.VMEM((n_seq * ML_HEADS, 8, HEAD_LANES), F32)],
        compiler_params=pltpu.CompilerParams(dimension_semantics=("arbitrary", "arbitrary"),
                                             vmem_limit_bytes=V7X_VMEM_LIMIT),
        name="mixers",
    )(conv3, conv3, conv3, gcol3, grow4, rest3, dn_nw, conv3, conv3, rest3, rest3, ml_nw)
    return out.reshape(t, 2 * width)


def _route_kernel(y_ref, wo_ref, x_ref, g1_ref, sh_ref, sc_ref, nw_ref, wrt_ref, br_ref,
                  x1_ref, h2_ref, pos_ref, wrow_ref, len_ref):
    tm = x_ref.shape[0]

    mix = _dot(y_ref[...].astype(BF16), wo_ref[...])
    x1 = x_ref[...] + g1_ref[...] * mix
    x1_ref[...] = x1
    ms = jnp.mean(x1 * x1, axis=-1, keepdims=True)
    h2 = x1 * lax.rsqrt(ms + EPS) * nw_ref[...]
    h2 = h2 * (1.0 + sc_ref[...]) + sh_ref[...]
    h2_ref[...] = h2.astype(BF16)

    hh, hm, _ = _split3(h2)
    wh, wm, _ = _split3(wrt_ref[...])
    logits = _dot_nt(wh, hh) + (_dot_nt(wh, hm) + _dot_nt(wm, hh)) + br_ref[:, 0:1]

    e_i = lax.broadcasted_iota(jnp.int32, (N_EXPERTS, tm), 0)
    work = logits
    tops = []
    sels = []
    hots = []
    for _ in range(TOP_K):
        m = jnp.max(work, axis=0, keepdims=True)
        sel = jnp.min(jnp.where(work == m, e_i, N_EXPERTS), axis=0, keepdims=True)
        hot = e_i == sel
        work = jnp.where(hot, NEG_BIG, work)
        tops.append(m)
        sels.append(sel)
        hots.append(hot)
    exps = [jnp.exp(tl - tops[0]) for tl in tops]
    denom = exps[0] + exps[1] + exps[2] + exps[3]
    ws = [e / denom for e in exps]

    chosen = jnp.zeros((N_EXPERTS, tm), F32)
    for hot in hots:
        chosen = chosen + jnp.where(hot, 1.0, 0.0)

    r_i = lax.broadcasted_iota(jnp.int32, (tm, tm), 0)
    c_i = lax.broadcasted_iota(jnp.int32, (tm, tm), 1)
    strict_upper = jnp.where(r_i < c_i, 1.0, 0.0).astype(BF16)
    prefix = _dot(chosen.astype(BF16), strict_upper)
    n_e = jnp.sum(chosen, axis=1, keepdims=True)
    len8 = jnp.ceil(n_e * 0.125) * 8.0
    er = lax.broadcasted_iota(jnp.int32, (N_EXPERTS, N_EXPERTS), 0)
    ec = lax.broadcasted_iota(jnp.int32, (N_EXPERTS, N_EXPERTS), 1)
    strict_lower = jnp.where(ec < er, 1.0, 0.0).astype(BF16)
    len8b = jnp.broadcast_to(len8, (N_EXPERTS, HEAD_LANES))
    off8 = _dot_exact_right(strict_lower, len8b)[:, 0:1]
    len_ref[...] = len8b.astype(jnp.int32)

    for kk in range(TOP_K):
        wrow_ref[kk:kk + 1, :] = ws[kk]
        pos = jnp.sum(jnp.where(hots[kk], prefix + off8, 0.0), axis=0, keepdims=True)
        pos_ref[kk:kk + 1, :] = pos.astype(jnp.int32)


def _route(ymix, wo, x2, g1, sh2, sc2, nw, wrt, br, seq, tm):
    t, d = x2.shape
    tps = seq // tm
    bmap = lambda i: (i // tps, 0, 0)
    return pl.pallas_call(
        _route_kernel,
        grid=(t // tm,),
        in_specs=[pl.BlockSpec((tm, ymix.shape[1]), lambda i: (i, 0)),
                  pl.BlockSpec(wo.shape, lambda i: (0, 0)),
                  pl.BlockSpec((tm, d), lambda i: (i, 0)),
                  pl.BlockSpec((None, 1, d), bmap),
                  pl.BlockSpec((None, 1, d), bmap),
                  pl.BlockSpec((None, 1, d), bmap),
                  pl.BlockSpec((1, d), lambda i: (0, 0)),
                  pl.BlockSpec((N_EXPERTS, d), lambda i: (0, 0)),
                  pl.BlockSpec((N_EXPERTS, HEAD_LANES), lambda i: (0, 0))],
        out_specs=[pl.BlockSpec((tm, d), lambda i: (i, 0)),
                   pl.BlockSpec((tm, d), lambda i: (i, 0)),
                   pl.BlockSpec((TOP_K, tm), lambda i: (0, i)),
                   pl.BlockSpec((TOP_K, tm), lambda i: (0, i)),
                   pl.BlockSpec((N_EXPERTS, HEAD_LANES), lambda i: (i, 0))],
        out_shape=[jax.ShapeDtypeStruct((t, d), F32),
                   jax.ShapeDtypeStruct((t, d), BF16),
                   jax.ShapeDtypeStruct((TOP_K, t), jnp.int32),
                   jax.ShapeDtypeStruct((TOP_K, t), F32),
                   jax.ShapeDtypeStruct((t // tm * N_EXPERTS, HEAD_LANES), jnp.int32)],
        compiler_params=pltpu.CompilerParams(dimension_semantics=("arbitrary",),
                                             vmem_limit_bytes=V7X_VMEM_LIMIT),
        name="route",
    )(ymix, wo, x2, g1, sh2, sc2, nw, wrt, br)


TAIL_START, TAIL_LEN, PAD_END, BLOCK_START, BLOCK_COUNT = range(5)


def _slots_kernel(bm, len_te_ref, ssrc_ref, sdst_ref, tab_ref):
    nt, lanes = len_te_ref.shape
    len_te = len_te_ref[...].astype(F32)

    r_l = lax.broadcasted_iota(jnp.int32, (lanes, lanes), 0)
    c_l = lax.broadcasted_iota(jnp.int32, (lanes, lanes), 1)
    upper_incl = jnp.where(r_l <= c_l, 1.0, 0.0).astype(BF16)
    upper_strict = jnp.where(r_l < c_l, 1.0, 0.0).astype(BF16)
    total_r = jnp.sum(len_te, axis=0, keepdims=True)
    padded_r = jnp.ceil(total_r * (1.0 / bm)) * bm
    pad_end_r = _dot_exact_left(jnp.broadcast_to(padded_r, (8, lanes)), upper_incl)[0:1, :]
    pad_start_r = pad_end_r - padded_r
    r_t = lax.broadcasted_iota(jnp.int32, (nt, nt), 0)
    c_t = lax.broadcasted_iota(jnp.int32, (nt, nt), 1)
    lower_strict_t = jnp.where(c_t < r_t, 1.0, 0.0).astype(BF16)
    before = _dot_exact_right(lower_strict_t, len_te)
    ssrc_ref[...] = _dot_exact_left(len_te, upper_strict).astype(jnp.int32)
    sdst_ref[...] = (pad_start_r + before).astype(jnp.int32)
    tab_ref[TAIL_START:TAIL_START + 1, :] = (pad_start_r + total_r).astype(jnp.int32)
    tab_ref[TAIL_LEN:TAIL_LEN + 1, :] = (padded_r - total_r).astype(jnp.int32)
    tab_ref[PAD_END:PAD_END + 1, :] = pad_end_r.astype(jnp.int32)
    tab_ref[BLOCK_START:BLOCK_START + 1, :] = (pad_start_r * (1.0 / bm)).astype(jnp.int32)
    tab_ref[BLOCK_COUNT:BLOCK_COUNT + 1, :] = (padded_r * (1.0 / bm)).astype(jnp.int32)
    tab_ref[5:8, :] = jnp.zeros((3, lanes), jnp.int32)


def _slots(len_te, bm):
    assert bm & (bm - 1) == 0, "block rows must be a power of two"
    nt, lanes = len_te.shape
    return pl.pallas_call(
        functools.partial(_slots_kernel, bm),
        out_shape=[jax.ShapeDtypeStruct((nt, lanes), jnp.int32),
                   jax.ShapeDtypeStruct((nt, lanes), jnp.int32),
                   jax.ShapeDtypeStruct((8, lanes), jnp.int32)],
        compiler_params=pltpu.CompilerParams(vmem_limit_bytes=V7X_VMEM_LIMIT),
        name="slots",
    )(len_te)


def _segment_pieces(max_rows):
    sizes = []
    s = 8
    while s <= max_rows:
        sizes.append(s)
        s *= 2
    return sizes[::-1]


def _segment_dma(src_ref, src0, dst_ref, dst0, nrows, sizes, sem, start):
    def pieces(group, off):
        for sz in group:
            bit = nrows & sz

            @pl.when(bit != 0)
            def _(off=off, sz=sz):
                s0 = pl.multiple_of(src0 + off, 8)
                d0 = pl.multiple_of(dst0 + off, 8)
                cp = pltpu.make_async_copy(src_ref.at[pl.ds(s0, sz)], dst_ref.at[pl.ds(d0, sz)], sem)
                if start:
                    cp.start()
                else:
                    cp.wait()

            off = off + bit

    large = [sz for sz in sizes if sz >= COMMON_SEGMENT_ROWS]
    small = [sz for sz in sizes if sz < COMMON_SEGMENT_ROWS]
    if large and small:
        @pl.when(nrows >= COMMON_SEGMENT_ROWS)
        def _():
            pieces(large, 0)

        pieces(small, nrows & ~(COMMON_SEGMENT_ROWS - 1))
    else:
        pieces(sizes, 0)


def _sorted_onehot(pos_ref, row0, nrows, tt):
    r_i = lax.broadcasted_iota(jnp.int32, (nrows, tt), 0) + row0
    hit = r_i == pos_ref[0:1, :]
    for kk in range(1, TOP_K):
        hit = hit | (r_i == pos_ref[kk:kk + 1, :])
    return hit


def _row_groups(nrows, n_groups):
    size = -(-nrows // n_groups // 8) * 8
    return [(r0, min(size, nrows - r0)) for r0 in range(0, nrows, size)]


def _dispatch_kernel(bm, len_ref, ssrc_ref, sdst_ref, tail_ref, pos_ref, h2_ref, xs_hbm, buf, sem):
    i = pl.program_id(0)
    nt = pl.num_programs(0)
    tt = h2_ref.shape[0]
    nrows = buf.shape[1]
    sizes = _segment_pieces(tt)
    slot = i % 2

    h2 = h2_ref[...]
    for r0, rn in _row_groups(nrows, 4):
        perm = jnp.where(_sorted_onehot(pos_ref, r0, rn, tt), 1.0, 0.0).astype(BF16)
        buf[slot, r0:r0 + rn, :] = _dot(perm, h2)

    def seg_start(tile, which):
        def body(e, carry):
            _segment_dma(buf.at[which], ssrc_ref[tile, e], xs_hbm, sdst_ref[tile, e], len_ref[tile, e], sizes,
                         sem.at[which], True)
            return carry
        lax.fori_loop(0, N_EXPERTS, body, 0)

    def seg_wait(tile, which):
        total = ssrc_ref[tile, N_EXPERTS - 1] + len_ref[tile, N_EXPERTS - 1]
        _segment_dma(buf.at[which], 0, xs_hbm, 0, total, _segment_pieces(nrows), sem.at[which], False)

    @pl.when(i > 0)
    def _():
        seg_wait(i - 1, 1 - slot)

    seg_start(i, slot)

    @pl.when(i == nt - 1)
    def _():
        seg_wait(i, slot)
        zrows = bm
        zbuf = buf.at[0]
        zsem = sem.at[0]
        buf[0, 0:zrows, :] = jnp.zeros((zrows, buf.shape[2]), F32)

        def tail(start):
            def body(e, carry):
                _segment_dma(zbuf, 0, xs_hbm, tail_ref[TAIL_START, e], tail_ref[TAIL_LEN, e],
                             _segment_pieces(zrows // 2), zsem, start)
                return carry
            lax.fori_loop(0, N_EXPERTS, body, 0)

        tail(True)
        tail(False)

        used = tail_ref[PAD_END, tail_ref.shape[1] - 1]
        n_unused = (xs_hbm.shape[0] - used) // zrows

        def unused_copy(j):
            d0 = pl.multiple_of(used + j * zrows, 8)
            return pltpu.make_async_copy(zbuf.at[pl.ds(0, zrows)], xs_hbm.at[pl.ds(d0, zrows)], zsem)

        def unused_start(j, carry):
            unused_copy(j).start()
            return carry

        def unused_wait(j, carry):
            unused_copy(j).wait()
            return carry

        lax.fori_loop(0, n_unused, unused_start, 0)
        lax.fori_loop(0, n_unused, unused_wait, 0)


def _dispatch(seg_len, seg_src, seg_dst, tail, pos, h2, n_slots, tt, bm):
    t, d = h2.shape
    nrows = TOP_K * tt + 8 * N_EXPERTS
    assert bm <= nrows
    smem = pl.BlockSpec(memory_space=pltpu.SMEM)
    return pl.pallas_call(
        functools.partial(_dispatch_kernel, bm),
        grid=(t // tt,),
        in_specs=[smem, smem, smem, smem,
                  pl.BlockSpec((TOP_K, tt), lambda i: (0, i)),
                  pl.BlockSpec((tt, d), lambda i: (i, 0))],
        out_specs=pl.BlockSpec(memory_space=pl.ANY),
        out_shape=jax.ShapeDtypeStruct((n_slots, d), F32),
        scratch_shapes=[pltpu.VMEM((2, nrows, d), F32), pltpu.SemaphoreType.DMA((2,))],
        compiler_params=pltpu.CompilerParams(dimension_semantics=("arbitrary",), has_side_effects=True,
                                             vmem_limit_bytes=V7X_VMEM_LIMIT),
        name="dispatch",
    )(seg_len, seg_src, seg_dst, tail, pos, h2)


def _experts_kernel(tab_ref, xs_hbm, wgu_hbm, bgu_ref, wd_hbm, bd_ref, y_hbm, wgu_s, wd_s, wgu_f, wd_f, xbuf, ybuf,
                    sem_in, sem_out, sem_w):
    e = pl.program_id(0)
    n_e = pl.num_programs(0)
    bm = xbuf.shape[1] // 2
    d_ff = wd_f.shape[1]
    ct = wgu_s.shape[2]
    wslot = e % 2

    def w_copies(expert, which):
        return (pltpu.make_async_copy(wgu_hbm.at[expert], wgu_f.at[which], sem_w.at[which]),
                pltpu.make_async_copy(wd_hbm.at[expert], wd_f.at[which], sem_w.at[which]))

    @pl.when(e == 0)
    def _():
        for cp in w_copies(0, 0):
            cp.start()

    for cp in w_copies(e, wslot):
        cp.wait()

    have_next = e + 1 < n_e

    def prefetch_next_weights(cond):
        @pl.when(have_next & cond)
        def _():
            for cp in w_copies(e + 1, 1 - wslot):
                cp.start()
    first = tab_ref[BLOCK_START, e]
    count = tab_ref[BLOCK_COUNT, e]
    npair = count // 2
    odd = count % 2
    tslot = npair % 2

    def pair_rows(j):
        return pl.ds(pl.multiple_of((first + 2 * j) * bm, bm), 2 * bm)

    def x_copy(j, slot):
        return pltpu.make_async_copy(xs_hbm.at[pair_rows(j)], xbuf.at[slot], sem_in.at[slot])

    def y_copy(j, slot):
        return pltpu.make_async_copy(ybuf.at[slot], y_hbm.at[pair_rows(j)], sem_out.at[slot])

    def last_rows():
        return pl.ds(pl.multiple_of((first + 2 * npair) * bm, bm), bm)

    def x_last(slot):
        return pltpu.make_async_copy(xs_hbm.at[last_rows()], xbuf.at[slot, pl.ds(0, bm)], sem_in.at[slot])

    def y_last(slot):
        return pltpu.make_async_copy(ybuf.at[slot, pl.ds(0, bm)], y_hbm.at[last_rows()], sem_out.at[slot])

    @pl.when(npair > 0)
    def _():
        x_copy(0, 0).start()

    @pl.when((npair == 0) & (odd == 1))
    def _():
        x_last(0).start()

    for t in range(wgu_s.shape[0]):
        wgu_s[t] = wgu_f[wslot, :, t * ct:(t + 1) * ct].astype(BF16)
    for t in range(wd_s.shape[0]):
        wd_s[t] = wd_f[wslot, :, t * ct:(t + 1) * ct].astype(BF16)

    def mlp(x):
        xb = x.astype(BF16)
        tpc = 2
        fc = tpc * ct
        nf = d_ff // fc
        n_out = wd_s.shape[0]

        def gate_up(f):
            gates, ups = [], []
            for t in range(f * tpc, (f + 1) * tpc):
                gates.append(_dot(xb, wgu_s[t]) + bgu_ref[:, t * ct:(t + 1) * ct])
                u = d_ff // ct + t
                ups.append(_dot(xb, wgu_s[u]) + bgu_ref[:, u * ct:(u + 1) * ct])
            return jnp.concatenate(gates, axis=-1), jnp.concatenate(ups, axis=-1)

        acts = []
        pre = gate_up(0)
        for f in range(nf):
            gate = jnp.minimum(pre[0], SWIGLU_LIMIT)
            up = jnp.clip(pre[1], -SWIGLU_LIMIT, SWIGLU_LIMIT)
            if f + 1 < nf:
                pre = gate_up(f + 1)
            acts.append(((up + 1.0) * gate * _sigmoid(SWIGLU_ALPHA * gate)).astype(BF16))
        act = jnp.concatenate(acts, axis=-1)
        return [_dot(act, wd_s[n]) + bd_ref[:, n * ct:(n + 1) * ct] for n in range(n_out)]

    def pair(j, carry):
        slot = j % 2
        x_copy(j, slot).wait()

        @pl.when(j + 1 < npair)
        def _():
            x_copy(j + 1, 1 - slot).start()

        @pl.when((j + 1 == npair) & (odd == 1))
        def _():
            x_last(1 - slot).start()

        @pl.when(j >= 2)
        def _():
            y_copy(j - 2, slot).wait()

        prefetch_next_weights(j == 0)
        for n, piece in enumerate(mlp(xbuf[slot])):
            ybuf[slot, :, n * ct:(n + 1) * ct] = piece
        y_copy(j, slot).start()
        return carry

    lax.fori_loop(0, npair, pair, 0)

    prefetch_next_weights(npair == 0)

    @pl.when(odd == 1)
    def _():
        x_last(tslot).wait()

        @pl.when(npair >= 2)
        def _():
            y_copy(npair - 2, tslot).wait()

        for n, piece in enumerate(mlp(xbuf[tslot, 0:bm, :])):
            ybuf[tslot, 0:bm, n * ct:(n + 1) * ct] = piece
        y_last(tslot).start()

    @pl.when((odd == 0) & (npair >= 2))
    def _():
        y_copy(npair - 2, tslot).wait()

    @pl.when(npair >= 1)
    def _():
        y_copy(npair - 1, 1 - tslot).wait()

    @pl.when(odd == 1)
    def _():
        y_last(tslot).wait()

    @pl.when(e == pl.num_programs(0) - 1)
    def _():
        used = first + count
        n_unused = y_hbm.shape[0] // bm - used
        ybuf[0, 0:bm, :] = jnp.zeros((bm, ybuf.shape[2]), F32)

        def z_copy(j):
            d0 = pl.multiple_of((used + j) * bm, bm)
            return pltpu.make_async_copy(ybuf.at[0, pl.ds(0, bm)], y_hbm.at[pl.ds(d0, bm)], sem_out.at[0])

        def z_start(j, carry):
            z_copy(j).start()
            return carry

        def z_wait(j, carry):
            z_copy(j).wait()
            return carry

        lax.fori_loop(0, n_unused, z_start, 0)
        lax.fori_loop(0, n_unused, z_wait, 0)


def _experts(table, xs, wgu, bgu, wd, bd, bm):
    ns, d = xs.shape
    n_e, _, two_ff = wgu.shape
    d_ff = two_ff // 2
    wmap = lambda e: (e, 0, 0)
    return pl.pallas_call(
        _experts_kernel,
        grid=(n_e,),
        in_specs=[pl.BlockSpec(memory_space=pltpu.SMEM),
                  pl.BlockSpec(memory_space=pl.ANY),
                  pl.BlockSpec(memory_space=pl.ANY),
                  pl.BlockSpec((None, 1, two_ff), wmap),
                  pl.BlockSpec(memory_space=pl.ANY),
                  pl.BlockSpec((None, 1, d), wmap)],
        out_specs=pl.BlockSpec(memory_space=pl.ANY),
        out_shape=jax.ShapeDtypeStruct((ns, d), F32),
        scratch_shapes=[pltpu.VMEM((two_ff // MXU_COLS, d, MXU_COLS), BF16),
                        pltpu.VMEM((d // MXU_COLS, d_ff, MXU_COLS), BF16),
                        pltpu.VMEM((2, d, two_ff), F32), pltpu.VMEM((2, d_ff, d), F32),
                        pltpu.VMEM((2, 2 * bm, d), F32), pltpu.VMEM((2, 2 * bm, d), F32),
                        pltpu.SemaphoreType.DMA((2,)), pltpu.SemaphoreType.DMA((2,)),
                        pltpu.SemaphoreType.DMA((2,))],
        compiler_params=pltpu.CompilerParams(dimension_semantics=("arbitrary",), has_side_effects=True,
                                             vmem_limit_bytes=V7X_VMEM_LIMIT),
        name="experts",
    )(table, xs, wgu, bgu, wd, bd)


def _combine_kernel(len_ref, ssrc_ref, sdst_ref, pos_ref, w_ref, y_hbm, x1_ref, g2_ref, nw_ref, sh_ref, sc_ref,
                    o_ref, ybuf, sem):
    i = pl.program_id(0)
    nt = pl.num_programs(0)
    tt = x1_ref.shape[0]
    nrows = ybuf.shape[1]
    sizes = _segment_pieces(tt)
    slot = i % 2

    def seg_start(tile, which):
        def body(e, carry):
            _segment_dma(y_hbm, sdst_ref[tile, e], ybuf.at[which], ssrc_ref[tile, e], len_ref[tile, e], sizes,
                         sem.at[which], True)
            return carry
        lax.fori_loop(0, N_EXPERTS, body, 0)

    def seg_wait(tile, which):
        total = ssrc_ref[tile, N_EXPERTS - 1] + len_ref[tile, N_EXPERTS - 1]
        _segment_dma(y_hbm, 0, ybuf.at[which], 0, total, _segment_pieces(nrows), sem.at[which], False)

    @pl.when(i == 0)
    def _():
        ybuf[...] = jnp.zeros(ybuf.shape, F32)
        seg_start(0, 0)

    @pl.when(i + 1 < nt)
    def _():
        seg_start(i + 1, 1 - slot)

    seg_wait(i, slot)

    acc = None
    for r0, rn in _row_groups(nrows, 3):
        r_i = lax.broadcasted_iota(jnp.int32, (rn, tt), 0) + r0
        wmat = 0.0
        for kk in range(TOP_K):
            wmat = jnp.where(r_i == pos_ref[kk:kk + 1, :], w_ref[kk:kk + 1, :], wmat)
        part = _dot_tn(wmat.astype(BF16), ybuf[slot, r0:r0 + rn, :].astype(BF16))
        acc = part if acc is None else acc + part
    xo = x1_ref[...] + g2_ref[...] * acc
    ms = jnp.mean(xo * xo, axis=-1, keepdims=True)
    hn = xo * lax.rsqrt(ms + EPS) * nw_ref[...]
    o_ref[...] = hn * (1.0 + sc_ref[...]) + sh_ref[...]


def _combine(seg_len, seg_src, seg_dst, pos, wrow, y, x1, g2, nw, shf, scf, seq, tt):
    t, d = x1.shape
    tps = seq // tt
    bmap = lambda i: (i // tps, 0, 0)
    nrows = TOP_K * tt + 8 * N_EXPERTS
    smem = pl.BlockSpec(memory_space=pltpu.SMEM)
    return pl.pallas_call(
        _combine_kernel,
        grid=(t // tt,),
        in_specs=[smem, smem, smem,
                  pl.BlockSpec((TOP_K, tt), lambda i: (0, i)),
                  pl.BlockSpec((TOP_K, tt), lambda i: (0, i)),
                  pl.BlockSpec(memory_space=pl.ANY),
                  pl.BlockSpec((tt, d), lambda i: (i, 0)),
                  pl.BlockSpec((None, 1, d), bmap),
                  pl.BlockSpec((1, d), lambda i: (0, 0)),
                  pl.BlockSpec((None, 1, d), bmap),
                  pl.BlockSpec((None, 1, d), bmap)],
        out_specs=pl.BlockSpec((tt, d), lambda i: (i, 0)),
        out_shape=jax.ShapeDtypeStruct((t, d), F32),
        scratch_shapes=[pltpu.VMEM((2, nrows, d), F32), pltpu.SemaphoreType.DMA((2,))],
        compiler_params=pltpu.CompilerParams(dimension_semantics=("arbitrary",),
                                             vmem_limit_bytes=V7X_VMEM_LIMIT),
        name="combine",
    )(seg_len, seg_src, seg_dst, pos, wrow, y, x1, g2, nw, shf, scf)


def _pad_heads(w, heads, dk):
    r = w.shape[0]
    w3 = w.reshape(r, heads, dk)
    return jnp.pad(w3, ((0, 0), (0, 0), (0, HEAD_LANES - dk))).reshape(r, heads * HEAD_LANES)


def _pick_tile(n, pref):
    tile = pref
    while n % tile:
        tile //= 2
    return tile


def kernel(x, c, w_ada, b_ada, norm_mix, w_in, dn_conv, dn_a_log, dn_dt_bias, dn_norm, ml_conv, ml_i_bias,
           ml_f_bias, ml_norm, w_out, norm_ffn, w_router, b_router, w_gate_up, b_gate_up, w_down, b_down,
           w_ada_final, b_ada_final, norm_final):
    batch, seq, d = x.shape
    assert w_ada.shape[0] == 1, "single-layer block"
    assert seq % CHUNK == 0
    t = batch * seq
    x2 = x.reshape(t, d)

    c_pad = jnp.pad(c, ((0, 8 - batch % 8 if batch % 8 else 0), (0, 0)))
    mod = _mods(c_pad, w_ada.reshape(d, 6 * d), b_ada.reshape(1, 6 * d))[:batch]
    modf = _mods(c_pad, w_ada_final, b_ada_final.reshape(1, 2 * d))[:batch]
    sh1, sc1, g1, sh2, sc2, g2 = [mod[:, None, j * d:(j + 1) * d] for j in range(6)]
    shf, scf = modf[:, None, 0:d], modf[:, None, d:2 * d]

    wi = w_in.reshape(d, -1)
    o_z = 1536
    o_b = 2048
    o_mq = 2056
    o_mk = o_mq + ML_HEADS * ML_DK
    o_mv = o_mk + ML_HEADS * ML_DK
    o_mo = o_mv + ML_HEADS * ML_DV
    o_mi = o_mo + ML_HEADS * ML_DV
    gates = jnp.concatenate([wi[:, o_b:o_mq], wi[:, o_mi:o_mi + 2 * ML_HEADS]], axis=1)
    w_new = jnp.concatenate([
        wi[:, 0:o_z],
        _pad_heads(wi[:, o_mq:o_mk], ML_HEADS, ML_DK),
        _pad_heads(wi[:, o_mk:o_mv], ML_HEADS, ML_DK),
        wi[:, o_z:o_b],
        wi[:, o_mv:o_mo],
        wi[:, o_mo:o_mi],
        jnp.pad(gates, ((0, 0), (0, HEAD_LANES - 16))),
    ], axis=1).astype(BF16)
    wgt = gates.T.astype(BF16)
    mlc = ml_conv.reshape(CONV_W, -1)
    cw = jnp.concatenate([dn_conv.reshape(CONV_W, -1),
                          _pad_heads(mlc[:, 0:ML_HEADS * ML_DK], ML_HEADS, ML_DK),
                          _pad_heads(mlc[:, ML_HEADS * ML_DK:], ML_HEADS, ML_DK)], axis=1)
    zeros4 = jnp.zeros((4,), F32)
    bias16 = jnp.concatenate([zeros4, dn_dt_bias.reshape(4), ml_i_bias.reshape(4), ml_f_bias.reshape(4)])
    alog16 = jnp.concatenate([zeros4, dn_a_log.reshape(4), zeros4, zeros4])
    gpc = jnp.zeros((8, HEAD_LANES), F32).at[0, 0:16].set(bias16).at[1, 0:16].set(alog16)
    gpr = jnp.zeros((16, HEAD_LANES), F32).at[:, 0].set(bias16).at[:, 1].set(alog16)

    tm_in = _pick_tile(seq, 512)
    conv_out, rest, gcol, grow = _inproj(x2, sh1, sc1, norm_mix.reshape(1, d), w_new, wgt, cw, gpc, gpr, seq, tm_in)
    grow3 = grow.reshape(16, t // CHUNK, CHUNK).transpose(1, 0, 2)

    rows = _pick_tile(seq, 256)
    ymix = _mixers(conv_out, rest, gcol, grow3, dn_norm.reshape(1, DN_DV), ml_norm.reshape(1, ML_HEADS * ML_DV),
                   batch, seq, rows)

    wo = w_out.reshape(-1, d).astype(BF16)
    tm_r = _pick_tile(seq, 512)
    brp = jnp.broadcast_to(b_router.reshape(N_EXPERTS, 1), (N_EXPERTS, HEAD_LANES))
    x1, h2, pos, wrow, len_col = _route(
        ymix, wo, x2, g1, sh2, sc2, norm_ffn.reshape(1, d),
        w_router.reshape(d, N_EXPERTS).T, brp, seq, tm_r)

    n_e = N_EXPERTS
    nt = t // tm_r
    bm = 256
    len_te =len_col.reshape(nt, n_e, HEAD_LANES)[:, :, 0]
    seg_len = jnp.pad(len_te, ((0, 0), (0, HEAD_LANES - n_e)))
    n_slots_max = t * TOP_K + n_e * (7 * nt + bm)
    nb = (n_slots_max + bm - 1) // bm
    seg_src, seg_dst, table = _slots(seg_len, bm)

    xs = _dispatch(seg_len, seg_src, seg_dst, table, pos, h2, nb * bm, tm_r, bm)
    y = _experts(table, xs, w_gate_up.reshape(n_e, d, -1), b_gate_up.reshape(n_e, 1, -1),
                 w_down.reshape(n_e, -1, d), b_down.reshape(n_e, 1, d), bm)
    out = _combine(seg_len, seg_src, seg_dst, pos, wrow, y, x1, g2, norm_final.reshape(1, d), shf, scf, seq, tm_r)
    return out.reshape(batch, seq, d)
```

```python
import functools

import jax
import jax.numpy as jnp
from jax import lax
from jax.experimental import pallas as pl
from jax.experimental.pallas import tpu as pltpu

F32 = jnp.float32
BF16 = jnp.bfloat16

CHUNK = 64
CONV_W = 4
EPS = 1e-6

DN_HEADS = 4
DN_DK = 128
DN_DV = 128
ML_HEADS = 4
ML_DK = 64
ML_DV = 128
HEAD_LANES = 128

N_EXPERTS = 32
TOP_K = 4
SWIGLU_LIMIT = 7.0
SWIGLU_ALPHA = 1.702

C_DNQ = 0
C_DNK = 512
C_DNV = 1024
C_MLQ = 1536
C_MLK = 2048
N_CONV = 2560
C_DNZ = 2560
C_MLV = 3072
C_MLO = 3584
C_GATE = 4096
N_PROJ = 4224
N_REST = C_GATE - N_CONV

V7X_VMEM_LIMIT = 56 * 1024 * 1024
MXU_COLS = 256
COMMON_SEGMENT_ROWS = 128
INPROJ_GROUP_COLS = 512

NEG_BIG = -1e30


def _sigmoid(x):
    return 1.0 / (1.0 + jnp.exp(-x))


def _softplus(x):
    return jnp.maximum(x, 0.0) + jnp.log(1.0 + jnp.exp(-jnp.abs(x)))


def _split3(v):
    hi = v.astype(BF16)
    r1 = v - hi.astype(F32)
    mid = r1.astype(BF16)
    lo = (r1 - mid.astype(F32)).astype(BF16)
    return hi, mid, lo


def _dot(a, b):
    return jnp.dot(a, b, preferred_element_type=F32)


def _dot_nt(a, b):
    return lax.dot_general(a, b, (((1,), (1,)), ((), ())), preferred_element_type=F32)


def _dot_tn(a, b):
    return lax.dot_general(a, b, (((0,), (0,)), ((), ())), preferred_element_type=F32)


def _dot_exact_right(sel_bf16, v):
    hi, mid, lo = _split3(v)
    return _dot(sel_bf16, hi) + _dot(sel_bf16, mid) + _dot(sel_bf16, lo)


def _dot_exact_left(v, sel_bf16):
    hi, mid, lo = _split3(v)
    return _dot(hi, sel_bf16) + _dot(mid, sel_bf16) + _dot(lo, sel_bf16)


def _mods_kernel(c_ref, w_ref, b_ref, o_ref):
    c = c_ref[...]
    cond = c * _sigmoid(c)
    ch, cm, _ = _split3(cond)
    wh, wm, _ = _split3(w_ref[...])
    acc = _dot(ch, wh) + (_dot(ch, wm) + _dot(cm, wh))
    o_ref[...] = acc + b_ref[...]


def _mods(c_pad, w, b):
    m, d = c_pad.shape
    n = w.shape[1]
    tn = 1024
    return pl.pallas_call(
        _mods_kernel,
        grid=(n // tn,),
        in_specs=[pl.BlockSpec((m, d), lambda j: (0, 0)),
                  pl.BlockSpec((d, tn), lambda j: (0, j)),
                  pl.BlockSpec((1, tn), lambda j: (0, j))],
        out_specs=pl.BlockSpec((m, tn), lambda j: (0, j)),
        out_shape=jax.ShapeDtypeStruct((m, n), F32),
        compiler_params=pltpu.CompilerParams(dimension_semantics=("arbitrary",),
                                             vmem_limit_bytes=V7X_VMEM_LIMIT),
        name="mods",
    )(c_pad, w, b)


def _gate_transform(v, bias, alog, cls):
    vb = v + bias
    beta = _sigmoid(v)
    g = -jnp.exp(alog) * _softplus(vb)
    logf = -_softplus(-vb)
    return jnp.where(cls == 0, beta, jnp.where(cls == 1, g, jnp.where(cls == 2, vb, jnp.where(cls == 3, logf, 0.0))))


def _inproj_kernel(tiles_per_seq, x_ref, sh_ref, sc_ref, nw_ref, w_ref, wgt_ref, cw_ref, gpc_ref, gpr_ref,
                   conv_ref, rest_ref, gcol_ref, grow_ref, cbuf):
    tm = x_ref.shape[0]
    i = pl.program_id(0)
    x = x_ref[...]
    ms = jnp.mean(x * x, axis=-1, keepdims=True)
    h = x * lax.rsqrt(ms + EPS) * nw_ref[...]
    h = h * (1.0 + sc_ref[...]) + sh_ref[...]
    hb = h.astype(BF16)

    @pl.when(i % tiles_per_seq == 0)
    def _():
        cbuf[0:8, :] = jnp.zeros((8, N_CONV), F32)

    group = INPROJ_GROUP_COLS
    for lo in range(0, N_CONV, group):
        cols = slice(lo, lo + group)
        pc = _dot(hb, w_ref[:, cols])
        cbuf[8:tm + 8, cols] = pc
        acc = cw_ref[CONV_W - 1:CONV_W, cols] * pc
        for j in range(CONV_W - 1):
            acc = acc + cw_ref[j:j + 1, cols] * cbuf[8 - (CONV_W - 1) + j:8 - (CONV_W - 1) + j + tm, cols]
        cbuf[0:8, cols] = cbuf[tm:tm + 8, cols]
        y = acc * _sigmoid(acc)
        if lo < C_DNV:
            for h0 in range(0, group, HEAD_LANES):
                uh = y[:, h0:h0 + HEAD_LANES]
                un = uh * lax.rsqrt(jnp.sum(uh * uh, axis=-1, keepdims=True) + EPS)
                conv_ref[:, lo + h0:lo + h0 + HEAD_LANES] = un * (DN_DK ** -0.5) if lo < C_DNK else un
        elif C_MLQ <= lo < C_MLK:
            conv_ref[:, cols] = y * (ML_DK ** -0.5)
        else:
            conv_ref[:, cols] = y

    z = _dot(hb, w_ref[:, C_DNZ:C_MLV])
    rest_ref[:, 0:512] = z * _sigmoid(z)
    rest_ref[:, 512:1024] = _dot(hb, w_ref[:, C_MLV:C_MLO])
    rest_ref[:, 1024:1536] = _sigmoid(_dot(hb, w_ref[:, C_MLO:C_GATE]))

    r_i = lax.broadcasted_iota(jnp.int32, (tm, tm), 0)
    c_i = lax.broadcasted_iota(jnp.int32, (tm, tm), 1)
    same_chunk = (r_i // CHUNK) == (c_i // CHUNK)
    tril = jnp.where(same_chunk & (c_i <= r_i), 1.0, 0.0).astype(BF16)
    triu = jnp.where(same_chunk & (r_i <= c_i), 1.0, 0.0).astype(BF16)

    gc = _dot(hb, w_ref[:, C_GATE:N_PROJ])
    cls_c = lax.broadcasted_iota(jnp.int32, (tm, HEAD_LANES), 1) // 4
    gt = _gate_transform(gc, gpc_ref[0:1, :], gpc_ref[1:2, :], cls_c)
    cs = _dot_exact_right(tril, gt)
    gcol_ref[...] = jnp.where((cls_c == 1) | (cls_c == 3), cs, gt)

    gr = _dot_nt(wgt_ref[...], hb)
    cls_r = lax.broadcasted_iota(jnp.int32, (16, tm), 0) // 4
    gtr = _gate_transform(gr, gpr_ref[:, 0:1], gpr_ref[:, 1:2], cls_r)
    csr = _dot_exact_left(gtr, triu)
    grow_ref[...] = jnp.where((cls_r == 1) | (cls_r == 3), csr, gtr)


def _mod_spec(tiles_per_seq, d, j):
    return pl.BlockSpec((None, 1, d), lambda i: (i // tiles_per_seq, 0, j))


def _inproj(x2, mod3, nw, w_new, wgt, cw, gpc, gpr, seq, tm):
    t, d = x2.shape
    tps = seq // tm
    kern = functools.partial(_inproj_kernel, tps)
    return pl.pallas_call(
        kern,
        grid=(t // tm,),
        in_specs=[pl.BlockSpec((tm, d), lambda i: (i, 0)),
                  _mod_spec(tps, d, 0),
                  _mod_spec(tps, d, 1),
                  pl.BlockSpec((1, d), lambda i: (0, 0)),
                  pl.BlockSpec((d, N_PROJ), lambda i: (0, 0)),
                  pl.BlockSpec((16, d), lambda i: (0, 0)),
                  pl.BlockSpec((CONV_W, N_CONV), lambda i: (0, 0)),
                  pl.BlockSpec((8, HEAD_LANES), lambda i: (0, 0)),
                  pl.BlockSpec((16, HEAD_LANES), lambda i: (0, 0))],
        out_specs=[pl.BlockSpec((tm, N_CONV), lambda i: (i, 0)),
                   pl.BlockSpec((tm, N_REST), lambda i: (i, 0)),
                   pl.BlockSpec((tm, HEAD_LANES), lambda i: (i, 0)),
                   pl.BlockSpec((16, tm), lambda i: (0, i))],
        out_shape=[jax.ShapeDtypeStruct((t, N_CONV), F32),
                   jax.ShapeDtypeStruct((t, N_REST), F32),
                   jax.ShapeDtypeStruct((t, HEAD_LANES), F32),
                   jax.ShapeDtypeStruct((16, t), F32)],
        scratch_shapes=[pltpu.VMEM((tm + 8, N_CONV), F32)],
        compiler_params=pltpu.CompilerParams(dimension_semantics=("arbitrary",),
                                             vmem_limit_bytes=V7X_VMEM_LIMIT),
        name="inproj",
    )(x2, mod3, mod3, nw, w_new, wgt, cw, gpc, gpr)


def _chunk_masks():
    r = lax.broadcasted_iota(jnp.int32, (CHUNK, CHUNK), 0)
    c = lax.broadcasted_iota(jnp.int32, (CHUNK, CHUNK), 1)
    return r >= c, r > c, r == c


def _bdot(a, b):
    return lax.dot_general(a, b, (((2,), (1,)), ((0,), (0,))), preferred_element_type=F32)


def _bdot_nt(a, b):
    return lax.dot_general(a, b, (((2,), (2,)), ((0,), (0,))), preferred_element_type=F32)


def _unit_lower_inverse(lower, row, col):
    x = jnp.where(row == col, 1.0, 0.0) - jnp.where((row >> 1) == (col >> 1), lower, 0.0)
    shift = 1
    while (1 << shift) < CHUNK:
        couple = ((row >> (shift + 1)) == (col >> (shift + 1))) & ((row >> shift) != (col >> shift))
        cb = jnp.where(couple, lower, 0.0).astype(BF16)
        xb = x.astype(BF16)
        x = x - _bdot(_bdot(xb, cb).astype(BF16), xb)
        shift += 1
        yield
    return x


def _seq_heads(n_seq, n_heads):
    return [(s, hh) for s in range(n_seq) for hh in range(n_heads)]


def _deltanet_steps(q_ref, k_ref, v_ref, gc_ref, gr_ref, z_ref, nw_ref, o_ref, s_ref):
    n_seq, nc = gr_ref.shape[0], gr_ref.shape[1]

    @pl.when(pl.program_id(1) == 0)
    def _():
        s_ref[...] = jnp.zeros(s_ref.shape, F32)

    row = lax.broadcasted_iota(jnp.int32, (CHUNK, CHUNK), 0)
    col = lax.broadcasted_iota(jnp.int32, (CHUNK, CHUNK), 1)
    incl = row >= col
    strict = row > col
    nw = nw_ref[...]
    gcc = gc_ref[...]
    grr = gr_ref[...]

    streams = _seq_heads(n_seq, DN_HEADS)
    nh = len(streams)
    nb = nh * nc

    def heads(ref):
        return jnp.stack([ref[s, :, hh * HEAD_LANES:(hh + 1) * HEAD_LANES] for s, hh in streams],
                         axis=0).reshape(nb, CHUNK, HEAD_LANES)

    def col_gate(lane0):
        return jnp.stack([gcc[s, :, lane0 + hh:lane0 + hh + 1] for s, hh in streams], axis=0).reshape(nb, CHUNK, 1)

    q = heads(q_ref)
    k = heads(k_ref)
    v = heads(v_ref)
    beta = col_gate(0)
    g_c = col_gate(4)
    g_r = jnp.stack([grr[s, :, 4 + hh:5 + hh, :] for s, hh in streams], axis=0).reshape(nb, 1, CHUNK)
    g_last = g_c[:, CHUNK - 1:CHUNK, :]
    decay = jnp.exp(jnp.where(incl, g_c - g_r, NEG_BIG))
    kb = k.astype(BF16)
    kk = _bdot_nt(kb, kb)
    lower = jnp.where(strict, beta * kk * decay, 0.0)
    yield
    tinv = yield from _unit_lower_inverse(lower, row, col)
    eg = jnp.exp(g_c)
    rhs = jnp.concatenate([v * beta, k * (beta * eg)], axis=-1)
    sol = _bdot(tinv.astype(BF16), rhs.astype(BF16))
    yield
    w_val = sol[:, :, 0:DN_DV].reshape(nh, nc, CHUNK, DN_DV)
    kq = jnp.concatenate([sol[:, :, DN_DV:DN_DV + DN_DK], q * eg], axis=1).astype(BF16)
    kq = kq.reshape(nh, nc, 2 * CHUNK, DN_DK)
    qk = (_bdot_nt(q.astype(BF16), kb) * decay).astype(BF16).reshape(nh, nc, CHUNK, CHUNK)
    k_dec_t = jnp.swapaxes(k * jnp.exp(g_last - g_c), 1, 2).astype(BF16).reshape(nh, nc, DN_DK, CHUNK)
    s_dec = jnp.exp(g_last).reshape(nh, nc, 1, 1)
    yield

    state = s_ref[...]
    outs = []
    for c in range(nc):
        both = _bdot(kq[:, c], state.astype(BF16))
        v_new = w_val[:, c] - both[:, 0:CHUNK]
        vb = v_new.astype(BF16)
        outs.append(both[:, CHUNK:2 * CHUNK] + _bdot(qk[:, c], vb))
        state = s_dec[:, c] * state + _bdot(k_dec_t[:, c], vb)
        yield
    s_ref[...] = state

    o = jnp.stack(outs, axis=1)
    on = o * lax.rsqrt(jnp.mean(o * o, axis=-1, keepdims=True) + EPS) * nw
    on = on.reshape(nh, nc * CHUNK, DN_DV)
    for idx, (s, hh) in enumerate(streams):
        lanes = slice(hh * HEAD_LANES, (hh + 1) * HEAD_LANES)
        o_ref[s, :, lanes] = on[idx] * z_ref[s, :, lanes]


def _mlstm_steps(q_ref, k_ref, v_ref, gc_ref, gr_ref, og_ref, nw_ref, o_ref, c_ref, n_ref, m_ref):
    n_seq, nc = gr_ref.shape[0], gr_ref.shape[1]

    @pl.when(pl.program_id(1) == 0)
    def _():
        c_ref[...] = jnp.zeros(c_ref.shape, F32)
        n_ref[...] = jnp.zeros(n_ref.shape, F32)
        m_ref[...] = jnp.zeros(m_ref.shape, F32)

    incl, _, _ = _chunk_masks()
    gcc = gc_ref[...]
    grr = gr_ref[...]

    streams = _seq_heads(n_seq, ML_HEADS)
    nh = len(streams)

    def heads(ref):
        return jnp.stack([ref[s, :, hh * HEAD_LANES:(hh + 1) * HEAD_LANES] for s, hh in streams],
                         axis=0).reshape(nh, nc, CHUNK, HEAD_LANES)

    def col_gate(lane0):
        return jnp.stack([gcc[s, :, lane0 + hh:lane0 + hh + 1] for s, hh in streams],
                         axis=0).reshape(nh, nc, CHUNK, 1)

    def row_gate(row0):
        return jnp.stack([grr[s, :, row0 + hh:row0 + hh + 1, :] for s, hh in streams], axis=0)

    q = heads(q_ref)
    k = heads(k_ref)
    v = heads(v_ref)
    i_c = col_gate(8)
    b_c = col_gate(12)
    i_r = row_gate(8)
    b_r = row_gate(12)
    b_last = b_c[:, :, CHUNK - 1:CHUNK, :]
    d_mat = jnp.where(incl, b_c - b_r + i_r, NEG_BIG)
    m_intra = jnp.max(d_mat, axis=-1, keepdims=True)
    g_end = b_last - b_c + i_c
    g_end_max = jnp.max(g_end, axis=2, keepdims=True)
    yield

    m_run = m_ref[:, 0:1, 0:1].reshape(nh, 1, 1, 1)
    m_before = []
    for c in range(nc):
        m_before.append(m_run)
        m_run = jnp.maximum(b_last[:, c:c + 1] + m_run, g_end_max[:, c:c + 1])
    m_s = jnp.concatenate(m_before, axis=1)
    m_new = jnp.maximum(b_last + m_s, g_end_max)
    keep = jnp.exp(b_last + m_s - m_new)
    yield

    nb = nh * nc
    qb = q.astype(BF16)
    kb = k.astype(BF16)
    vb = v.astype(BF16)
    m_t = jnp.maximum(b_c + m_s, m_intra)
    inter = jnp.exp(b_c + m_s - m_t)
    qk = _bdot_nt(qb.reshape(nb, CHUNK, HEAD_LANES), kb.reshape(nb, CHUNK, HEAD_LANES))
    p = jnp.exp(d_mat - m_t) * qk.reshape(nh, nc, CHUNK, CHUNK)
    yield
    intra = _bdot(p.astype(BF16).reshape(nb, CHUNK, CHUNK), vb.reshape(nb, CHUNK, ML_DV)).reshape(nh, nc, CHUNK, ML_DV)
    p_sum = jnp.sum(p, axis=-1, keepdims=True)
    yield
    kw = k * jnp.exp(g_end - m_new)
    kw_t = jnp.swapaxes(kw.reshape(nb, CHUNK, HEAD_LANES), 1, 2).astype(BF16)
    d_state = _bdot(kw_t, vb.reshape(nb, CHUNK, ML_DV)).reshape(nh, nc, HEAD_LANES, ML_DV)
    kw_sum = jnp.sum(kw, axis=2, keepdims=True)
    yield

    c_s = c_ref[...]
    n_s = n_ref[:, 0:1, :]
    q_c = []
    q_n = []
    for c in range(nc):
        q_c.append(_bdot(qb[:, c], c_s.astype(BF16)))
        q_n.append(jnp.sum(q[:, c] * n_s, axis=-1, keepdims=True))
        c_s = keep[:, c] * c_s + d_state[:, c]
        n_s = keep[:, c] * n_s + kw_sum[:, c]
        yield
    c_ref[...] = c_s
    n_ref[...] = jnp.broadcast_to(n_s, (nh, 8, HEAD_LANES))
    m_ref[...] = jnp.broadcast_to(m_run.reshape(nh, 1, 1), (nh, 8, HEAD_LANES))

    num = inter * jnp.stack(q_c, axis=1) + intra
    den = inter * jnp.stack(q_n, axis=1) + p_sum
    h = num / jnp.maximum(jnp.abs(den), jnp.exp(-m_t))
    hr = h * lax.rsqrt(jnp.mean(h * h, axis=-1, keepdims=True) + EPS)
    hr = hr.reshape(nh, nc * CHUNK, ML_DV)
    for idx, (s, hh) in enumerate(streams):
        lanes = slice(hh * HEAD_LANES, (hh + 1) * HEAD_LANES)
        o_ref[s, :, lanes] = hr[idx] * nw_ref[:, lanes] * og_ref[s, :, lanes]


def _mixers_kernel(dq_ref, dk_ref, dv_ref, gc_ref, gr_ref, z_ref, dnw_ref, mq_ref, mk_ref, mv_ref, og_ref, mnw_ref,
                   o_ref, s_ref, c_ref, n_ref, m_ref):
    n_a = DN_HEADS * HEAD_LANES
    n_b = ML_HEADS * HEAD_LANES
    stages = [_deltanet_steps(dq_ref, dk_ref, dv_ref, gc_ref, gr_ref, z_ref, dnw_ref, o_ref.at[:, :, 0:n_a], s_ref),
              _mlstm_steps(mq_ref, mk_ref, mv_ref, gc_ref, gr_ref, og_ref, mnw_ref, o_ref.at[:, :, n_a:n_a + n_b],
                           c_ref, n_ref, m_ref)]
    while stages:
        for stage in list(stages):
            try:
                next(stage)
            except StopIteration:
                stages.remove(stage)


def _mixers(conv_out, rest, gcol, grow3, dn_nw, ml_nw, batch, seq, rows):
    t = conv_out.shape[0]
    nj = seq // rows
    cpb = rows // CHUNK
    width = 4 * HEAD_LANES
    assert DN_HEADS * HEAD_LANES == width and ML_HEADS * HEAD_LANES == width
    n_seq = 2 if batch % 2 == 0 else 1
    conv3 = conv_out.reshape(batch, seq, -1)
    rest3 = rest.reshape(batch, seq, -1)
    gcol3 = gcol.reshape(batch, seq, -1)
    grow4 = grow3.reshape(batch, seq // CHUNK, 16, CHUNK)

    def conv_block(c0):
        return pl.BlockSpec((n_seq, rows, width), lambda b, j: (b, j, c0 // width))

    def rest_block(c0):
        return pl.BlockSpec((n_seq, rows, width), lambda b, j: (b, j, (c0 - N_CONV) // width))

    gates = [pl.BlockSpec((n_seq, rows, HEAD_LANES), lambda b, j: (b, j, 0)),
             pl.BlockSpec((n_seq, cpb, 16, CHUNK), lambda b, j: (b, j, 0, 0))]
    out = pl.pallas_call(
        _mixers_kernel,
        grid=(batch // n_seq, nj),
        in_specs=[conv_block(C_DNQ), conv_block(C_DNK), conv_block(C_DNV)] + gates
                 + [rest_block(C_DNZ), pl.BlockSpec((1, HEAD_LANES), lambda b, j: (0, 0)),
                    conv_block(C_MLQ), conv_block(C_MLK), rest_block(C_MLV), rest_block(C_MLO),
                    pl.BlockSpec((1, width), lambda b, j: (0, 0))],
        out_specs=pl.BlockSpec((n_seq, rows, 2 * width), lambda b, j: (b, j, 0)),
        out_shape=jax.ShapeDtypeStruct((batch, seq, 2 * width), F32),
        scratch_shapes=[pltpu.VMEM((n_seq * DN_HEADS, DN_DK, DN_DV), F32),
                        pltpu.VMEM((n_seq * ML_HEADS, HEAD_LANES, ML_DV), F32),
                        pltpu.VMEM((n_seq * ML_HEADS, 8, HEAD_LANES), F32),
                        pltpu.VMEM((n_seq * ML_HEADS, 8, HEAD_LANES), F32)],
        compiler_params=pltpu.CompilerParams(dimension_semantics=("arbitrary", "arbitrary"),
                                             vmem_limit_bytes=V7X_VMEM_LIMIT),
        name="mixers",
    )(conv3, conv3, conv3, gcol3, grow4, rest3, dn_nw, conv3, conv3, rest3, rest3, ml_nw)
    return out.reshape(t, 2 * width)


def _route_kernel(y_ref, wo_ref, x_ref, g1_ref, sh_ref, sc_ref, nw_ref, wrt_ref, br_ref,
                  x1_ref, h2_ref, pos_ref, wrow_ref, len_ref):
    tm = x_ref.shape[0]

    mix = _dot(y_ref[...].astype(BF16), wo_ref[...])
    x1 = x_ref[...] + g1_ref[...] * mix
    x1_ref[...] = x1
    ms = jnp.mean(x1 * x1, axis=-1, keepdims=True)
    h2 = x1 * lax.rsqrt(ms + EPS) * nw_ref[...]
    h2 = h2 * (1.0 + sc_ref[...]) + sh_ref[...]
    h2_ref[...] = h2.astype(BF16)

    hh, hm, _ = _split3(h2)
    wh, wm, _ = _split3(wrt_ref[...])
    logits = _dot_nt(wh, hh) + (_dot_nt(wh, hm) + _dot_nt(wm, hh)) + br_ref[:, 0:1]

    e_i = lax.broadcasted_iota(jnp.int32, (N_EXPERTS, tm), 0)
    work = logits
    tops = []
    sels = []
    hots = []
    for _ in range(TOP_K):
        m = jnp.max(work, axis=0, keepdims=True)
        sel = jnp.min(jnp.where(work == m, e_i, N_EXPERTS), axis=0, keepdims=True)
        hot = e_i == sel
        work = jnp.where(hot, NEG_BIG, work)
        tops.append(m)
        sels.append(sel)
        hots.append(hot)
    exps = [jnp.exp(tl - tops[0]) for tl in tops]
    denom = exps[0] + exps[1] + exps[2] + exps[3]
    ws = [e / denom for e in exps]

    chosen = jnp.zeros((N_EXPERTS, tm), F32)
    for hot in hots:
        chosen = chosen + jnp.where(hot, 1.0, 0.0)

    r_i = lax.broadcasted_iota(jnp.int32, (tm, tm), 0)
    c_i = lax.broadcasted_iota(jnp.int32, (tm, tm), 1)
    strict_upper = jnp.where(r_i < c_i, 1.0, 0.0).astype(BF16)
    prefix = _dot(chosen.astype(BF16), strict_upper)
    n_e = jnp.sum(chosen, axis=1, keepdims=True)
    len8 = jnp.ceil(n_e * 0.125) * 8.0
    er = lax.broadcasted_iota(jnp.int32, (N_EXPERTS, N_EXPERTS), 0)
    ec = lax.broadcasted_iota(jnp.int32, (N_EXPERTS, N_EXPERTS), 1)
    strict_lower = jnp.where(ec < er, 1.0, 0.0).astype(BF16)
    len8b = jnp.broadcast_to(len8, (N_EXPERTS, HEAD_LANES))
    off8 = _dot_exact_right(strict_lower, len8b)[:, 0:1]
    len_ref[...] = len8b.astype(jnp.int32)

    for kk in range(TOP_K):
        wrow_ref[kk:kk + 1, :] = ws[kk]
        pos = jnp.sum(jnp.where(hots[kk], prefix + off8, 0.0), axis=0, keepdims=True)
        pos_ref[kk:kk + 1, :] = pos.astype(jnp.int32)


def _route(ymix, wo, x2, mod3, nw, wrt, br, seq, tm):
    t, d = x2.shape
    tps = seq // tm
    return pl.pallas_call(
        _route_kernel,
        grid=(t // tm,),
        in_specs=[pl.BlockSpec((tm, ymix.shape[1]), lambda i: (i, 0)),
                  pl.BlockSpec(wo.shape, lambda i: (0, 0)),
                  pl.BlockSpec((tm, d), lambda i: (i, 0)),
                  _mod_spec(tps, d, 2),
                  _mod_spec(tps, d, 3),
                  _mod_spec(tps, d, 4),
                  pl.BlockSpec((1, d), lambda i: (0, 0)),
                  pl.BlockSpec((N_EXPERTS, d), lambda i: (0, 0)),
                  pl.BlockSpec((N_EXPERTS, HEAD_LANES), lambda i: (0, 0))],
        out_specs=[pl.BlockSpec((tm, d), lambda i: (i, 0)),
                   pl.BlockSpec((tm, d), lambda i: (i, 0)),
                   pl.BlockSpec((TOP_K, tm), lambda i: (0, i)),
                   pl.BlockSpec((TOP_K, tm), lambda i: (0, i)),
                   pl.BlockSpec((N_EXPERTS, HEAD_LANES), lambda i: (i, 0))],
        out_shape=[jax.ShapeDtypeStruct((t, d), F32),
                   jax.ShapeDtypeStruct((t, d), BF16),
                   jax.ShapeDtypeStruct((TOP_K, t), jnp.int32),
                   jax.ShapeDtypeStruct((TOP_K, t), F32),
                   jax.ShapeDtypeStruct((t // tm * N_EXPERTS, HEAD_LANES), jnp.int32)],
        compiler_params=pltpu.CompilerParams(dimension_semantics=("arbitrary",),
                                             vmem_limit_bytes=V7X_VMEM_LIMIT),
        name="route",
    )(ymix, wo, x2, mod3, mod3, mod3, nw, wrt, br)


TAIL_START, TAIL_LEN, PAD_END, BLOCK_START, BLOCK_COUNT = range(5)


def _slots_kernel(bm, len_te_ref, ssrc_ref, sdst_ref, tab_ref):
    nt, lanes = len_te_ref.shape
    len_te = len_te_ref[...].astype(F32)

    r_l = lax.broadcasted_iota(jnp.int32, (lanes, lanes), 0)
    c_l = lax.broadcasted_iota(jnp.int32, (lanes, lanes), 1)
    upper_incl = jnp.where(r_l <= c_l, 1.0, 0.0).astype(BF16)
    upper_strict = jnp.where(r_l < c_l, 1.0, 0.0).astype(BF16)
    total_r = jnp.sum(len_te, axis=0, keepdims=True)
    padded_r = jnp.ceil(total_r * (1.0 / bm)) * bm
    pad_end_r = _dot_exact_left(jnp.broadcast_to(padded_r, (8, lanes)), upper_incl)[0:1, :]
    pad_start_r = pad_end_r - padded_r
    r_t = lax.broadcasted_iota(jnp.int32, (nt, nt), 0)
    c_t = lax.broadcasted_iota(jnp.int32, (nt, nt), 1)
    lower_strict_t = jnp.where(c_t < r_t, 1.0, 0.0).astype(BF16)
    before = _dot_exact_right(lower_strict_t, len_te)
    ssrc_ref[...] = _dot_exact_left(len_te, upper_strict).astype(jnp.int32)
    sdst_ref[...] = (pad_start_r + before).astype(jnp.int32)
    tab_ref[TAIL_START:TAIL_START + 1, :] = (pad_start_r + total_r).astype(jnp.int32)
    tab_ref[TAIL_LEN:TAIL_LEN + 1, :] = (padded_r - total_r).astype(jnp.int32)
    tab_ref[PAD_END:PAD_END + 1, :] = pad_end_r.astype(jnp.int32)
    tab_ref[BLOCK_START:BLOCK_START + 1, :] = (pad_start_r * (1.0 / bm)).astype(jnp.int32)
    tab_ref[BLOCK_COUNT:BLOCK_COUNT + 1, :] = (padded_r * (1.0 / bm)).astype(jnp.int32)
    tab_ref[5:8, :] = jnp.zeros((3, lanes), jnp.int32)


def _slots(len_te, bm):
    assert bm & (bm - 1) == 0, "block rows must be a power of two"
    nt, lanes = len_te.shape
    return pl.pallas_call(
        functools.partial(_slots_kernel, bm),
        out_shape=[jax.ShapeDtypeStruct((nt, lanes), jnp.int32),
                   jax.ShapeDtypeStruct((nt, lanes), jnp.int32),
                   jax.ShapeDtypeStruct((8, lanes), jnp.int32)],
        compiler_params=pltpu.CompilerParams(vmem_limit_bytes=V7X_VMEM_LIMIT),
        name="slots",
    )(len_te)


def _segment_pieces(max_rows):
    sizes = []
    s = 8
    while s <= max_rows:
        sizes.append(s)
        s *= 2
    return sizes[::-1]


def _segment_dma(src_ref, src0, dst_ref, dst0, nrows, sizes, sem, start):
    def pieces(group, off):
        for sz in group:
            bit = nrows & sz

            @pl.when(bit != 0)
            def _(off=off, sz=sz):
                s0 = pl.multiple_of(src0 + off, 8)
                d0 = pl.multiple_of(dst0 + off, 8)
                cp = pltpu.make_async_copy(src_ref.at[pl.ds(s0, sz)], dst_ref.at[pl.ds(d0, sz)], sem)
                if start:
                    cp.start()
                else:
                    cp.wait()

            off = off + bit

    large = [sz for sz in sizes if sz >= COMMON_SEGMENT_ROWS]
    small = [sz for sz in sizes if sz < COMMON_SEGMENT_ROWS]
    if large and small:
        @pl.when(nrows >= COMMON_SEGMENT_ROWS)
        def _():
            pieces(large, 0)

        pieces(small, nrows & ~(COMMON_SEGMENT_ROWS - 1))
    else:
        pieces(sizes, 0)


def _sorted_onehot(pos_ref, row0, nrows, tt):
    r_i = lax.broadcasted_iota(jnp.int32, (nrows, tt), 0) + row0
    hit = r_i == pos_ref[0:1, :]
    for kk in range(1, TOP_K):
        hit = hit | (r_i == pos_ref[kk:kk + 1, :])
    return hit


def _row_groups(nrows, n_groups):
    size = -(-nrows // n_groups // 8) * 8
    return [(r0, min(size, nrows - r0)) for r0 in range(0, nrows, size)]


def _dispatch_kernel(bm, len_ref, ssrc_ref, sdst_ref, tail_ref, pos_ref, h2_ref, xs_hbm, buf, sem):
    i = pl.program_id(0)
    nt = pl.num_programs(0)
    tt = h2_ref.shape[0]
    nrows = buf.shape[1]
    sizes = _segment_pieces(tt)
    slot = i % 2

    h2 = h2_ref[...]
    for r0, rn in _row_groups(nrows, 4):
        perm = jnp.where(_sorted_onehot(pos_ref, r0, rn, tt), 1.0, 0.0).astype(BF16)
        buf[slot, r0:r0 + rn, :] = _dot(perm, h2)

    def seg_start(tile, which):
        def body(e, carry):
            _segment_dma(buf.at[which], ssrc_ref[tile, e], xs_hbm, sdst_ref[tile, e], len_ref[tile, e], sizes,
                         sem.at[which], True)
            return carry
        lax.fori_loop(0, N_EXPERTS, body, 0)

    def seg_wait(tile, which):
        total = ssrc_ref[tile, N_EXPERTS - 1] + len_ref[tile, N_EXPERTS - 1]
        _segment_dma(buf.at[which], 0, xs_hbm, 0, total, _segment_pieces(nrows), sem.at[which], False)

    @pl.when(i > 0)
    def _():
        seg_wait(i - 1, 1 - slot)

    seg_start(i, slot)

    @pl.when(i == nt - 1)
    def _():
        seg_wait(i, slot)
        zrows = bm
        zbuf = buf.at[0]
        zsem = sem.at[0]
        buf[0, 0:zrows, :] = jnp.zeros((zrows, buf.shape[2]), F32)

        def tail(start):
            def body(e, carry):
                _segment_dma(zbuf, 0, xs_hbm, tail_ref[TAIL_START, e], tail_ref[TAIL_LEN, e],
                             _segment_pieces(zrows // 2), zsem, start)
                return carry
            lax.fori_loop(0, N_EXPERTS, body, 0)

        tail(True)
        tail(False)

        used = tail_ref[PAD_END, tail_ref.shape[1] - 1]
        n_unused = (xs_hbm.shape[0] - used) // zrows

        def unused_copy(j):
            d0 = pl.multiple_of(used + j * zrows, 8)
            return pltpu.make_async_copy(zbuf.at[pl.ds(0, zrows)], xs_hbm.at[pl.ds(d0, zrows)], zsem)

        def unused_start(j, carry):
            unused_copy(j).start()
            return carry

        def unused_wait(j, carry):
            unused_copy(j).wait()
            return carry

        lax.fori_loop(0, n_unused, unused_start, 0)
        lax.fori_loop(0, n_unused, unused_wait, 0)


def _dispatch(seg_len, seg_src, seg_dst, tail, pos, h2, n_slots, tt, bm):
    t, d = h2.shape
    nrows = TOP_K * tt + 8 * N_EXPERTS
    assert bm <= nrows
    smem = pl.BlockSpec(memory_space=pltpu.SMEM)
    return pl.pallas_call(
        functools.partial(_dispatch_kernel, bm),
        grid=(t // tt,),
        in_specs=[smem, smem, smem, smem,
                  pl.BlockSpec((TOP_K, tt), lambda i: (0, i)),
                  pl.BlockSpec((tt, d), lambda i: (i, 0))],
        out_specs=pl.BlockSpec(memory_space=pl.ANY),
        out_shape=jax.ShapeDtypeStruct((n_slots, d), F32),
        scratch_shapes=[pltpu.VMEM((2, nrows, d), F32), pltpu.SemaphoreType.DMA((2,))],
        compiler_params=pltpu.CompilerParams(dimension_semantics=("arbitrary",), has_side_effects=True,
                                             vmem_limit_bytes=V7X_VMEM_LIMIT),
        name="dispatch",
    )(seg_len, seg_src, seg_dst, tail, pos, h2)


def _experts_kernel(tab_ref, xs_hbm, wgu_hbm, bgu_ref, wd_hbm, bd_ref, y_hbm, wgu_s, wd_s, wgu_f, wd_f, xbuf, ybuf,
                    sem_in, sem_out, sem_w):
    e = pl.program_id(0)
    n_e = pl.num_programs(0)
    bm = xbuf.shape[1] // 2
    d_ff = wd_f.shape[1]
    ct = wgu_s.shape[2]
    wslot = e % 2

    def w_copies(expert, which):
        return (pltpu.make_async_copy(wgu_hbm.at[expert], wgu_f.at[which], sem_w.at[which]),
                pltpu.make_async_copy(wd_hbm.at[expert], wd_f.at[which], sem_w.at[which]))

    @pl.when(e == 0)
    def _():
        for cp in w_copies(0, 0):
            cp.start()

    for cp in w_copies(e, wslot):
        cp.wait()

    have_next = e + 1 < n_e

    def prefetch_next_weights(cond):
        @pl.when(have_next & cond)
        def _():
            for cp in w_copies(e + 1, 1 - wslot):
                cp.start()
    first = tab_ref[BLOCK_START, e]
    count = tab_ref[BLOCK_COUNT, e]
    npair = count // 2
    odd = count % 2
    tslot = npair % 2

    def pair_rows(j):
        return pl.ds(pl.multiple_of((first + 2 * j) * bm, bm), 2 * bm)

    def x_copy(j, slot):
        return pltpu.make_async_copy(xs_hbm.at[pair_rows(j)], xbuf.at[slot], sem_in.at[slot])

    def y_copy(j, slot):
        return pltpu.make_async_copy(ybuf.at[slot], y_hbm.at[pair_rows(j)], sem_out.at[slot])

    def last_rows():
        return pl.ds(pl.multiple_of((first + 2 * npair) * bm, bm), bm)

    def x_last(slot):
        return pltpu.make_async_copy(xs_hbm.at[last_rows()], xbuf.at[slot, pl.ds(0, bm)], sem_in.at[slot])

    def y_last(slot):
        return pltpu.make_async_copy(ybuf.at[slot, pl.ds(0, bm)], y_hbm.at[last_rows()], sem_out.at[slot])

    @pl.when(npair > 0)
    def _():
        x_copy(0, 0).start()

    @pl.when((npair == 0) & (odd == 1))
    def _():
        x_last(0).start()

    for t in range(wgu_s.shape[0]):
        wgu_s[t] = wgu_f[wslot, :, t * ct:(t + 1) * ct].astype(BF16)
    for t in range(wd_s.shape[0]):
        wd_s[t] = wd_f[wslot, :, t * ct:(t + 1) * ct].astype(BF16)

    def mlp(x):
        xb = x.astype(BF16)
        tpc = 2
        fc = tpc * ct
        nf = d_ff // fc
        n_out = wd_s.shape[0]

        def gate_up(f):
            gates, ups = [], []
            for t in range(f * tpc, (f + 1) * tpc):
                gates.append(_dot(xb, wgu_s[t]) + bgu_ref[:, t * ct:(t + 1) * ct])
                u = d_ff // ct + t
                ups.append(_dot(xb, wgu_s[u]) + bgu_ref[:, u * ct:(u + 1) * ct])
            return jnp.concatenate(gates, axis=-1), jnp.concatenate(ups, axis=-1)

        acts = []
        pre = gate_up(0)
        for f in range(nf):
            gate = jnp.minimum(pre[0], SWIGLU_LIMIT)
            up = jnp.clip(pre[1], -SWIGLU_LIMIT, SWIGLU_LIMIT)
            if f + 1 < nf:
                pre = gate_up(f + 1)
            acts.append(((up + 1.0) * gate * _sigmoid(SWIGLU_ALPHA * gate)).astype(BF16))
        act = jnp.concatenate(acts, axis=-1)
        return [_dot(act, wd_s[n]) + bd_ref[:, n * ct:(n + 1) * ct] for n in range(n_out)]

    def pair(j, carry):
        slot = j % 2
        x_copy(j, slot).wait()

        @pl.when(j + 1 < npair)
        def _():
            x_copy(j + 1, 1 - slot).start()

        @pl.when((j + 1 == npair) & (odd == 1))
        def _():
            x_last(1 - slot).start()

        @pl.when(j >= 2)
        def _():
            y_copy(j - 2, slot).wait()

        prefetch_next_weights(j == 0)
        for n, piece in enumerate(mlp(xbuf[slot])):
            ybuf[slot, :, n * ct:(n + 1) * ct] = piece
        y_copy(j, slot).start()
        return carry

    lax.fori_loop(0, npair, pair, 0)

    prefetch_next_weights(npair == 0)

    @pl.when(odd == 1)
    def _():
        x_last(tslot).wait()

        @pl.when(npair >= 2)
        def _():
            y_copy(npair - 2, tslot).wait()

        for n, piece in enumerate(mlp(xbuf[tslot, 0:bm, :])):
            ybuf[tslot, 0:bm, n * ct:(n + 1) * ct] = piece
        y_last(tslot).start()

    @pl.when((odd == 0) & (npair >= 2))
    def _():
        y_copy(npair - 2, tslot).wait()

    @pl.when(npair >= 1)
    def _():
        y_copy(npair - 1, 1 - tslot).wait()

    @pl.when(odd == 1)
    def _():
        y_last(tslot).wait()

    @pl.when(e == pl.num_programs(0) - 1)
    def _():
        used = first + count
        n_unused = y_hbm.shape[0] // bm - used
        ybuf[0, 0:bm, :] = jnp.zeros((bm, ybuf.shape[2]), F32)

        def z_copy(j):
            d0 = pl.multiple_of((used + j) * bm, bm)
            return pltpu.make_async_copy(ybuf.at[0, pl.ds(0, bm)], y_hbm.at[pl.ds(d0, bm)], sem_out.at[0])

        def z_start(j, carry):
            z_copy(j).start()
            return carry

        def z_wait(j, carry):
            z_copy(j).wait()
            return carry

        lax.fori_loop(0, n_unused, z_start, 0)
        lax.fori_loop(0, n_unused, z_wait, 0)


def _experts(table, xs, wgu, bgu, wd, bd, bm):
    ns, d = xs.shape
    n_e, _, two_ff = wgu.shape
    d_ff = two_ff // 2
    wmap = lambda e: (e, 0, 0)
    return pl.pallas_call(
        _experts_kernel,
        grid=(n_e,),
        in_specs=[pl.BlockSpec(memory_space=pltpu.SMEM),
                  pl.BlockSpec(memory_space=pl.ANY),
                  pl.BlockSpec(memory_space=pl.ANY),
                  pl.BlockSpec((None, 1, two_ff), wmap),
                  pl.BlockSpec(memory_space=pl.ANY),
                  pl.BlockSpec((None, 1, d), wmap)],
        out_specs=pl.BlockSpec(memory_space=pl.ANY),
        out_shape=jax.ShapeDtypeStruct((ns, d), F32),
        scratch_shapes=[pltpu.VMEM((two_ff // MXU_COLS, d, MXU_COLS), BF16),
                        pltpu.VMEM((d // MXU_COLS, d_ff, MXU_COLS), BF16),
                        pltpu.VMEM((2, d, two_ff), F32), pltpu.VMEM((2, d_ff, d), F32),
                        pltpu.VMEM((2, 2 * bm, d), F32), pltpu.VMEM((2, 2 * bm, d), F32),
                        pltpu.SemaphoreType.DMA((2,)), pltpu.SemaphoreType.DMA((2,)),
                        pltpu.SemaphoreType.DMA((2,))],
        compiler_params=pltpu.CompilerParams(dimension_semantics=("arbitrary",), has_side_effects=True,
                                             vmem_limit_bytes=V7X_VMEM_LIMIT),
        name="experts",
    )(table, xs, wgu, bgu, wd, bd)


def _combine_kernel(len_ref, ssrc_ref, sdst_ref, pos_ref, w_ref, y_hbm, x1_ref, g2_ref, nw_ref, sh_ref, sc_ref,
                    o_ref, ybuf, sem):
    i = pl.program_id(0)
    nt = pl.num_programs(0)
    tt = x1_ref.shape[0]
    nrows = ybuf.shape[1]
    sizes = _segment_pieces(tt)
    slot = i % 2

    def seg_start(tile, which):
        def body(e, carry):
            _segment_dma(y_hbm, sdst_ref[tile, e], ybuf.at[which], ssrc_ref[tile, e], len_ref[tile, e], sizes,
                         sem.at[which], True)
            return carry
        lax.fori_loop(0, N_EXPERTS, body, 0)

    def seg_wait(tile, which):
        total = ssrc_ref[tile, N_EXPERTS - 1] + len_ref[tile, N_EXPERTS - 1]
        _segment_dma(y_hbm, 0, ybuf.at[which], 0, total, _segment_pieces(nrows), sem.at[which], False)

    @pl.when(i == 0)
    def _():
        ybuf[...] = jnp.zeros(ybuf.shape, F32)
        seg_start(0, 0)

    @pl.when(i + 1 < nt)
    def _():
        seg_start(i + 1, 1 - slot)

    seg_wait(i, slot)

    acc = None
    for r0, rn in _row_groups(nrows, 3):
        r_i = lax.broadcasted_iota(jnp.int32, (rn, tt), 0) + r0
        wmat = 0.0
        for kk in range(TOP_K):
            wmat = jnp.where(r_i == pos_ref[kk:kk + 1, :], w_ref[kk:kk + 1, :], wmat)
        part = _dot_tn(wmat.astype(BF16), ybuf[slot, r0:r0 + rn, :].astype(BF16))
        acc = part if acc is None else acc + part
    xo = x1_ref[...] + g2_ref[...] * acc
    ms = jnp.mean(xo * xo, axis=-1, keepdims=True)
    hn = xo * lax.rsqrt(ms + EPS) * nw_ref[...]
    o_ref[...] = hn * (1.0 + sc_ref[...]) + sh_ref[...]


def _combine(seg_len, seg_src, seg_dst, pos, wrow, y, x1, mod3, nw, modf3, seq, tt):
    t, d = x1.shape
    tps = seq // tt
    nrows = TOP_K * tt + 8 * N_EXPERTS
    smem = pl.BlockSpec(memory_space=pltpu.SMEM)
    return pl.pallas_call(
        _combine_kernel,
        grid=(t // tt,),
        in_specs=[smem, smem, smem,
                  pl.BlockSpec((TOP_K, tt), lambda i: (0, i)),
                  pl.BlockSpec((TOP_K, tt), lambda i: (0, i)),
                  pl.BlockSpec(memory_space=pl.ANY),
                  pl.BlockSpec((tt, d), lambda i: (i, 0)),
                  _mod_spec(tps, d, 5),
                  pl.BlockSpec((1, d), lambda i: (0, 0)),
                  _mod_spec(tps, d, 0),
                  _mod_spec(tps, d, 1)],
        out_specs=pl.BlockSpec((tt, d), lambda i: (i, 0)),
        out_shape=jax.ShapeDtypeStruct((t, d), F32),
        scratch_shapes=[pltpu.VMEM((2, nrows, d), F32), pltpu.SemaphoreType.DMA((2,))],
        compiler_params=pltpu.CompilerParams(dimension_semantics=("arbitrary",),
                                             vmem_limit_bytes=V7X_VMEM_LIMIT),
        name="combine",
    )(seg_len, seg_src, seg_dst, pos, wrow, y, x1, mod3, nw, modf3, modf3)


def _pad_heads(w, heads, dk):
    r = w.shape[0]
    w3 = w.reshape(r, heads, dk)
    return jnp.pad(w3, ((0, 0), (0, 0), (0, HEAD_LANES - dk))).reshape(r, heads * HEAD_LANES)


def _pick_tile(n, pref):
    tile = pref
    while n % tile:
        tile //= 2
    return tile


def kernel(x, c, w_ada, b_ada, norm_mix, w_in, dn_conv, dn_a_log, dn_dt_bias, dn_norm, ml_conv, ml_i_bias,
           ml_f_bias, ml_norm, w_out, norm_ffn, w_router, b_router, w_gate_up, b_gate_up, w_down, b_down,
           w_ada_final, b_ada_final, norm_final):
    batch, seq, d = x.shape
    assert w_ada.shape[0] == 1, "single-layer block"
    assert seq % CHUNK == 0
    t = batch * seq
    x2 = x.reshape(t, d)

    c_pad = jnp.pad(c, ((0, 8 - batch % 8 if batch % 8 else 0), (0, 0)))
    mod3 = _mods(c_pad, w_ada.reshape(d, 6 * d), b_ada.reshape(1, 6 * d)).reshape(-1, 1, 6 * d)
    modf3 = _mods(c_pad, w_ada_final, b_ada_final.reshape(1, 2 * d)).reshape(-1, 1, 2 * d)

    wi = w_in.reshape(d, -1)
    o_z = 1536
    o_b = 2048
    o_mq = 2056
    o_mk = o_mq + ML_HEADS * ML_DK
    o_mv = o_mk + ML_HEADS * ML_DK
    o_mo = o_mv + ML_HEADS * ML_DV
    o_mi = o_mo + ML_HEADS * ML_DV
    gates = jnp.concatenate([wi[:, o_b:o_mq], wi[:, o_mi:o_mi + 2 * ML_HEADS]], axis=1)
    w_new = jnp.concatenate([
        wi[:, 0:o_z],
        _pad_heads(wi[:, o_mq:o_mk], ML_HEADS, ML_DK),
        _pad_heads(wi[:, o_mk:o_mv], ML_HEADS, ML_DK),
        wi[:, o_z:o_b],
        wi[:, o_mv:o_mo],
        wi[:, o_mo:o_mi],
        jnp.pad(gates, ((0, 0), (0, HEAD_LANES - 16))),
    ], axis=1).astype(BF16)
    wgt = gates.T.astype(BF16)
    mlc = ml_conv.reshape(CONV_W, -1)
    cw = jnp.concatenate([dn_conv.reshape(CONV_W, -1),
                          _pad_heads(mlc[:, 0:ML_HEADS * ML_DK], ML_HEADS, ML_DK),
                          _pad_heads(mlc[:, ML_HEADS * ML_DK:], ML_HEADS, ML_DK)], axis=1)
    zeros4 = jnp.zeros((4,), F32)
    bias16 = jnp.concatenate([zeros4, dn_dt_bias.reshape(4), ml_i_bias.reshape(4), ml_f_bias.reshape(4)])
    alog16 = jnp.concatenate([zeros4, dn_a_log.reshape(4), zeros4, zeros4])
    gpc = jnp.zeros((8, HEAD_LANES), F32).at[0, 0:16].set(bias16).at[1, 0:16].set(alog16)
    gpr = jnp.zeros((16, HEAD_LANES), F32).at[:, 0].set(bias16).at[:, 1].set(alog16)

    tm_in = _pick_tile(seq, 512)
    conv_out, rest, gcol, grow = _inproj(x2, mod3, norm_mix.reshape(1, d), w_new, wgt, cw, gpc, gpr, seq, tm_in)
    grow3 = grow.reshape(16, t // CHUNK, CHUNK).transpose(1, 0, 2)

    rows = _pick_tile(seq, 256)
    ymix = _mixers(conv_out, rest, gcol, grow3, dn_norm.reshape(1, DN_DV), ml_norm.reshape(1, ML_HEADS * ML_DV),
                   batch, seq, rows)

    wo = w_out.reshape(-1, d).astype(BF16)
    tm_r = _pick_tile(seq, 512)
    brp = jnp.broadcast_to(b_router.reshape(N_EXPERTS, 1), (N_EXPERTS, HEAD_LANES))
    x1, h2, pos, wrow, len_col = _route(
        ymix, wo, x2, mod3, norm_ffn.reshape(1, d),
        w_router.reshape(d, N_EXPERTS).T, brp, seq, tm_r)

    n_e = N_EXPERTS
    nt = t // tm_r
    bm = 256
    len_te =len_col.reshape(nt, n_e, HEAD_LANES)[:, :, 0]
    seg_len = jnp.pad(len_te, ((0, 0), (0, HEAD_LANES - n_e)))
    n_slots_max = t * TOP_K + n_e * (7 * nt + bm)
    nb = (n_slots_max + bm - 1) // bm
    seg_src, seg_dst, table = _slots(seg_len, bm)

    xs = _dispatch(seg_len, seg_src, seg_dst, table, pos, h2, nb * bm, tm_r, bm)
    y = _experts(table, xs, w_gate_up.reshape(n_e, d, -1), b_gate_up.reshape(n_e, 1, -1),
                 w_down.reshape(n_e, -1, d), b_down.reshape(n_e, 1, d), bm)
    out = _combine(seg_len, seg_src, seg_dst, pos, wrow, y, x1, mod3, norm_final.reshape(1, d), modf3, seq, tm_r)
    return out.reshape(batch, seq, d)
```

```python
import functools

import jax
import jax.numpy as jnp
from jax import lax
from jax.experimental import pallas as pl
from jax.experimental.pallas import tpu as pltpu

F32 = jnp.float32
BF16 = jnp.bfloat16

CHUNK = 64
CONV_W = 4
EPS = 1e-6

DN_HEADS = 4
DN_DK = 128
DN_DV = 128
ML_HEADS = 4
ML_DK = 64
ML_DV = 128
HEAD_LANES = 128

N_EXPERTS = 32
TOP_K = 4
SWIGLU_LIMIT = 7.0
SWIGLU_ALPHA = 1.702

C_DNQ = 0
C_DNK = 512
C_DNV = 1024
C_MLQ = 1536
C_MLK = 1792
N_CONV = 2048
C_DNZ = 2048
C_MLV = 2560
C_MLO = 3072
C_GATE = 3584
N_PROJ = 3712
N_REST = C_GATE - N_CONV

V7X_VMEM_LIMIT = 56 * 1024 * 1024
MXU_COLS = 256
COMMON_SEGMENT_ROWS = 128
INPROJ_GROUP_COLS = 512

NEG_BIG = -1e30


def _sigmoid(x):
    return 1.0 / (1.0 + jnp.exp(-x))


def _softplus(x):
    return jnp.maximum(x, 0.0) + jnp.log(1.0 + jnp.exp(-jnp.abs(x)))


def _split3(v):
    hi = v.astype(BF16)
    r1 = v - hi.astype(F32)
    mid = r1.astype(BF16)
    lo = (r1 - mid.astype(F32)).astype(BF16)
    return hi, mid, lo


def _dot(a, b):
    return jnp.dot(a, b, preferred_element_type=F32)


def _dot_nt(a, b):
    return lax.dot_general(a, b, (((1,), (1,)), ((), ())), preferred_element_type=F32)


def _dot_tn(a, b):
    return lax.dot_general(a, b, (((0,), (0,)), ((), ())), preferred_element_type=F32)


def _dot_exact_right(sel_bf16, v):
    hi, mid, lo = _split3(v)
    return _dot(sel_bf16, hi) + _dot(sel_bf16, mid) + _dot(sel_bf16, lo)


def _dot_exact_left(v, sel_bf16):
    hi, mid, lo = _split3(v)
    return _dot(hi, sel_bf16) + _dot(mid, sel_bf16) + _dot(lo, sel_bf16)


def _mods_kernel(c_ref, w_ref, b_ref, o_ref):
    c = c_ref[...]
    cond = c * _sigmoid(c)
    ch, cm, _ = _split3(cond)
    wh, wm, _ = _split3(w_ref[...])
    acc = _dot(ch, wh) + (_dot(ch, wm) + _dot(cm, wh))
    o_ref[...] = acc + b_ref[...]


def _mods(c_pad, w, b):
    m, d = c_pad.shape
    n = w.shape[1]
    tn = 1024
    return pl.pallas_call(
        _mods_kernel,
        grid=(n // tn,),
        in_specs=[pl.BlockSpec((m, d), lambda j: (0, 0)),
                  pl.BlockSpec((d, tn), lambda j: (0, j)),
                  pl.BlockSpec((1, tn), lambda j: (0, j))],
        out_specs=pl.BlockSpec((m, tn), lambda j: (0, j)),
        out_shape=jax.ShapeDtypeStruct((m, n), F32),
        compiler_params=pltpu.CompilerParams(dimension_semantics=("arbitrary",),
                                             vmem_limit_bytes=V7X_VMEM_LIMIT),
        name="mods",
    )(c_pad, w, b)


def _gate_transform(v, bias, alog, cls):
    vb = v + bias
    beta = _sigmoid(v)
    g = -jnp.exp(alog) * _softplus(vb)
    logf = -_softplus(-vb)
    return jnp.where(cls == 0, beta, jnp.where(cls == 1, g, jnp.where(cls == 2, vb, jnp.where(cls == 3, logf, 0.0))))


def _inproj_kernel(tiles_per_seq, x_ref, sh_ref, sc_ref, nw_ref, w_ref, wgt_ref, cw_ref, gpc_ref, gpr_ref,
                   conv_ref, rest_ref, gcol_ref, grow_ref, cbuf):
    tm = x_ref.shape[0]
    i = pl.program_id(0)
    x = x_ref[...]
    ms = jnp.mean(x * x, axis=-1, keepdims=True)
    h = x * lax.rsqrt(ms + EPS) * nw_ref[...]
    h = h * (1.0 + sc_ref[...]) + sh_ref[...]
    hb = h.astype(BF16)

    @pl.when(i % tiles_per_seq == 0)
    def _():
        cbuf[0:8, :] = jnp.zeros((8, N_CONV), F32)

    group = INPROJ_GROUP_COLS
    for lo in range(0, N_CONV, group):
        cols = slice(lo, lo + group)
        pc = _dot(hb, w_ref[:, cols])
        cbuf[8:tm + 8, cols] = pc
        acc = cw_ref[CONV_W - 1:CONV_W, cols] * pc
        for j in range(CONV_W - 1):
            acc = acc + cw_ref[j:j + 1, cols] * cbuf[8 - (CONV_W - 1) + j:8 - (CONV_W - 1) + j + tm, cols]
        cbuf[0:8, cols] = cbuf[tm:tm + 8, cols]
        y = acc * _sigmoid(acc)
        if lo < C_DNV:
            for h0 in range(0, group, HEAD_LANES):
                uh = y[:, h0:h0 + HEAD_LANES]
                un = uh * lax.rsqrt(jnp.sum(uh * uh, axis=-1, keepdims=True) + EPS)
                conv_ref[:, lo + h0:lo + h0 + HEAD_LANES] = un * (DN_DK ** -0.5) if lo < C_DNK else un
        else:
            for h0 in range(0, group, HEAD_LANES):
                piece = y[:, h0:h0 + HEAD_LANES]
                is_mlq = C_MLQ <= lo + h0 < C_MLK
                conv_ref[:, lo + h0:lo + h0 + HEAD_LANES] = piece * (ML_DK ** -0.5) if is_mlq else piece

    z = _dot(hb, w_ref[:, C_DNZ:C_MLV])
    rest_ref[:, 0:512] = z * _sigmoid(z)
    rest_ref[:, 512:1024] = _dot(hb, w_ref[:, C_MLV:C_MLO])
    rest_ref[:, 1024:1536] = _sigmoid(_dot(hb, w_ref[:, C_MLO:C_GATE]))

    r_i = lax.broadcasted_iota(jnp.int32, (tm, tm), 0)
    c_i = lax.broadcasted_iota(jnp.int32, (tm, tm), 1)
    same_chunk = (r_i // CHUNK) == (c_i // CHUNK)
    tril = jnp.where(same_chunk & (c_i <= r_i), 1.0, 0.0).astype(BF16)
    triu = jnp.where(same_chunk & (r_i <= c_i), 1.0, 0.0).astype(BF16)

    gc = _dot(hb, w_ref[:, C_GATE:N_PROJ])
    cls_c = lax.broadcasted_iota(jnp.int32, (tm, HEAD_LANES), 1) // 4
    gt = _gate_transform(gc, gpc_ref[0:1, :], gpc_ref[1:2, :], cls_c)
    cs = _dot_exact_right(tril, gt)
    gcol_ref[...] = jnp.where((cls_c == 1) | (cls_c == 3), cs, gt)

    gr = _dot_nt(wgt_ref[...], hb)
    cls_r = lax.broadcasted_iota(jnp.int32, (16, tm), 0) // 4
    gtr = _gate_transform(gr, gpr_ref[:, 0:1], gpr_ref[:, 1:2], cls_r)
    csr = _dot_exact_left(gtr, triu)
    grow_ref[...] = jnp.where((cls_r == 1) | (cls_r == 3), csr, gtr)


def _mod_spec(tiles_per_seq, d, j):
    return pl.BlockSpec((None, 1, d), lambda i: (i // tiles_per_seq, 0, j))


def _inproj(x2, mod3, nw, w_new, wgt, cw, gpc, gpr, seq, tm):
    t, d = x2.shape
    tps = seq // tm
    kern = functools.partial(_inproj_kernel, tps)
    return pl.pallas_call(
        kern,
        grid=(t // tm,),
        in_specs=[pl.BlockSpec((tm, d), lambda i: (i, 0)),
                  _mod_spec(tps, d, 0),
                  _mod_spec(tps, d, 1),
                  pl.BlockSpec((1, d), lambda i: (0, 0)),
                  pl.BlockSpec((d, N_PROJ), lambda i: (0, 0)),
                  pl.BlockSpec((16, d), lambda i: (0, 0)),
                  pl.BlockSpec((CONV_W, N_CONV), lambda i: (0, 0)),
                  pl.BlockSpec((8, HEAD_LANES), lambda i: (0, 0)),
                  pl.BlockSpec((16, HEAD_LANES), lambda i: (0, 0))],
        out_specs=[pl.BlockSpec((tm, N_CONV), lambda i: (i, 0)),
                   pl.BlockSpec((tm, N_REST), lambda i: (i, 0)),
                   pl.BlockSpec((tm, HEAD_LANES), lambda i: (i, 0)),
                   pl.BlockSpec((16, tm), lambda i: (0, i))],
        out_shape=[jax.ShapeDtypeStruct((t, N_CONV), F32),
                   jax.ShapeDtypeStruct((t, N_REST), F32),
                   jax.ShapeDtypeStruct((t, HEAD_LANES), F32),
                   jax.ShapeDtypeStruct((16, t), F32)],
        scratch_shapes=[pltpu.VMEM((tm + 8, N_CONV), F32)],
        compiler_params=pltpu.CompilerParams(dimension_semantics=("arbitrary",),
                                             vmem_limit_bytes=V7X_VMEM_LIMIT),
        name="inproj",
    )(x2, mod3, mod3, nw, w_new, wgt, cw, gpc, gpr)


def _chunk_masks():
    r = lax.broadcasted_iota(jnp.int32, (CHUNK, CHUNK), 0)
    c = lax.broadcasted_iota(jnp.int32, (CHUNK, CHUNK), 1)
    return r >= c, r > c, r == c


def _bdot(a, b):
    return lax.dot_general(a, b, (((2,), (1,)), ((0,), (0,))), preferred_element_type=F32)


def _bdot_nt(a, b):
    return lax.dot_general(a, b, (((2,), (2,)), ((0,), (0,))), preferred_element_type=F32)


def _unit_lower_inverse(lower, row, col):
    x = jnp.where(row == col, 1.0, 0.0) - jnp.where((row >> 1) == (col >> 1), lower, 0.0)
    shift = 1
    while (1 << shift) < CHUNK:
        couple = ((row >> (shift + 1)) == (col >> (shift + 1))) & ((row >> shift) != (col >> shift))
        cb = jnp.where(couple, lower, 0.0).astype(BF16)
        xb = x.astype(BF16)
        x = x - _bdot(_bdot(xb, cb).astype(BF16), xb)
        shift += 1
        yield
    return x


def _seq_heads(n_seq, n_heads):
    return [(s, hh) for s in range(n_seq) for hh in range(n_heads)]


def _deltanet_steps(q_ref, k_ref, v_ref, gc_ref, gr_ref, z_ref, nw_ref, o_ref, s_ref):
    n_seq, nc = gr_ref.shape[0], gr_ref.shape[1]

    @pl.when(pl.program_id(1) == 0)
    def _():
        s_ref[...] = jnp.zeros(s_ref.shape, F32)

    row = lax.broadcasted_iota(jnp.int32, (CHUNK, CHUNK), 0)
    col = lax.broadcasted_iota(jnp.int32, (CHUNK, CHUNK), 1)
    incl = row >= col
    strict = row > col
    nw = nw_ref[...]
    gcc = gc_ref[...]
    grr = gr_ref[...]

    streams = _seq_heads(n_seq, DN_HEADS)
    nh = len(streams)
    nb = nh * nc

    def heads(ref):
        return jnp.stack([ref[s, :, hh * HEAD_LANES:(hh + 1) * HEAD_LANES] for s, hh in streams],
                         axis=0).reshape(nb, CHUNK, HEAD_LANES)

    def col_gate(lane0):
        return jnp.stack([gcc[s, :, lane0 + hh:lane0 + hh + 1] for s, hh in streams], axis=0).reshape(nb, CHUNK, 1)

    q = heads(q_ref)
    k = heads(k_ref)
    v = heads(v_ref)
    beta = col_gate(0)
    g_c = col_gate(4)
    g_r = jnp.stack([grr[s, :, 4 + hh:5 + hh, :] for s, hh in streams], axis=0).reshape(nb, 1, CHUNK)
    g_last = g_c[:, CHUNK - 1:CHUNK, :]
    decay = jnp.exp(jnp.where(incl, g_c - g_r, NEG_BIG))
    kb = k.astype(BF16)
    kk = _bdot_nt(kb, kb)
    lower = jnp.where(strict, beta * kk * decay, 0.0)
    yield
    tinv = yield from _unit_lower_inverse(lower, row, col)
    eg = jnp.exp(g_c)
    rhs = jnp.concatenate([v * beta, k * (beta * eg)], axis=-1)
    sol = _bdot(tinv.astype(BF16), rhs.astype(BF16))
    yield
    w_val = sol[:, :, 0:DN_DV].reshape(nh, nc, CHUNK, DN_DV)
    kq = jnp.concatenate([sol[:, :, DN_DV:DN_DV + DN_DK], q * eg], axis=1).astype(BF16)
    kq = kq.reshape(nh, nc, 2 * CHUNK, DN_DK)
    qk = (_bdot_nt(q.astype(BF16), kb) * decay).astype(BF16).reshape(nh, nc, CHUNK, CHUNK)
    k_dec_t = jnp.swapaxes(k * jnp.exp(g_last - g_c), 1, 2).astype(BF16).reshape(nh, nc, DN_DK, CHUNK)
    s_dec = jnp.exp(g_last).reshape(nh, nc, 1, 1)
    yield

    state = s_ref[...]
    outs = []
    for c in range(nc):
        both = _bdot(kq[:, c], state.astype(BF16))
        v_new = w_val[:, c] - both[:, 0:CHUNK]
        vb = v_new.astype(BF16)
        outs.append(both[:, CHUNK:2 * CHUNK] + _bdot(qk[:, c], vb))
        state = s_dec[:, c] * state + _bdot(k_dec_t[:, c], vb)
        yield
    s_ref[...] = state

    o = jnp.stack(outs, axis=1)
    on = o * lax.rsqrt(jnp.mean(o * o, axis=-1, keepdims=True) + EPS) * nw
    on = on.reshape(nh, nc * CHUNK, DN_DV)
    for idx, (s, hh) in enumerate(streams):
        lanes = slice(hh * HEAD_LANES, (hh + 1) * HEAD_LANES)
        o_ref[s, :, lanes] = on[idx] * z_ref[s, :, lanes]


def _mlstm_steps(q_ref, k_ref, v_ref, gc_ref, gr_ref, og_ref, nw_ref, o_ref, c_ref, n_ref, m_ref):
    n_seq, nc = gr_ref.shape[0], gr_ref.shape[1]

    @pl.when(pl.program_id(1) == 0)
    def _():
        c_ref[...] = jnp.zeros(c_ref.shape, F32)
        n_ref[...] = jnp.zeros(n_ref.shape, F32)
        m_ref[...] = jnp.zeros(m_ref.shape, F32)

    incl, _, _ = _chunk_masks()
    gcc = gc_ref[...]
    grr = gr_ref[...]

    streams = _seq_heads(n_seq, ML_HEADS)
    nh = len(streams)

    def heads(ref):
        return jnp.stack([ref[s, :, hh * HEAD_LANES:(hh + 1) * HEAD_LANES] for s, hh in streams],
                         axis=0).reshape(nh, nc, CHUNK, HEAD_LANES)

    def col_gate(lane0):
        return jnp.stack([gcc[s, :, lane0 + hh:lane0 + hh + 1] for s, hh in streams],
                         axis=0).reshape(nh, nc, CHUNK, 1)

    def row_gate(row0):
        return jnp.stack([grr[s, :, row0 + hh:row0 + hh + 1, :] for s, hh in streams], axis=0)

    def qk_heads(ref):
        lane = lax.broadcasted_iota(jnp.int32, (ref.shape[1], HEAD_LANES), 1)
        tiles = []
        for s, hh in streams:
            tile = ref[s, :, (hh // 2) * HEAD_LANES:(hh // 2 + 1) * HEAD_LANES]
            mine = (lane >= ML_DK) if hh % 2 else (lane < ML_DK)
            tiles.append(jnp.where(mine, tile, 0.0))
        return jnp.stack(tiles, axis=0).reshape(nh, nc, CHUNK, HEAD_LANES)

    q = qk_heads(q_ref)
    k = qk_heads(k_ref)
    v = heads(v_ref)
    i_c = col_gate(8)
    b_c = col_gate(12)
    i_r = row_gate(8)
    b_r = row_gate(12)
    b_last = b_c[:, :, CHUNK - 1:CHUNK, :]
    d_mat = jnp.where(incl, b_c - b_r + i_r, NEG_BIG)
    m_intra = jnp.max(d_mat, axis=-1, keepdims=True)
    g_end = b_last - b_c + i_c
    g_end_max = jnp.max(g_end, axis=2, keepdims=True)
    yield

    m_run = m_ref[:, 0:1, 0:1].reshape(nh, 1, 1, 1)
    m_before = []
    for c in range(nc):
        m_before.append(m_run)
        m_run = jnp.maximum(b_last[:, c:c + 1] + m_run, g_end_max[:, c:c + 1])
    m_s = jnp.concatenate(m_before, axis=1)
    m_new = jnp.maximum(b_last + m_s, g_end_max)
    keep = jnp.exp(b_last + m_s - m_new)
    yield

    nb = nh * nc
    qb = q.astype(BF16)
    kb = k.astype(BF16)
    vb = v.astype(BF16)
    m_t = jnp.maximum(b_c + m_s, m_intra)
    inter = jnp.exp(b_c + m_s - m_t)
    qk = _bdot_nt(qb.reshape(nb, CHUNK, HEAD_LANES), kb.reshape(nb, CHUNK, HEAD_LANES))
    p = jnp.exp(d_mat - m_t) * qk.reshape(nh, nc, CHUNK, CHUNK)
    yield
    intra = _bdot(p.astype(BF16).reshape(nb, CHUNK, CHUNK), vb.reshape(nb, CHUNK, ML_DV)).reshape(nh, nc, CHUNK, ML_DV)
    p_sum = jnp.sum(p, axis=-1, keepdims=True)
    yield
    kw = k * jnp.exp(g_end - m_new)
    kw_t = jnp.swapaxes(kw.reshape(nb, CHUNK, HEAD_LANES), 1, 2).astype(BF16)
    d_state = _bdot(kw_t, vb.reshape(nb, CHUNK, ML_DV)).reshape(nh, nc, HEAD_LANES, ML_DV)
    kw_sum = jnp.sum(kw, axis=2, keepdims=True)
    yield

    c_s = c_ref[...]
    n_s = n_ref[:, 0:1, :]
    q_c = []
    q_n = []
    for c in range(nc):
        q_c.append(_bdot(qb[:, c], c_s.astype(BF16)))
        q_n.append(jnp.sum(q[:, c] * n_s, axis=-1, keepdims=True))
        c_s = keep[:, c] * c_s + d_state[:, c]
        n_s = keep[:, c] * n_s + kw_sum[:, c]
        yield
    c_ref[...] = c_s
    n_ref[...] = jnp.broadcast_to(n_s, (nh, 8, HEAD_LANES))
    m_ref[...] = jnp.broadcast_to(m_run.reshape(nh, 1, 1), (nh, 8, HEAD_LANES))

    num = inter * jnp.stack(q_c, axis=1) + intra
    den = inter * jnp.stack(q_n, axis=1) + p_sum
    h = num / jnp.maximum(jnp.abs(den), jnp.exp(-m_t))
    hr = h * lax.rsqrt(jnp.mean(h * h, axis=-1, keepdims=True) + EPS)
    hr = hr.reshape(nh, nc * CHUNK, ML_DV)
    for idx, (s, hh) in enumerate(streams):
        lanes = slice(hh * HEAD_LANES, (hh + 1) * HEAD_LANES)
        o_ref[s, :, lanes] = hr[idx] * nw_ref[:, lanes] * og_ref[s, :, lanes]


def _mixers_kernel(dq_ref, dk_ref, dv_ref, gc_ref, gr_ref, z_ref, dnw_ref, mq_ref, mk_ref, mv_ref, og_ref, mnw_ref,
                   o_ref, s_ref, c_ref, n_ref, m_ref):
    n_a = DN_HEADS * HEAD_LANES
    n_b = ML_HEADS * HEAD_LANES
    stages = [_deltanet_steps(dq_ref, dk_ref, dv_ref, gc_ref, gr_ref, z_ref, dnw_ref, o_ref.at[:, :, 0:n_a], s_ref),
              _mlstm_steps(mq_ref, mk_ref, mv_ref, gc_ref, gr_ref, og_ref, mnw_ref, o_ref.at[:, :, n_a:n_a + n_b],
                           c_ref, n_ref, m_ref)]
    while stages:
        for stage in list(stages):
            try:
                next(stage)
            except StopIteration:
                stages.remove(stage)


def _mixers(conv_out, rest, gcol, grow3, dn_nw, ml_nw, batch, seq, rows):
    t = conv_out.shape[0]
    nj = seq // rows
    cpb = rows // CHUNK
    width = 4 * HEAD_LANES
    assert DN_HEADS * HEAD_LANES == width and ML_HEADS * HEAD_LANES == width
    n_seq = 2 if batch % 2 == 0 else 1
    conv3 = conv_out.reshape(batch, seq, -1)
    rest3 = rest.reshape(batch, seq, -1)
    gcol3 = gcol.reshape(batch, seq, -1)
    grow4 = grow3.reshape(batch, seq // CHUNK, 16, CHUNK)

    def conv_block(c0):
        return pl.BlockSpec((n_seq, rows, width), lambda b, j: (b, j, c0 // width))

    def rest_block(c0):
        return pl.BlockSpec((n_seq, rows, width), lambda b, j: (b, j, (c0 - N_CONV) // width))

    ml_qk_width = ML_HEADS * ML_DK

    def ml_qk_block(c0):
        return pl.BlockSpec((n_seq, rows, ml_qk_width), lambda b, j: (b, j, c0 // ml_qk_width))

    gates = [pl.BlockSpec((n_seq, rows, HEAD_LANES), lambda b, j: (b, j, 0)),
             pl.BlockSpec((n_seq, cpb, 16, CHUNK), lambda b, j: (b, j, 0, 0))]
    out = pl.pallas_call(
        _mixers_kernel,
        grid=(batch // n_seq, nj),
        in_specs=[conv_block(C_DNQ), conv_block(C_DNK), conv_block(C_DNV)] + gates
                 + [rest_block(C_DNZ), pl.BlockSpec((1, HEAD_LANES), lambda b, j: (0, 0)),
                    ml_qk_block(C_MLQ), ml_qk_block(C_MLK), rest_block(C_MLV), rest_block(C_MLO),
                    pl.BlockSpec((1, width), lambda b, j: (0, 0))],
        out_specs=pl.BlockSpec((n_seq, rows, 2 * width), lambda b, j: (b, j, 0)),
        out_shape=jax.ShapeDtypeStruct((batch, seq, 2 * width), F32),
        scratch_shapes=[pltpu.VMEM((n_seq * DN_HEADS, DN_DK, DN_DV), F32),
                        pltpu.VMEM((n_seq * ML_HEADS, HEAD_LANES, ML_DV), F32),
                        pltpu.VMEM((n_seq * ML_HEADS, 8, HEAD_LANES), F32),
                        pltpu.VMEM((n_seq * ML_HEADS, 8, HEAD_LANES), F32)],
        compiler_params=pltpu.CompilerParams(dimension_semantics=("arbitrary", "arbitrary"),
                                             vmem_limit_bytes=V7X_VMEM_LIMIT),
        name="mixers",
    )(conv3, conv3, conv3, gcol3, grow4, rest3, dn_nw, conv3, conv3, rest3, rest3, ml_nw)
    return out.reshape(t, 2 * width)


def _route_kernel(y_ref, wo_ref, x_ref, g1_ref, sh_ref, sc_ref, nw_ref, wrt_ref, br_ref,
                  x1_ref, h2_ref, pos_ref, wrow_ref, len_ref):
    tm = x_ref.shape[0]

    mix = _dot(y_ref[...].astype(BF16), wo_ref[...])
    x1 = x_ref[...] + g1_ref[...] * mix
    x1_ref[...] = x1
    ms = jnp.mean(x1 * x1, axis=-1, keepdims=True)
    h2 = x1 * lax.rsqrt(ms + EPS) * nw_ref[...]
    h2 = h2 * (1.0 + sc_ref[...]) + sh_ref[...]
    h2_ref[...] = h2.astype(BF16)

    hh, hm, _ = _split3(h2)
    wh, wm, _ = _split3(wrt_ref[...])
    logits = _dot_nt(wh, hh) + (_dot_nt(wh, hm) + _dot_nt(wm, hh)) + br_ref[:, 0:1]

    e_i = lax.broadcasted_iota(jnp.int32, (N_EXPERTS, tm), 0)
    work = logits
    tops = []
    sels = []
    hots = []
    for _ in range(TOP_K):
        m = jnp.max(work, axis=0, keepdims=True)
        sel = jnp.min(jnp.where(work == m, e_i, N_EXPERTS), axis=0, keepdims=True)
        hot = e_i == sel
        work = jnp.where(hot, NEG_BIG, work)
        tops.append(m)
        sels.append(sel)
        hots.append(hot)
    exps = [jnp.exp(tl - tops[0]) for tl in tops]
    denom = exps[0] + exps[1] + exps[2] + exps[3]
    ws = [e / denom for e in exps]

    chosen = jnp.zeros((N_EXPERTS, tm), F32)
    for hot in hots:
        chosen = chosen + jnp.where(hot, 1.0, 0.0)

    r_i = lax.broadcasted_iota(jnp.int32, (tm, tm), 0)
    c_i = lax.broadcasted_iota(jnp.int32, (tm, tm), 1)
    strict_upper = jnp.where(r_i < c_i, 1.0, 0.0).astype(BF16)
    prefix = _dot(chosen.astype(BF16), strict_upper)
    n_e = jnp.sum(chosen, axis=1, keepdims=True)
    len8 = jnp.ceil(n_e * 0.125) * 8.0
    er = lax.broadcasted_iota(jnp.int32, (N_EXPERTS, N_EXPERTS), 0)
    ec = lax.broadcasted_iota(jnp.int32, (N_EXPERTS, N_EXPERTS), 1)
    strict_lower = jnp.where(ec < er, 1.0, 0.0).astype(BF16)
    len8b = jnp.broadcast_to(len8, (N_EXPERTS, HEAD_LANES))
    off8 = _dot_exact_right(strict_lower, len8b)[:, 0:1]
    len_ref[...] = len8b.astype(jnp.int32)

    for kk in range(TOP_K):
        wrow_ref[kk:kk + 1, :] = ws[kk]
        pos = jnp.sum(jnp.where(hots[kk], prefix + off8, 0.0), axis=0, keepdims=True)
        pos_ref[kk:kk + 1, :] = pos.astype(jnp.int32)


def _route(ymix, wo, x2, mod3, nw, wrt, br, seq, tm):
    t, d = x2.shape
    tps = seq // tm
    return pl.pallas_call(
        _route_kernel,
        grid=(t // tm,),
        in_specs=[pl.BlockSpec((tm, ymix.shape[1]), lambda i: (i, 0)),
                  pl.BlockSpec(wo.shape, lambda i: (0, 0)),
                  pl.BlockSpec((tm, d), lambda i: (i, 0)),
                  _mod_spec(tps, d, 2),
                  _mod_spec(tps, d, 3),
                  _mod_spec(tps, d, 4),
                  pl.BlockSpec((1, d), lambda i: (0, 0)),
                  pl.BlockSpec((N_EXPERTS, d), lambda i: (0, 0)),
                  pl.BlockSpec((N_EXPERTS, HEAD_LANES), lambda i: (0, 0))],
        out_specs=[pl.BlockSpec((tm, d), lambda i: (i, 0)),
                   pl.BlockSpec((tm, d), lambda i: (i, 0)),
                   pl.BlockSpec((TOP_K, tm), lambda i: (0, i)),
                   pl.BlockSpec((TOP_K, tm), lambda i: (0, i)),
                   pl.BlockSpec((N_EXPERTS, HEAD_LANES), lambda i: (i, 0))],
        out_shape=[jax.ShapeDtypeStruct((t, d), F32),
                   jax.ShapeDtypeStruct((t, d), BF16),
                   jax.ShapeDtypeStruct((TOP_K, t), jnp.int32),
                   jax.ShapeDtypeStruct((TOP_K, t), F32),
                   jax.ShapeDtypeStruct((t // tm * N_EXPERTS, HEAD_LANES), jnp.int32)],
        compiler_params=pltpu.CompilerParams(dimension_semantics=("arbitrary",),
                                             vmem_limit_bytes=V7X_VMEM_LIMIT),
        name="route",
    )(ymix, wo, x2, mod3, mod3, mod3, nw, wrt, br)


TAIL_START, TAIL_LEN, PAD_END, BLOCK_START, BLOCK_COUNT = range(5)


def _slots_kernel(bm, len_te_ref, ssrc_ref, sdst_ref, tab_ref):
    nt, lanes = len_te_ref.shape
    len_te = len_te_ref[...].astype(F32)

    r_l = lax.broadcasted_iota(jnp.int32, (lanes, lanes), 0)
    c_l = lax.broadcasted_iota(jnp.int32, (lanes, lanes), 1)
    upper_incl = jnp.where(r_l <= c_l, 1.0, 0.0).astype(BF16)
    upper_strict = jnp.where(r_l < c_l, 1.0, 0.0).astype(BF16)
    total_r = jnp.sum(len_te, axis=0, keepdims=True)
    padded_r = jnp.ceil(total_r * (1.0 / bm)) * bm
    pad_end_r = _dot_exact_left(jnp.broadcast_to(padded_r, (8, lanes)), upper_incl)[0:1, :]
    pad_start_r = pad_end_r - padded_r
    r_t = lax.broadcasted_iota(jnp.int32, (nt, nt), 0)
    c_t = lax.broadcasted_iota(jnp.int32, (nt, nt), 1)
    lower_strict_t = jnp.where(c_t < r_t, 1.0, 0.0).astype(BF16)
    before = _dot_exact_right(lower_strict_t, len_te)
    ssrc_ref[...] = _dot_exact_left(len_te, upper_strict).astype(jnp.int32)
    sdst_ref[...] = (pad_start_r + before).astype(jnp.int32)
    tab_ref[TAIL_START:TAIL_START + 1, :] = (pad_start_r + total_r).astype(jnp.int32)
    tab_ref[TAIL_LEN:TAIL_LEN + 1, :] = (padded_r - total_r).astype(jnp.int32)
    tab_ref[PAD_END:PAD_END + 1, :] = pad_end_r.astype(jnp.int32)
    tab_ref[BLOCK_START:BLOCK_START + 1, :] = (pad_start_r * (1.0 / bm)).astype(jnp.int32)
    tab_ref[BLOCK_COUNT:BLOCK_COUNT + 1, :] = (padded_r * (1.0 / bm)).astype(jnp.int32)
    tab_ref[5:8, :] = jnp.zeros((3, lanes), jnp.int32)


def _slots(len_te, bm):
    assert bm & (bm - 1) == 0, "block rows must be a power of two"
    nt, lanes = len_te.shape
    return pl.pallas_call(
        functools.partial(_slots_kernel, bm),
        out_shape=[jax.ShapeDtypeStruct((nt, lanes), jnp.int32),
                   jax.ShapeDtypeStruct((nt, lanes), jnp.int32),
                   jax.ShapeDtypeStruct((8, lanes), jnp.int32)],
        compiler_params=pltpu.CompilerParams(vmem_limit_bytes=V7X_VMEM_LIMIT),
        name="slots",
    )(len_te)


def _segment_pieces(max_rows):
    sizes = []
    s = 8
    while s <= max_rows:
        sizes.append(s)
        s *= 2
    return sizes[::-1]


def _segment_dma(src_ref, src0, dst_ref, dst0, nrows, sizes, sem, start):
    def pieces(group, off):
        for sz in group:
            bit = nrows & sz

            @pl.when(bit != 0)
            def _(off=off, sz=sz):
                s0 = pl.multiple_of(src0 + off, 8)
                d0 = pl.multiple_of(dst0 + off, 8)
                cp = pltpu.make_async_copy(src_ref.at[pl.ds(s0, sz)], dst_ref.at[pl.ds(d0, sz)], sem)
                if start:
                    cp.start()
                else:
                    cp.wait()

            off = off + bit

    large = [sz for sz in sizes if sz >= COMMON_SEGMENT_ROWS]
    small = [sz for sz in sizes if sz < COMMON_SEGMENT_ROWS]
    if large and small:
        @pl.when(nrows >= COMMON_SEGMENT_ROWS)
        def _():
            pieces(large, 0)

        pieces(small, nrows & ~(COMMON_SEGMENT_ROWS - 1))
    else:
        pieces(sizes, 0)


def _sorted_onehot(pos_ref, row0, nrows, tt):
    r_i = lax.broadcasted_iota(jnp.int32, (nrows, tt), 0) + row0
    hit = r_i == pos_ref[0:1, :]
    for kk in range(1, TOP_K):
        hit = hit | (r_i == pos_ref[kk:kk + 1, :])
    return hit


def _row_groups(nrows, n_groups):
    size = -(-nrows // n_groups // 8) * 8
    return [(r0, min(size, nrows - r0)) for r0 in range(0, nrows, size)]


def _dispatch_kernel(bm, len_ref, ssrc_ref, sdst_ref, tail_ref, pos_ref, h2_ref, xs_hbm, buf, sem):
    i = pl.program_id(0)
    nt = pl.num_programs(0)
    tt = h2_ref.shape[0]
    nrows = buf.shape[1]
    sizes = _segment_pieces(tt)
    slot = i % 2

    h2 = h2_ref[...]
    for r0, rn in _row_groups(nrows, 4):
        perm = jnp.where(_sorted_onehot(pos_ref, r0, rn, tt), 1.0, 0.0).astype(BF16)
        buf[slot, r0:r0 + rn, :] = _dot(perm, h2)

    def seg_start(tile, which):
        def body(e, carry):
            _segment_dma(buf.at[which], ssrc_ref[tile, e], xs_hbm, sdst_ref[tile, e], len_ref[tile, e], sizes,
                         sem.at[which], True)
            return carry
        lax.fori_loop(0, N_EXPERTS, body, 0)

    def seg_wait(tile, which):
        total = ssrc_ref[tile, N_EXPERTS - 1] + len_ref[tile, N_EXPERTS - 1]
        _segment_dma(buf.at[which], 0, xs_hbm, 0, total, _segment_pieces(nrows), sem.at[which], False)

    @pl.when(i > 0)
    def _():
        seg_wait(i - 1, 1 - slot)

    seg_start(i, slot)

    @pl.when(i == nt - 1)
    def _():
        seg_wait(i, slot)
        zrows = bm
        zbuf = buf.at[0]
        zsem = sem.at[0]
        buf[0, 0:zrows, :] = jnp.zeros((zrows, buf.shape[2]), F32)

        def tail(start):
            def body(e, carry):
                _segment_dma(zbuf, 0, xs_hbm, tail_ref[TAIL_START, e], tail_ref[TAIL_LEN, e],
                             _segment_pieces(zrows // 2), zsem, start)
                return carry
            lax.fori_loop(0, N_EXPERTS, body, 0)

        tail(True)
        tail(False)

        used = tail_ref[PAD_END, tail_ref.shape[1] - 1]
        n_unused = (xs_hbm.shape[0] - used) // zrows

        def unused_copy(j):
            d0 = pl.multiple_of(used + j * zrows, 8)
            return pltpu.make_async_copy(zbuf.at[pl.ds(0, zrows)], xs_hbm.at[pl.ds(d0, zrows)], zsem)

        def unused_start(j, carry):
            unused_copy(j).start()
            return carry

        def unused_wait(j, carry):
            unused_copy(j).wait()
            return carry

        lax.fori_loop(0, n_unused, unused_start, 0)
        lax.fori_loop(0, n_unused, unused_wait, 0)


def _dispatch(seg_len, seg_src, seg_dst, tail, pos, h2, n_slots, tt, bm):
    t, d = h2.shape
    nrows = TOP_K * tt + 8 * N_EXPERTS
    assert bm <= nrows
    smem = pl.BlockSpec(memory_space=pltpu.SMEM)
    return pl.pallas_call(
        functools.partial(_dispatch_kernel, bm),
        grid=(t // tt,),
        in_specs=[smem, smem, smem, smem,
                  pl.BlockSpec((TOP_K, tt), lambda i: (0, i)),
                  pl.BlockSpec((tt, d), lambda i: (i, 0))],
        out_specs=pl.BlockSpec(memory_space=pl.ANY),
        out_shape=jax.ShapeDtypeStruct((n_slots, d), F32),
        scratch_shapes=[pltpu.VMEM((2, nrows, d), F32), pltpu.SemaphoreType.DMA((2,))],
        compiler_params=pltpu.CompilerParams(dimension_semantics=("arbitrary",), has_side_effects=True,
                                             vmem_limit_bytes=V7X_VMEM_LIMIT),
        name="dispatch",
    )(seg_len, seg_src, seg_dst, tail, pos, h2)


def _experts_kernel(tab_ref, xs_hbm, wgu_hbm, bgu_ref, wd_hbm, bd_ref, y_hbm, wgu_s, wd_s, wgu_f, wd_f, xbuf, ybuf,
                    sem_in, sem_out, sem_w):
    e = pl.program_id(0)
    n_e = pl.num_programs(0)
    bm = xbuf.shape[1] // 2
    d_ff = wd_f.shape[1]
    ct = wgu_s.shape[2]
    wslot = e % 2

    def w_copies(expert, which):
        return (pltpu.make_async_copy(wgu_hbm.at[expert], wgu_f.at[which], sem_w.at[which]),
                pltpu.make_async_copy(wd_hbm.at[expert], wd_f.at[which], sem_w.at[which]))

    @pl.when(e == 0)
    def _():
        for cp in w_copies(0, 0):
            cp.start()

    for cp in w_copies(e, wslot):
        cp.wait()

    have_next = e + 1 < n_e

    def prefetch_next_weights(cond):
        @pl.when(have_next & cond)
        def _():
            for cp in w_copies(e + 1, 1 - wslot):
                cp.start()
    first = tab_ref[BLOCK_START, e]
    count = tab_ref[BLOCK_COUNT, e]
    npair = count // 2
    odd = count % 2
    tslot = npair % 2

    def pair_rows(j):
        return pl.ds(pl.multiple_of((first + 2 * j) * bm, bm), 2 * bm)

    def x_copy(j, slot):
        return pltpu.make_async_copy(xs_hbm.at[pair_rows(j)], xbuf.at[slot], sem_in.at[slot])

    def y_copy(j, slot):
        return pltpu.make_async_copy(ybuf.at[slot], y_hbm.at[pair_rows(j)], sem_out.at[slot])

    def last_rows():
        return pl.ds(pl.multiple_of((first + 2 * npair) * bm, bm), bm)

    def x_last(slot):
        return pltpu.make_async_copy(xs_hbm.at[last_rows()], xbuf.at[slot, pl.ds(0, bm)], sem_in.at[slot])

    def y_last(slot):
        return pltpu.make_async_copy(ybuf.at[slot, pl.ds(0, bm)], y_hbm.at[last_rows()], sem_out.at[slot])

    @pl.when(npair > 0)
    def _():
        x_copy(0, 0).start()

    @pl.when((npair == 0) & (odd == 1))
    def _():
        x_last(0).start()

    for t in range(wgu_s.shape[0]):
        wgu_s[t] = wgu_f[wslot, :, t * ct:(t + 1) * ct].astype(BF16)
    for t in range(wd_s.shape[0]):
        wd_s[t] = wd_f[wslot, :, t * ct:(t + 1) * ct].astype(BF16)

    def mlp(x):
        xb = x.astype(BF16)
        tpc = 2
        fc = tpc * ct
        nf = d_ff // fc
        n_out = wd_s.shape[0]

        def gate_up(f):
            gates, ups = [], []
            for t in range(f * tpc, (f + 1) * tpc):
                gates.append(_dot(xb, wgu_s[t]) + bgu_ref[:, t * ct:(t + 1) * ct])
                u = d_ff // ct + t
                ups.append(_dot(xb, wgu_s[u]) + bgu_ref[:, u * ct:(u + 1) * ct])
            return jnp.concatenate(gates, axis=-1), jnp.concatenate(ups, axis=-1)

        acts = []
        pre = gate_up(0)
        for f in range(nf):
            gate = jnp.minimum(pre[0], SWIGLU_LIMIT)
            up = jnp.clip(pre[1], -SWIGLU_LIMIT, SWIGLU_LIMIT)
            if f + 1 < nf:
                pre = gate_up(f + 1)
            acts.append(((up + 1.0) * gate * _sigmoid(SWIGLU_ALPHA * gate)).astype(BF16))
        act = jnp.concatenate(acts, axis=-1)
        return [_dot(act, wd_s[n]) + bd_ref[:, n * ct:(n + 1) * ct] for n in range(n_out)]

    def pair(j, carry):
        slot = j % 2
        x_copy(j, slot).wait()

        @pl.when(j + 1 < npair)
        def _():
            x_copy(j + 1, 1 - slot).start()

        @pl.when((j + 1 == npair) & (odd == 1))
        def _():
            x_last(1 - slot).start()

        @pl.when(j >= 2)
        def _():
            y_copy(j - 2, slot).wait()

        prefetch_next_weights(j == 0)
        for n, piece in enumerate(mlp(xbuf[slot])):
            ybuf[slot, :, n * ct:(n + 1) * ct] = piece
        y_copy(j, slot).start()
        return carry

    lax.fori_loop(0, npair, pair, 0)

    prefetch_next_weights(npair == 0)

    @pl.when(odd == 1)
    def _():
        x_last(tslot).wait()

        @pl.when(npair >= 2)
        def _():
            y_copy(npair - 2, tslot).wait()

        for n, piece in enumerate(mlp(xbuf[tslot, 0:bm, :])):
            ybuf[tslot, 0:bm, n * ct:(n + 1) * ct] = piece
        y_last(tslot).start()

    @pl.when((odd == 0) & (npair >= 2))
    def _():
        y_copy(npair - 2, tslot).wait()

    @pl.when(npair >= 1)
    def _():
        y_copy(npair - 1, 1 - tslot).wait()

    @pl.when(odd == 1)
    def _():
        y_last(tslot).wait()

    @pl.when(e == pl.num_programs(0) - 1)
    def _():
        used = first + count
        n_unused = y_hbm.shape[0] // bm - used
        ybuf[0, 0:bm, :] = jnp.zeros((bm, ybuf.shape[2]), F32)

        def z_copy(j):
            d0 = pl.multiple_of((used + j) * bm, bm)
            return pltpu.make_async_copy(ybuf.at[0, pl.ds(0, bm)], y_hbm.at[pl.ds(d0, bm)], sem_out.at[0])

        def z_start(j, carry):
            z_copy(j).start()
            return carry

        def z_wait(j, carry):
            z_copy(j).wait()
            return carry

        lax.fori_loop(0, n_unused, z_start, 0)
        lax.fori_loop(0, n_unused, z_wait, 0)


def _experts(table, xs, wgu, bgu, wd, bd, bm):
    ns, d = xs.shape
    n_e, _, two_ff = wgu.shape
    d_ff = two_ff // 2
    wmap = lambda e: (e, 0, 0)
    return pl.pallas_call(
        _experts_kernel,
        grid=(n_e,),
        in_specs=[pl.BlockSpec(memory_space=pltpu.SMEM),
                  pl.BlockSpec(memory_space=pl.ANY),
                  pl.BlockSpec(memory_space=pl.ANY),
                  pl.BlockSpec((None, 1, two_ff), wmap),
                  pl.BlockSpec(memory_space=pl.ANY),
                  pl.BlockSpec((None, 1, d), wmap)],
        out_specs=pl.BlockSpec(memory_space=pl.ANY),
        out_shape=jax.ShapeDtypeStruct((ns, d), F32),
        scratch_shapes=[pltpu.VMEM((two_ff // MXU_COLS, d, MXU_COLS), BF16),
                        pltpu.VMEM((d // MXU_COLS, d_ff, MXU_COLS), BF16),
                        pltpu.VMEM((2, d, two_ff), F32), pltpu.VMEM((2, d_ff, d), F32),
                        pltpu.VMEM((2, 2 * bm, d), F32), pltpu.VMEM((2, 2 * bm, d), F32),
                        pltpu.SemaphoreType.DMA((2,)), pltpu.SemaphoreType.DMA((2,)),
                        pltpu.SemaphoreType.DMA((2,))],
        compiler_params=pltpu.CompilerParams(dimension_semantics=("arbitrary",), has_side_effects=True,
                                             vmem_limit_bytes=V7X_VMEM_LIMIT),
        name="experts",
    )(table, xs, wgu, bgu, wd, bd)


def _combine_kernel(len_ref, ssrc_ref, sdst_ref, pos_ref, w_ref, y_hbm, x1_ref, g2_ref, nw_ref, sh_ref, sc_ref,
                    o_ref, ybuf, sem):
    i = pl.program_id(0)
    nt = pl.num_programs(0)
    tt = x1_ref.shape[0]
    nrows = ybuf.shape[1]
    sizes = _segment_pieces(tt)
    slot = i % 2

    def seg_start(tile, which):
        def body(e, carry):
            _segment_dma(y_hbm, sdst_ref[tile, e], ybuf.at[which], ssrc_ref[tile, e], len_ref[tile, e], sizes,
                         sem.at[which], True)
            return carry
        lax.fori_loop(0, N_EXPERTS, body, 0)

    def seg_wait(tile, which):
        total = ssrc_ref[tile, N_EXPERTS - 1] + len_ref[tile, N_EXPERTS - 1]
        _segment_dma(y_hbm, 0, ybuf.at[which], 0, total, _segment_pieces(nrows), sem.at[which], False)

    @pl.when(i == 0)
    def _():
        ybuf[...] = jnp.zeros(ybuf.shape, F32)
        seg_start(0, 0)

    @pl.when(i + 1 < nt)
    def _():
        seg_start(i + 1, 1 - slot)

    seg_wait(i, slot)

    acc = None
    for r0, rn in _row_groups(nrows, 3):
        r_i = lax.broadcasted_iota(jnp.int32, (rn, tt), 0) + r0
        wmat = 0.0
        for kk in range(TOP_K):
            wmat = jnp.where(r_i == pos_ref[kk:kk + 1, :], w_ref[kk:kk + 1, :], wmat)
        part = _dot_tn(wmat.astype(BF16), ybuf[slot, r0:r0 + rn, :].astype(BF16))
        acc = part if acc is None else acc + part
    xo = x1_ref[...] + g2_ref[...] * acc
    ms = jnp.mean(xo * xo, axis=-1, keepdims=True)
    hn = xo * lax.rsqrt(ms + EPS) * nw_ref[...]
    o_ref[...] = hn * (1.0 + sc_ref[...]) + sh_ref[...]


def _combine(seg_len, seg_src, seg_dst, pos, wrow, y, x1, mod3, nw, modf3, seq, tt):
    t, d = x1.shape
    tps = seq // tt
    nrows = TOP_K * tt + 8 * N_EXPERTS
    smem = pl.BlockSpec(memory_space=pltpu.SMEM)
    return pl.pallas_call(
        _combine_kernel,
        grid=(t // tt,),
        in_specs=[smem, smem, smem,
                  pl.BlockSpec((TOP_K, tt), lambda i: (0, i)),
                  pl.BlockSpec((TOP_K, tt), lambda i: (0, i)),
                  pl.BlockSpec(memory_space=pl.ANY),
                  pl.BlockSpec((tt, d), lambda i: (i, 0)),
                  _mod_spec(tps, d, 5),
                  pl.BlockSpec((1, d), lambda i: (0, 0)),
                  _mod_spec(tps, d, 0),
                  _mod_spec(tps, d, 1)],
        out_specs=pl.BlockSpec((tt, d), lambda i: (i, 0)),
        out_shape=jax.ShapeDtypeStruct((t, d), F32),
        scratch_shapes=[pltpu.VMEM((2, nrows, d), F32), pltpu.SemaphoreType.DMA((2,))],
        compiler_params=pltpu.CompilerParams(dimension_semantics=("arbitrary",),
                                             vmem_limit_bytes=V7X_VMEM_LIMIT),
        name="combine",
    )(seg_len, seg_src, seg_dst, pos, wrow, y, x1, mod3, nw, modf3, modf3)


def _pick_tile(n, pref):
    tile = pref
    while n % tile:
        tile //= 2
    return tile


def kernel(x, c, w_ada, b_ada, norm_mix, w_in, dn_conv, dn_a_log, dn_dt_bias, dn_norm, ml_conv, ml_i_bias,
           ml_f_bias, ml_norm, w_out, norm_ffn, w_router, b_router, w_gate_up, b_gate_up, w_down, b_down,
           w_ada_final, b_ada_final, norm_final):
    batch, seq, d = x.shape
    assert w_ada.shape[0] == 1, "single-layer block"
    assert seq % CHUNK == 0
    t = batch * seq
    x2 = x.reshape(t, d)

    c_pad = jnp.pad(c, ((0, 8 - batch % 8 if batch % 8 else 0), (0, 0)))
    mod3 = _mods(c_pad, w_ada.reshape(d, 6 * d), b_ada.reshape(1, 6 * d)).reshape(-1, 1, 6 * d)
    modf3 = _mods(c_pad, w_ada_final, b_ada_final.reshape(1, 2 * d)).reshape(-1, 1, 2 * d)

    wi = w_in.reshape(d, -1)
    o_z = 1536
    o_b = 2048
    o_mq = 2056
    o_mk = o_mq + ML_HEADS * ML_DK
    o_mv = o_mk + ML_HEADS * ML_DK
    o_mo = o_mv + ML_HEADS * ML_DV
    o_mi = o_mo + ML_HEADS * ML_DV
    gates = jnp.concatenate([wi[:, o_b:o_mq], wi[:, o_mi:o_mi + 2 * ML_HEADS]], axis=1)
    w_new = jnp.concatenate([
        wi[:, 0:o_z],
        wi[:, o_mq:o_mv],
        wi[:, o_z:o_b],
        wi[:, o_mv:o_mo],
        wi[:, o_mo:o_mi],
        jnp.pad(gates, ((0, 0), (0, HEAD_LANES - 16))),
    ], axis=1).astype(BF16)
    wgt = gates.T.astype(BF16)
    cw = jnp.concatenate([dn_conv.reshape(CONV_W, -1), ml_conv.reshape(CONV_W, -1)], axis=1)
    zeros4 = jnp.zeros((4,), F32)
    bias16 = jnp.concatenate([zeros4, dn_dt_bias.reshape(4), ml_i_bias.reshape(4), ml_f_bias.reshape(4)])
    alog16 = jnp.concatenate([zeros4, dn_a_log.reshape(4), zeros4, zeros4])
    gpc = jnp.zeros((8, HEAD_LANES), F32).at[0, 0:16].set(bias16).at[1, 0:16].set(alog16)
    gpr = jnp.zeros((16, HEAD_LANES), F32).at[:, 0].set(bias16).at[:, 1].set(alog16)

    tm_in = _pick_tile(seq, 512)
    conv_out, rest, gcol, grow = _inproj(x2, mod3, norm_mix.reshape(1, d), w_new, wgt, cw, gpc, gpr, seq, tm_in)
    grow3 = grow.reshape(16, t // CHUNK, CHUNK).transpose(1, 0, 2)

    rows = _pick_tile(seq, 256)
    ymix = _mixers(conv_out, rest, gcol, grow3, dn_norm.reshape(1, DN_DV), ml_norm.reshape(1, ML_HEADS * ML_DV),
                   batch, seq, rows)

    wo = w_out.reshape(-1, d).astype(BF16)
    tm_r = _pick_tile(seq, 512)
    brp = jnp.broadcast_to(b_router.reshape(N_EXPERTS, 1), (N_EXPERTS, HEAD_LANES))
    x1, h2, pos, wrow, len_col = _route(
        ymix, wo, x2, mod3, norm_ffn.reshape(1, d),
        w_router.reshape(d, N_EXPERTS).T, brp, seq, tm_r)

    n_e = N_EXPERTS
    nt = t // tm_r
    bm = 256
    len_te =len_col.reshape(nt, n_e, HEAD_LANES)[:, :, 0]
    seg_len = jnp.pad(len_te, ((0, 0), (0, HEAD_LANES - n_e)))
    n_slots_max = t * TOP_K + n_e * (7 * nt + bm)
    nb = (n_slots_max + bm - 1) // bm
    seg_src, seg_dst, table = _slots(seg_len, bm)

    xs = _dispatch(seg_len, seg_src, seg_dst, table, pos, h2, nb * bm, tm_r, bm)
    y = _experts(table, xs, w_gate_up.reshape(n_e, d, -1), b_gate_up.reshape(n_e, 1, -1),
                 w_down.reshape(n_e, -1, d), b_down.reshape(n_e, 1, d), bm)
    out = _combine(seg_len, seg_src, seg_dst, pos, wrow, y, x1, mod3, norm_final.reshape(1, d), modf3, seq, tm_r)
    return out.reshape(batch, seq, d)
```

```python
import functools

import jax
import jax.numpy as jnp
from jax import lax
from jax.experimental import pallas as pl
from jax.experimental.pallas import tpu as pltpu

F32 = jnp.float32
BF16 = jnp.bfloat16

CHUNK = 64
CONV_W = 4
EPS = 1e-6

DN_HEADS = 4
DN_DK = 128
DN_DV = 128
ML_HEADS = 4
ML_DK = 64
ML_DV = 128
HEAD_LANES = 128

N_EXPERTS = 32
TOP_K = 4
SWIGLU_LIMIT = 7.0
SWIGLU_ALPHA = 1.702

C_DNQ = 0
C_DNK = 512
C_DNV = 1024
C_MLQ = 1536
C_MLK = 1792
N_CONV = 2048
C_DNZ = 2048
C_MLV = 2560
C_MLO = 3072
C_GATE = 3584
N_PROJ = 3712
N_REST = C_GATE - N_CONV

V7X_VMEM_LIMIT = 56 * 1024 * 1024
MXU_COLS = 256
COMMON_SEGMENT_ROWS = 128
INPROJ_GROUP_COLS = 512

NEG_BIG = -1e30


def _sigmoid(x):
    return 1.0 / (1.0 + jnp.exp(-x))


def _softplus(x):
    return jnp.maximum(x, 0.0) + jnp.log(1.0 + jnp.exp(-jnp.abs(x)))


def _split3(v):
    hi = v.astype(BF16)
    r1 = v - hi.astype(F32)
    mid = r1.astype(BF16)
    lo = (r1 - mid.astype(F32)).astype(BF16)
    return hi, mid, lo


def _dot(a, b):
    return jnp.dot(a, b, preferred_element_type=F32)


def _dot_nt(a, b):
    return lax.dot_general(a, b, (((1,), (1,)), ((), ())), preferred_element_type=F32)


def _dot_tn(a, b):
    return lax.dot_general(a, b, (((0,), (0,)), ((), ())), preferred_element_type=F32)


def _dot_exact_right(sel_bf16, v):
    hi, mid, lo = _split3(v)
    return _dot(sel_bf16, hi) + _dot(sel_bf16, mid) + _dot(sel_bf16, lo)


def _dot_exact_left(v, sel_bf16):
    hi, mid, lo = _split3(v)
    return _dot(hi, sel_bf16) + _dot(mid, sel_bf16) + _dot(lo, sel_bf16)


def _mods_kernel(c_ref, w_ref, b_ref, o_ref):
    c = c_ref[...]
    cond = c * _sigmoid(c)
    ch, cm, _ = _split3(cond)
    wh, wm, _ = _split3(w_ref[...])
    acc = _dot(ch, wh) + (_dot(ch, wm) + _dot(cm, wh))
    o_ref[...] = acc + b_ref[...]


def _mods(c_pad, w, b):
    m, d = c_pad.shape
    n = w.shape[1]
    tn = 1024
    return pl.pallas_call(
        _mods_kernel,
        grid=(n // tn,),
        in_specs=[pl.BlockSpec((m, d), lambda j: (0, 0)),
                  pl.BlockSpec((d, tn), lambda j: (0, j)),
                  pl.BlockSpec((1, tn), lambda j: (0, j))],
        out_specs=pl.BlockSpec((m, tn), lambda j: (0, j)),
        out_shape=jax.ShapeDtypeStruct((m, n), F32),
        compiler_params=pltpu.CompilerParams(dimension_semantics=("arbitrary",),
                                             vmem_limit_bytes=V7X_VMEM_LIMIT),
        name="mods",
    )(c_pad, w, b)


def _gate_transform(v, bias, alog, cls):
    vb = v + bias
    beta = _sigmoid(v)
    g = -jnp.exp(alog) * _softplus(vb)
    logf = -_softplus(-vb)
    return jnp.where(cls == 0, beta, jnp.where(cls == 1, g, jnp.where(cls == 2, vb, jnp.where(cls == 3, logf, 0.0))))


def _inproj_kernel(tiles_per_seq, x_ref, sh_ref, sc_ref, nw_ref, w_ref, wgt_ref, cw_ref, gpc_ref, gpr_ref,
                   conv_ref, rest_ref, gcol_ref, grow_ref, cbuf):
    tm = x_ref.shape[0]
    i = pl.program_id(0)
    x = x_ref[...]
    ms = jnp.mean(x * x, axis=-1, keepdims=True)
    h = x * lax.rsqrt(ms + EPS) * nw_ref[...]
    h = h * (1.0 + sc_ref[...]) + sh_ref[...]
    hb = h.astype(BF16)

    @pl.when(i % tiles_per_seq == 0)
    def _():
        cbuf[0:8, :] = jnp.zeros((8, N_CONV), F32)

    group = INPROJ_GROUP_COLS
    for lo in range(0, N_CONV, group):
        cols = slice(lo, lo + group)
        pc = _dot(hb, w_ref[:, cols])
        cbuf[8:tm + 8, cols] = pc
        acc = cw_ref[CONV_W - 1:CONV_W, cols] * pc
        for j in range(CONV_W - 1):
            acc = acc + cw_ref[j:j + 1, cols] * cbuf[8 - (CONV_W - 1) + j:8 - (CONV_W - 1) + j + tm, cols]
        cbuf[0:8, cols] = cbuf[tm:tm + 8, cols]
        y = acc * _sigmoid(acc)
        if lo < C_DNV:
            for h0 in range(0, group, HEAD_LANES):
                uh = y[:, h0:h0 + HEAD_LANES]
                un = uh * lax.rsqrt(jnp.sum(uh * uh, axis=-1, keepdims=True) + EPS)
                conv_ref[:, lo + h0:lo + h0 + HEAD_LANES] = un * (DN_DK ** -0.5) if lo < C_DNK else un
        else:
            for h0 in range(0, group, HEAD_LANES):
                piece = y[:, h0:h0 + HEAD_LANES]
                is_mlq = C_MLQ <= lo + h0 < C_MLK
                conv_ref[:, lo + h0:lo + h0 + HEAD_LANES] = piece * (ML_DK ** -0.5) if is_mlq else piece

    z = _dot(hb, w_ref[:, C_DNZ:C_MLV])
    rest_ref[:, 0:512] = z * _sigmoid(z)
    rest_ref[:, 512:1024] = _dot(hb, w_ref[:, C_MLV:C_MLO])
    rest_ref[:, 1024:1536] = _sigmoid(_dot(hb, w_ref[:, C_MLO:C_GATE]))

    r_i = lax.broadcasted_iota(jnp.int32, (tm, tm), 0)
    c_i = lax.broadcasted_iota(jnp.int32, (tm, tm), 1)
    same_chunk = (r_i // CHUNK) == (c_i // CHUNK)
    tril = jnp.where(same_chunk & (c_i <= r_i), 1.0, 0.0).astype(BF16)
    triu = jnp.where(same_chunk & (r_i <= c_i), 1.0, 0.0).astype(BF16)

    gc = _dot(hb, w_ref[:, C_GATE:N_PROJ])
    cls_c = lax.broadcasted_iota(jnp.int32, (tm, HEAD_LANES), 1) // 4
    gt = _gate_transform(gc, gpc_ref[0:1, :], gpc_ref[1:2, :], cls_c)
    cs = _dot_exact_right(tril, gt)
    gcol_ref[...] = jnp.where((cls_c == 1) | (cls_c == 3), cs, gt)

    gr = _dot_nt(wgt_ref[...], hb)
    cls_r = lax.broadcasted_iota(jnp.int32, (16, tm), 0) // 4
    gtr = _gate_transform(gr, gpr_ref[:, 0:1], gpr_ref[:, 1:2], cls_r)
    csr = _dot_exact_left(gtr, triu)
    grow_ref[...] = jnp.where((cls_r == 1) | (cls_r == 3), csr, gtr)


def _mod_spec(tiles_per_seq, d, j):
    return pl.BlockSpec((None, 1, d), lambda i: (i // tiles_per_seq, 0, j))


def _inproj(x2, mod3, nw, w_new, wgt, cw, gpc, gpr, seq, tm):
    t, d = x2.shape
    tps = seq // tm
    kern = functools.partial(_inproj_kernel, tps)
    return pl.pallas_call(
        kern,
        grid=(t // tm,),
        in_specs=[pl.BlockSpec((tm, d), lambda i: (i, 0)),
                  _mod_spec(tps, d, 0),
                  _mod_spec(tps, d, 1),
                  pl.BlockSpec((1, d), lambda i: (0, 0)),
                  pl.BlockSpec((d, N_PROJ), lambda i: (0, 0)),
                  pl.BlockSpec((16, d), lambda i: (0, 0)),
                  pl.BlockSpec((CONV_W, N_CONV), lambda i: (0, 0)),
                  pl.BlockSpec((8, HEAD_LANES), lambda i: (0, 0)),
                  pl.BlockSpec((16, HEAD_LANES), lambda i: (0, 0))],
        out_specs=[pl.BlockSpec((tm, N_CONV), lambda i: (i, 0)),
                   pl.BlockSpec((tm, N_REST), lambda i: (i, 0)),
                   pl.BlockSpec((tm, HEAD_LANES), lambda i: (i, 0)),
                   pl.BlockSpec((16, tm), lambda i: (0, i))],
        out_shape=[jax.ShapeDtypeStruct((t, N_CONV), F32),
                   jax.ShapeDtypeStruct((t, N_REST), F32),
                   jax.ShapeDtypeStruct((t, HEAD_LANES), F32),
                   jax.ShapeDtypeStruct((16, t), F32)],
        scratch_shapes=[pltpu.VMEM((tm + 8, N_CONV), F32)],
        compiler_params=pltpu.CompilerParams(dimension_semantics=("arbitrary",),
                                             vmem_limit_bytes=V7X_VMEM_LIMIT),
        name="inproj",
    )(x2, mod3, mod3, nw, w_new, wgt, cw, gpc, gpr)


def _chunk_masks():
    r = lax.broadcasted_iota(jnp.int32, (CHUNK, CHUNK), 0)
    c = lax.broadcasted_iota(jnp.int32, (CHUNK, CHUNK), 1)
    return r >= c, r > c, r == c


def _bdot(a, b):
    return lax.dot_general(a, b, (((2,), (1,)), ((0,), (0,))), preferred_element_type=F32)


def _bdot_nt(a, b):
    return lax.dot_general(a, b, (((2,), (2,)), ((0,), (0,))), preferred_element_type=F32)


def _unit_lower_inverse(lower, row, col):
    x = jnp.where(row == col, 1.0, 0.0) - jnp.where((row >> 1) == (col >> 1), lower, 0.0)
    shift = 1
    while (1 << shift) < CHUNK:
        couple = ((row >> (shift + 1)) == (col >> (shift + 1))) & ((row >> shift) != (col >> shift))
        cb = jnp.where(couple, lower, 0.0).astype(BF16)
        xb = x.astype(BF16)
        x = x - _bdot(_bdot(xb, cb).astype(BF16), xb)
        shift += 1
        yield
    return x


def _seq_heads(n_seq, n_heads):
    return [(s, hh) for s in range(n_seq) for hh in range(n_heads)]


def _deltanet_steps(q_ref, k_ref, v_ref, gc_ref, gr_ref, z_ref, nw_ref, o_ref, s_ref):
    n_seq, nc = gr_ref.shape[0], gr_ref.shape[1]

    @pl.when(pl.program_id(1) == 0)
    def _():
        s_ref[...] = jnp.zeros(s_ref.shape, F32)

    row = lax.broadcasted_iota(jnp.int32, (CHUNK, CHUNK), 0)
    col = lax.broadcasted_iota(jnp.int32, (CHUNK, CHUNK), 1)
    incl = row >= col
    strict = row > col
    nw = nw_ref[...]
    gcc = gc_ref[...]
    grr = gr_ref[...]

    streams = _seq_heads(n_seq, DN_HEADS)
    nh = len(streams)
    nb = nh * nc

    def heads(ref):
        return jnp.stack([ref[s, :, hh * HEAD_LANES:(hh + 1) * HEAD_LANES] for s, hh in streams],
                         axis=0).reshape(nb, CHUNK, HEAD_LANES)

    def col_gate(lane0):
        return jnp.stack([gcc[s, :, lane0 + hh:lane0 + hh + 1] for s, hh in streams], axis=0).reshape(nb, CHUNK, 1)

    q = heads(q_ref)
    k = heads(k_ref)
    v = heads(v_ref)
    beta = col_gate(0)
    g_c = col_gate(4)
    g_r = jnp.stack([grr[s, :, 4 + hh:5 + hh, :] for s, hh in streams], axis=0).reshape(nb, 1, CHUNK)
    g_last = g_c[:, CHUNK - 1:CHUNK, :]
    decay = jnp.exp(jnp.where(incl, g_c - g_r, NEG_BIG))
    kb = k.astype(BF16)
    kk = _bdot_nt(kb, kb)
    lower = jnp.where(strict, beta * kk * decay, 0.0)
    yield
    tinv = yield from _unit_lower_inverse(lower, row, col)
    eg = jnp.exp(g_c)
    rhs = jnp.concatenate([v * beta, k * (beta * eg)], axis=-1)
    sol = _bdot(tinv.astype(BF16), rhs.astype(BF16))
    yield
    w_val = sol[:, :, 0:DN_DV].reshape(nh, nc, CHUNK, DN_DV)
    kq = jnp.concatenate([sol[:, :, DN_DV:DN_DV + DN_DK], q * eg], axis=1).astype(BF16)
    kq = kq.reshape(nh, nc, 2 * CHUNK, DN_DK)
    qk = (_bdot_nt(q.astype(BF16), kb) * decay).astype(BF16).reshape(nh, nc, CHUNK, CHUNK)
    k_dec_t = jnp.swapaxes(k * jnp.exp(g_last - g_c), 1, 2).astype(BF16).reshape(nh, nc, DN_DK, CHUNK)
    s_dec = jnp.exp(g_last).reshape(nh, nc, 1, 1)
    yield

    state = s_ref[...]
    outs = []
    for c in range(nc):
        both = _bdot(kq[:, c], state.astype(BF16))
        v_new = w_val[:, c] - both[:, 0:CHUNK]
        vb = v_new.astype(BF16)
        outs.append(both[:, CHUNK:2 * CHUNK] + _bdot(qk[:, c], vb))
        state = s_dec[:, c] * state + _bdot(k_dec_t[:, c], vb)
        yield
    s_ref[...] = state

    o = jnp.stack(outs, axis=1)
    on = o * lax.rsqrt(jnp.mean(o * o, axis=-1, keepdims=True) + EPS) * nw
    on = on.reshape(nh, nc * CHUNK, DN_DV)
    for idx, (s, hh) in enumerate(streams):
        lanes = slice(hh * HEAD_LANES, (hh + 1) * HEAD_LANES)
        o_ref[s, :, lanes] = on[idx] * z_ref[s, :, lanes]


def _mlstm_steps(q_ref, k_ref, v_ref, gc_ref, gr_ref, og_ref, nw_ref, o_ref, c_ref, n_ref, m_ref):
    n_seq, nc = gr_ref.shape[0], gr_ref.shape[1]

    @pl.when(pl.program_id(1) == 0)
    def _():
        c_ref[...] = jnp.zeros(c_ref.shape, F32)
        n_ref[...] = jnp.zeros(n_ref.shape, F32)
        m_ref[...] = jnp.zeros(m_ref.shape, F32)

    incl, _, _ = _chunk_masks()
    gcc = gc_ref[...]
    grr = gr_ref[...]

    streams = _seq_heads(n_seq, ML_HEADS)
    nh = len(streams)

    def heads(ref):
        return jnp.stack([ref[s, :, hh * HEAD_LANES:(hh + 1) * HEAD_LANES] for s, hh in streams],
                         axis=0).reshape(nh, nc, CHUNK, HEAD_LANES)

    def col_gate(lane0):
        return jnp.stack([gcc[s, :, lane0 + hh:lane0 + hh + 1] for s, hh in streams],
                         axis=0).reshape(nh, nc, CHUNK, 1)

    def row_gate(row0):
        return jnp.stack([grr[s, :, row0 + hh:row0 + hh + 1, :] for s, hh in streams], axis=0)

    def qk_heads(ref):
        lane = lax.broadcasted_iota(jnp.int32, (ref.shape[1], HEAD_LANES), 1)
        tiles = []
        for s, hh in streams:
            tile = ref[s, :, (hh // 2) * HEAD_LANES:(hh // 2 + 1) * HEAD_LANES]
            mine = (lane >= ML_DK) if hh % 2 else (lane < ML_DK)
            tiles.append(jnp.where(mine, tile, 0.0))
        return jnp.stack(tiles, axis=0).reshape(nh, nc, CHUNK, HEAD_LANES)

    q = qk_heads(q_ref)
    k = qk_heads(k_ref)
    v = heads(v_ref)
    i_c = col_gate(8)
    b_c = col_gate(12)
    i_r = row_gate(8)
    b_r = row_gate(12)
    b_last = b_c[:, :, CHUNK - 1:CHUNK, :]
    d_mat = jnp.where(incl, b_c - b_r + i_r, NEG_BIG)
    m_intra = jnp.max(d_mat, axis=-1, keepdims=True)
    g_end = b_last - b_c + i_c
    g_end_max = jnp.max(g_end, axis=2, keepdims=True)
    yield

    m_run = m_ref[:, 0:1, 0:1].reshape(nh, 1, 1, 1)
    m_before = []
    for c in range(nc):
        m_before.append(m_run)
        m_run = jnp.maximum(b_last[:, c:c + 1] + m_run, g_end_max[:, c:c + 1])
    m_s = jnp.concatenate(m_before, axis=1)
    m_new = jnp.maximum(b_last + m_s, g_end_max)
    keep = jnp.exp(b_last + m_s - m_new)
    yield

    nb = nh * nc
    qb = q.astype(BF16)
    kb = k.astype(BF16)
    vb = v.astype(BF16)
    m_t = jnp.maximum(b_c + m_s, m_intra)
    inter = jnp.exp(b_c + m_s - m_t)
    qk = _bdot_nt(qb.reshape(nb, CHUNK, HEAD_LANES), kb.reshape(nb, CHUNK, HEAD_LANES))
    p = jnp.exp(d_mat - m_t) * qk.reshape(nh, nc, CHUNK, CHUNK)
    yield
    intra = _bdot(p.astype(BF16).reshape(nb, CHUNK, CHUNK), vb.reshape(nb, CHUNK, ML_DV)).reshape(nh, nc, CHUNK, ML_DV)
    p_sum = jnp.sum(p, axis=-1, keepdims=True)
    yield
    kw = k * jnp.exp(g_end - m_new)
    kw_t = jnp.swapaxes(kw.reshape(nb, CHUNK, HEAD_LANES), 1, 2).astype(BF16)
    d_state = _bdot(kw_t, vb.reshape(nb, CHUNK, ML_DV)).reshape(nh, nc, HEAD_LANES, ML_DV)
    kw_sum = jnp.sum(kw, axis=2, keepdims=True)
    yield

    c_s = c_ref[...]
    n_s = n_ref[:, 0:1, :]
    q_c = []
    q_n = []
    for c in range(nc):
        q_c.append(_bdot(qb[:, c], c_s.astype(BF16)))
        q_n.append(jnp.sum(q[:, c] * n_s, axis=-1, keepdims=True))
        c_s = keep[:, c] * c_s + d_state[:, c]
        n_s = keep[:, c] * n_s + kw_sum[:, c]
        yield
    c_ref[...] = c_s
    n_ref[...] = jnp.broadcast_to(n_s, (nh, 8, HEAD_LANES))
    m_ref[...] = jnp.broadcast_to(m_run.reshape(nh, 1, 1), (nh, 8, HEAD_LANES))

    num = inter * jnp.stack(q_c, axis=1) + intra
    den = inter * jnp.stack(q_n, axis=1) + p_sum
    h = num / jnp.maximum(jnp.abs(den), jnp.exp(-m_t))
    hr = h * lax.rsqrt(jnp.mean(h * h, axis=-1, keepdims=True) + EPS)
    hr = hr.reshape(nh, nc * CHUNK, ML_DV)
    for idx, (s, hh) in enumerate(streams):
        lanes = slice(hh * HEAD_LANES, (hh + 1) * HEAD_LANES)
        o_ref[s, :, lanes] = hr[idx] * nw_ref[:, lanes] * og_ref[s, :, lanes]


def _mixers_kernel(dq_ref, dk_ref, dv_ref, gc_ref, gr_ref, z_ref, dnw_ref, mq_ref, mk_ref, mv_ref, og_ref, mnw_ref,
                   o_ref, s_ref, c_ref, n_ref, m_ref):
    n_a = DN_HEADS * HEAD_LANES
    n_b = ML_HEADS * HEAD_LANES
    stages = [_deltanet_steps(dq_ref, dk_ref, dv_ref, gc_ref, gr_ref, z_ref, dnw_ref, o_ref.at[:, :, 0:n_a], s_ref),
              _mlstm_steps(mq_ref, mk_ref, mv_ref, gc_ref, gr_ref, og_ref, mnw_ref, o_ref.at[:, :, n_a:n_a + n_b],
                           c_ref, n_ref, m_ref)]
    while stages:
        for stage in list(stages):
            try:
                next(stage)
            except StopIteration:
                stages.remove(stage)


def _mixers(conv_out, rest, gcol, grow3, dn_nw, ml_nw, batch, seq, rows):
    t = conv_out.shape[0]
    nj = seq // rows
    cpb = rows // CHUNK
    width = 4 * HEAD_LANES
    assert DN_HEADS * HEAD_LANES == width and ML_HEADS * HEAD_LANES == width
    n_seq = 2 if batch % 2 == 0 else 1
    conv3 = conv_out.reshape(batch, seq, -1)
    rest3 = rest.reshape(batch, seq, -1)
    gcol3 = gcol.reshape(batch, seq, -1)
    grow4 = grow3.reshape(batch, seq // CHUNK, 16, CHUNK)

    def conv_block(c0):
        return pl.BlockSpec((n_seq, rows, width), lambda b, j: (b, j, c0 // width))

    def rest_block(c0):
        return pl.BlockSpec((n_seq, rows, width), lambda b, j: (b, j, (c0 - N_CONV) // width))

    ml_qk_width = ML_HEADS * ML_DK

    def ml_qk_block(c0):
        return pl.BlockSpec((n_seq, rows, ml_qk_width), lambda b, j: (b, j, c0 // ml_qk_width))

    gates = [pl.BlockSpec((n_seq, rows, HEAD_LANES), lambda b, j: (b, j, 0)),
             pl.BlockSpec((n_seq, cpb, 16, CHUNK), lambda b, j: (b, j, 0, 0))]
    out = pl.pallas_call(
        _mixers_kernel,
        grid=(batch // n_seq, nj),
        in_specs=[conv_block(C_DNQ), conv_block(C_DNK), conv_block(C_DNV)] + gates
                 + [rest_block(C_DNZ), pl.BlockSpec((1, HEAD_LANES), lambda b, j: (0, 0)),
                    ml_qk_block(C_MLQ), ml_qk_block(C_MLK), rest_block(C_MLV), rest_block(C_MLO),
                    pl.BlockSpec((1, width), lambda b, j: (0, 0))],
        out_specs=pl.BlockSpec((n_seq, rows, 2 * width), lambda b, j: (b, j, 0)),
        out_shape=jax.ShapeDtypeStruct((batch, seq, 2 * width), F32),
        scratch_shapes=[pltpu.VMEM((n_seq * DN_HEADS, DN_DK, DN_DV), F32),
                        pltpu.VMEM((n_seq * ML_HEADS, HEAD_LANES, ML_DV), F32),
                        pltpu.VMEM((n_seq * ML_HEADS, 8, HEAD_LANES), F32),
                        pltpu.VMEM((n_seq * ML_HEADS, 8, HEAD_LANES), F32)],
        compiler_params=pltpu.CompilerParams(dimension_semantics=("arbitrary", "arbitrary"),
                                             vmem_limit_bytes=V7X_VMEM_LIMIT),
        name="mixers",
    )(conv3, conv3, conv3, gcol3, grow4, rest3, dn_nw, conv3, conv3, rest3, rest3, ml_nw)
    return out.reshape(t, 2 * width)


def _route_kernel(y_ref, wo_ref, x_ref, g1_ref, sh_ref, sc_ref, nw_ref, wrt_ref, br_ref,
                  x1_ref, h2_ref, pos_ref, wrow_ref, len_ref):
    tm = x_ref.shape[0]

    mix = _dot(y_ref[...].astype(BF16), wo_ref[...])
    x1 = x_ref[...] + g1_ref[...] * mix
    x1_ref[...] = x1
    ms = jnp.mean(x1 * x1, axis=-1, keepdims=True)
    h2 = x1 * lax.rsqrt(ms + EPS) * nw_ref[...]
    h2 = h2 * (1.0 + sc_ref[...]) + sh_ref[...]
    h2_ref[...] = h2.astype(BF16)

    hh, hm, _ = _split3(h2)
    wh, wm, _ = _split3(wrt_ref[...])
    logits = _dot_nt(wh, hh) + (_dot_nt(wh, hm) + _dot_nt(wm, hh)) + br_ref[:, 0:1]

    e_i = lax.broadcasted_iota(jnp.int32, (N_EXPERTS, tm), 0)
    work = logits
    tops = []
    sels = []
    hots = []
    for _ in range(TOP_K):
        m = jnp.max(work, axis=0, keepdims=True)
        sel = jnp.min(jnp.where(work == m, e_i, N_EXPERTS), axis=0, keepdims=True)
        hot = e_i == sel
        work = jnp.where(hot, NEG_BIG, work)
        tops.append(m)
        sels.append(sel)
        hots.append(hot)
    exps = [jnp.exp(tl - tops[0]) for tl in tops]
    denom = exps[0] + exps[1] + exps[2] + exps[3]
    ws = [e / denom for e in exps]

    chosen = jnp.zeros((N_EXPERTS, tm), F32)
    for hot in hots:
        chosen = chosen + jnp.where(hot, 1.0, 0.0)

    r_i = lax.broadcasted_iota(jnp.int32, (tm, tm), 0)
    c_i = lax.broadcasted_iota(jnp.int32, (tm, tm), 1)
    strict_upper = jnp.where(r_i < c_i, 1.0, 0.0).astype(BF16)
    prefix = _dot(chosen.astype(BF16), strict_upper)
    n_e = jnp.sum(chosen, axis=1, keepdims=True)
    len8 = jnp.ceil(n_e * 0.125) * 8.0
    er = lax.broadcasted_iota(jnp.int32, (N_EXPERTS, N_EXPERTS), 0)
    ec = lax.broadcasted_iota(jnp.int32, (N_EXPERTS, N_EXPERTS), 1)
    strict_lower = jnp.where(ec < er, 1.0, 0.0).astype(BF16)
    len8b = jnp.broadcast_to(len8, (N_EXPERTS, HEAD_LANES))
    off8 = _dot_exact_right(strict_lower, len8b)[:, 0:1]
    len_ref[...] = len8b.astype(jnp.int32)

    for kk in range(TOP_K):
        wrow_ref[kk:kk + 1, :] = ws[kk]
        pos = jnp.sum(jnp.where(hots[kk], prefix + off8, 0.0), axis=0, keepdims=True)
        pos_ref[kk:kk + 1, :] = pos.astype(jnp.int32)


def _route(ymix, wo, x2, mod3, nw, wrt, br, seq, tm):
    t, d = x2.shape
    tps = seq // tm
    return pl.pallas_call(
        _route_kernel,
        grid=(t // tm,),
        in_specs=[pl.BlockSpec((tm, ymix.shape[1]), lambda i: (i, 0)),
                  pl.BlockSpec(wo.shape, lambda i: (0, 0)),
                  pl.BlockSpec((tm, d), lambda i: (i, 0)),
                  _mod_spec(tps, d, 2),
                  _mod_spec(tps, d, 3),
                  _mod_spec(tps, d, 4),
                  pl.BlockSpec((1, d), lambda i: (0, 0)),
                  pl.BlockSpec((N_EXPERTS, d), lambda i: (0, 0)),
                  pl.BlockSpec((N_EXPERTS, HEAD_LANES), lambda i: (0, 0))],
        out_specs=[pl.BlockSpec((tm, d), lambda i: (i, 0)),
                   pl.BlockSpec((tm, d), lambda i: (i, 0)),
                   pl.BlockSpec((TOP_K, tm), lambda i: (0, i)),
                   pl.BlockSpec((TOP_K, tm), lambda i: (0, i)),
                   pl.BlockSpec((N_EXPERTS, HEAD_LANES), lambda i: (i, 0))],
        out_shape=[jax.ShapeDtypeStruct((t, d), F32),
                   jax.ShapeDtypeStruct((t, d), BF16),
                   jax.ShapeDtypeStruct((TOP_K, t), jnp.int32),
                   jax.ShapeDtypeStruct((TOP_K, t), F32),
                   jax.ShapeDtypeStruct((t // tm * N_EXPERTS, HEAD_LANES), jnp.int32)],
        compiler_params=pltpu.CompilerParams(dimension_semantics=("arbitrary",),
                                             vmem_limit_bytes=V7X_VMEM_LIMIT),
        name="route",
    )(ymix, wo, x2, mod3, mod3, mod3, nw, wrt, br)


TAIL_START, TAIL_LEN, PAD_END, BLOCK_START, BLOCK_COUNT = range(5)


def _slots_kernel(bm, len_te_ref, ssrc_ref, sdst_ref, tab_ref):
    nt, lanes = len_te_ref.shape
    len_te = len_te_ref[...].astype(F32)

    r_l = lax.broadcasted_iota(jnp.int32, (lanes, lanes), 0)
    c_l = lax.broadcasted_iota(jnp.int32, (lanes, lanes), 1)
    upper_incl = jnp.where(r_l <= c_l, 1.0, 0.0).astype(BF16)
    upper_strict = jnp.where(r_l < c_l, 1.0, 0.0).astype(BF16)
    total_r = jnp.sum(len_te, axis=0, keepdims=True)
    padded_r = jnp.ceil(total_r * (1.0 / bm)) * bm
    pad_end_r = _dot_exact_left(jnp.broadcast_to(padded_r, (8, lanes)), upper_incl)[0:1, :]
    pad_start_r = pad_end_r - padded_r
    r_t = lax.broadcasted_iota(jnp.int32, (nt, nt), 0)
    c_t = lax.broadcasted_iota(jnp.int32, (nt, nt), 1)
    lower_strict_t = jnp.where(c_t < r_t, 1.0, 0.0).astype(BF16)
    before = _dot_exact_right(lower_strict_t, len_te)
    ssrc_ref[...] = _dot_exact_left(len_te, upper_strict).astype(jnp.int32)
    sdst_ref[...] = (pad_start_r + before).astype(jnp.int32)
    tab_ref[TAIL_START:TAIL_START + 1, :] = (pad_start_r + total_r).astype(jnp.int32)
    tab_ref[TAIL_LEN:TAIL_LEN + 1, :] = (padded_r - total_r).astype(jnp.int32)
    tab_ref[PAD_END:PAD_END + 1, :] = pad_end_r.astype(jnp.int32)
    tab_ref[BLOCK_START:BLOCK_START + 1, :] = (pad_start_r * (1.0 / bm)).astype(jnp.int32)
    tab_ref[BLOCK_COUNT:BLOCK_COUNT + 1, :] = (padded_r * (1.0 / bm)).astype(jnp.int32)
    tab_ref[5:8, :] = jnp.zeros((3, lanes), jnp.int32)


def _slots(len_te, bm):
    assert bm & (bm - 1) == 0, "block rows must be a power of two"
    nt, lanes = len_te.shape
    return pl.pallas_call(
        functools.partial(_slots_kernel, bm),
        out_shape=[jax.ShapeDtypeStruct((nt, lanes), jnp.int32),
                   jax.ShapeDtypeStruct((nt, lanes), jnp.int32),
                   jax.ShapeDtypeStruct((8, lanes), jnp.int32)],
        compiler_params=pltpu.CompilerParams(vmem_limit_bytes=V7X_VMEM_LIMIT),
        name="slots",
    )(len_te)


def _segment_pieces(max_rows):
    sizes = []
    s = 8
    while s <= max_rows:
        sizes.append(s)
        s *= 2
    return sizes[::-1]


def _segment_dma(src_ref, src0, dst_ref, dst0, nrows, sizes, sem, start):
    def pieces(group, off):
        for sz in group:
            bit = nrows & sz

            @pl.when(bit != 0)
            def _(off=off, sz=sz):
                s0 = pl.multiple_of(src0 + off, 8)
                d0 = pl.multiple_of(dst0 + off, 8)
                cp = pltpu.make_async_copy(src_ref.at[pl.ds(s0, sz)], dst_ref.at[pl.ds(d0, sz)], sem)
                if start:
                    cp.start()
                else:
                    cp.wait()

            off = off + bit

    large = [sz for sz in sizes if sz >= COMMON_SEGMENT_ROWS]
    small = [sz for sz in sizes if sz < COMMON_SEGMENT_ROWS]
    if large and small:
        @pl.when(nrows >= COMMON_SEGMENT_ROWS)
        def _():
            pieces(large, 0)

        pieces(small, nrows & ~(COMMON_SEGMENT_ROWS - 1))
    else:
        pieces(sizes, 0)


def _sorted_onehot(pos_ref, row0, nrows, tt):
    r_i = lax.broadcasted_iota(jnp.int32, (nrows, tt), 0) + row0
    hit = r_i == pos_ref[0:1, :]
    for kk in range(1, TOP_K):
        hit = hit | (r_i == pos_ref[kk:kk + 1, :])
    return hit


def _row_groups(nrows, n_groups):
    size = -(-nrows // n_groups // 8) * 8
    return [(r0, min(size, nrows - r0)) for r0 in range(0, nrows, size)]


def _dispatch_kernel(bm, len_ref, ssrc_ref, sdst_ref, tail_ref, pos_ref, h2_ref, xs_hbm, buf, sem):
    i = pl.program_id(0)
    nt = pl.num_programs(0)
    tt = h2_ref.shape[0]
    nrows = buf.shape[1]
    sizes = _segment_pieces(tt)
    slot = i % 2

    h2 = h2_ref[...]
    for r0, rn in _row_groups(nrows, 4):
        perm = jnp.where(_sorted_onehot(pos_ref, r0, rn, tt), 1.0, 0.0).astype(BF16)
        buf[slot, r0:r0 + rn, :] = _dot(perm, h2)

    def seg_start(tile, which):
        def body(e, carry):
            _segment_dma(buf.at[which], ssrc_ref[tile, e], xs_hbm, sdst_ref[tile, e], len_ref[tile, e], sizes,
                         sem.at[which], True)
            return carry
        lax.fori_loop(0, N_EXPERTS, body, 0)

    def seg_wait(tile, which):
        total = ssrc_ref[tile, N_EXPERTS - 1] + len_ref[tile, N_EXPERTS - 1]
        _segment_dma(buf.at[which], 0, xs_hbm, 0, total, _segment_pieces(nrows), sem.at[which], False)

    @pl.when(i > 0)
    def _():
        seg_wait(i - 1, 1 - slot)

    seg_start(i, slot)

    @pl.when(i == nt - 1)
    def _():
        seg_wait(i, slot)
        zrows = bm
        zbuf = buf.at[0]
        zsem = sem.at[0]
        buf[0, 0:zrows, :] = jnp.zeros((zrows, buf.shape[2]), F32)

        def tail(start):
            def body(e, carry):
                _segment_dma(zbuf, 0, xs_hbm, tail_ref[TAIL_START, e], tail_ref[TAIL_LEN, e],
                             _segment_pieces(zrows // 2), zsem, start)
                return carry
            lax.fori_loop(0, N_EXPERTS, body, 0)

        tail(True)
        tail(False)

        used = tail_ref[PAD_END, tail_ref.shape[1] - 1]
        n_unused = (xs_hbm.shape[0] - used) // zrows

        def unused_copy(j):
            d0 = pl.multiple_of(used + j * zrows, 8)
            return pltpu.make_async_copy(zbuf.at[pl.ds(0, zrows)], xs_hbm.at[pl.ds(d0, zrows)], zsem)

        def unused_start(j, carry):
            unused_copy(j).start()
            return carry

        def unused_wait(j, carry):
            unused_copy(j).wait()
            return carry

        lax.fori_loop(0, n_unused, unused_start, 0)
        lax.fori_loop(0, n_unused, unused_wait, 0)


def _dispatch(seg_len, seg_src, seg_dst, tail, pos, h2, n_slots, tt, bm):
    t, d = h2.shape
    nrows = TOP_K * tt + 8 * N_EXPERTS
    assert bm <= nrows
    smem = pl.BlockSpec(memory_space=pltpu.SMEM)
    return pl.pallas_call(
        functools.partial(_dispatch_kernel, bm),
        grid=(t // tt,),
        in_specs=[smem, smem, smem, smem,
                  pl.BlockSpec((TOP_K, tt), lambda i: (0, i)),
                  pl.BlockSpec((tt, d), lambda i: (i, 0))],
        out_specs=pl.BlockSpec(memory_space=pl.ANY),
        out_shape=jax.ShapeDtypeStruct((n_slots, d), F32),
        scratch_shapes=[pltpu.VMEM((2, nrows, d), F32), pltpu.SemaphoreType.DMA((2,))],
        compiler_params=pltpu.CompilerParams(dimension_semantics=("arbitrary",), has_side_effects=True,
                                             vmem_limit_bytes=V7X_VMEM_LIMIT),
        name="dispatch",
    )(seg_len, seg_src, seg_dst, tail, pos, h2)


def _experts_kernel(tab_ref, xs_hbm, wgu_hbm, bgu_ref, wd_hbm, bd_ref, y_hbm, wgu_s, wd_s, wgu_f, wd_f, xbuf, ybuf,
                    sem_in, sem_out, sem_w):
    e = pl.program_id(0)
    n_e = pl.num_programs(0)
    bm = xbuf.shape[1] // 2
    d_ff = wd_f.shape[1]
    ct = wgu_s.shape[2]
    wslot = e % 2

    def w_copies(expert, which):
        return (pltpu.make_async_copy(wgu_hbm.at[expert], wgu_f.at[which], sem_w.at[which]),
                pltpu.make_async_copy(wd_hbm.at[expert], wd_f.at[which], sem_w.at[which]))

    @pl.when(e == 0)
    def _():
        for cp in w_copies(0, 0):
            cp.start()

    for cp in w_copies(e, wslot):
        cp.wait()

    have_next = e + 1 < n_e

    def prefetch_next_weights(cond):
        @pl.when(have_next & cond)
        def _():
            for cp in w_copies(e + 1, 1 - wslot):
                cp.start()
    first = tab_ref[BLOCK_START, e]
    count = tab_ref[BLOCK_COUNT, e]
    npair = count // 2
    odd = count % 2
    tslot = npair % 2

    def pair_rows(j):
        return pl.ds(pl.multiple_of((first + 2 * j) * bm, bm), 2 * bm)

    def x_copy(j, slot):
        return pltpu.make_async_copy(xs_hbm.at[pair_rows(j)], xbuf.at[slot], sem_in.at[slot])

    def y_copy(j, slot):
        return pltpu.make_async_copy(ybuf.at[slot], y_hbm.at[pair_rows(j)], sem_out.at[slot])

    def last_rows():
        return pl.ds(pl.multiple_of((first + 2 * npair) * bm, bm), bm)

    def x_last(slot):
        return pltpu.make_async_copy(xs_hbm.at[last_rows()], xbuf.at[slot, pl.ds(0, bm)], sem_in.at[slot])

    def y_last(slot):
        return pltpu.make_async_copy(ybuf.at[slot, pl.ds(0, bm)], y_hbm.at[last_rows()], sem_out.at[slot])

    @pl.when(npair > 0)
    def _():
        x_copy(0, 0).start()

    @pl.when((npair == 0) & (odd == 1))
    def _():
        x_last(0).start()

    for t in range(wgu_s.shape[0]):
        wgu_s[t] = wgu_f[wslot, :, t * ct:(t + 1) * ct].astype(BF16)
    for t in range(wd_s.shape[0]):
        wd_s[t] = wd_f[wslot, :, t * ct:(t + 1) * ct].astype(BF16)

    def drain_outputs(n_blocks):
        pairs = n_blocks // 2
        last = n_blocks % 2
        last_slot = pairs % 2

        def wait_rows(slot, rows):
            pltpu.make_async_copy(ybuf.at[slot, pl.ds(0, rows)], y_hbm.at[pl.ds(0, rows)], sem_out.at[slot]).wait()

        @pl.when((last == 0) & (pairs >= 2))
        def _():
            wait_rows(last_slot, 2 * bm)

        @pl.when(pairs >= 1)
        def _():
            wait_rows(1 - last_slot, 2 * bm)

        @pl.when(last == 1)
        def _():
            wait_rows(last_slot, bm)

    @pl.when(e > 0)
    def _():
        drain_outputs(tab_ref[BLOCK_COUNT, jnp.maximum(e - 1, 0)])

    def mlp(x):
        xb = x.astype(BF16)
        tpc = 2
        fc = tpc * ct
        nf = d_ff // fc
        n_out = wd_s.shape[0]

        def gate_up(f):
            gates, ups = [], []
            for t in range(f * tpc, (f + 1) * tpc):
                gates.append(_dot(xb, wgu_s[t]) + bgu_ref[:, t * ct:(t + 1) * ct])
                u = d_ff // ct + t
                ups.append(_dot(xb, wgu_s[u]) + bgu_ref[:, u * ct:(u + 1) * ct])
            return jnp.concatenate(gates, axis=-1), jnp.concatenate(ups, axis=-1)

        acts = []
        pre = gate_up(0)
        for f in range(nf):
            gate = jnp.minimum(pre[0], SWIGLU_LIMIT)
            up = jnp.clip(pre[1], -SWIGLU_LIMIT, SWIGLU_LIMIT)
            if f + 1 < nf:
                pre = gate_up(f + 1)
            acts.append(((up + 1.0) * gate * _sigmoid(SWIGLU_ALPHA * gate)).astype(BF16))
        act = jnp.concatenate(acts, axis=-1)
        return [_dot(act, wd_s[n]) + bd_ref[:, n * ct:(n + 1) * ct] for n in range(n_out)]

    def pair(j, carry):
        slot = j % 2
        x_copy(j, slot).wait()

        @pl.when(j + 1 < npair)
        def _():
            x_copy(j + 1, 1 - slot).start()

        @pl.when((j + 1 == npair) & (odd == 1))
        def _():
            x_last(1 - slot).start()

        @pl.when(j >= 2)
        def _():
            y_copy(j - 2, slot).wait()

        prefetch_next_weights(j == 0)
        for n, piece in enumerate(mlp(xbuf[slot])):
            ybuf[slot, :, n * ct:(n + 1) * ct] = piece
        y_copy(j, slot).start()
        return carry

    lax.fori_loop(0, npair, pair, 0)

    prefetch_next_weights(npair == 0)

    @pl.when(odd == 1)
    def _():
        x_last(tslot).wait()

        @pl.when(npair >= 2)
        def _():
            y_copy(npair - 2, tslot).wait()

        for n, piece in enumerate(mlp(xbuf[tslot, 0:bm, :])):
            ybuf[tslot, 0:bm, n * ct:(n + 1) * ct] = piece
        y_last(tslot).start()

    @pl.when(e == n_e - 1)
    def _():
        drain_outputs(count)

    @pl.when(e == pl.num_programs(0) - 1)
    def _():
        used = first + count
        n_unused = y_hbm.shape[0] // bm - used
        ybuf[0, 0:bm, :] = jnp.zeros((bm, ybuf.shape[2]), F32)

        def z_copy(j):
            d0 = pl.multiple_of((used + j) * bm, bm)
            return pltpu.make_async_copy(ybuf.at[0, pl.ds(0, bm)], y_hbm.at[pl.ds(d0, bm)], sem_out.at[0])

        def z_start(j, carry):
            z_copy(j).start()
            return carry

        def z_wait(j, carry):
            z_copy(j).wait()
            return carry

        lax.fori_loop(0, n_unused, z_start, 0)
        lax.fori_loop(0, n_unused, z_wait, 0)


def _experts(table, xs, wgu, bgu, wd, bd, bm):
    ns, d = xs.shape
    n_e, _, two_ff = wgu.shape
    d_ff = two_ff // 2
    wmap = lambda e: (e, 0, 0)
    return pl.pallas_call(
        _experts_kernel,
        grid=(n_e,),
        in_specs=[pl.BlockSpec(memory_space=pltpu.SMEM),
                  pl.BlockSpec(memory_space=pl.ANY),
                  pl.BlockSpec(memory_space=pl.ANY),
                  pl.BlockSpec((None, 1, two_ff), wmap),
                  pl.BlockSpec(memory_space=pl.ANY),
                  pl.BlockSpec((None, 1, d), wmap)],
        out_specs=pl.BlockSpec(memory_space=pl.ANY),
        out_shape=jax.ShapeDtypeStruct((ns, d), F32),
        scratch_shapes=[pltpu.VMEM((two_ff // MXU_COLS, d, MXU_COLS), BF16),
                        pltpu.VMEM((d // MXU_COLS, d_ff, MXU_COLS), BF16),
                        pltpu.VMEM((2, d, two_ff), F32), pltpu.VMEM((2, d_ff, d), F32),
                        pltpu.VMEM((2, 2 * bm, d), F32), pltpu.VMEM((2, 2 * bm, d), F32),
                        pltpu.SemaphoreType.DMA((2,)), pltpu.SemaphoreType.DMA((2,)),
                        pltpu.SemaphoreType.DMA((2,))],
        compiler_params=pltpu.CompilerParams(dimension_semantics=("arbitrary",), has_side_effects=True,
                                             vmem_limit_bytes=V7X_VMEM_LIMIT),
        name="experts",
    )(table, xs, wgu, bgu, wd, bd)


def _combine_kernel(len_ref, ssrc_ref, sdst_ref, pos_ref, w_ref, y_hbm, x1_ref, g2_ref, nw_ref, sh_ref, sc_ref,
                    o_ref, ybuf, sem):
    i = pl.program_id(0)
    nt = pl.num_programs(0)
    tt = x1_ref.shape[0]
    nrows = ybuf.shape[1]
    sizes = _segment_pieces(tt)
    slot = i % 2

    def seg_start(tile, which):
        def body(e, carry):
            _segment_dma(y_hbm, sdst_ref[tile, e], ybuf.at[which], ssrc_ref[tile, e], len_ref[tile, e], sizes,
                         sem.at[which], True)
            return carry
        lax.fori_loop(0, N_EXPERTS, body, 0)

    def seg_wait(tile, which):
        total = ssrc_ref[tile, N_EXPERTS - 1] + len_ref[tile, N_EXPERTS - 1]
        _segment_dma(y_hbm, 0, ybuf.at[which], 0, total, _segment_pieces(nrows), sem.at[which], False)

    @pl.when(i == 0)
    def _():
        ybuf[...] = jnp.zeros(ybuf.shape, F32)
        seg_start(0, 0)

    @pl.when(i + 1 < nt)
    def _():
        seg_start(i + 1, 1 - slot)

    seg_wait(i, slot)

    acc = None
    for r0, rn in _row_groups(nrows, 3):
        r_i = lax.broadcasted_iota(jnp.int32, (rn, tt), 0) + r0
        wmat = 0.0
        for kk in range(TOP_K):
            wmat = jnp.where(r_i == pos_ref[kk:kk + 1, :], w_ref[kk:kk + 1, :], wmat)
        part = _dot_tn(wmat.astype(BF16), ybuf[slot, r0:r0 + rn, :].astype(BF16))
        acc = part if acc is None else acc + part
    xo = x1_ref[...] + g2_ref[...] * acc
    ms = jnp.mean(xo * xo, axis=-1, keepdims=True)
    hn = xo * lax.rsqrt(ms + EPS) * nw_ref[...]
    o_ref[...] = hn * (1.0 + sc_ref[...]) + sh_ref[...]


def _combine(seg_len, seg_src, seg_dst, pos, wrow, y, x1, mod3, nw, modf3, seq, tt):
    t, d = x1.shape
    tps = seq // tt
    nrows = TOP_K * tt + 8 * N_EXPERTS
    smem = pl.BlockSpec(memory_space=pltpu.SMEM)
    return pl.pallas_call(
        _combine_kernel,
        grid=(t // tt,),
        in_specs=[smem, smem, smem,
                  pl.BlockSpec((TOP_K, tt), lambda i: (0, i)),
                  pl.BlockSpec((TOP_K, tt), lambda i: (0, i)),
                  pl.BlockSpec(memory_space=pl.ANY),
                  pl.BlockSpec((tt, d), lambda i: (i, 0)),
                  _mod_spec(tps, d, 5),
                  pl.BlockSpec((1, d), lambda i: (0, 0)),
                  _mod_spec(tps, d, 0),
                  _mod_spec(tps, d, 1)],
        out_specs=pl.BlockSpec((tt, d), lambda i: (i, 0)),
        out_shape=jax.ShapeDtypeStruct((t, d), F32),
        scratch_shapes=[pltpu.VMEM((2, nrows, d), F32), pltpu.SemaphoreType.DMA((2,))],
        compiler_params=pltpu.CompilerParams(dimension_semantics=("arbitrary",),
                                             vmem_limit_bytes=V7X_VMEM_LIMIT),
        name="combine",
    )(seg_len, seg_src, seg_dst, pos, wrow, y, x1, mod3, nw, modf3, modf3)


def _pick_tile(n, pref):
    tile = pref
    while n % tile:
        tile //= 2
    return tile


def kernel(x, c, w_ada, b_ada, norm_mix, w_in, dn_conv, dn_a_log, dn_dt_bias, dn_norm, ml_conv, ml_i_bias,
           ml_f_bias, ml_norm, w_out, norm_ffn, w_router, b_router, w_gate_up, b_gate_up, w_down, b_down,
           w_ada_final, b_ada_final, norm_final):
    batch, seq, d = x.shape
    assert w_ada.shape[0] == 1, "single-layer block"
    assert seq % CHUNK == 0
    t = batch * seq
    x2 = x.reshape(t, d)

    c_pad = jnp.pad(c, ((0, 8 - batch % 8 if batch % 8 else 0), (0, 0)))
    mod3 = _mods(c_pad, w_ada.reshape(d, 6 * d), b_ada.reshape(1, 6 * d)).reshape(-1, 1, 6 * d)
    modf3 = _mods(c_pad, w_ada_final, b_ada_final.reshape(1, 2 * d)).reshape(-1, 1, 2 * d)

    wi = w_in.reshape(d, -1)
    o_z = 1536
    o_b = 2048
    o_mq = 2056
    o_mk = o_mq + ML_HEADS * ML_DK
    o_mv = o_mk + ML_HEADS * ML_DK
    o_mo = o_mv + ML_HEADS * ML_DV
    o_mi = o_mo + ML_HEADS * ML_DV
    gates = jnp.concatenate([wi[:, o_b:o_mq], wi[:, o_mi:o_mi + 2 * ML_HEADS]], axis=1)
    w_new = jnp.concatenate([
        wi[:, 0:o_z],
        wi[:, o_mq:o_mv],
        wi[:, o_z:o_b],
        wi[:, o_mv:o_mo],
        wi[:, o_mo:o_mi],
        jnp.pad(gates, ((0, 0), (0, HEAD_LANES - 16))),
    ], axis=1).astype(BF16)
    wgt = gates.T.astype(BF16)
    cw = jnp.concatenate([dn_conv.reshape(CONV_W, -1), ml_conv.reshape(CONV_W, -1)], axis=1)
    zeros4 = jnp.zeros((4,), F32)
    bias16 = jnp.concatenate([zeros4, dn_dt_bias.reshape(4), ml_i_bias.reshape(4), ml_f_bias.reshape(4)])
    alog16 = jnp.concatenate([zeros4, dn_a_log.reshape(4), zeros4, zeros4])
    gpc = jnp.zeros((8, HEAD_LANES), F32).at[0, 0:16].set(bias16).at[1, 0:16].set(alog16)
    gpr = jnp.zeros((16, HEAD_LANES), F32).at[:, 0].set(bias16).at[:, 1].set(alog16)

    tm_in = _pick_tile(seq, 512)
    conv_out, rest, gcol, grow = _inproj(x2, mod3, norm_mix.reshape(1, d), w_new, wgt, cw, gpc, gpr, seq, tm_in)
    grow3 = grow.reshape(16, t // CHUNK, CHUNK).transpose(1, 0, 2)

    rows = _pick_tile(seq, 256)
    ymix = _mixers(conv_out, rest, gcol, grow3, dn_norm.reshape(1, DN_DV), ml_norm.reshape(1, ML_HEADS * ML_DV),
                   batch, seq, rows)

    wo = w_out.reshape(-1, d).astype(BF16)
    tm_r = _pick_tile(seq, 512)
    brp = jnp.broadcast_to(b_router.reshape(N_EXPERTS, 1), (N_EXPERTS, HEAD_LANES))
    x1, h2, pos, wrow, len_col = _route(
        ymix, wo, x2, mod3, norm_ffn.reshape(1, d),
        w_router.reshape(d, N_EXPERTS).T, brp, seq, tm_r)

    n_e = N_EXPERTS
    nt = t // tm_r
    bm = 256
    len_te =len_col.reshape(nt, n_e, HEAD_LANES)[:, :, 0]
    seg_len = jnp.pad(len_te, ((0, 0), (0, HEAD_LANES - n_e)))
    n_slots_max = t * TOP_K + n_e * (7 * nt + bm)
    nb = (n_slots_max + bm - 1) // bm
    seg_src, seg_dst, table = _slots(seg_len, bm)

    xs = _dispatch(seg_len, seg_src, seg_dst, table, pos, h2, nb * bm, tm_r, bm)
    y = _experts(table, xs, w_gate_up.reshape(n_e, d, -1), b_gate_up.reshape(n_e, 1, -1),
                 w_down.reshape(n_e, -1, d), b_down.reshape(n_e, 1, d), bm)
    out = _combine(seg_len, seg_src, seg_dst, pos, wrow, y, x1, mod3, norm_final.reshape(1, d), modf3, seq, tm_r)
    return out.reshape(batch, seq, d)
```

```python
import functools

import jax
import jax.numpy as jnp
from jax import lax
from jax.experimental import pallas as pl
from jax.experimental.pallas import tpu as pltpu

F32 = jnp.float32
BF16 = jnp.bfloat16

CHUNK = 64
CONV_W = 4
EPS = 1e-6

DN_HEADS = 4
DN_DK = 128
DN_DV = 128
ML_HEADS = 4
ML_DK = 64
ML_DV = 128
HEAD_LANES = 128

N_EXPERTS = 32
TOP_K = 4
SWIGLU_LIMIT = 7.0
SWIGLU_ALPHA = 1.702

C_DNQ = 0
C_DNK = 512
C_DNV = 1024
C_MLQ = 1536
C_MLK = 1792
N_CONV = 2048
C_DNZ = 2048
C_MLV = 2560
C_MLO = 3072
C_GATE = 3584
N_PROJ = 3712
N_REST = C_GATE - N_CONV

V7X_VMEM_LIMIT = 56 * 1024 * 1024
MXU_COLS = 256
COMMON_SEGMENT_ROWS = 128
INPROJ_GROUP_COLS = 512

NEG_BIG = -1e30


def _sigmoid(x):
    return 1.0 / (1.0 + jnp.exp(-x))


def _softplus(x):
    return jnp.maximum(x, 0.0) + jnp.log(1.0 + jnp.exp(-jnp.abs(x)))


def _split3(v):
    hi = v.astype(BF16)
    r1 = v - hi.astype(F32)
    mid = r1.astype(BF16)
    lo = (r1 - mid.astype(F32)).astype(BF16)
    return hi, mid, lo


def _dot(a, b):
    return jnp.dot(a, b, preferred_element_type=F32)


def _dot_nt(a, b):
    return lax.dot_general(a, b, (((1,), (1,)), ((), ())), preferred_element_type=F32)


def _dot_tn(a, b):
    return lax.dot_general(a, b, (((0,), (0,)), ((), ())), preferred_element_type=F32)


def _dot_exact_right(sel_bf16, v):
    hi, mid, lo = _split3(v)
    return _dot(sel_bf16, hi) + _dot(sel_bf16, mid) + _dot(sel_bf16, lo)


def _dot_exact_left(v, sel_bf16):
    hi, mid, lo = _split3(v)
    return _dot(hi, sel_bf16) + _dot(mid, sel_bf16) + _dot(lo, sel_bf16)


def _mods_kernel(c_ref, w_ref, b_ref, o_ref):
    c = c_ref[...]
    cond = c * _sigmoid(c)
    ch, cm, _ = _split3(cond)
    wh, wm, _ = _split3(w_ref[...])
    acc = _dot(ch, wh) + (_dot(ch, wm) + _dot(cm, wh))
    o_ref[...] = acc + b_ref[...]


def _mods(c_pad, w, b):
    m, d = c_pad.shape
    n = w.shape[2]
    tn = 1024
    return pl.pallas_call(
        _mods_kernel,
        grid=(n // tn,),
        in_specs=[pl.BlockSpec((m, d), lambda j: (0, 0)),
                  pl.BlockSpec((None, d, tn), lambda j: (0, 0, j)),
                  pl.BlockSpec((1, tn), lambda j: (0, j))],
        out_specs=pl.BlockSpec((m, tn), lambda j: (0, j)),
        out_shape=jax.ShapeDtypeStruct((m, n), F32),
        compiler_params=pltpu.CompilerParams(dimension_semantics=("arbitrary",),
                                             vmem_limit_bytes=V7X_VMEM_LIMIT),
        name="mods",
    )(c_pad, w, b)


SRC_DNZ = 1536
SRC_DNGATE = 2048
SRC_MLQK = 2056
SRC_MLV = SRC_MLQK + 2 * ML_HEADS * ML_DK
SRC_MLO = SRC_MLV + ML_HEADS * ML_DV
SRC_MLGATE = SRC_MLO + ML_HEADS * ML_DV
N_GATES = 16


def _wprep_kernel(w_ref, o_ref, gt_ref):
    w = w_ref[...]
    o_ref[:, C_DNQ:C_MLQ] = w[:, 0:SRC_DNZ].astype(BF16)
    o_ref[:, C_MLQ:N_CONV] = w[:, SRC_MLQK:SRC_MLV].astype(BF16)
    o_ref[:, C_DNZ:C_MLV] = w[:, SRC_DNZ:SRC_DNGATE].astype(BF16)
    o_ref[:, C_MLV:C_MLO] = w[:, SRC_MLV:SRC_MLO].astype(BF16)
    o_ref[:, C_MLO:C_GATE] = w[:, SRC_MLO:SRC_MLGATE].astype(BF16)
    gates = jnp.concatenate([w[:, SRC_DNGATE:SRC_MLQK], w[:, SRC_MLGATE:SRC_MLGATE + N_GATES // 2],
                             jnp.zeros((w.shape[0], HEAD_LANES - N_GATES), F32)], axis=1)
    o_ref[:, C_GATE:N_PROJ] = gates.astype(BF16)
    gt_ref[...] = gates.T[0:N_GATES, :].astype(BF16)


def _wprep(w_in):
    _, d, n_in = w_in.shape
    assert n_in == SRC_MLGATE + N_GATES // 2
    rows = _pick_tile(d, 256)
    return pl.pallas_call(
        _wprep_kernel,
        grid=(d // rows,),
        in_specs=[pl.BlockSpec((None, rows, n_in), lambda i: (0, i, 0))],
        out_specs=[pl.BlockSpec((rows, N_PROJ), lambda i: (i, 0)),
                   pl.BlockSpec((N_GATES, rows), lambda i: (0, i))],
        out_shape=[jax.ShapeDtypeStruct((d, N_PROJ), BF16), jax.ShapeDtypeStruct((N_GATES, d), BF16)],
        compiler_params=pltpu.CompilerParams(dimension_semantics=("arbitrary",),
                                             vmem_limit_bytes=V7X_VMEM_LIMIT),
        name="wprep",
    )(w_in)


def _gate_transform(v, bias, alog, cls):
    vb = v + bias
    beta = _sigmoid(v)
    g = -jnp.exp(alog) * _softplus(vb)
    logf = -_softplus(-vb)
    return jnp.where(cls == 0, beta, jnp.where(cls == 1, g, jnp.where(cls == 2, vb, jnp.where(cls == 3, logf, 0.0))))


def _inproj_kernel(tiles_per_seq, x_ref, sh_ref, sc_ref, nw_ref, w_ref, wgt_ref, cw_ref, gpc_ref, gpr_ref,
                   conv_ref, rest_ref, gcol_ref, grow_ref, cbuf):
    tm = x_ref.shape[0]
    i = pl.program_id(0)
    x = x_ref[...]
    ms = jnp.mean(x * x, axis=-1, keepdims=True)
    h = x * lax.rsqrt(ms + EPS) * nw_ref[...]
    h = h * (1.0 + sc_ref[...]) + sh_ref[...]
    hb = h.astype(BF16)

    @pl.when(i % tiles_per_seq == 0)
    def _():
        cbuf[0:8, :] = jnp.zeros((8, N_CONV), F32)

    group = INPROJ_GROUP_COLS
    for lo in range(0, N_CONV, group):
        cols = slice(lo, lo + group)
        pc = _dot(hb, w_ref[:, cols])
        cbuf[8:tm + 8, cols] = pc
        acc = cw_ref[CONV_W - 1:CONV_W, cols] * pc
        for j in range(CONV_W - 1):
            acc = acc + cw_ref[j:j + 1, cols] * cbuf[8 - (CONV_W - 1) + j:8 - (CONV_W - 1) + j + tm, cols]
        cbuf[0:8, cols] = cbuf[tm:tm + 8, cols]
        y = acc * _sigmoid(acc)
        if lo < C_DNV:
            for h0 in range(0, group, HEAD_LANES):
                uh = y[:, h0:h0 + HEAD_LANES]
                un = uh * lax.rsqrt(jnp.sum(uh * uh, axis=-1, keepdims=True) + EPS)
                conv_ref[:, lo + h0:lo + h0 + HEAD_LANES] = un * (DN_DK ** -0.5) if lo < C_DNK else un
        else:
            for h0 in range(0, group, HEAD_LANES):
                piece = y[:, h0:h0 + HEAD_LANES]
                is_mlq = C_MLQ <= lo + h0 < C_MLK
                conv_ref[:, lo + h0:lo + h0 + HEAD_LANES] = piece * (ML_DK ** -0.5) if is_mlq else piece

    z = _dot(hb, w_ref[:, C_DNZ:C_MLV])
    rest_ref[:, 0:512] = z * _sigmoid(z)
    rest_ref[:, 512:1024] = _dot(hb, w_ref[:, C_MLV:C_MLO])
    rest_ref[:, 1024:1536] = _sigmoid(_dot(hb, w_ref[:, C_MLO:C_GATE]))

    r_i = lax.broadcasted_iota(jnp.int32, (tm, tm), 0)
    c_i = lax.broadcasted_iota(jnp.int32, (tm, tm), 1)
    same_chunk = (r_i // CHUNK) == (c_i // CHUNK)
    tril = jnp.where(same_chunk & (c_i <= r_i), 1.0, 0.0).astype(BF16)
    triu = jnp.where(same_chunk & (r_i <= c_i), 1.0, 0.0).astype(BF16)

    gc = _dot(hb, w_ref[:, C_GATE:N_PROJ])
    cls_c = lax.broadcasted_iota(jnp.int32, (tm, HEAD_LANES), 1) // 4
    gt = _gate_transform(gc, gpc_ref[0:1, :], gpc_ref[1:2, :], cls_c)
    cs = _dot_exact_right(tril, gt)
    gcol_ref[...] = jnp.where((cls_c == 1) | (cls_c == 3), cs, gt)

    gr = _dot_nt(wgt_ref[...], hb)
    cls_r = lax.broadcasted_iota(jnp.int32, (16, tm), 0) // 4
    gtr = _gate_transform(gr, gpr_ref[:, 0:1], gpr_ref[:, 1:2], cls_r)
    csr = _dot_exact_left(gtr, triu)
    grow_ref[...] = jnp.where((cls_r == 1) | (cls_r == 3), csr, gtr)


def _mod_spec(tiles_per_seq, d, j):
    return pl.BlockSpec((None, 1, d), lambda i: (i // tiles_per_seq, 0, j))


def _inproj(x2, mod3, nw, w_new, wgt, cw, gpc, gpr, seq, tm):
    t, d = x2.shape
    tps = seq // tm
    kern = functools.partial(_inproj_kernel, tps)
    return pl.pallas_call(
        kern,
        grid=(t // tm,),
        in_specs=[pl.BlockSpec((tm, d), lambda i: (i, 0)),
                  _mod_spec(tps, d, 0),
                  _mod_spec(tps, d, 1),
                  pl.BlockSpec((1, d), lambda i: (0, 0)),
                  pl.BlockSpec((d, N_PROJ), lambda i: (0, 0)),
                  pl.BlockSpec((16, d), lambda i: (0, 0)),
                  pl.BlockSpec((CONV_W, N_CONV), lambda i: (0, 0)),
                  pl.BlockSpec((8, HEAD_LANES), lambda i: (0, 0)),
                  pl.BlockSpec((16, HEAD_LANES), lambda i: (0, 0))],
        out_specs=[pl.BlockSpec((tm, N_CONV), lambda i: (i, 0)),
                   pl.BlockSpec((tm, N_REST), lambda i: (i, 0)),
                   pl.BlockSpec((tm, HEAD_LANES), lambda i: (i, 0)),
                   pl.BlockSpec((16, tm), lambda i: (0, i))],
        out_shape=[jax.ShapeDtypeStruct((t, N_CONV), F32),
                   jax.ShapeDtypeStruct((t, N_REST), F32),
                   jax.ShapeDtypeStruct((t, HEAD_LANES), F32),
                   jax.ShapeDtypeStruct((16, t), F32)],
        scratch_shapes=[pltpu.VMEM((tm + 8, N_CONV), F32)],
        compiler_params=pltpu.CompilerParams(dimension_semantics=("arbitrary",),
                                             vmem_limit_bytes=V7X_VMEM_LIMIT),
        name="inproj",
    )(x2, mod3, mod3, nw, w_new, wgt, cw, gpc, gpr)


def _chunk_masks():
    r = lax.broadcasted_iota(jnp.int32, (CHUNK, CHUNK), 0)
    c = lax.broadcasted_iota(jnp.int32, (CHUNK, CHUNK), 1)
    return r >= c, r > c, r == c


def _bdot(a, b):
    return lax.dot_general(a, b, (((2,), (1,)), ((0,), (0,))), preferred_element_type=F32)


def _bdot_nt(a, b):
    return lax.dot_general(a, b, (((2,), (2,)), ((0,), (0,))), preferred_element_type=F32)


def _unit_lower_inverse(lower, row, col):
    x = jnp.where(row == col, 1.0, 0.0) - jnp.where((row >> 1) == (col >> 1), lower, 0.0)
    shift = 1
    while (1 << shift) < CHUNK:
        couple = ((row >> (shift + 1)) == (col >> (shift + 1))) & ((row >> shift) != (col >> shift))
        cb = jnp.where(couple, lower, 0.0).astype(BF16)
        xb = x.astype(BF16)
        x = x - _bdot(_bdot(xb, cb).astype(BF16), xb)
        shift += 1
        yield
    return x


def _seq_heads(n_seq, n_heads):
    return [(s, hh) for s in range(n_seq) for hh in range(n_heads)]


def _deltanet_steps(q_ref, k_ref, v_ref, gc_ref, gr_ref, z_ref, nw_ref, o_ref, s_ref):
    n_seq, nc = gr_ref.shape[0], gr_ref.shape[1]

    @pl.when(pl.program_id(1) == 0)
    def _():
        s_ref[...] = jnp.zeros(s_ref.shape, F32)

    row = lax.broadcasted_iota(jnp.int32, (CHUNK, CHUNK), 0)
    col = lax.broadcasted_iota(jnp.int32, (CHUNK, CHUNK), 1)
    incl = row >= col
    strict = row > col
    nw = nw_ref[...]
    gcc = gc_ref[...]
    grr = gr_ref[...]

    streams = _seq_heads(n_seq, DN_HEADS)
    nh = len(streams)
    nb = nh * nc

    def heads(ref):
        return jnp.stack([ref[s, :, hh * HEAD_LANES:(hh + 1) * HEAD_LANES] for s, hh in streams],
                         axis=0).reshape(nb, CHUNK, HEAD_LANES)

    def col_gate(lane0):
        return jnp.stack([gcc[s, :, lane0 + hh:lane0 + hh + 1] for s, hh in streams], axis=0).reshape(nb, CHUNK, 1)

    q = heads(q_ref)
    k = heads(k_ref)
    v = heads(v_ref)
    beta = col_gate(0)
    g_c = col_gate(4)
    g_r = jnp.stack([grr[s, :, 4 + hh:5 + hh, :] for s, hh in streams], axis=0).reshape(nb, 1, CHUNK)
    g_last = g_c[:, CHUNK - 1:CHUNK, :]
    decay = jnp.exp(jnp.where(incl, g_c - g_r, NEG_BIG))
    kb = k.astype(BF16)
    kk = _bdot_nt(kb, kb)
    lower = jnp.where(strict, beta * kk * decay, 0.0)
    yield
    tinv = yield from _unit_lower_inverse(lower, row, col)
    eg = jnp.exp(g_c)
    rhs = jnp.concatenate([v * beta, k * (beta * eg)], axis=-1)
    sol = _bdot(tinv.astype(BF16), rhs.astype(BF16))
    yield
    w_val = sol[:, :, 0:DN_DV].reshape(nh, nc, CHUNK, DN_DV)
    kq = jnp.concatenate([sol[:, :, DN_DV:DN_DV + DN_DK], q * eg], axis=1).astype(BF16)
    kq = kq.reshape(nh, nc, 2 * CHUNK, DN_DK)
    qk = (_bdot_nt(q.astype(BF16), kb) * decay).astype(BF16).reshape(nh, nc, CHUNK, CHUNK)
    k_dec_t = jnp.swapaxes(k * jnp.exp(g_last - g_c), 1, 2).astype(BF16).reshape(nh, nc, DN_DK, CHUNK)
    s_dec = jnp.exp(g_last).reshape(nh, nc, 1, 1)
    yield

    state = s_ref[...]
    outs = []
    for c in range(nc):
        both = _bdot(kq[:, c], state.astype(BF16))
        v_new = w_val[:, c] - both[:, 0:CHUNK]
        vb = v_new.astype(BF16)
        outs.append(both[:, CHUNK:2 * CHUNK] + _bdot(qk[:, c], vb))
        state = s_dec[:, c] * state + _bdot(k_dec_t[:, c], vb)
        yield
    s_ref[...] = state

    o = jnp.stack(outs, axis=1)
    on = o * lax.rsqrt(jnp.mean(o * o, axis=-1, keepdims=True) + EPS) * nw
    on = on.reshape(nh, nc * CHUNK, DN_DV)
    for idx, (s, hh) in enumerate(streams):
        lanes = slice(hh * HEAD_LANES, (hh + 1) * HEAD_LANES)
        o_ref[s, :, lanes] = on[idx] * z_ref[s, :, lanes]


def _mlstm_steps(q_ref, k_ref, v_ref, gc_ref, gr_ref, og_ref, nw_ref, o_ref, c_ref, n_ref, m_ref):
    n_seq, nc = gr_ref.shape[0], gr_ref.shape[1]

    @pl.when(pl.program_id(1) == 0)
    def _():
        c_ref[...] = jnp.zeros(c_ref.shape, F32)
        n_ref[...] = jnp.zeros(n_ref.shape, F32)
        m_ref[...] = jnp.zeros(m_ref.shape, F32)

    incl, _, _ = _chunk_masks()
    gcc = gc_ref[...]
    grr = gr_ref[...]

    streams = _seq_heads(n_seq, ML_HEADS)
    nh = len(streams)

    def heads(ref):
        return jnp.stack([ref[s, :, hh * HEAD_LANES:(hh + 1) * HEAD_LANES] for s, hh in streams],
                         axis=0).reshape(nh, nc, CHUNK, HEAD_LANES)

    def col_gate(lane0):
        return jnp.stack([gcc[s, :, lane0 + hh:lane0 + hh + 1] for s, hh in streams],
                         axis=0).reshape(nh, nc, CHUNK, 1)

    def row_gate(row0):
        return jnp.stack([grr[s, :, row0 + hh:row0 + hh + 1, :] for s, hh in streams], axis=0)

    def qk_heads(ref):
        lane = lax.broadcasted_iota(jnp.int32, (ref.shape[1], HEAD_LANES), 1)
        tiles = []
        for s, hh in streams:
            tile = ref[s, :, (hh // 2) * HEAD_LANES:(hh // 2 + 1) * HEAD_LANES]
            mine = (lane >= ML_DK) if hh % 2 else (lane < ML_DK)
            tiles.append(jnp.where(mine, tile, 0.0))
        return jnp.stack(tiles, axis=0).reshape(nh, nc, CHUNK, HEAD_LANES)

    q = qk_heads(q_ref)
    k = qk_heads(k_ref)
    v = heads(v_ref)
    i_c = col_gate(8)
    b_c = col_gate(12)
    i_r = row_gate(8)
    b_r = row_gate(12)
    b_last = b_c[:, :, CHUNK - 1:CHUNK, :]
    d_mat = jnp.where(incl, b_c - b_r + i_r, NEG_BIG)
    m_intra = jnp.max(d_mat, axis=-1, keepdims=True)
    g_end = b_last - b_c + i_c
    g_end_max = jnp.max(g_end, axis=2, keepdims=True)
    yield

    m_run = m_ref[:, 0:1, 0:1].reshape(nh, 1, 1, 1)
    m_before = []
    for c in range(nc):
        m_before.append(m_run)
        m_run = jnp.maximum(b_last[:, c:c + 1] + m_run, g_end_max[:, c:c + 1])
    m_s = jnp.concatenate(m_before, axis=1)
    m_new = jnp.maximum(b_last + m_s, g_end_max)
    keep = jnp.exp(b_last + m_s - m_new)
    yield

    nb = nh * nc
    qb = q.astype(BF16)
    kb = k.astype(BF16)
    vb = v.astype(BF16)
    m_t = jnp.maximum(b_c + m_s, m_intra)
    inter = jnp.exp(b_c + m_s - m_t)
    qk = _bdot_nt(qb.reshape(nb, CHUNK, HEAD_LANES), kb.reshape(nb, CHUNK, HEAD_LANES))
    p = jnp.exp(d_mat - m_t) * qk.reshape(nh, nc, CHUNK, CHUNK)
    yield
    intra = _bdot(p.astype(BF16).reshape(nb, CHUNK, CHUNK), vb.reshape(nb, CHUNK, ML_DV)).reshape(nh, nc, CHUNK, ML_DV)
    p_sum = jnp.sum(p, axis=-1, keepdims=True)
    yield
    kw = k * jnp.exp(g_end - m_new)
    kw_t = jnp.swapaxes(kw.reshape(nb, CHUNK, HEAD_LANES), 1, 2).astype(BF16)
    d_state = _bdot(kw_t, vb.reshape(nb, CHUNK, ML_DV)).reshape(nh, nc, HEAD_LANES, ML_DV)
    kw_sum = jnp.sum(kw, axis=2, keepdims=True)
    yield

    c_s = c_ref[...]
    n_s = n_ref[:, 0:1, :]
    q_c = []
    q_n = []
    for c in range(nc):
        q_c.append(_bdot(qb[:, c], c_s.astype(BF16)))
        q_n.append(jnp.sum(q[:, c] * n_s, axis=-1, keepdims=True))
        c_s = keep[:, c] * c_s + d_state[:, c]
        n_s = keep[:, c] * n_s + kw_sum[:, c]
        yield
    c_ref[...] = c_s
    n_ref[...] = jnp.broadcast_to(n_s, (nh, 8, HEAD_LANES))
    m_ref[...] = jnp.broadcast_to(m_run.reshape(nh, 1, 1), (nh, 8, HEAD_LANES))

    num = inter * jnp.stack(q_c, axis=1) + intra
    den = inter * jnp.stack(q_n, axis=1) + p_sum
    h = num / jnp.maximum(jnp.abs(den), jnp.exp(-m_t))
    hr = h * lax.rsqrt(jnp.mean(h * h, axis=-1, keepdims=True) + EPS)
    hr = hr.reshape(nh, nc * CHUNK, ML_DV)
    for idx, (s, hh) in enumerate(streams):
        lanes = slice(hh * HEAD_LANES, (hh + 1) * HEAD_LANES)
        o_ref[s, :, lanes] = hr[idx] * nw_ref[:, lanes] * og_ref[s, :, lanes]


def _mixers_kernel(dq_ref, dk_ref, dv_ref, gc_ref, gr_ref, z_ref, dnw_ref, mq_ref, mk_ref, mv_ref, og_ref, mnw_ref,
                   o_ref, s_ref, c_ref, n_ref, m_ref):
    n_a = DN_HEADS * HEAD_LANES
    n_b = ML_HEADS * HEAD_LANES
    stages = [_deltanet_steps(dq_ref, dk_ref, dv_ref, gc_ref, gr_ref, z_ref, dnw_ref, o_ref.at[:, :, 0:n_a], s_ref),
              _mlstm_steps(mq_ref, mk_ref, mv_ref, gc_ref, gr_ref, og_ref, mnw_ref, o_ref.at[:, :, n_a:n_a + n_b],
                           c_ref, n_ref, m_ref)]
    while stages:
        for stage in list(stages):
            try:
                next(stage)
            except StopIteration:
                stages.remove(stage)


def _mixers(conv_out, rest, gcol, grow3, dn_nw, ml_nw, batch, seq, rows):
    t = conv_out.shape[0]
    nj = seq // rows
    cpb = rows // CHUNK
    width = 4 * HEAD_LANES
    assert DN_HEADS * HEAD_LANES == width and ML_HEADS * HEAD_LANES == width
    n_seq = 2 if batch % 2 == 0 else 1
    conv3 = conv_out.reshape(batch, seq, -1)
    rest3 = rest.reshape(batch, seq, -1)
    gcol3 = gcol.reshape(batch, seq, -1)
    grow4 = grow3.reshape(batch, seq // CHUNK, 16, CHUNK)

    def conv_block(c0):
        return pl.BlockSpec((n_seq, rows, width), lambda b, j: (b, j, c0 // width))

    def rest_block(c0):
        return pl.BlockSpec((n_seq, rows, width), lambda b, j: (b, j, (c0 - N_CONV) // width))

    ml_qk_width = ML_HEADS * ML_DK

    def ml_qk_block(c0):
        return pl.BlockSpec((n_seq, rows, ml_qk_width), lambda b, j: (b, j, c0 // ml_qk_width))

    gates = [pl.BlockSpec((n_seq, rows, HEAD_LANES), lambda b, j: (b, j, 0)),
             pl.BlockSpec((n_seq, cpb, 16, CHUNK), lambda b, j: (b, j, 0, 0))]
    out = pl.pallas_call(
        _mixers_kernel,
        grid=(batch // n_seq, nj),
        in_specs=[conv_block(C_DNQ), conv_block(C_DNK), conv_block(C_DNV)] + gates
                 + [rest_block(C_DNZ), pl.BlockSpec((1, HEAD_LANES), lambda b, j: (0, 0)),
                    ml_qk_block(C_MLQ), ml_qk_block(C_MLK), rest_block(C_MLV), rest_block(C_MLO),
                    pl.BlockSpec((1, width), lambda b, j: (0, 0))],
        out_specs=pl.BlockSpec((n_seq, rows, 2 * width), lambda b, j: (b, j, 0)),
        out_shape=jax.ShapeDtypeStruct((batch, seq, 2 * width), F32),
        scratch_shapes=[pltpu.VMEM((n_seq * DN_HEADS, DN_DK, DN_DV), F32),
                        pltpu.VMEM((n_seq * ML_HEADS, HEAD_LANES, ML_DV), F32),
                        pltpu.VMEM((n_seq * ML_HEADS, 8, HEAD_LANES), F32),
                        pltpu.VMEM((n_seq * ML_HEADS, 8, HEAD_LANES), F32)],
        compiler_params=pltpu.CompilerParams(dimension_semantics=("arbitrary", "arbitrary"),
                                             vmem_limit_bytes=V7X_VMEM_LIMIT),
        name="mixers",
    )(conv3, conv3, conv3, gcol3, grow4, rest3, dn_nw, conv3, conv3, rest3, rest3, ml_nw)
    return out.reshape(t, 2 * width)


def _route_kernel(y_ref, wo_ref, x_ref, g1_ref, sh_ref, sc_ref, nw_ref, wrt_ref, br_ref,
                  x1_ref, h2_ref, pos_ref, wrow_ref, len_ref):
    tm = x_ref.shape[0]

    mix = _dot(y_ref[...].astype(BF16), wo_ref[...])
    x1 = x_ref[...] + g1_ref[...] * mix
    x1_ref[...] = x1
    ms = jnp.mean(x1 * x1, axis=-1, keepdims=True)
    h2 = x1 * lax.rsqrt(ms + EPS) * nw_ref[...]
    h2 = h2 * (1.0 + sc_ref[...]) + sh_ref[...]
    h2_ref[...] = h2.astype(BF16)

    hh, hm, _ = _split3(h2)
    wh, wm, _ = _split3(wrt_ref[...])
    logits = _dot_nt(wh, hh) + (_dot_nt(wh, hm) + _dot_nt(wm, hh)) + br_ref[:, 0:1]

    e_i = lax.broadcasted_iota(jnp.int32, (N_EXPERTS, tm), 0)
    work = logits
    tops = []
    sels = []
    hots = []
    for _ in range(TOP_K):
        m = jnp.max(work, axis=0, keepdims=True)
        sel = jnp.min(jnp.where(work == m, e_i, N_EXPERTS), axis=0, keepdims=True)
        hot = e_i == sel
        work = jnp.where(hot, NEG_BIG, work)
        tops.append(m)
        sels.append(sel)
        hots.append(hot)
    exps = [jnp.exp(tl - tops[0]) for tl in tops]
    denom = exps[0] + exps[1] + exps[2] + exps[3]
    ws = [e / denom for e in exps]

    chosen = jnp.zeros((N_EXPERTS, tm), F32)
    for hot in hots:
        chosen = chosen + jnp.where(hot, 1.0, 0.0)

    r_i = lax.broadcasted_iota(jnp.int32, (tm, tm), 0)
    c_i = lax.broadcasted_iota(jnp.int32, (tm, tm), 1)
    strict_upper = jnp.where(r_i < c_i, 1.0, 0.0).astype(BF16)
    prefix = _dot(chosen.astype(BF16), strict_upper)
    n_e = jnp.sum(chosen, axis=1, keepdims=True)
    len8 = jnp.ceil(n_e * 0.125) * 8.0
    er = lax.broadcasted_iota(jnp.int32, (N_EXPERTS, N_EXPERTS), 0)
    ec = lax.broadcasted_iota(jnp.int32, (N_EXPERTS, N_EXPERTS), 1)
    strict_lower = jnp.where(ec < er, 1.0, 0.0).astype(BF16)
    len8b = jnp.broadcast_to(len8, (N_EXPERTS, HEAD_LANES))
    off8 = _dot_exact_right(strict_lower, len8b)[:, 0:1]
    len_ref[...] = len8b.astype(jnp.int32)

    for kk in range(TOP_K):
        wrow_ref[kk:kk + 1, :] = ws[kk]
        pos = jnp.sum(jnp.where(hots[kk], prefix + off8, 0.0), axis=0, keepdims=True)
        pos_ref[kk:kk + 1, :] = pos.astype(jnp.int32)


def _route(ymix, wo, x2, mod3, nw, wrt, br, seq, tm):
    t, d = x2.shape
    tps = seq // tm
    return pl.pallas_call(
        _route_kernel,
        grid=(t // tm,),
        in_specs=[pl.BlockSpec((tm, ymix.shape[1]), lambda i: (i, 0)),
                  pl.BlockSpec(wo.shape, lambda i: (0, 0)),
                  pl.BlockSpec((tm, d), lambda i: (i, 0)),
                  _mod_spec(tps, d, 2),
                  _mod_spec(tps, d, 3),
                  _mod_spec(tps, d, 4),
                  pl.BlockSpec((1, d), lambda i: (0, 0)),
                  pl.BlockSpec((N_EXPERTS, d), lambda i: (0, 0)),
                  pl.BlockSpec((N_EXPERTS, HEAD_LANES), lambda i: (0, 0))],
        out_specs=[pl.BlockSpec((tm, d), lambda i: (i, 0)),
                   pl.BlockSpec((tm, d), lambda i: (i, 0)),
                   pl.BlockSpec((TOP_K, tm), lambda i: (0, i)),
                   pl.BlockSpec((TOP_K, tm), lambda i: (0, i)),
                   pl.BlockSpec((N_EXPERTS, HEAD_LANES), lambda i: (i, 0))],
        out_shape=[jax.ShapeDtypeStruct((t, d), F32),
                   jax.ShapeDtypeStruct((t, d), BF16),
                   jax.ShapeDtypeStruct((TOP_K, t), jnp.int32),
                   jax.ShapeDtypeStruct((TOP_K, t), F32),
                   jax.ShapeDtypeStruct((t // tm * N_EXPERTS, HEAD_LANES), jnp.int32)],
        compiler_params=pltpu.CompilerParams(dimension_semantics=("arbitrary",),
                                             vmem_limit_bytes=V7X_VMEM_LIMIT),
        name="route",
    )(ymix, wo, x2, mod3, mod3, mod3, nw, wrt, br)


TAIL_START, TAIL_LEN, PAD_END, BLOCK_START, BLOCK_COUNT = range(5)


def _slots_kernel(bm, len_te_ref, ssrc_ref, sdst_ref, tab_ref):
    nt, lanes = len_te_ref.shape
    len_te = len_te_ref[...].astype(F32)

    r_l = lax.broadcasted_iota(jnp.int32, (lanes, lanes), 0)
    c_l = lax.broadcasted_iota(jnp.int32, (lanes, lanes), 1)
    upper_incl = jnp.where(r_l <= c_l, 1.0, 0.0).astype(BF16)
    upper_strict = jnp.where(r_l < c_l, 1.0, 0.0).astype(BF16)
    total_r = jnp.sum(len_te, axis=0, keepdims=True)
    padded_r = jnp.ceil(total_r * (1.0 / bm)) * bm
    pad_end_r = _dot_exact_left(jnp.broadcast_to(padded_r, (8, lanes)), upper_incl)[0:1, :]
    pad_start_r = pad_end_r - padded_r
    r_t = lax.broadcasted_iota(jnp.int32, (nt, nt), 0)
    c_t = lax.broadcasted_iota(jnp.int32, (nt, nt), 1)
    lower_strict_t = jnp.where(c_t < r_t, 1.0, 0.0).astype(BF16)
    before = _dot_exact_right(lower_strict_t, len_te)
    ssrc_ref[...] = _dot_exact_left(len_te, upper_strict).astype(jnp.int32)
    sdst_ref[...] = (pad_start_r + before).astype(jnp.int32)
    tab_ref[TAIL_START:TAIL_START + 1, :] = (pad_start_r + total_r).astype(jnp.int32)
    tab_ref[TAIL_LEN:TAIL_LEN + 1, :] = (padded_r - total_r).astype(jnp.int32)
    tab_ref[PAD_END:PAD_END + 1, :] = pad_end_r.astype(jnp.int32)
    tab_ref[BLOCK_START:BLOCK_START + 1, :] = (pad_start_r * (1.0 / bm)).astype(jnp.int32)
    tab_ref[BLOCK_COUNT:BLOCK_COUNT + 1, :] = (padded_r * (1.0 / bm)).astype(jnp.int32)
    tab_ref[5:8, :] = jnp.zeros((3, lanes), jnp.int32)


def _slots(len_te, bm):
    assert bm & (bm - 1) == 0, "block rows must be a power of two"
    nt, lanes = len_te.shape
    return pl.pallas_call(
        functools.partial(_slots_kernel, bm),
        out_shape=[jax.ShapeDtypeStruct((nt, lanes), jnp.int32),
                   jax.ShapeDtypeStruct((nt, lanes), jnp.int32),
                   jax.ShapeDtypeStruct((8, lanes), jnp.int32)],
        compiler_params=pltpu.CompilerParams(vmem_limit_bytes=V7X_VMEM_LIMIT),
        name="slots",
    )(len_te)


def _segment_pieces(max_rows):
    sizes = []
    s = 8
    while s <= max_rows:
        sizes.append(s)
        s *= 2
    return sizes[::-1]


def _segment_dma(src_ref, src0, dst_ref, dst0, nrows, sizes, sem, start):
    def pieces(group, off):
        for sz in group:
            bit = nrows & sz

            @pl.when(bit != 0)
            def _(off=off, sz=sz):
                s0 = pl.multiple_of(src0 + off, 8)
                d0 = pl.multiple_of(dst0 + off, 8)
                cp = pltpu.make_async_copy(src_ref.at[pl.ds(s0, sz)], dst_ref.at[pl.ds(d0, sz)], sem)
                if start:
                    cp.start()
                else:
                    cp.wait()

            off = off + bit

    large = [sz for sz in sizes if sz >= COMMON_SEGMENT_ROWS]
    small = [sz for sz in sizes if sz < COMMON_SEGMENT_ROWS]
    if large and small:
        @pl.when(nrows >= COMMON_SEGMENT_ROWS)
        def _():
            pieces(large, 0)

        pieces(small, nrows & ~(COMMON_SEGMENT_ROWS - 1))
    else:
        pieces(sizes, 0)


def _sorted_onehot(pos_ref, row0, nrows, tt):
    r_i = lax.broadcasted_iota(jnp.int32, (nrows, tt), 0) + row0
    hit = r_i == pos_ref[0:1, :]
    for kk in range(1, TOP_K):
        hit = hit | (r_i == pos_ref[kk:kk + 1, :])
    return hit


def _row_groups(nrows, n_groups):
    size = -(-nrows // n_groups // 8) * 8
    return [(r0, min(size, nrows - r0)) for r0 in range(0, nrows, size)]


def _dispatch_kernel(bm, len_ref, ssrc_ref, sdst_ref, tail_ref, pos_ref, h2_ref, xs_hbm, buf, sem):
    i = pl.program_id(0)
    nt = pl.num_programs(0)
    tt = h2_ref.shape[0]
    nrows = buf.shape[1]
    sizes = _segment_pieces(tt)
    slot = i % 2

    h2 = h2_ref[...]
    for r0, rn in _row_groups(nrows, 4):
        perm = jnp.where(_sorted_onehot(pos_ref, r0, rn, tt), 1.0, 0.0).astype(BF16)
        buf[slot, r0:r0 + rn, :] = _dot(perm, h2)

    def seg_start(tile, which):
        def body(e, carry):
            _segment_dma(buf.at[which], ssrc_ref[tile, e], xs_hbm, sdst_ref[tile, e], len_ref[tile, e], sizes,
                         sem.at[which], True)
            return carry
        lax.fori_loop(0, N_EXPERTS, body, 0)

    def seg_wait(tile, which):
        total = ssrc_ref[tile, N_EXPERTS - 1] + len_ref[tile, N_EXPERTS - 1]
        _segment_dma(buf.at[which], 0, xs_hbm, 0, total, _segment_pieces(nrows), sem.at[which], False)

    @pl.when(i > 0)
    def _():
        seg_wait(i - 1, 1 - slot)

    seg_start(i, slot)

    @pl.when(i == nt - 1)
    def _():
        seg_wait(i, slot)
        zrows = bm
        zbuf = buf.at[0]
        zsem = sem.at[0]
        buf[0, 0:zrows, :] = jnp.zeros((zrows, buf.shape[2]), F32)

        def tail(start):
            def body(e, carry):
                _segment_dma(zbuf, 0, xs_hbm, tail_ref[TAIL_START, e], tail_ref[TAIL_LEN, e],
                             _segment_pieces(zrows // 2), zsem, start)
                return carry
            lax.fori_loop(0, N_EXPERTS, body, 0)

        tail(True)
        tail(False)

        used = tail_ref[PAD_END, tail_ref.shape[1] - 1]
        n_unused = (xs_hbm.shape[0] - used) // zrows

        def unused_copy(j):
            d0 = pl.multiple_of(used + j * zrows, 8)
            return pltpu.make_async_copy(zbuf.at[pl.ds(0, zrows)], xs_hbm.at[pl.ds(d0, zrows)], zsem)

        def unused_start(j, carry):
            unused_copy(j).start()
            return carry

        def unused_wait(j, carry):
            unused_copy(j).wait()
            return carry

        lax.fori_loop(0, n_unused, unused_start, 0)
        lax.fori_loop(0, n_unused, unused_wait, 0)


def _dispatch(seg_len, seg_src, seg_dst, tail, pos, h2, n_slots, tt, bm):
    t, d = h2.shape
    nrows = TOP_K * tt + 8 * N_EXPERTS
    assert bm <= nrows
    smem = pl.BlockSpec(memory_space=pltpu.SMEM)
    return pl.pallas_call(
        functools.partial(_dispatch_kernel, bm),
        grid=(t // tt,),
        in_specs=[smem, smem, smem, smem,
                  pl.BlockSpec((TOP_K, tt), lambda i: (0, i)),
                  pl.BlockSpec((tt, d), lambda i: (i, 0))],
        out_specs=pl.BlockSpec(memory_space=pl.ANY),
        out_shape=jax.ShapeDtypeStruct((n_slots, d), F32),
        scratch_shapes=[pltpu.VMEM((2, nrows, d), F32), pltpu.SemaphoreType.DMA((2,))],
        compiler_params=pltpu.CompilerParams(dimension_semantics=("arbitrary",), has_side_effects=True,
                                             vmem_limit_bytes=V7X_VMEM_LIMIT),
        name="dispatch",
    )(seg_len, seg_src, seg_dst, tail, pos, h2)


def _experts_kernel(tab_ref, xs_hbm, wgu_hbm, bgu_ref, wd_hbm, bd_ref, y_hbm, wgu_s, wd_s, wgu_f, wd_f, xbuf, ybuf,
                    sem_in, sem_out, sem_w):
    e = pl.program_id(0)
    n_e = pl.num_programs(0)
    bm = xbuf.shape[1] // 2
    d_ff = wd_f.shape[1]
    ct = wgu_s.shape[2]
    wslot = e % 2

    def w_copies(expert, which):
        return (pltpu.make_async_copy(wgu_hbm.at[expert], wgu_f.at[which], sem_w.at[which]),
                pltpu.make_async_copy(wd_hbm.at[expert], wd_f.at[which], sem_w.at[which]))

    @pl.when(e == 0)
    def _():
        for cp in w_copies(0, 0):
            cp.start()

    for cp in w_copies(e, wslot):
        cp.wait()

    have_next = e + 1 < n_e

    def prefetch_next_weights(cond):
        @pl.when(have_next & cond)
        def _():
            for cp in w_copies(e + 1, 1 - wslot):
                cp.start()
    first = tab_ref[BLOCK_START, e]
    count = tab_ref[BLOCK_COUNT, e]
    npair = count // 2
    odd = count % 2
    tslot = npair % 2

    def pair_rows(j):
        return pl.ds(pl.multiple_of((first + 2 * j) * bm, bm), 2 * bm)

    def x_copy(j, slot):
        return pltpu.make_async_copy(xs_hbm.at[pair_rows(j)], xbuf.at[slot], sem_in.at[slot])

    def y_copy(j, slot):
        return pltpu.make_async_copy(ybuf.at[slot], y_hbm.at[pair_rows(j)], sem_out.at[slot])

    def last_rows():
        return pl.ds(pl.multiple_of((first + 2 * npair) * bm, bm), bm)

    def x_last(slot):
        return pltpu.make_async_copy(xs_hbm.at[last_rows()], xbuf.at[slot, pl.ds(0, bm)], sem_in.at[slot])

    def y_last(slot):
        return pltpu.make_async_copy(ybuf.at[slot, pl.ds(0, bm)], y_hbm.at[last_rows()], sem_out.at[slot])

    @pl.when(npair > 0)
    def _():
        x_copy(0, 0).start()

    @pl.when((npair == 0) & (odd == 1))
    def _():
        x_last(0).start()

    for t in range(wgu_s.shape[0]):
        wgu_s[t] = wgu_f[wslot, :, t * ct:(t + 1) * ct].astype(BF16)
    for t in range(wd_s.shape[0]):
        wd_s[t] = wd_f[wslot, :, t * ct:(t + 1) * ct].astype(BF16)

    def drain_outputs(n_blocks):
        pairs = n_blocks // 2
        last = n_blocks % 2
        last_slot = pairs % 2

        def wait_rows(slot, rows):
            pltpu.make_async_copy(ybuf.at[slot, pl.ds(0, rows)], y_hbm.at[pl.ds(0, rows)], sem_out.at[slot]).wait()

        @pl.when((last == 0) & (pairs >= 2))
        def _():
            wait_rows(last_slot, 2 * bm)

        @pl.when(pairs >= 1)
        def _():
            wait_rows(1 - last_slot, 2 * bm)

        @pl.when(last == 1)
        def _():
            wait_rows(last_slot, bm)

    @pl.when(e > 0)
    def _():
        drain_outputs(tab_ref[BLOCK_COUNT, jnp.maximum(e - 1, 0)])

    def mlp(x):
        xb = x.astype(BF16)
        tpc = 2
        fc = tpc * ct
        nf = d_ff // fc
        n_out = wd_s.shape[0]

        def gate_up(f):
            gates, ups = [], []
            for t in range(f * tpc, (f + 1) * tpc):
                gates.append(_dot(xb, wgu_s[t]) + bgu_ref[:, t * ct:(t + 1) * ct])
                u = d_ff // ct + t
                ups.append(_dot(xb, wgu_s[u]) + bgu_ref[:, u * ct:(u + 1) * ct])
            return jnp.concatenate(gates, axis=-1), jnp.concatenate(ups, axis=-1)

        acts = []
        pre = gate_up(0)
        for f in range(nf):
            gate = jnp.minimum(pre[0], SWIGLU_LIMIT)
            up = jnp.clip(pre[1], -SWIGLU_LIMIT, SWIGLU_LIMIT)
            if f + 1 < nf:
                pre = gate_up(f + 1)
            acts.append(((up + 1.0) * gate * _sigmoid(SWIGLU_ALPHA * gate)).astype(BF16))
        act = jnp.concatenate(acts, axis=-1)
        return [_dot(act, wd_s[n]) + bd_ref[:, n * ct:(n + 1) * ct] for n in range(n_out)]

    def pair(j, carry):
        slot = j % 2
        x_copy(j, slot).wait()

        @pl.when(j + 1 < npair)
        def _():
            x_copy(j + 1, 1 - slot).start()

        @pl.when((j + 1 == npair) & (odd == 1))
        def _():
            x_last(1 - slot).start()

        @pl.when(j >= 2)
        def _():
            y_copy(j - 2, slot).wait()

        prefetch_next_weights(j == 0)
        for n, piece in enumerate(mlp(xbuf[slot])):
            ybuf[slot, :, n * ct:(n + 1) * ct] = piece
        y_copy(j, slot).start()
        return carry

    lax.fori_loop(0, npair, pair, 0)

    prefetch_next_weights(npair == 0)

    @pl.when(odd == 1)
    def _():
        x_last(tslot).wait()

        @pl.when(npair >= 2)
        def _():
            y_copy(npair - 2, tslot).wait()

        for n, piece in enumerate(mlp(xbuf[tslot, 0:bm, :])):
            ybuf[tslot, 0:bm, n * ct:(n + 1) * ct] = piece
        y_last(tslot).start()

    @pl.when(e == n_e - 1)
    def _():
        drain_outputs(count)

    @pl.when(e == pl.num_programs(0) - 1)
    def _():
        used = first + count
        n_unused = y_hbm.shape[0] // bm - used
        ybuf[0, 0:bm, :] = jnp.zeros((bm, ybuf.shape[2]), F32)

        def z_copy(j):
            d0 = pl.multiple_of((used + j) * bm, bm)
            return pltpu.make_async_copy(ybuf.at[0, pl.ds(0, bm)], y_hbm.at[pl.ds(d0, bm)], sem_out.at[0])

        def z_start(j, carry):
            z_copy(j).start()
            return carry

        def z_wait(j, carry):
            z_copy(j).wait()
            return carry

        lax.fori_loop(0, n_unused, z_start, 0)
        lax.fori_loop(0, n_unused, z_wait, 0)


def _experts(table, xs, wgu, bgu, wd, bd, bm):
    ns, d = xs.shape
    n_e, _, two_ff = wgu.shape
    d_ff = two_ff // 2
    wmap = lambda e: (e, 0, 0)
    return pl.pallas_call(
        _experts_kernel,
        grid=(n_e,),
        in_specs=[pl.BlockSpec(memory_space=pltpu.SMEM),
                  pl.BlockSpec(memory_space=pl.ANY),
                  pl.BlockSpec(memory_space=pl.ANY),
                  pl.BlockSpec((None, 1, two_ff), wmap),
                  pl.BlockSpec(memory_space=pl.ANY),
                  pl.BlockSpec((None, 1, d), wmap)],
        out_specs=pl.BlockSpec(memory_space=pl.ANY),
        out_shape=jax.ShapeDtypeStruct((ns, d), F32),
        scratch_shapes=[pltpu.VMEM((two_ff // MXU_COLS, d, MXU_COLS), BF16),
                        pltpu.VMEM((d // MXU_COLS, d_ff, MXU_COLS), BF16),
                        pltpu.VMEM((2, d, two_ff), F32), pltpu.VMEM((2, d_ff, d), F32),
                        pltpu.VMEM((2, 2 * bm, d), F32), pltpu.VMEM((2, 2 * bm, d), F32),
                        pltpu.SemaphoreType.DMA((2,)), pltpu.SemaphoreType.DMA((2,)),
                        pltpu.SemaphoreType.DMA((2,))],
        compiler_params=pltpu.CompilerParams(dimension_semantics=("arbitrary",), has_side_effects=True,
                                             vmem_limit_bytes=V7X_VMEM_LIMIT),
        name="experts",
    )(table, xs, wgu, bgu, wd, bd)


def _combine_kernel(len_ref, ssrc_ref, sdst_ref, pos_ref, w_ref, y_hbm, x1_ref, g2_ref, nw_ref, sh_ref, sc_ref,
                    o_ref, ybuf, sem):
    i = pl.program_id(0)
    nt = pl.num_programs(0)
    tt = x1_ref.shape[0]
    nrows = ybuf.shape[1]
    sizes = _segment_pieces(tt)
    slot = i % 2

    def seg_start(tile, which):
        def body(e, carry):
            _segment_dma(y_hbm, sdst_ref[tile, e], ybuf.at[which], ssrc_ref[tile, e], len_ref[tile, e], sizes,
                         sem.at[which], True)
            return carry
        lax.fori_loop(0, N_EXPERTS, body, 0)

    def seg_wait(tile, which):
        total = ssrc_ref[tile, N_EXPERTS - 1] + len_ref[tile, N_EXPERTS - 1]
        _segment_dma(y_hbm, 0, ybuf.at[which], 0, total, _segment_pieces(nrows), sem.at[which], False)

    @pl.when(i == 0)
    def _():
        ybuf[...] = jnp.zeros(ybuf.shape, F32)
        seg_start(0, 0)

    @pl.when(i + 1 < nt)
    def _():
        seg_start(i + 1, 1 - slot)

    seg_wait(i, slot)

    acc = None
    for r0, rn in _row_groups(nrows, 3):
        r_i = lax.broadcasted_iota(jnp.int32, (rn, tt), 0) + r0
        wmat = 0.0
        for kk in range(TOP_K):
            wmat = jnp.where(r_i == pos_ref[kk:kk + 1, :], w_ref[kk:kk + 1, :], wmat)
        part = _dot_tn(wmat.astype(BF16), ybuf[slot, r0:r0 + rn, :].astype(BF16))
        acc = part if acc is None else acc + part
    xo = x1_ref[...] + g2_ref[...] * acc
    ms = jnp.mean(xo * xo, axis=-1, keepdims=True)
    hn = xo * lax.rsqrt(ms + EPS) * nw_ref[...]
    o_ref[...] = hn * (1.0 + sc_ref[...]) + sh_ref[...]


def _combine(seg_len, seg_src, seg_dst, pos, wrow, y, x1, mod3, nw, modf3, seq, tt):
    t, d = x1.shape
    tps = seq // tt
    nrows = TOP_K * tt + 8 * N_EXPERTS
    smem = pl.BlockSpec(memory_space=pltpu.SMEM)
    return pl.pallas_call(
        _combine_kernel,
        grid=(t // tt,),
        in_specs=[smem, smem, smem,
                  pl.BlockSpec((TOP_K, tt), lambda i: (0, i)),
                  pl.BlockSpec((TOP_K, tt), lambda i: (0, i)),
                  pl.BlockSpec(memory_space=pl.ANY),
                  pl.BlockSpec((tt, d), lambda i: (i, 0)),
                  _mod_spec(tps, d, 5),
                  pl.BlockSpec((1, d), lambda i: (0, 0)),
                  _mod_spec(tps, d, 0),
                  _mod_spec(tps, d, 1)],
        out_specs=pl.BlockSpec((tt, d), lambda i: (i, 0)),
        out_shape=jax.ShapeDtypeStruct((t, d), F32),
        scratch_shapes=[pltpu.VMEM((2, nrows, d), F32), pltpu.SemaphoreType.DMA((2,))],
        compiler_params=pltpu.CompilerParams(dimension_semantics=("arbitrary",),
                                             vmem_limit_bytes=V7X_VMEM_LIMIT),
        name="combine",
    )(seg_len, seg_src, seg_dst, pos, wrow, y, x1, mod3, nw, modf3, modf3)


def _pick_tile(n, pref):
    tile = pref
    while n % tile:
        tile //= 2
    return tile


def kernel(x, c, w_ada, b_ada, norm_mix, w_in, dn_conv, dn_a_log, dn_dt_bias, dn_norm, ml_conv, ml_i_bias,
           ml_f_bias, ml_norm, w_out, norm_ffn, w_router, b_router, w_gate_up, b_gate_up, w_down, b_down,
           w_ada_final, b_ada_final, norm_final):
    batch, seq, d = x.shape
    assert w_ada.shape[0] == 1, "single-layer block"
    assert seq % CHUNK == 0
    t = batch * seq
    x2 = x.reshape(t, d)

    c_pad = jnp.pad(c, ((0, 8 - batch % 8 if batch % 8 else 0), (0, 0)))
    mod3 = _mods(c_pad, w_ada, b_ada.reshape(1, 6 * d)).reshape(-1, 1, 6 * d)
    modf3 = _mods(c_pad, w_ada_final[None], b_ada_final.reshape(1, 2 * d)).reshape(-1, 1, 2 * d)

    w_new, wgt = _wprep(w_in)
    cw = jnp.concatenate([dn_conv.reshape(CONV_W, -1), ml_conv.reshape(CONV_W, -1)], axis=1)
    zeros4 = jnp.zeros((4,), F32)
    bias16 = jnp.concatenate([zeros4, dn_dt_bias.reshape(4), ml_i_bias.reshape(4), ml_f_bias.reshape(4)])
    alog16 = jnp.concatenate([zeros4, dn_a_log.reshape(4), zeros4, zeros4])
    gpc = jnp.zeros((8, HEAD_LANES), F32).at[0, 0:16].set(bias16).at[1, 0:16].set(alog16)
    gpr = jnp.zeros((16, HEAD_LANES), F32).at[:, 0].set(bias16).at[:, 1].set(alog16)

    tm_in = _pick_tile(seq, 512)
    conv_out, rest, gcol, grow = _inproj(x2, mod3, norm_mix.reshape(1, d), w_new, wgt, cw, gpc, gpr, seq, tm_in)
    grow3 = grow.reshape(16, t // CHUNK, CHUNK).transpose(1, 0, 2)

    rows = _pick_tile(seq, 256)
    ymix = _mixers(conv_out, rest, gcol, grow3, dn_norm.reshape(1, DN_DV), ml_norm.reshape(1, ML_HEADS * ML_DV),
                   batch, seq, rows)

    wo = w_out.reshape(-1, d).astype(BF16)
    tm_r = _pick_tile(seq, 512)
    brp = jnp.broadcast_to(b_router.reshape(N_EXPERTS, 1), (N_EXPERTS, HEAD_LANES))
    x1, h2, pos, wrow, len_col = _route(
        ymix, wo, x2, mod3, norm_ffn.reshape(1, d),
        w_router.reshape(d, N_EXPERTS).T, brp, seq, tm_r)

    n_e = N_EXPERTS
    nt = t // tm_r
    bm = 256
    len_te =len_col.reshape(nt, n_e, HEAD_LANES)[:, :, 0]
    seg_len = jnp.pad(len_te, ((0, 0), (0, HEAD_LANES - n_e)))
    n_slots_max = t * TOP_K + n_e * (7 * nt + bm)
    nb = (n_slots_max + bm - 1) // bm
    seg_src, seg_dst, table = _slots(seg_len, bm)

    xs = _dispatch(seg_len, seg_src, seg_dst, table, pos, h2, nb * bm, tm_r, bm)
    y = _experts(table, xs, w_gate_up.reshape(n_e, d, -1), b_gate_up.reshape(n_e, 1, -1),
                 w_down.reshape(n_e, -1, d), b_down.reshape(n_e, 1, d), bm)
    out = _combine(seg_len, seg_src, seg_dst, pos, wrow, y, x1, mod3, norm_final.reshape(1, d), modf3, seq, tm_r)
    return out.reshape(batch, seq, d)
```

```python
import functools

import jax
import jax.numpy as jnp
from jax import lax
from jax.experimental import pallas as pl
from jax.experimental.pallas import tpu as pltpu

F32 = jnp.float32
BF16 = jnp.bfloat16

CHUNK = 64
CONV_W = 4
EPS = 1e-6

DN_HEADS = 4
DN_DK = 128
DN_DV = 128
ML_HEADS = 4
ML_DK = 64
ML_DV = 128
HEAD_LANES = 128

N_EXPERTS = 32
TOP_K = 4
SWIGLU_LIMIT = 7.0
SWIGLU_ALPHA = 1.702

C_DNQ = 0
C_DNK = 512
C_DNV = 1024
C_MLQ = 1536
C_MLK = 1792
N_CONV = 2048
C_DNZ = 2048
C_MLV = 2560
C_MLO = 3072
C_GATE = 3584
N_PROJ = 3712
N_REST = C_GATE - N_CONV

V7X_VMEM_LIMIT = 56 * 1024 * 1024
MXU_COLS = 256
COMMON_SEGMENT_ROWS = 128
INPROJ_GROUP_COLS = 512

NEG_BIG = -1e30


def _sigmoid(x):
    return 1.0 / (1.0 + jnp.exp(-x))


def _softplus(x):
    return jnp.maximum(x, 0.0) + jnp.log(1.0 + jnp.exp(-jnp.abs(x)))


def _split3(v):
    hi = v.astype(BF16)
    r1 = v - hi.astype(F32)
    mid = r1.astype(BF16)
    lo = (r1 - mid.astype(F32)).astype(BF16)
    return hi, mid, lo


def _dot(a, b):
    return jnp.dot(a, b, preferred_element_type=F32)


def _dot_nt(a, b):
    return lax.dot_general(a, b, (((1,), (1,)), ((), ())), preferred_element_type=F32)


def _dot_tn(a, b):
    return lax.dot_general(a, b, (((0,), (0,)), ((), ())), preferred_element_type=F32)


def _dot_exact_right(sel_bf16, v):
    hi, mid, lo = _split3(v)
    return _dot(sel_bf16, hi) + _dot(sel_bf16, mid) + _dot(sel_bf16, lo)


def _dot_exact_left(v, sel_bf16):
    hi, mid, lo = _split3(v)
    return _dot(hi, sel_bf16) + _dot(mid, sel_bf16) + _dot(lo, sel_bf16)


def _mods_kernel(c_ref, w_ref, b_ref, o_ref):
    c = c_ref[...]
    cond = c * _sigmoid(c)
    ch, cm, _ = _split3(cond)
    wh, wm, _ = _split3(w_ref[...])
    acc = _dot(ch, wh) + (_dot(ch, wm) + _dot(cm, wh))
    o_ref[...] = acc + b_ref[...]


def _mods(c_pad, w, b):
    m, d = c_pad.shape
    n = w.shape[2]
    tn = 1024
    return pl.pallas_call(
        _mods_kernel,
        grid=(n // tn,),
        in_specs=[pl.BlockSpec((m, d), lambda j: (0, 0)),
                  pl.BlockSpec((None, d, tn), lambda j: (0, 0, j)),
                  pl.BlockSpec((1, tn), lambda j: (0, j))],
        out_specs=pl.BlockSpec((m, tn), lambda j: (0, j)),
        out_shape=jax.ShapeDtypeStruct((m, n), F32),
        compiler_params=pltpu.CompilerParams(dimension_semantics=("arbitrary",),
                                             vmem_limit_bytes=V7X_VMEM_LIMIT),
        name="mods",
    )(c_pad, w, b)


SRC_DNZ = 1536
SRC_DNGATE = 2048
SRC_MLQK = 2056
SRC_MLV = SRC_MLQK + 2 * ML_HEADS * ML_DK
SRC_MLO = SRC_MLV + ML_HEADS * ML_DV
SRC_MLGATE = SRC_MLO + ML_HEADS * ML_DV
N_GATES = 16


def _wprep_kernel(wt_ref, o_ref, gt_ref):
    def put(dst0, src0, n):
        o_ref[:, dst0:dst0 + n] = wt_ref[src0:src0 + n, :].T.astype(BF16)

    put(C_DNQ, 0, SRC_DNZ)
    put(C_MLQ, SRC_MLQK, SRC_MLV - SRC_MLQK)
    put(C_DNZ, SRC_DNZ, SRC_DNGATE - SRC_DNZ)
    put(C_MLV, SRC_MLV, SRC_MLO - SRC_MLV)
    put(C_MLO, SRC_MLO, SRC_MLGATE - SRC_MLO)
    gates = jnp.concatenate([wt_ref[SRC_DNGATE:SRC_MLQK, :], wt_ref[SRC_MLGATE:SRC_MLGATE + N_GATES // 2, :],
                             jnp.zeros((HEAD_LANES - N_GATES, wt_ref.shape[1]), F32)], axis=0)
    o_ref[:, C_GATE:N_PROJ] = gates.T.astype(BF16)
    gt_ref[...] = gates[0:N_GATES, :].astype(BF16)


def _wprep(w_in):
    _, d, n_in = w_in.shape
    assert n_in == SRC_MLGATE + N_GATES // 2
    rows = _pick_tile(d, 256)
    w_t = jnp.swapaxes(w_in, 1, 2)
    return pl.pallas_call(
        _wprep_kernel,
        grid=(d // rows,),
        in_specs=[pl.BlockSpec((None, n_in, rows), lambda i: (0, 0, i))],
        out_specs=[pl.BlockSpec((rows, N_PROJ), lambda i: (i, 0)),
                   pl.BlockSpec((N_GATES, rows), lambda i: (0, i))],
        out_shape=[jax.ShapeDtypeStruct((d, N_PROJ), BF16), jax.ShapeDtypeStruct((N_GATES, d), BF16)],
        compiler_params=pltpu.CompilerParams(dimension_semantics=("arbitrary",),
                                             vmem_limit_bytes=V7X_VMEM_LIMIT),
        name="wprep",
    )(w_t)


def _gate_transform(v, bias, alog, cls):
    vb = v + bias
    beta = _sigmoid(v)
    g = -jnp.exp(alog) * _softplus(vb)
    logf = -_softplus(-vb)
    return jnp.where(cls == 0, beta, jnp.where(cls == 1, g, jnp.where(cls == 2, vb, jnp.where(cls == 3, logf, 0.0))))


def _inproj_kernel(tiles_per_seq, x_ref, sh_ref, sc_ref, nw_ref, w_ref, wgt_ref, cw_ref, gpc_ref, gpr_ref,
                   conv_ref, rest_ref, gcol_ref, grow_ref, cbuf):
    tm = x_ref.shape[0]
    i = pl.program_id(0)
    x = x_ref[...]
    ms = jnp.mean(x * x, axis=-1, keepdims=True)
    h = x * lax.rsqrt(ms + EPS) * nw_ref[...]
    h = h * (1.0 + sc_ref[...]) + sh_ref[...]
    hb = h.astype(BF16)

    @pl.when(i % tiles_per_seq == 0)
    def _():
        cbuf[0:8, :] = jnp.zeros((8, N_CONV), F32)

    group = INPROJ_GROUP_COLS
    for lo in range(0, N_CONV, group):
        cols = slice(lo, lo + group)
        pc = _dot(hb, w_ref[:, cols])
        cbuf[8:tm + 8, cols] = pc
        acc = cw_ref[CONV_W - 1:CONV_W, cols] * pc
        for j in range(CONV_W - 1):
            acc = acc + cw_ref[j:j + 1, cols] * cbuf[8 - (CONV_W - 1) + j:8 - (CONV_W - 1) + j + tm, cols]
        cbuf[0:8, cols] = cbuf[tm:tm + 8, cols]
        y = acc * _sigmoid(acc)
        if lo < C_DNV:
            for h0 in range(0, group, HEAD_LANES):
                uh = y[:, h0:h0 + HEAD_LANES]
                un = uh * lax.rsqrt(jnp.sum(uh * uh, axis=-1, keepdims=True) + EPS)
                conv_ref[:, lo + h0:lo + h0 + HEAD_LANES] = un * (DN_DK ** -0.5) if lo < C_DNK else un
        else:
            for h0 in range(0, group, HEAD_LANES):
                piece = y[:, h0:h0 + HEAD_LANES]
                is_mlq = C_MLQ <= lo + h0 < C_MLK
                conv_ref[:, lo + h0:lo + h0 + HEAD_LANES] = piece * (ML_DK ** -0.5) if is_mlq else piece

    z = _dot(hb, w_ref[:, C_DNZ:C_MLV])
    rest_ref[:, 0:512] = z * _sigmoid(z)
    rest_ref[:, 512:1024] = _dot(hb, w_ref[:, C_MLV:C_MLO])
    rest_ref[:, 1024:1536] = _sigmoid(_dot(hb, w_ref[:, C_MLO:C_GATE]))

    r_i = lax.broadcasted_iota(jnp.int32, (tm, tm), 0)
    c_i = lax.broadcasted_iota(jnp.int32, (tm, tm), 1)
    same_chunk = (r_i // CHUNK) == (c_i // CHUNK)
    tril = jnp.where(same_chunk & (c_i <= r_i), 1.0, 0.0).astype(BF16)
    triu = jnp.where(same_chunk & (r_i <= c_i), 1.0, 0.0).astype(BF16)

    gc = _dot(hb, w_ref[:, C_GATE:N_PROJ])
    cls_c = lax.broadcasted_iota(jnp.int32, (tm, HEAD_LANES), 1) // 4
    gt = _gate_transform(gc, gpc_ref[0:1, :], gpc_ref[1:2, :], cls_c)
    cs = _dot_exact_right(tril, gt)
    gcol_ref[...] = jnp.where((cls_c == 1) | (cls_c == 3), cs, gt)

    gr = _dot_nt(wgt_ref[...], hb)
    cls_r = lax.broadcasted_iota(jnp.int32, (16, tm), 0) // 4
    gtr = _gate_transform(gr, gpr_ref[:, 0:1], gpr_ref[:, 1:2], cls_r)
    csr = _dot_exact_left(gtr, triu)
    grow_ref[...] = jnp.where((cls_r == 1) | (cls_r == 3), csr, gtr)


def _mod_spec(tiles_per_seq, d, j):
    return pl.BlockSpec((None, 1, d), lambda i: (i // tiles_per_seq, 0, j))


def _inproj(x2, mod3, nw, w_new, wgt, cw, gpc, gpr, seq, tm):
    t, d = x2.shape
    tps = seq // tm
    kern = functools.partial(_inproj_kernel, tps)
    return pl.pallas_call(
        kern,
        grid=(t // tm,),
        in_specs=[pl.BlockSpec((tm, d), lambda i: (i, 0)),
                  _mod_spec(tps, d, 0),
                  _mod_spec(tps, d, 1),
                  pl.BlockSpec((1, d), lambda i: (0, 0)),
                  pl.BlockSpec((d, N_PROJ), lambda i: (0, 0)),
                  pl.BlockSpec((16, d), lambda i: (0, 0)),
                  pl.BlockSpec((CONV_W, N_CONV), lambda i: (0, 0)),
                  pl.BlockSpec((8, HEAD_LANES), lambda i: (0, 0)),
                  pl.BlockSpec((16, HEAD_LANES), lambda i: (0, 0))],
        out_specs=[pl.BlockSpec((tm, N_CONV), lambda i: (i, 0)),
                   pl.BlockSpec((tm, N_REST), lambda i: (i, 0)),
                   pl.BlockSpec((tm, HEAD_LANES), lambda i: (i, 0)),
                   pl.BlockSpec((16, tm), lambda i: (0, i))],
        out_shape=[jax.ShapeDtypeStruct((t, N_CONV), F32),
                   jax.ShapeDtypeStruct((t, N_REST), F32),
                   jax.ShapeDtypeStruct((t, HEAD_LANES), F32),
                   jax.ShapeDtypeStruct((16, t), F32)],
        scratch_shapes=[pltpu.VMEM((tm + 8, N_CONV), F32)],
        compiler_params=pltpu.CompilerParams(dimension_semantics=("arbitrary",),
                                             vmem_limit_bytes=V7X_VMEM_LIMIT),
        name="inproj",
    )(x2, mod3, mod3, nw, w_new, wgt, cw, gpc, gpr)


def _chunk_masks():
    r = lax.broadcasted_iota(jnp.int32, (CHUNK, CHUNK), 0)
    c = lax.broadcasted_iota(jnp.int32, (CHUNK, CHUNK), 1)
    return r >= c, r > c, r == c


def _bdot(a, b):
    return lax.dot_general(a, b, (((2,), (1,)), ((0,), (0,))), preferred_element_type=F32)


def _bdot_nt(a, b):
    return lax.dot_general(a, b, (((2,), (2,)), ((0,), (0,))), preferred_element_type=F32)


def _unit_lower_inverse(lower, row, col):
    x = jnp.where(row == col, 1.0, 0.0) - jnp.where((row >> 1) == (col >> 1), lower, 0.0)
    shift = 1
    while (1 << shift) < CHUNK:
        couple = ((row >> (shift + 1)) == (col >> (shift + 1))) & ((row >> shift) != (col >> shift))
        cb = jnp.where(couple, lower, 0.0).astype(BF16)
        xb = x.astype(BF16)
        x = x - _bdot(_bdot(xb, cb).astype(BF16), xb)
        shift += 1
        yield
    return x


def _seq_heads(n_seq, n_heads):
    return [(s, hh) for s in range(n_seq) for hh in range(n_heads)]


def _deltanet_steps(q_ref, k_ref, v_ref, gc_ref, gr_ref, z_ref, nw_ref, o_ref, s_ref):
    n_seq, nc = gr_ref.shape[0], gr_ref.shape[1]

    @pl.when(pl.program_id(1) == 0)
    def _():
        s_ref[...] = jnp.zeros(s_ref.shape, F32)

    row = lax.broadcasted_iota(jnp.int32, (CHUNK, CHUNK), 0)
    col = lax.broadcasted_iota(jnp.int32, (CHUNK, CHUNK), 1)
    incl = row >= col
    strict = row > col
    nw = nw_ref[...]
    gcc = gc_ref[...]
    grr = gr_ref[...]

    streams = _seq_heads(n_seq, DN_HEADS)
    nh = len(streams)
    nb = nh * nc

    def heads(ref):
        return jnp.stack([ref[s, :, hh * HEAD_LANES:(hh + 1) * HEAD_LANES] for s, hh in streams],
                         axis=0).reshape(nb, CHUNK, HEAD_LANES)

    def col_gate(lane0):
        return jnp.stack([gcc[s, :, lane0 + hh:lane0 + hh + 1] for s, hh in streams], axis=0).reshape(nb, CHUNK, 1)

    q = heads(q_ref)
    k = heads(k_ref)
    v = heads(v_ref)
    beta = col_gate(0)
    g_c = col_gate(4)
    g_r = jnp.stack([grr[s, :, 4 + hh:5 + hh, :] for s, hh in streams], axis=0).reshape(nb, 1, CHUNK)
    g_last = g_c[:, CHUNK - 1:CHUNK, :]
    decay = jnp.exp(jnp.where(incl, g_c - g_r, NEG_BIG))
    kb = k.astype(BF16)
    kk = _bdot_nt(kb, kb)
    lower = jnp.where(strict, beta * kk * decay, 0.0)
    yield
    tinv = yield from _unit_lower_inverse(lower, row, col)
    eg = jnp.exp(g_c)
    rhs = jnp.concatenate([v * beta, k * (beta * eg)], axis=-1)
    sol = _bdot(tinv.astype(BF16), rhs.astype(BF16))
    yield
    w_val = sol[:, :, 0:DN_DV].reshape(nh, nc, CHUNK, DN_DV)
    kq = jnp.concatenate([sol[:, :, DN_DV:DN_DV + DN_DK], q * eg], axis=1).astype(BF16)
    kq = kq.reshape(nh, nc, 2 * CHUNK, DN_DK)
    qk = (_bdot_nt(q.astype(BF16), kb) * decay).astype(BF16).reshape(nh, nc, CHUNK, CHUNK)
    k_dec_t = jnp.swapaxes(k * jnp.exp(g_last - g_c), 1, 2).astype(BF16).reshape(nh, nc, DN_DK, CHUNK)
    s_dec = jnp.exp(g_last).reshape(nh, nc, 1, 1)
    yield

    state = s_ref[...]
    outs = []
    for c in range(nc):
        both = _bdot(kq[:, c], state.astype(BF16))
        v_new = w_val[:, c] - both[:, 0:CHUNK]
        vb = v_new.astype(BF16)
        outs.append(both[:, CHUNK:2 * CHUNK] + _bdot(qk[:, c], vb))
        state = s_dec[:, c] * state + _bdot(k_dec_t[:, c], vb)
        yield
    s_ref[...] = state

    o = jnp.stack(outs, axis=1)
    on = o * lax.rsqrt(jnp.mean(o * o, axis=-1, keepdims=True) + EPS) * nw
    on = on.reshape(nh, nc * CHUNK, DN_DV)
    for idx, (s, hh) in enumerate(streams):
        lanes = slice(hh * HEAD_LANES, (hh + 1) * HEAD_LANES)
        o_ref[s, :, lanes] = on[idx] * z_ref[s, :, lanes]


def _mlstm_steps(q_ref, k_ref, v_ref, gc_ref, gr_ref, og_ref, nw_ref, o_ref, c_ref, n_ref, m_ref):
    n_seq, nc = gr_ref.shape[0], gr_ref.shape[1]

    @pl.when(pl.program_id(1) == 0)
    def _():
        c_ref[...] = jnp.zeros(c_ref.shape, F32)
        n_ref[...] = jnp.zeros(n_ref.shape, F32)
        m_ref[...] = jnp.zeros(m_ref.shape, F32)

    incl, _, _ = _chunk_masks()
    gcc = gc_ref[...]
    grr = gr_ref[...]

    streams = _seq_heads(n_seq, ML_HEADS)
    nh = len(streams)

    def heads(ref):
        return jnp.stack([ref[s, :, hh * HEAD_LANES:(hh + 1) * HEAD_LANES] for s, hh in streams],
                         axis=0).reshape(nh, nc, CHUNK, HEAD_LANES)

    def col_gate(lane0):
        return jnp.stack([gcc[s, :, lane0 + hh:lane0 + hh + 1] for s, hh in streams],
                         axis=0).reshape(nh, nc, CHUNK, 1)

    def row_gate(row0):
        return jnp.stack([grr[s, :, row0 + hh:row0 + hh + 1, :] for s, hh in streams], axis=0)

    def qk_heads(ref):
        lane = lax.broadcasted_iota(jnp.int32, (ref.shape[1], HEAD_LANES), 1)
        tiles = []
        for s, hh in streams:
            tile = ref[s, :, (hh // 2) * HEAD_LANES:(hh // 2 + 1) * HEAD_LANES]
            mine = (lane >= ML_DK) if hh % 2 else (lane < ML_DK)
            tiles.append(jnp.where(mine, tile, 0.0))
        return jnp.stack(tiles, axis=0).reshape(nh, nc, CHUNK, HEAD_LANES)

    q = qk_heads(q_ref)
    k = qk_heads(k_ref)
    v = heads(v_ref)
    i_c = col_gate(8)
    b_c = col_gate(12)
    i_r = row_gate(8)
    b_r = row_gate(12)
    b_last = b_c[:, :, CHUNK - 1:CHUNK, :]
    d_mat = jnp.where(incl, b_c - b_r + i_r, NEG_BIG)
    m_intra = jnp.max(d_mat, axis=-1, keepdims=True)
    g_end = b_last - b_c + i_c
    g_end_max = jnp.max(g_end, axis=2, keepdims=True)
    yield

    m_run = m_ref[:, 0:1, 0:1].reshape(nh, 1, 1, 1)
    m_before = []
    for c in range(nc):
        m_before.append(m_run)
        m_run = jnp.maximum(b_last[:, c:c + 1] + m_run, g_end_max[:, c:c + 1])
    m_s = jnp.concatenate(m_before, axis=1)
    m_new = jnp.maximum(b_last + m_s, g_end_max)
    keep = jnp.exp(b_last + m_s - m_new)
    yield

    nb = nh * nc
    qb = q.astype(BF16)
    kb = k.astype(BF16)
    vb = v.astype(BF16)
    m_t = jnp.maximum(b_c + m_s, m_intra)
    inter = jnp.exp(b_c + m_s - m_t)
    qk = _bdot_nt(qb.reshape(nb, CHUNK, HEAD_LANES), kb.reshape(nb, CHUNK, HEAD_LANES))
    p = jnp.exp(d_mat - m_t) * qk.reshape(nh, nc, CHUNK, CHUNK)
    yield
    intra = _bdot(p.astype(BF16).reshape(nb, CHUNK, CHUNK), vb.reshape(nb, CHUNK, ML_DV)).reshape(nh, nc, CHUNK, ML_DV)
    p_sum = jnp.sum(p, axis=-1, keepdims=True)
    yield
    kw = k * jnp.exp(g_end - m_new)
    kw_t = jnp.swapaxes(kw.reshape(nb, CHUNK, HEAD_LANES), 1, 2).astype(BF16)
    d_state = _bdot(kw_t, vb.reshape(nb, CHUNK, ML_DV)).reshape(nh, nc, HEAD_LANES, ML_DV)
    kw_sum = jnp.sum(kw, axis=2, keepdims=True)
    yield

    c_s = c_ref[...]
    n_s = n_ref[:, 0:1, :]
    q_c = []
    q_n = []
    for c in range(nc):
        q_c.append(_bdot(qb[:, c], c_s.astype(BF16)))
        q_n.append(jnp.sum(q[:, c] * n_s, axis=-1, keepdims=True))
        c_s = keep[:, c] * c_s + d_state[:, c]
        n_s = keep[:, c] * n_s + kw_sum[:, c]
        yield
    c_ref[...] = c_s
    n_ref[...] = jnp.broadcast_to(n_s, (nh, 8, HEAD_LANES))
    m_ref[...] = jnp.broadcast_to(m_run.reshape(nh, 1, 1), (nh, 8, HEAD_LANES))

    num = inter * jnp.stack(q_c, axis=1) + intra
    den = inter * jnp.stack(q_n, axis=1) + p_sum
    h = num / jnp.maximum(jnp.abs(den), jnp.exp(-m_t))
    hr = h * lax.rsqrt(jnp.mean(h * h, axis=-1, keepdims=True) + EPS)
    hr = hr.reshape(nh, nc * CHUNK, ML_DV)
    for idx, (s, hh) in enumerate(streams):
        lanes = slice(hh * HEAD_LANES, (hh + 1) * HEAD_LANES)
        o_ref[s, :, lanes] = hr[idx] * nw_ref[:, lanes] * og_ref[s, :, lanes]


def _mixers_kernel(dq_ref, dk_ref, dv_ref, gc_ref, gr_ref, z_ref, dnw_ref, mq_ref, mk_ref, mv_ref, og_ref, mnw_ref,
                   o_ref, s_ref, c_ref, n_ref, m_ref):
    n_a = DN_HEADS * HEAD_LANES
    n_b = ML_HEADS * HEAD_LANES
    stages = [_deltanet_steps(dq_ref, dk_ref, dv_ref, gc_ref, gr_ref, z_ref, dnw_ref, o_ref.at[:, :, 0:n_a], s_ref),
              _mlstm_steps(mq_ref, mk_ref, mv_ref, gc_ref, gr_ref, og_ref, mnw_ref, o_ref.at[:, :, n_a:n_a + n_b],
                           c_ref, n_ref, m_ref)]
    while stages:
        for stage in list(stages):
            try:
                next(stage)
            except StopIteration:
                stages.remove(stage)


def _mixers(conv_out, rest, gcol, grow3, dn_nw, ml_nw, batch, seq, rows):
    t = conv_out.shape[0]
    nj = seq // rows
    cpb = rows // CHUNK
    width = 4 * HEAD_LANES
    assert DN_HEADS * HEAD_LANES == width and ML_HEADS * HEAD_LANES == width
    n_seq = 2 if batch % 2 == 0 else 1
    conv3 = conv_out.reshape(batch, seq, -1)
    rest3 = rest.reshape(batch, seq, -1)
    gcol3 = gcol.reshape(batch, seq, -1)
    grow4 = grow3.reshape(batch, seq // CHUNK, 16, CHUNK)

    def conv_block(c0):
        return pl.BlockSpec((n_seq, rows, width), lambda b, j: (b, j, c0 // width))

    def rest_block(c0):
        return pl.BlockSpec((n_seq, rows, width), lambda b, j: (b, j, (c0 - N_CONV) // width))

    ml_qk_width = ML_HEADS * ML_DK

    def ml_qk_block(c0):
        return pl.BlockSpec((n_seq, rows, ml_qk_width), lambda b, j: (b, j, c0 // ml_qk_width))

    gates = [pl.BlockSpec((n_seq, rows, HEAD_LANES), lambda b, j: (b, j, 0)),
             pl.BlockSpec((n_seq, cpb, 16, CHUNK), lambda b, j: (b, j, 0, 0))]
    out = pl.pallas_call(
        _mixers_kernel,
        grid=(batch // n_seq, nj),
        in_specs=[conv_block(C_DNQ), conv_block(C_DNK), conv_block(C_DNV)] + gates
                 + [rest_block(C_DNZ), pl.BlockSpec((1, HEAD_LANES), lambda b, j: (0, 0)),
                    ml_qk_block(C_MLQ), ml_qk_block(C_MLK), rest_block(C_MLV), rest_block(C_MLO),
                    pl.BlockSpec((1, width), lambda b, j: (0, 0))],
        out_specs=pl.BlockSpec((n_seq, rows, 2 * width), lambda b, j: (b, j, 0)),
        out_shape=jax.ShapeDtypeStruct((batch, seq, 2 * width), F32),
        scratch_shapes=[pltpu.VMEM((n_seq * DN_HEADS, DN_DK, DN_DV), F32),
                        pltpu.VMEM((n_seq * ML_HEADS, HEAD_LANES, ML_DV), F32),
                        pltpu.VMEM((n_seq * ML_HEADS, 8, HEAD_LANES), F32),
                        pltpu.VMEM((n_seq * ML_HEADS, 8, HEAD_LANES), F32)],
        compiler_params=pltpu.CompilerParams(dimension_semantics=("arbitrary", "arbitrary"),
                                             vmem_limit_bytes=V7X_VMEM_LIMIT),
        name="mixers",
    )(conv3, conv3, conv3, gcol3, grow4, rest3, dn_nw, conv3, conv3, rest3, rest3, ml_nw)
    return out.reshape(t, 2 * width)


def _route_kernel(y_ref, wo_ref, x_ref, g1_ref, sh_ref, sc_ref, nw_ref, wrt_ref, br_ref,
                  x1_ref, h2_ref, pos_ref, wrow_ref, len_ref):
    tm = x_ref.shape[0]

    mix = _dot(y_ref[...].astype(BF16), wo_ref[...])
    x1 = x_ref[...] + g1_ref[...] * mix
    x1_ref[...] = x1
    ms = jnp.mean(x1 * x1, axis=-1, keepdims=True)
    h2 = x1 * lax.rsqrt(ms + EPS) * nw_ref[...]
    h2 = h2 * (1.0 + sc_ref[...]) + sh_ref[...]
    h2_ref[...] = h2.astype(BF16)

    hh, hm, _ = _split3(h2)
    wh, wm, _ = _split3(wrt_ref[...])
    logits = _dot_nt(wh, hh) + (_dot_nt(wh, hm) + _dot_nt(wm, hh)) + br_ref[:, 0:1]

    e_i = lax.broadcasted_iota(jnp.int32, (N_EXPERTS, tm), 0)
    work = logits
    tops = []
    sels = []
    hots = []
    for _ in range(TOP_K):
        m = jnp.max(work, axis=0, keepdims=True)
        sel = jnp.min(jnp.where(work == m, e_i, N_EXPERTS), axis=0, keepdims=True)
        hot = e_i == sel
        work = jnp.where(hot, NEG_BIG, work)
        tops.append(m)
        sels.append(sel)
        hots.append(hot)
    exps = [jnp.exp(tl - tops[0]) for tl in tops]
    denom = exps[0] + exps[1] + exps[2] + exps[3]
    ws = [e / denom for e in exps]

    chosen = jnp.zeros((N_EXPERTS, tm), F32)
    for hot in hots:
        chosen = chosen + jnp.where(hot, 1.0, 0.0)

    r_i = lax.broadcasted_iota(jnp.int32, (tm, tm), 0)
    c_i = lax.broadcasted_iota(jnp.int32, (tm, tm), 1)
    strict_upper = jnp.where(r_i < c_i, 1.0, 0.0).astype(BF16)
    prefix = _dot(chosen.astype(BF16), strict_upper)
    n_e = jnp.sum(chosen, axis=1, keepdims=True)
    len8 = jnp.ceil(n_e * 0.125) * 8.0
    er = lax.broadcasted_iota(jnp.int32, (N_EXPERTS, N_EXPERTS), 0)
    ec = lax.broadcasted_iota(jnp.int32, (N_EXPERTS, N_EXPERTS), 1)
    strict_lower = jnp.where(ec < er, 1.0, 0.0).astype(BF16)
    len8b = jnp.broadcast_to(len8, (N_EXPERTS, HEAD_LANES))
    off8 = _dot_exact_right(strict_lower, len8b)[:, 0:1]
    len_ref[...] = len8b.astype(jnp.int32)

    for kk in range(TOP_K):
        wrow_ref[kk:kk + 1, :] = ws[kk]
        pos = jnp.sum(jnp.where(hots[kk], prefix + off8, 0.0), axis=0, keepdims=True)
        pos_ref[kk:kk + 1, :] = pos.astype(jnp.int32)


def _route(ymix, wo, x2, mod3, nw, wrt, br, seq, tm):
    t, d = x2.shape
    tps = seq // tm
    return pl.pallas_call(
        _route_kernel,
        grid=(t // tm,),
        in_specs=[pl.BlockSpec((tm, ymix.shape[1]), lambda i: (i, 0)),
                  pl.BlockSpec(wo.shape, lambda i: (0, 0)),
                  pl.BlockSpec((tm, d), lambda i: (i, 0)),
                  _mod_spec(tps, d, 2),
                  _mod_spec(tps, d, 3),
                  _mod_spec(tps, d, 4),
                  pl.BlockSpec((1, d), lambda i: (0, 0)),
                  pl.BlockSpec((N_EXPERTS, d), lambda i: (0, 0)),
                  pl.BlockSpec((N_EXPERTS, HEAD_LANES), lambda i: (0, 0))],
        out_specs=[pl.BlockSpec((tm, d), lambda i: (i, 0)),
                   pl.BlockSpec((tm, d), lambda i: (i, 0)),
                   pl.BlockSpec((TOP_K, tm), lambda i: (0, i)),
                   pl.BlockSpec((TOP_K, tm), lambda i: (0, i)),
                   pl.BlockSpec((N_EXPERTS, HEAD_LANES), lambda i: (i, 0))],
        out_shape=[jax.ShapeDtypeStruct((t, d), F32),
                   jax.ShapeDtypeStruct((t, d), BF16),
                   jax.ShapeDtypeStruct((TOP_K, t), jnp.int32),
                   jax.ShapeDtypeStruct((TOP_K, t), F32),
                   jax.ShapeDtypeStruct((t // tm * N_EXPERTS, HEAD_LANES), jnp.int32)],
        compiler_params=pltpu.CompilerParams(dimension_semantics=("arbitrary",),
                                             vmem_limit_bytes=V7X_VMEM_LIMIT),
        name="route",
    )(ymix, wo, x2, mod3, mod3, mod3, nw, wrt, br)


TAIL_START, TAIL_LEN, PAD_END, BLOCK_START, BLOCK_COUNT = range(5)


def _slots_kernel(bm, len_te_ref, ssrc_ref, sdst_ref, tab_ref):
    nt, lanes = len_te_ref.shape
    len_te = len_te_ref[...].astype(F32)

    r_l = lax.broadcasted_iota(jnp.int32, (lanes, lanes), 0)
    c_l = lax.broadcasted_iota(jnp.int32, (lanes, lanes), 1)
    upper_incl = jnp.where(r_l <= c_l, 1.0, 0.0).astype(BF16)
    upper_strict = jnp.where(r_l < c_l, 1.0, 0.0).astype(BF16)
    total_r = jnp.sum(len_te, axis=0, keepdims=True)
    padded_r = jnp.ceil(total_r * (1.0 / bm)) * bm
    pad_end_r = _dot_exact_left(jnp.broadcast_to(padded_r, (8, lanes)), upper_incl)[0:1, :]
    pad_start_r = pad_end_r - padded_r
    r_t = lax.broadcasted_iota(jnp.int32, (nt, nt), 0)
    c_t = lax.broadcasted_iota(jnp.int32, (nt, nt), 1)
    lower_strict_t = jnp.where(c_t < r_t, 1.0, 0.0).astype(BF16)
    before = _dot_exact_right(lower_strict_t, len_te)
    ssrc_ref[...] = _dot_exact_left(len_te, upper_strict).astype(jnp.int32)
    sdst_ref[...] = (pad_start_r + before).astype(jnp.int32)
    tab_ref[TAIL_START:TAIL_START + 1, :] = (pad_start_r + total_r).astype(jnp.int32)
    tab_ref[TAIL_LEN:TAIL_LEN + 1, :] = (padded_r - total_r).astype(jnp.int32)
    tab_ref[PAD_END:PAD_END + 1, :] = pad_end_r.astype(jnp.int32)
    tab_ref[BLOCK_START:BLOCK_START + 1, :] = (pad_start_r * (1.0 / bm)).astype(jnp.int32)
    tab_ref[BLOCK_COUNT:BLOCK_COUNT + 1, :] = (padded_r * (1.0 / bm)).astype(jnp.int32)
    tab_ref[5:8, :] = jnp.zeros((3, lanes), jnp.int32)


def _slots(len_te, bm):
    assert bm & (bm - 1) == 0, "block rows must be a power of two"
    nt, lanes = len_te.shape
    return pl.pallas_call(
        functools.partial(_slots_kernel, bm),
        out_shape=[jax.ShapeDtypeStruct((nt, lanes), jnp.int32),
                   jax.ShapeDtypeStruct((nt, lanes), jnp.int32),
                   jax.ShapeDtypeStruct((8, lanes), jnp.int32)],
        compiler_params=pltpu.CompilerParams(vmem_limit_bytes=V7X_VMEM_LIMIT),
        name="slots",
    )(len_te)


def _segment_pieces(max_rows):
    sizes = []
    s = 8
    while s <= max_rows:
        sizes.append(s)
        s *= 2
    return sizes[::-1]


def _segment_dma(src_ref, src0, dst_ref, dst0, nrows, sizes, sem, start):
    def pieces(group, off):
        for sz in group:
            bit = nrows & sz

            @pl.when(bit != 0)
            def _(off=off, sz=sz):
                s0 = pl.multiple_of(src0 + off, 8)
                d0 = pl.multiple_of(dst0 + off, 8)
                cp = pltpu.make_async_copy(src_ref.at[pl.ds(s0, sz)], dst_ref.at[pl.ds(d0, sz)], sem)
                if start:
                    cp.start()
                else:
                    cp.wait()

            off = off + bit

    large = [sz for sz in sizes if sz >= COMMON_SEGMENT_ROWS]
    small = [sz for sz in sizes if sz < COMMON_SEGMENT_ROWS]
    if large and small:
        @pl.when(nrows >= COMMON_SEGMENT_ROWS)
        def _():
            pieces(large, 0)

        pieces(small, nrows & ~(COMMON_SEGMENT_ROWS - 1))
    else:
        pieces(sizes, 0)


def _sorted_onehot(pos_ref, row0, nrows, tt):
    r_i = lax.broadcasted_iota(jnp.int32, (nrows, tt), 0) + row0
    hit = r_i == pos_ref[0:1, :]
    for kk in range(1, TOP_K):
        hit = hit | (r_i == pos_ref[kk:kk + 1, :])
    return hit


def _row_groups(nrows, n_groups):
    size = -(-nrows // n_groups // 8) * 8
    return [(r0, min(size, nrows - r0)) for r0 in range(0, nrows, size)]


def _dispatch_kernel(bm, len_ref, ssrc_ref, sdst_ref, tail_ref, pos_ref, h2_ref, xs_hbm, buf, sem):
    i = pl.program_id(0)
    nt = pl.num_programs(0)
    tt = h2_ref.shape[0]
    nrows = buf.shape[1]
    sizes = _segment_pieces(tt)
    slot = i % 2

    h2 = h2_ref[...]
    for r0, rn in _row_groups(nrows, 4):
        perm = jnp.where(_sorted_onehot(pos_ref, r0, rn, tt), 1.0, 0.0).astype(BF16)
        buf[slot, r0:r0 + rn, :] = _dot(perm, h2)

    def seg_start(tile, which):
        def body(e, carry):
            _segment_dma(buf.at[which], ssrc_ref[tile, e], xs_hbm, sdst_ref[tile, e], len_ref[tile, e], sizes,
                         sem.at[which], True)
            return carry
        lax.fori_loop(0, N_EXPERTS, body, 0)

    def seg_wait(tile, which):
        total = ssrc_ref[tile, N_EXPERTS - 1] + len_ref[tile, N_EXPERTS - 1]
        _segment_dma(buf.at[which], 0, xs_hbm, 0, total, _segment_pieces(nrows), sem.at[which], False)

    @pl.when(i > 0)
    def _():
        seg_wait(i - 1, 1 - slot)

    seg_start(i, slot)

    @pl.when(i == nt - 1)
    def _():
        seg_wait(i, slot)
        zrows = bm
        zbuf = buf.at[0]
        zsem = sem.at[0]
        buf[0, 0:zrows, :] = jnp.zeros((zrows, buf.shape[2]), F32)

        def tail(start):
            def body(e, carry):
                _segment_dma(zbuf, 0, xs_hbm, tail_ref[TAIL_START, e], tail_ref[TAIL_LEN, e],
                             _segment_pieces(zrows // 2), zsem, start)
                return carry
            lax.fori_loop(0, N_EXPERTS, body, 0)

        tail(True)
        tail(False)

        used = tail_ref[PAD_END, tail_ref.shape[1] - 1]
        n_unused = (xs_hbm.shape[0] - used) // zrows

        def unused_copy(j):
            d0 = pl.multiple_of(used + j * zrows, 8)
            return pltpu.make_async_copy(zbuf.at[pl.ds(0, zrows)], xs_hbm.at[pl.ds(d0, zrows)], zsem)

        def unused_start(j, carry):
            unused_copy(j).start()
            return carry

        def unused_wait(j, carry):
            unused_copy(j).wait()
            return carry

        lax.fori_loop(0, n_unused, unused_start, 0)
        lax.fori_loop(0, n_unused, unused_wait, 0)


def _dispatch(seg_len, seg_src, seg_dst, tail, pos, h2, n_slots, tt, bm):
    t, d = h2.shape
    nrows = TOP_K * tt + 8 * N_EXPERTS
    assert bm <= nrows
    smem = pl.BlockSpec(memory_space=pltpu.SMEM)
    return pl.pallas_call(
        functools.partial(_dispatch_kernel, bm),
        grid=(t // tt,),
        in_specs=[smem, smem, smem, smem,
                  pl.BlockSpec((TOP_K, tt), lambda i: (0, i)),
                  pl.BlockSpec((tt, d), lambda i: (i, 0))],
        out_specs=pl.BlockSpec(memory_space=pl.ANY),
        out_shape=jax.ShapeDtypeStruct((n_slots, d), F32),
        scratch_shapes=[pltpu.VMEM((2, nrows, d), F32), pltpu.SemaphoreType.DMA((2,))],
        compiler_params=pltpu.CompilerParams(dimension_semantics=("arbitrary",), has_side_effects=True,
                                             vmem_limit_bytes=V7X_VMEM_LIMIT),
        name="dispatch",
    )(seg_len, seg_src, seg_dst, tail, pos, h2)


def _experts_kernel(tab_ref, xs_hbm, wgu_hbm, bgu_ref, wd_hbm, bd_ref, y_hbm, wgu_s, wd_s, wgu_f, wd_f, xbuf, ybuf,
                    sem_in, sem_out, sem_w):
    e = pl.program_id(0)
    n_e = pl.num_programs(0)
    bm = xbuf.shape[1] // 2
    d_ff = wd_f.shape[1]
    ct = wgu_s.shape[2]
    wslot = e % 2

    def w_copies(expert, which):
        return (pltpu.make_async_copy(wgu_hbm.at[expert], wgu_f.at[which], sem_w.at[which]),
                pltpu.make_async_copy(wd_hbm.at[expert], wd_f.at[which], sem_w.at[which]))

    @pl.when(e == 0)
    def _():
        for cp in w_copies(0, 0):
            cp.start()

    for cp in w_copies(e, wslot):
        cp.wait()

    have_next = e + 1 < n_e

    def prefetch_next_weights(cond):
        @pl.when(have_next & cond)
        def _():
            for cp in w_copies(e + 1, 1 - wslot):
                cp.start()
    first = tab_ref[BLOCK_START, e]
    count = tab_ref[BLOCK_COUNT, e]
    npair = count // 2
    odd = count % 2
    tslot = npair % 2

    def pair_rows(j):
        return pl.ds(pl.multiple_of((first + 2 * j) * bm, bm), 2 * bm)

    def x_copy(j, slot):
        return pltpu.make_async_copy(xs_hbm.at[pair_rows(j)], xbuf.at[slot], sem_in.at[slot])

    def y_copy(j, slot):
        return pltpu.make_async_copy(ybuf.at[slot], y_hbm.at[pair_rows(j)], sem_out.at[slot])

    def last_rows():
        return pl.ds(pl.multiple_of((first + 2 * npair) * bm, bm), bm)

    def x_last(slot):
        return pltpu.make_async_copy(xs_hbm.at[last_rows()], xbuf.at[slot, pl.ds(0, bm)], sem_in.at[slot])

    def y_last(slot):
        return pltpu.make_async_copy(ybuf.at[slot, pl.ds(0, bm)], y_hbm.at[last_rows()], sem_out.at[slot])

    @pl.when(npair > 0)
    def _():
        x_copy(0, 0).start()

    @pl.when((npair == 0) & (odd == 1))
    def _():
        x_last(0).start()

    for t in range(wgu_s.shape[0]):
        wgu_s[t] = wgu_f[wslot, :, t * ct:(t + 1) * ct].astype(BF16)
    for t in range(wd_s.shape[0]):
        wd_s[t] = wd_f[wslot, :, t * ct:(t + 1) * ct].astype(BF16)

    def drain_outputs(n_blocks):
        pairs = n_blocks // 2
        last = n_blocks % 2
        last_slot = pairs % 2

        def wait_rows(slot, rows):
            pltpu.make_async_copy(ybuf.at[slot, pl.ds(0, rows)], y_hbm.at[pl.ds(0, rows)], sem_out.at[slot]).wait()

        @pl.when((last == 0) & (pairs >= 2))
        def _():
            wait_rows(last_slot, 2 * bm)

        @pl.when(pairs >= 1)
        def _():
            wait_rows(1 - last_slot, 2 * bm)

        @pl.when(last == 1)
        def _():
            wait_rows(last_slot, bm)

    @pl.when(e > 0)
    def _():
        drain_outputs(tab_ref[BLOCK_COUNT, jnp.maximum(e - 1, 0)])

    def mlp(x):
        xb = x.astype(BF16)
        tpc = 2
        fc = tpc * ct
        nf = d_ff // fc
        n_out = wd_s.shape[0]

        def gate_up(f):
            gates, ups = [], []
            for t in range(f * tpc, (f + 1) * tpc):
                gates.append(_dot(xb, wgu_s[t]) + bgu_ref[:, t * ct:(t + 1) * ct])
                u = d_ff // ct + t
                ups.append(_dot(xb, wgu_s[u]) + bgu_ref[:, u * ct:(u + 1) * ct])
            return jnp.concatenate(gates, axis=-1), jnp.concatenate(ups, axis=-1)

        acts = []
        pre = gate_up(0)
        for f in range(nf):
            gate = jnp.minimum(pre[0], SWIGLU_LIMIT)
            up = jnp.clip(pre[1], -SWIGLU_LIMIT, SWIGLU_LIMIT)
            if f + 1 < nf:
                pre = gate_up(f + 1)
            acts.append(((up + 1.0) * gate * _sigmoid(SWIGLU_ALPHA * gate)).astype(BF16))
        act = jnp.concatenate(acts, axis=-1)
        return [_dot(act, wd_s[n]) + bd_ref[:, n * ct:(n + 1) * ct] for n in range(n_out)]

    def pair(j, carry):
        slot = j % 2
        x_copy(j, slot).wait()

        @pl.when(j + 1 < npair)
        def _():
            x_copy(j + 1, 1 - slot).start()

        @pl.when((j + 1 == npair) & (odd == 1))
        def _():
            x_last(1 - slot).start()

        @pl.when(j >= 2)
        def _():
            y_copy(j - 2, slot).wait()

        prefetch_next_weights(j == 0)
        for n, piece in enumerate(mlp(xbuf[slot])):
            ybuf[slot, :, n * ct:(n + 1) * ct] = piece
        y_copy(j, slot).start()
        return carry

    lax.fori_loop(0, npair, pair, 0)

    prefetch_next_weights(npair == 0)

    @pl.when(odd == 1)
    def _():
        x_last(tslot).wait()

        @pl.when(npair >= 2)
        def _():
            y_copy(npair - 2, tslot).wait()

        for n, piece in enumerate(mlp(xbuf[tslot, 0:bm, :])):
            ybuf[tslot, 0:bm, n * ct:(n + 1) * ct] = piece
        y_last(tslot).start()

    @pl.when(e == n_e - 1)
    def _():
        drain_outputs(count)

    @pl.when(e == pl.num_programs(0) - 1)
    def _():
        used = first + count
        n_unused = y_hbm.shape[0] // bm - used
        ybuf[0, 0:bm, :] = jnp.zeros((bm, ybuf.shape[2]), F32)

        def z_copy(j):
            d0 = pl.multiple_of((used + j) * bm, bm)
            return pltpu.make_async_copy(ybuf.at[0, pl.ds(0, bm)], y_hbm.at[pl.ds(d0, bm)], sem_out.at[0])

        def z_start(j, carry):
            z_copy(j).start()
            return carry

        def z_wait(j, carry):
            z_copy(j).wait()
            return carry

        lax.fori_loop(0, n_unused, z_start, 0)
        lax.fori_loop(0, n_unused, z_wait, 0)


def _experts(table, xs, wgu, bgu, wd, bd, bm):
    ns, d = xs.shape
    n_e, _, two_ff = wgu.shape
    d_ff = two_ff // 2
    wmap = lambda e: (e, 0, 0)
    return pl.pallas_call(
        _experts_kernel,
        grid=(n_e,),
        in_specs=[pl.BlockSpec(memory_space=pltpu.SMEM),
                  pl.BlockSpec(memory_space=pl.ANY),
                  pl.BlockSpec(memory_space=pl.ANY),
                  pl.BlockSpec((None, 1, two_ff), wmap),
                  pl.BlockSpec(memory_space=pl.ANY),
                  pl.BlockSpec((None, 1, d), wmap)],
        out_specs=pl.BlockSpec(memory_space=pl.ANY),
        out_shape=jax.ShapeDtypeStruct((ns, d), F32),
        scratch_shapes=[pltpu.VMEM((two_ff // MXU_COLS, d, MXU_COLS), BF16),
                        pltpu.VMEM((d // MXU_COLS, d_ff, MXU_COLS), BF16),
                        pltpu.VMEM((2, d, two_ff), F32), pltpu.VMEM((2, d_ff, d), F32),
                        pltpu.VMEM((2, 2 * bm, d), F32), pltpu.VMEM((2, 2 * bm, d), F32),
                        pltpu.SemaphoreType.DMA((2,)), pltpu.SemaphoreType.DMA((2,)),
                        pltpu.SemaphoreType.DMA((2,))],
        compiler_params=pltpu.CompilerParams(dimension_semantics=("arbitrary",), has_side_effects=True,
                                             vmem_limit_bytes=V7X_VMEM_LIMIT),
        name="experts",
    )(table, xs, wgu, bgu, wd, bd)


def _combine_kernel(len_ref, ssrc_ref, sdst_ref, pos_ref, w_ref, y_hbm, x1_ref, g2_ref, nw_ref, sh_ref, sc_ref,
                    o_ref, ybuf, sem):
    i = pl.program_id(0)
    nt = pl.num_programs(0)
    tt = x1_ref.shape[0]
    nrows = ybuf.shape[1]
    sizes = _segment_pieces(tt)
    slot = i % 2

    def seg_start(tile, which):
        def body(e, carry):
            _segment_dma(y_hbm, sdst_ref[tile, e], ybuf.at[which], ssrc_ref[tile, e], len_ref[tile, e], sizes,
                         sem.at[which], True)
            return carry
        lax.fori_loop(0, N_EXPERTS, body, 0)

    def seg_wait(tile, which):
        total = ssrc_ref[tile, N_EXPERTS - 1] + len_ref[tile, N_EXPERTS - 1]
        _segment_dma(y_hbm, 0, ybuf.at[which], 0, total, _segment_pieces(nrows), sem.at[which], False)

    @pl.when(i == 0)
    def _():
        ybuf[...] = jnp.zeros(ybuf.shape, F32)
        seg_start(0, 0)

    @pl.when(i + 1 < nt)
    def _():
        seg_start(i + 1, 1 - slot)

    seg_wait(i, slot)

    acc = None
    for r0, rn in _row_groups(nrows, 3):
        r_i = lax.broadcasted_iota(jnp.int32, (rn, tt), 0) + r0
        wmat = 0.0
        for kk in range(TOP_K):
            wmat = jnp.where(r_i == pos_ref[kk:kk + 1, :], w_ref[kk:kk + 1, :], wmat)
        part = _dot_tn(wmat.astype(BF16), ybuf[slot, r0:r0 + rn, :].astype(BF16))
        acc = part if acc is None else acc + part
    xo = x1_ref[...] + g2_ref[...] * acc
    ms = jnp.mean(xo * xo, axis=-1, keepdims=True)
    hn = xo * lax.rsqrt(ms + EPS) * nw_ref[...]
    o_ref[...] = hn * (1.0 + sc_ref[...]) + sh_ref[...]


def _combine(seg_len, seg_src, seg_dst, pos, wrow, y, x1, mod3, nw, modf3, seq, tt):
    t, d = x1.shape
    tps = seq // tt
    nrows = TOP_K * tt + 8 * N_EXPERTS
    smem = pl.BlockSpec(memory_space=pltpu.SMEM)
    return pl.pallas_call(
        _combine_kernel,
        grid=(t // tt,),
        in_specs=[smem, smem, smem,
                  pl.BlockSpec((TOP_K, tt), lambda i: (0, i)),
                  pl.BlockSpec((TOP_K, tt), lambda i: (0, i)),
                  pl.BlockSpec(memory_space=pl.ANY),
                  pl.BlockSpec((tt, d), lambda i: (i, 0)),
                  _mod_spec(tps, d, 5),
                  pl.BlockSpec((1, d), lambda i: (0, 0)),
                  _mod_spec(tps, d, 0),
                  _mod_spec(tps, d, 1)],
        out_specs=pl.BlockSpec((tt, d), lambda i: (i, 0)),
        out_shape=jax.ShapeDtypeStruct((t, d), F32),
        scratch_shapes=[pltpu.VMEM((2, nrows, d), F32), pltpu.SemaphoreType.DMA((2,))],
        compiler_params=pltpu.CompilerParams(dimension_semantics=("arbitrary",),
                                             vmem_limit_bytes=V7X_VMEM_LIMIT),
        name="combine",
    )(seg_len, seg_src, seg_dst, pos, wrow, y, x1, mod3, nw, modf3, modf3)


def _pick_tile(n, pref):
    tile = pref
    while n % tile:
        tile //= 2
    return tile


def kernel(x, c, w_ada, b_ada, norm_mix, w_in, dn_conv, dn_a_log, dn_dt_bias, dn_norm, ml_conv, ml_i_bias,
           ml_f_bias, ml_norm, w_out, norm_ffn, w_router, b_router, w_gate_up, b_gate_up, w_down, b_down,
           w_ada_final, b_ada_final, norm_final):
    batch, seq, d = x.shape
    assert w_ada.shape[0] == 1, "single-layer block"
    assert seq % CHUNK == 0
    t = batch * seq
    x2 = x.reshape(t, d)

    c_pad = jnp.pad(c, ((0, 8 - batch % 8 if batch % 8 else 0), (0, 0)))
    mod3 = _mods(c_pad, w_ada, b_ada.reshape(1, 6 * d)).reshape(-1, 1, 6 * d)
    modf3 = _mods(c_pad, w_ada_final[None], b_ada_final.reshape(1, 2 * d)).reshape(-1, 1, 2 * d)

    w_new, wgt = _wprep(w_in)
    cw = jnp.concatenate([dn_conv.reshape(CONV_W, -1), ml_conv.reshape(CONV_W, -1)], axis=1)
    zeros4 = jnp.zeros((4,), F32)
    bias16 = jnp.concatenate([zeros4, dn_dt_bias.reshape(4), ml_i_bias.reshape(4), ml_f_bias.reshape(4)])
    alog16 = jnp.concatenate([zeros4, dn_a_log.reshape(4), zeros4, zeros4])
    gpc = jnp.zeros((8, HEAD_LANES), F32).at[0, 0:16].set(bias16).at[1, 0:16].set(alog16)
    gpr = jnp.zeros((16, HEAD_LANES), F32).at[:, 0].set(bias16).at[:, 1].set(alog16)

    tm_in = _pick_tile(seq, 512)
    conv_out, rest, gcol, grow = _inproj(x2, mod3, norm_mix.reshape(1, d), w_new, wgt, cw, gpc, gpr, seq, tm_in)
    grow3 = grow.reshape(16, t // CHUNK, CHUNK).transpose(1, 0, 2)

    rows = _pick_tile(seq, 256)
    ymix = _mixers(conv_out, rest, gcol, grow3, dn_norm.reshape(1, DN_DV), ml_norm.reshape(1, ML_HEADS * ML_DV),
                   batch, seq, rows)

    wo = w_out.reshape(-1, d).astype(BF16)
    tm_r = _pick_tile(seq, 512)
    brp = jnp.broadcast_to(b_router.reshape(N_EXPERTS, 1), (N_EXPERTS, HEAD_LANES))
    x1, h2, pos, wrow, len_col = _route(
        ymix, wo, x2, mod3, norm_ffn.reshape(1, d),
        w_router.reshape(d, N_EXPERTS).T, brp, seq, tm_r)

    n_e = N_EXPERTS
    nt = t // tm_r
    bm = 256
    len_te =len_col.reshape(nt, n_e, HEAD_LANES)[:, :, 0]
    seg_len = jnp.pad(len_te, ((0, 0), (0, HEAD_LANES - n_e)))
    n_slots_max = t * TOP_K + n_e * (7 * nt + bm)
    nb = (n_slots_max + bm - 1) // bm
    seg_src, seg_dst, table = _slots(seg_len, bm)

    xs = _dispatch(seg_len, seg_src, seg_dst, table, pos, h2, nb * bm, tm_r, bm)
    y = _experts(table, xs, w_gate_up.reshape(n_e, d, -1), b_gate_up.reshape(n_e, 1, -1),
                 w_down.reshape(n_e, -1, d), b_down.reshape(n_e, 1, d), bm)
    out = _combine(seg_len, seg_src, seg_dst, pos, wrow, y, x1, mod3, norm_final.reshape(1, d), modf3, seq, tm_r)
    return out.reshape(batch, seq, d)
```
